```python
import math
import jax
import jax.numpy as jnp
from jax import lax
import numpy as np


D_MODEL = 1024
BATCH = 1
SEQ = 16384
DEPTH = 4

N_MIXERS = 2
S5_GROUP = 16
S5_GROUPS = D_MODEL // S5_GROUP
S5_STATE = 64
N_HEADS = 16
HEAD_DIM = D_MODEL // N_HEADS
IDX_HEADS = 8
IDX_DIM = 64
TOPK_MAX = 256
Q_BLOCK = 128
ROPE_THETA = 10000.0
D_FF = ((8 * D_MODEL // 3 + 255) // 256) * 256
N_S5 = (DEPTH + 1) // 2
N_DSA = DEPTH // 2
DSA_PROJ = 3 * D_MODEL + IDX_HEADS * IDX_DIM + IDX_DIM + IDX_HEADS
EPS = 1e-6

kernel_name = 'hybrid_s5_dsa_swiglu_trunk'


def rmsnorm(x, gain):
    xf = x.astype(jnp.float32)
    y = xf * lax.rsqrt(jnp.mean(xf * xf, axis=-1, keepdims=True) + EPS)
    if gain is not None:
        y = y * gain.astype(jnp.float32)
    return y.astype(x.dtype)


def rope_tables(length):
    inv_freq = ROPE_THETA ** (-jnp.arange(0, HEAD_DIM, 2, dtype=jnp.float32) / HEAD_DIM)
    ang = jnp.arange(length, dtype=jnp.float32)[:, None] * inv_freq[None, :]
    return jnp.cos(ang), jnp.sin(ang)


def apply_rope(t, cos, sin):
    tf = t.astype(jnp.float32)
    t1, t2 = jnp.split(tf, 2, axis=-1)
    c = cos[None, :, None, :]
    s = sin[None, :, None, :]
    out = jnp.concatenate([t1 * c - t2 * s, t2 * c + t1 * s], axis=-1)
    return out.astype(t.dtype)


def _ssm_combine(e1, e2):
    a1, b1 = e1
    a2, b2 = e2
    return a1 * a2, a2 * b1 + b2


def s5_mixer(u, lam_re, lam_im, log_dt, b_re, b_im, c_re, c_im, d_skip, w_glu):
    bsz, length, _ = u.shape
    f32 = jnp.float32
    lam = lax.complex(lam_re.astype(f32), lam_im.astype(f32))
    dt = jnp.exp(log_dt.astype(f32))[:, None]
    a_bar = jnp.exp(lam * dt)
    b_bar = ((a_bar - 1.0) / lam)[..., None] * lax.complex(b_re.astype(f32), b_im.astype(f32))
    ug = u.astype(f32).reshape(bsz, length, S5_GROUPS, S5_GROUP).astype(jnp.complex64)
    bu = jnp.einsum('blgh,gph->blgp', ug, b_bar)
    a_seq = jnp.broadcast_to(a_bar, bu.shape)
    _, states = lax.associative_scan(_ssm_combine, (a_seq, bu), axis=1)
    c = lax.complex(c_re.astype(f32), c_im.astype(f32))
    y = jnp.einsum('blgp,ghp->blgh', states, c).real.reshape(bsz, length, D_MODEL)
    y = y + d_skip.astype(f32) * u.astype(f32)
    y = jax.nn.gelu(y).astype(u.dtype)
    val, gate = jnp.split(y @ w_glu, 2, axis=-1)
    return val * jax.nn.sigmoid(gate)


def dsa_mixer(h, w_in, q_gain, k_gain, w_o, cos, sin, topk):
    bsz, length, _ = h.shape
    f32 = jnp.float32
    proj = h @ w_in
    cuts = [D_MODEL, 2 * D_MODEL, 3 * D_MODEL, 3 * D_MODEL + IDX_HEADS * IDX_DIM,
            3 * D_MODEL + IDX_HEADS * IDX_DIM + IDX_DIM]
    q, k, v, qi, ki, wi = jnp.split(proj, cuts, axis=-1)
    q = q.reshape(bsz, length, N_HEADS, HEAD_DIM)
    k = k.reshape(bsz, length, N_HEADS, HEAD_DIM)
    v = v.reshape(bsz, length, N_HEADS, HEAD_DIM)
    qi = qi.reshape(bsz, length, IDX_HEADS, IDX_DIM)
    q = apply_rope(rmsnorm(q, q_gain), cos, sin)
    k = apply_rope(rmsnorm(k, k_gain), cos, sin)
    qi = apply_rope(qi, cos, sin)
    ki = apply_rope(rmsnorm(ki, None)[:, :, None, :], cos, sin)[:, :, 0, :]
    wi = wi.astype(f32) * (IDX_HEADS ** -0.5)
    nb = length // Q_BLOCK
    kpos = jnp.arange(length, dtype=jnp.int32)
    qpos_blocks = kpos.reshape(nb, Q_BLOCK)
    att_scale = HEAD_DIM ** -0.5
    idx_scale = IDX_DIM ** -0.5

    def to_blocks(t):
        return t.reshape(bsz, nb, Q_BLOCK, *t.shape[2:]).swapaxes(0, 1)

    def attend_block(args):
        qb, qib, wb, qpos = args
        s_idx = jnp.einsum('bqhd,bkd->bqhk', qib, ki).astype(f32) * idx_scale
        score = jnp.einsum('bqhk,bqh->bqk', jax.nn.relu(s_idx), wb)
        causal = kpos[None, None, :] <= qpos[None, :, None]
        score = jnp.where(causal, score, -jnp.inf)
        _, sel = lax.top_k(score, topk)
        kg = jax.vmap(lambda kb, ib: kb[ib])(k, sel)
        vg = jax.vmap(lambda vb, ib: vb[ib])(v, sel)
        logits = jnp.einsum('bqhd,bqkhd->bhqk', qb, kg).astype(f32) * att_scale
        valid = (sel <= qpos[None, :, None])[:, None, :, :]
        logits = jnp.where(valid, logits, -jnp.inf)
        p = jax.nn.softmax(logits, axis=-1).astype(vg.dtype)
        return jnp.einsum('bhqk,bqkhd->bqhd', p, vg)

    out = lax.map(attend_block, (to_blocks(q), to_blocks(qi), to_blocks(wi), qpos_blocks))
    out = out.swapaxes(0, 1).reshape(bsz, length, D_MODEL)
    return out @ w_o


def swiglu(h, w_gate_up, w_down):
    gate, up = jnp.split(h @ w_gate_up, 2, axis=-1)
    return (jax.nn.silu(gate) * up) @ w_down


def setup_inputs(seed: int = 0) -> dict:
    key = jax.random.key(seed)
    ks = jax.random.split(key, 20)
    f32 = jnp.float32
    nrm = lambda k, shape, s: jax.random.normal(k, shape, f32) * s
    x = jax.random.normal(ks[0], (BATCH, SEQ, D_MODEL), f32)
    n_idx = jnp.arange(S5_STATE, dtype=f32)
    s5_lambda_re = -0.5 + nrm(ks[1], (N_S5, S5_GROUPS, S5_STATE), 0.01)
    s5_lambda_im = math.pi * n_idx[None, None, :] + nrm(ks[2], (N_S5, S5_GROUPS, S5_STATE), 0.01)
    s5_log_dt = jax.random.uniform(ks[3], (N_S5, S5_GROUPS), f32, math.log(1e-3), math.log(1e-1))
    s5_b_re = nrm(ks[4], (N_S5, S5_GROUPS, S5_STATE, S5_GROUP), (2 * S5_GROUP) ** -0.5)
    s5_b_im = nrm(ks[5], (N_S5, S5_GROUPS, S5_STATE, S5_GROUP), (2 * S5_GROUP) ** -0.5)
    s5_c_re = nrm(ks[6], (N_S5, S5_GROUPS, S5_GROUP, S5_STATE), (2 * S5_STATE) ** -0.5)
    s5_c_im = nrm(ks[7], (N_S5, S5_GROUPS, S5_GROUP, S5_STATE), (2 * S5_STATE) ** -0.5)
    s5_d = nrm(ks[8], (N_S5, D_MODEL), 1.0)
    s5_w_glu = nrm(ks[9], (N_S5, D_MODEL, 2 * D_MODEL), D_MODEL ** -0.5)
    dsa_w_in = nrm(ks[10], (N_DSA, D_MODEL, DSA_PROJ), D_MODEL ** -0.5)
    dsa_q_norm = 1.0 + nrm(ks[11], (N_DSA, HEAD_DIM), 0.02)
    dsa_k_norm = 1.0 + nrm(ks[12], (N_DSA, HEAD_DIM), 0.02)
    dsa_w_o = nrm(ks[13], (N_DSA, D_MODEL, D_MODEL), D_MODEL ** -0.5)
    ffn_w_gate_up = nrm(ks[14], (DEPTH, D_MODEL, 2 * D_FF), D_MODEL ** -0.5)
    ffn_w_down = nrm(ks[15], (DEPTH, D_FF, D_MODEL), D_FF ** -0.5)
    norm_mix = 1.0 + nrm(ks[16], (DEPTH, D_MODEL), 0.02)
    norm_ffn = 1.0 + nrm(ks[17], (DEPTH, D_MODEL), 0.02)
    return {'x': x, 's5_lambda_re': s5_lambda_re, 's5_lambda_im': s5_lambda_im, 's5_log_dt': s5_log_dt,
            's5_b_re': s5_b_re, 's5_b_im': s5_b_im, 's5_c_re': s5_c_re, 's5_c_im': s5_c_im,
            's5_d': s5_d, 's5_w_glu': s5_w_glu, 'dsa_w_in': dsa_w_in, 'dsa_q_norm': dsa_q_norm,
            'dsa_k_norm': dsa_k_norm, 'dsa_w_o': dsa_w_o, 'ffn_w_gate_up': ffn_w_gate_up,
            'ffn_w_down': ffn_w_down, 'norm_mix': norm_mix, 'norm_ffn': norm_ffn}


def reference(x, s5_lambda_re, s5_lambda_im, s5_log_dt, s5_b_re, s5_b_im, s5_c_re, s5_c_im,
              s5_d, s5_w_glu, dsa_w_in, dsa_q_norm, dsa_k_norm, dsa_w_o, ffn_w_gate_up,
              ffn_w_down, norm_mix, norm_ffn):
    length = x.shape[1]
    topk = min(TOPK_MAX, length // 4)
    cos, sin = rope_tables(length)
    for i in range(DEPTH):
        h = rmsnorm(x, norm_mix[i])
        j = i // N_MIXERS
        if i % N_MIXERS == 0:
            mix = s5_mixer(h, s5_lambda_re[j], s5_lambda_im[j], s5_log_dt[j], s5_b_re[j], s5_b_im[j],
                           s5_c_re[j], s5_c_im[j], s5_d[j], s5_w_glu[j])
        else:
            mix = dsa_mixer(h, dsa_w_in[j], dsa_q_norm[j], dsa_k_norm[j], dsa_w_o[j], cos, sin, topk)
        x = x + mix.astype(x.dtype)
        x = x + swiglu(rmsnorm(x, norm_ffn[i]), ffn_w_gate_up[i], ffn_w_down[i]).astype(x.dtype)
    return x
```

```python
import functools
import math

import jax
import jax.numpy as jnp
import numpy as np
from jax import lax
from jax.experimental import pallas as pl
from jax.experimental.pallas import tpu as pltpu

F32 = jnp.float32
BF16 = jnp.bfloat16
I32 = jnp.int32
MXU_DTYPE = BF16

D_MODEL = 1024
S5_GROUP = 16
S5_STATE = 64
N_HEADS = 16
HEAD_DIM = 64
IDX_HEADS = 8
IDX_DIM = 64
TOPK_MAX = 256
ROPE_THETA = 10000.0
EPS = 1e-6

LANES = 128
SUBLANES = 8
MXU_DIM = 256
VMEM_LIMIT = 56 * 1024 * 1024

S5_CHUNK = 16
S5_SLAB_GROUPS = LANES // S5_GROUP
NEG_BIG = -1e30

ROW_TM = 512
NORM_TM = 1024
S5_TC = 512
PROJ_TM = 512
SEL_TQ = 256
SEL_TK = 512
SEL_RB = 64
ATT_TQ = 512
ATT_TK = 512


def _cparams(sem):
    return pltpu.CompilerParams(dimension_semantics=sem, vmem_limit_bytes=VMEM_LIMIT)


def _rms(x, gain=None):
    y = x * lax.rsqrt(jnp.mean(x * x, axis=-1, keepdims=True) + EPS)
    return y if gain is None else y * gain


def _dot(a, b):
    return jnp.dot(a, b, preferred_element_type=F32)


def _dot_nt(a, b):
    return lax.dot_general(a, b, (((1,), (1,)), ((), ())), preferred_element_type=F32)


def _ffn_kernel(x_ref, g_ref, wgu_ref, wd_ref, o_ref, acc_ref, *, d_ff, fc):
    x = x_ref[...]
    h = _rms(x, g_ref[...]).astype(MXU_DTYPE)
    for c in range(d_ff // fc):
        g = _dot(h, wgu_ref[:, c * fc:(c + 1) * fc])
        u = _dot(h, wgu_ref[:, d_ff + c * fc:d_ff + (c + 1) * fc])
        a = (g * jax.nn.sigmoid(g) * u).astype(MXU_DTYPE)
        d = _dot(a, wd_ref[c * fc:(c + 1) * fc, :])
        if c == 0:
            acc_ref[...] = d
        else:
            acc_ref[...] += d
    o_ref[...] = x + acc_ref[...]


def _ffn(x, gain, wgu, wd):
    n, d = x.shape
    d_ff = wd.shape[0]
    tm = min(ROW_TM, n)
    fc = MXU_DIM
    return pl.pallas_call(
        functools.partial(_ffn_kernel, d_ff=d_ff, fc=fc),
        out_shape=jax.ShapeDtypeStruct((n, d), F32),
        grid=(n // tm,),
        in_specs=[
            pl.BlockSpec((tm, d), lambda i: (i, 0)),
            pl.BlockSpec((1, d), lambda i: (0, 0)),
            pl.BlockSpec((d, 2 * d_ff), lambda i: (0, 0)),
            pl.BlockSpec((d_ff, d), lambda i: (0, 0)),
        ],
        out_specs=pl.BlockSpec((tm, d), lambda i: (i, 0)),
        scratch_shapes=[pltpu.VMEM((tm, d), F32)],
        compiler_params=_cparams(("parallel",)),
        name="ffn",
    )(x, gain.reshape(1, d), wgu, wd)


def _norm_kernel(x_ref, g_ref, o_ref):
    o_ref[...] = _rms(x_ref[...], g_ref[...]).astype(o_ref.dtype)


def _norm(x, gain):
    n, d = x.shape
    tm = min(NORM_TM, n)
    return pl.pallas_call(
        _norm_kernel,
        out_shape=jax.ShapeDtypeStruct((n, d), MXU_DTYPE),
        grid=(n // tm,),
        in_specs=[pl.BlockSpec((tm, d), lambda i: (i, 0)),
                  pl.BlockSpec((1, d), lambda i: (0, 0))],
        out_specs=pl.BlockSpec((tm, d), lambda i: (i, 0)),
        compiler_params=_cparams(("parallel",)),
        name="norm",
    )(x, gain.reshape(1, d))


def _s5_tables(lam_re, lam_im, log_dt, b_re, b_im, c_re, c_im):
    hp = lax.Precision.HIGHEST
    g, p = lam_re.shape
    h = S5_GROUP
    nsl = g // S5_SLAB_GROUPS
    sg = S5_SLAB_GROUPS
    t = S5_CHUNK
    lam_re, lam_im, log_dt = lam_re.astype(F32), lam_im.astype(F32), log_dt.astype(F32)
    b_re, b_im, c_re, c_im = (a.astype(F32) for a in (b_re, b_im, c_re, c_im))
    dt = jnp.exp(log_dt)[:, None]

    def apow(k):
        k = jnp.asarray(k, F32).reshape((-1, 1, 1))
        mag = jnp.exp(lam_re[None] * dt[None] * k)
        ang = lam_im[None] * dt[None] * k
        return mag * jnp.cos(ang), mag * jnp.sin(ang)

    ar, ai = apow([1.0])
    ar, ai = ar[0], ai[0]
    den = lam_re * lam_re + lam_im * lam_im
    nr, ni = ar - 1.0, ai
    qr = (nr * lam_re + ni * lam_im) / den
    qi = (ni * lam_re - nr * lam_im) / den
    bbr = qr[..., None] * b_re - qi[..., None] * b_im
    bbi = qr[..., None] * b_im + qi[..., None] * b_re

    pr, pi = apow(np.arange(t + 1))
    mr = c_re[None] * pr[:, :, None, :] - c_im[None] * pi[:, :, None, :]
    mi = c_re[None] * pi[:, :, None, :] + c_im[None] * pr[:, :, None, :]

    kk = (jnp.einsum('tghp,gpk->tghk', mr[:t], bbr, precision=hp)
          - jnp.einsum('tghp,gpk->tghk', mi[:t], bbi, precision=hp))
    kp = jnp.concatenate([jnp.zeros_like(kk[:1]), kk], axis=0)
    eye = jnp.eye(sg, dtype=F32)

    dl = np.arange(t // 2)[:, None, None]
    sl = np.arange(2)[None, :, None]
    jl = np.arange(2)[None, None, :]
    idx = 2 * dl + jl - sl + 1
    kg = kp[idx]
    kg = kg.reshape(t // 2, 2, 2, nsl, sg, h, h)
    tp = jnp.einsum('dljbgok,gm->bdlgkjmo', kg, eye)
    tp = tp.reshape(nsl, t // 2, 2 * sg * h, 2 * sg * h)

    prs, pis = pr[t - 1::-1][:t], pi[t - 1::-1][:t]
    er = prs[..., None] * bbr[None] - pis[..., None] * bbi[None]
    ei = prs[..., None] * bbi[None] + pis[..., None] * bbr[None]
    bf = jnp.stack([er, ei], axis=2)
    bf = bf.reshape(t // 2, 2, nsl, sg, 2, p, h)
    bz = jnp.einsum('zlbgrpk,gm->bzlgkrmp', bf, eye)
    bz = bz.reshape(nsl, t // 2, 2 * sg * h, 2 * sg * p)

    cf = jnp.stack([mr[1:], -mi[1:]], axis=2)
    cf = cf.reshape(t // 2, 2, nsl, sg, 2, h, p)
    cz = jnp.einsum('ijbgrop,gm->birgpjmo', cf, eye)
    cz = cz.reshape(nsl, t // 2, 2 * sg * p, 2 * sg * h)

    def slab_state(re, im):
        k = re.shape[0]
        x = jnp.stack([re, im], axis=1).reshape(k, 2, nsl, sg * p)
        return x.transpose(2, 0, 1, 3).reshape(nsl, k, 2 * sg * p)

    ad = slab_state(*apow([t * 1.0, t * 2.0, t * 4.0]))
    ap8 = slab_state(*apow(t * (np.arange(SUBLANES) + 1.0)))
    return (tp.astype(MXU_DTYPE), bz.astype(MXU_DTYPE), cz.astype(MXU_DTYPE), ad, ap8)


def _s5_kernel(h_ref, tp_ref, bz_ref, cz_ref, ad_ref, ap8_ref, dsk_ref, o_ref,
               carry_ref, z_ref, xp_ref, *, tc):
    half = z_ref.shape[1] // 2
    npair = S5_CHUNK // 2

    @pl.when(pl.program_id(1) == 0)
    def _():
        carry_ref[...] = jnp.zeros_like(carry_ref)

    u = [jnp.concatenate([h_ref[2 * s], h_ref[2 * s + 1]], axis=1) for s in range(npair)]

    z = _dot(u[0], bz_ref[0])
    for s in range(1, npair):
        z = z + _dot(u[s], bz_ref[s])
    z_ref[...] = z

    row = lax.broadcasted_iota(I32, (SUBLANES, half), 0)
    pr8, pi8 = ap8_ref[:, :half], ap8_ref[:, half:]

    def tile_step(t, carry):
        r0 = pl.multiple_of(t * SUBLANES, SUBLANES)
        zt = z_ref[pl.ds(r0, SUBLANES), :]
        xr, xi = zt[:, :half], zt[:, half:]
        for k, d in enumerate((1, 2, 4)):
            a = ad_ref[k:k + 1, :]
            a_r, a_i = a[:, :half], a[:, half:]
            sr = jnp.where(row >= d, pltpu.roll(xr, d, 0), 0.0)
            si = jnp.where(row >= d, pltpu.roll(xi, d, 0), 0.0)
            xr, xi = xr + a_r * sr - a_i * si, xi + a_r * si + a_i * sr
        cr, ci = carry[:, :half], carry[:, half:]
        xr, xi = xr + pr8 * cr - pi8 * ci, xi + pr8 * ci + pi8 * cr
        xpr = jnp.where(row >= 1, pltpu.roll(xr, 1, 0), cr)
        xpi = jnp.where(row >= 1, pltpu.roll(xi, 1, 0), ci)
        xp_ref[pl.ds(r0, SUBLANES), :] = jnp.concatenate([xpr, xpi], axis=1)
        return jnp.concatenate([xr[SUBLANES - 1:], xi[SUBLANES - 1:]], axis=1)

    carry_ref[...] = lax.fori_loop(0, tc // SUBLANES, tile_step, carry_ref[...])

    xp = xp_ref[...].astype(MXU_DTYPE)
    dsk = dsk_ref[...]
    for i in range(npair):
        y = _dot(xp, cz_ref[i])
        for s in range(i + 1):
            y = y + _dot(u[s], tp_ref[i - s])
        for jl in range(2):
            j = 2 * i + jl
            yj = y[:, jl * LANES:(jl + 1) * LANES] + dsk * h_ref[j].astype(F32)
            o_ref[j] = jax.nn.gelu(yj).astype(o_ref.dtype)


def _s5_scan(h_t, tables, d_skip):
    tp, bz, cz, ad, ap8 = tables
    t, c, d = h_t.shape
    nsl = d // LANES
    tc = min(S5_TC, c)
    st = bz.shape[-1]
    dsk = d_skip.astype(F32).reshape(nsl, 1, LANES)
    return pl.pallas_call(
        functools.partial(_s5_kernel, tc=tc),
        out_shape=jax.ShapeDtypeStruct((t, c, d), MXU_DTYPE),
        grid=(nsl, c // tc),
        in_specs=[
            pl.BlockSpec((t, tc, LANES), lambda b, i: (0, i, b)),
            pl.BlockSpec((None,) + tp.shape[1:], lambda b, i: (b, 0, 0, 0)),
            pl.BlockSpec((None,) + bz.shape[1:], lambda b, i: (b, 0, 0, 0)),
            pl.BlockSpec((None,) + cz.shape[1:], lambda b, i: (b, 0, 0, 0)),
            pl.BlockSpec((None,) + ad.shape[1:], lambda b, i: (b, 0, 0)),
            pl.BlockSpec((None,) + ap8.shape[1:], lambda b, i: (b, 0, 0)),
            pl.BlockSpec((None, 1, LANES), lambda b, i: (b, 0, 0)),
        ],
        out_specs=pl.BlockSpec((t, tc, LANES), lambda b, i: (0, i, b)),
        scratch_shapes=[pltpu.VMEM((1, st), F32), pltpu.VMEM((tc, st), F32),
                        pltpu.VMEM((tc, st), F32)],
        compiler_params=_cparams(("arbitrary", "arbitrary")),
        name="s5_scan",
    )(h_t, tp, bz, cz, ad, ap8, dsk)


def _glu_kernel(x_ref, g_ref, w_ref, o_ref):
    d = x_ref.shape[1]
    vg = _dot(g_ref[...], w_ref[...])
    o_ref[...] = x_ref[...] + vg[:, :d] * jax.nn.sigmoid(vg[:, d:])


def _glu_residual(x, g, w):
    n, d = x.shape
    tm = min(ROW_TM, n)
    return pl.pallas_call(
        _glu_kernel,
        out_shape=jax.ShapeDtypeStruct((n, d), F32),
        grid=(n // tm,),
        in_specs=[pl.BlockSpec((tm, d), lambda i: (i, 0)),
                  pl.BlockSpec((tm, d), lambda i: (i, 0)),
                  pl.BlockSpec((d, 2 * d), lambda i: (0, 0))],
        out_specs=pl.BlockSpec((tm, d), lambda i: (i, 0)),
        compiler_params=_cparams(("parallel",)),
        name="glu",
    )(x, g, w)


def _head_perm(n_heads):
    n = np.arange(n_heads * HEAD_DIM)
    pb, r = n // LANES, n % LANES
    half, r2 = r // 64, r % 64
    hl, dp = r2 // 32, r2 % 32
    return (2 * pb + hl) * HEAD_DIM + 32 * half + dp


def _lane_head_mask(shape, hl):
    lane = lax.broadcasted_iota(I32, shape, len(shape) - 1)
    return ((lane % 64) // 32) == hl


def _proj_kernel(x_ref, g_ref, w_ref, gq_ref, gk_ref, cos_ref, sin_ref, hm_ref,
                 q_ref, k_ref, v_ref, qi_ref, ki_ref, wi_ref, *, d, dqi, att_scale, w_scale):
    h = _rms(x_ref[...], g_ref[...]).astype(MXU_DTYPE)
    cos, sin = cos_ref[...], sin_ref[...]
    hm = hm_ref[...]

    def rope(t):
        return t * cos + pltpu.roll(t, 64, 1) * sin

    def headnorm_rope(col0, gain_ref, out_ref, scale):
        for sb in range(d // MXU_DIM):
            c0 = sb * MXU_DIM
            t = _dot(h, w_ref[:, col0 + c0:col0 + c0 + MXU_DIM])
            sq = t * t
            hi = sq.astype(MXU_DTYPE)
            lo = (sq - hi.astype(F32)).astype(MXU_DTYPE)
            ss = _dot(hi, hm) + _dot(lo, hm)
            tn = t * lax.rsqrt(ss * (1.0 / HEAD_DIM) + EPS) * gain_ref[:, c0:c0 + MXU_DIM]
            for b in range(MXU_DIM // LANES):
                r = rope(tn[:, b * LANES:(b + 1) * LANES])
                if scale != 1.0:
                    r = r * scale
                out_ref[:, c0 + b * LANES:c0 + (b + 1) * LANES] = r.astype(out_ref.dtype)

    headnorm_rope(0, gq_ref, q_ref, att_scale)
    headnorm_rope(d, gk_ref, k_ref, 1.0)
    v_ref[...] = _dot(h, w_ref[:, 2 * d:3 * d]).astype(v_ref.dtype)
    c0 = 3 * d
    t = _dot(h, w_ref[:, c0:c0 + dqi])
    for b in range(dqi // LANES):
        qi_ref[:, b * LANES:(b + 1) * LANES] = rope(t[:, b * LANES:(b + 1) * LANES]).astype(qi_ref.dtype)
    c0 += dqi
    t = _dot(h, w_ref[:, c0:c0 + LANES])
    ms = jnp.sum(t * t, axis=-1, keepdims=True) * (0.5 / IDX_DIM)
    ki_ref[...] = rope(t * lax.rsqrt(ms + EPS)).astype(ki_ref.dtype)
    c0 += LANES
    wi_ref[...] = _dot(h, w_ref[:, c0:c0 + LANES]) * w_scale


def _dsa_project(x, gain, w_in, q_gain, k_gain, cos_t, sin_t):
    n, d = x.shape
    dqi = IDX_HEADS * IDX_DIM
    pq = _head_perm(N_HEADS)
    pqi = _head_perm(IDX_HEADS)
    wq = w_in[:, 0:d][:, pq]
    wk = w_in[:, d:2 * d][:, pq]
    wv = w_in[:, 2 * d:3 * d]
    wqi = w_in[:, 3 * d:3 * d + dqi][:, pqi]
    lane = np.arange(LANES)
    wki = w_in[:, 3 * d + dqi:3 * d + dqi + IDX_DIM][:, 32 * (lane // 64) + lane % 32]
    wwi = jnp.pad(w_in[:, 3 * d + dqi + IDX_DIM:], ((0, 0), (0, LANES - IDX_HEADS)))
    w_all = jnp.concatenate([wq, wk, wv, wqi, wki, wwi], axis=1).astype(MXU_DTYPE)
    dcol = (pq % HEAD_DIM)
    gq = q_gain.astype(F32)[dcol].reshape(1, d)
    gk = k_gain.astype(F32)[dcol].reshape(1, d)
    l2 = np.arange(MXU_DIM)
    hm = ((l2[:, None] // LANES == l2[None, :] // LANES)
          & ((l2[:, None] % 64) // 32 == (l2[None, :] % 64) // 32))
    hm = jnp.asarray(hm, MXU_DTYPE)
    tm = min(PROJ_TM, n)
    nw = w_all.shape[1]
    outs = pl.pallas_call(
        functools.partial(_proj_kernel, d=d, dqi=dqi, att_scale=HEAD_DIM ** -0.5,
                          w_scale=(IDX_HEADS ** -0.5) * (IDX_DIM ** -0.5)),
        out_shape=[jax.ShapeDtypeStruct((n, d), MXU_DTYPE)] * 3
        + [jax.ShapeDtypeStruct((n, dqi), MXU_DTYPE),
           jax.ShapeDtypeStruct((n, LANES), MXU_DTYPE),
           jax.ShapeDtypeStruct((n, LANES), F32)],
        grid=(n // tm,),
        in_specs=[
            pl.BlockSpec((tm, d), lambda i: (i, 0)),
            pl.BlockSpec((1, d), lambda i: (0, 0)),
            pl.BlockSpec((d, nw), lambda i: (0, 0)),
            pl.BlockSpec((1, d), lambda i: (0, 0)),
            pl.BlockSpec((1, d), lambda i: (0, 0)),
            pl.BlockSpec((tm, LANES), lambda i: (i, 0)),
            pl.BlockSpec((tm, LANES), lambda i: (i, 0)),
            pl.BlockSpec((MXU_DIM, MXU_DIM), lambda i: (0, 0)),
        ],
        out_specs=[pl.BlockSpec((tm, d), lambda i: (i, 0))] * 3
        + [pl.BlockSpec((tm, dqi), lambda i: (i, 0)),
           pl.BlockSpec((tm, LANES), lambda i: (i, 0)),
           pl.BlockSpec((tm, LANES), lambda i: (i, 0))],
        compiler_params=_cparams(("parallel",)),
        name="dsa_proj",
    )(x, gain.reshape(1, d), w_all, gq, gk, cos_t, sin_t, hm)
    return outs


KEY_NEG_INF = -2139095041
KEY_POS_INF = 2139095040


def _key_to_f32(key):
    bits = key ^ ((key >> 31) & 0x7FFFFFFF)
    return lax.bitcast_convert_type(bits, F32)


def _select_kernel(qi_ref, wi_ref, ki_ref, mask_ref, s_ref, lo_ref, hi_ref, clo_ref, chi_ref,
                   *, tq, tk, topk, rb):
    qb = pl.program_id(0)
    n_all = mask_ref.shape[1]
    nkt = (qb * tq) // tk + 1
    ncol = nkt * (tk // LANES)

    qm = []
    for hh in range(IDX_HEADS):
        blk = qi_ref[:, (hh // 2) * LANES:(hh // 2 + 1) * LANES]
        qm.append(jnp.where(_lane_head_mask(blk.shape, hh % 2), blk, jnp.zeros_like(blk)))
    qm = jnp.concatenate(qm, axis=0)
    wv = wi_ref[...]
    row_pos = qb * tq + lax.broadcasted_iota(I32, (tq, tk), 0)
    col_in = lax.broadcasted_iota(I32, (tq, tk), 1)

    def score_tile(kt, _):
        k0 = pl.multiple_of(kt * tk, tk)
        s = _dot_nt(qm, ki_ref[pl.ds(k0, tk), :])
        acc = wv[:, 0:1] * jnp.maximum(s[0:tq], 0.0)
        for hh in range(1, IDX_HEADS):
            acc = acc + wv[:, hh:hh + 1] * jnp.maximum(s[hh * tq:(hh + 1) * tq], 0.0)
        acc = jnp.where(col_in + k0 <= row_pos, acc, -jnp.inf)
        s_ref[:, pl.ds(k0, tk)] = acc
        return 0

    lax.fori_loop(0, nkt, score_tile, 0)

    lo_ref[...] = jnp.full(lo_ref.shape, KEY_NEG_INF, I32)
    hi_ref[...] = jnp.full(hi_ref.shape, KEY_POS_INF, I32)
    clo_ref[...] = jnp.full(clo_ref.shape, 1, I32) * (ncol * LANES)
    chi_ref[...] = jnp.zeros(chi_ref.shape, I32)

    def count_rows(r0, pred):
        def body(j, acc):
            c0 = pl.multiple_of(j * LANES, LANES)
            return acc + pred(s_ref[r0:r0 + rb, pl.ds(c0, LANES)], j).astype(I32)
        acc = lax.fori_loop(0, ncol, body, jnp.zeros((rb, LANES), I32))
        return jnp.broadcast_to(jnp.sum(acc, axis=1, keepdims=True), (rb, LANES))

    def bisect_step(carry):
        it, _ = carry
        pending = jnp.zeros((), I32)
        for r0 in range(0, tq, rb):
            lo, hi = lo_ref[r0:r0 + rb, :], hi_ref[r0:r0 + rb, :]
            mid = (lo & hi) + ((lo ^ hi) >> 1)
            active = mid != lo
            cand = _key_to_f32(mid)
            cnt = count_rows(r0, lambda blk, j: blk >= cand)
            ge = cnt >= topk
            up = active & ge
            dn = active & jnp.logical_not(ge)
            hit = active & (cnt == topk)
            lo_ref[r0:r0 + rb, :] = jnp.where(up, mid, lo)
            hi_ref[r0:r0 + rb, :] = jnp.where(hit, mid + 1, jnp.where(dn, mid, hi))
            clo_ref[r0:r0 + rb, :] = jnp.where(up, cnt, clo_ref[r0:r0 + rb, :])
            chi_ref[r0:r0 + rb, :] = jnp.where(dn, cnt, chi_ref[r0:r0 + rb, :])
            pending = jnp.maximum(pending, jnp.max(active.astype(I32)))
        return it + 1, pending

    lax.while_loop(lambda c: (c[0] < 40) & (c[1] > 0), bisect_step,
                   (jnp.zeros((), I32), jnp.ones((), I32)))

    tie = (clo_ref[...] > topk) & (lo_ref[...] > KEY_NEG_INF)
    any_tie = jnp.max(tie.astype(I32))

    @pl.when(any_tie == 0)
    def _():
        hi_ref[...] = jnp.full(hi_ref.shape, n_all, I32)

    @pl.when(any_tie > 0)
    def _():
        need = topk - chi_ref[...]
        chi_ref[...] = need
        hi_ref[...] = jnp.full(hi_ref.shape, ncol * LANES - 1, I32)
        clo_ref[...] = jnp.full(clo_ref.shape, -1, I32)
        lane = lax.broadcasted_iota(I32, (rb, LANES), 1)

        def tie_step(_, c):
            for r0 in range(0, tq, rb):
                jl, jh = clo_ref[r0:r0 + rb, :], hi_ref[r0:r0 + rb, :]
                mid = jl + ((jh - jl) >> 1)
                active = (jh - jl) > 1
                thr = _key_to_f32(lo_ref[r0:r0 + rb, :])
                cnt = count_rows(r0, lambda blk, j: (blk == thr) & (lane + j * LANES <= mid))
                ok = cnt >= chi_ref[r0:r0 + rb, :]
                hi_ref[r0:r0 + rb, :] = jnp.where(active & ok, mid, jh)
                clo_ref[r0:r0 + rb, :] = jnp.where(active & jnp.logical_not(ok), mid, jl)
            return c

        lax.fori_loop(0, int(math.ceil(math.log2(n_all))) + 1, tie_step, 0)
        keep_all = jnp.logical_not(tie)
        hi_ref[...] = jnp.where(keep_all, n_all, hi_ref[...])

    lane_k = lax.broadcasted_iota(I32, (tq, tk), 1)
    thr_col = _key_to_f32(lo_ref[:, 0:1])
    last_col = hi_ref[:, 0:1]

    def mask_tile(kt, _):
        k0 = pl.multiple_of(kt * tk, tk)
        s = s_ref[:, pl.ds(k0, tk)]
        sel = ((s > thr_col) | ((s == thr_col) & (lane_k + k0 <= last_col))) & (s > -jnp.inf)
        mask_ref[:, pl.ds(k0, tk)] = jnp.where(sel, 1, 0).astype(mask_ref.dtype)
        return 0

    lax.fori_loop(0, nkt, mask_tile, 0)

    def zero_tile(kt, _):
        k0 = pl.multiple_of(kt * tk, tk)
        mask_ref[:, pl.ds(k0, tk)] = jnp.zeros((tq, tk), mask_ref.dtype)
        return 0

    lax.fori_loop(nkt, n_all // tk, zero_tile, 0)


def _dsa_select(qi, wi, ki, topk):
    n = qi.shape[0]
    tq = min(SEL_TQ, n)
    tk = min(SEL_TK, n)
    rb = min(SEL_RB, tq)
    return pl.pallas_call(
        functools.partial(_select_kernel, tq=tq, tk=tk, topk=topk, rb=rb),
        out_shape=jax.ShapeDtypeStruct((n, n), jnp.int8),
        grid=(n // tq,),
        in_specs=[pl.BlockSpec((tq, qi.shape[1]), lambda i: (i, 0)),
                  pl.BlockSpec((tq, LANES), lambda i: (i, 0)),
                  pl.BlockSpec((n, LANES), lambda i: (0, 0))],
        out_specs=pl.BlockSpec((tq, n), lambda i: (i, 0)),
        scratch_shapes=[pltpu.VMEM((tq, n), F32)] + [pltpu.VMEM((tq, LANES), I32)] * 4,
        compiler_params=_cparams(("parallel",)),
        name="dsa_select",
    )(qi, wi, ki)


def _attn_kernel(qb_tab, kt_tab, q_ref, k_ref, v_ref, mask_ref, o_ref, acc_ref, m_ref, l_ref,
                 *, tq, tk):
    step = pl.program_id(0)
    kt = kt_tab[step]
    d = q_ref.shape[1]
    slab = MXU_DIM
    heads_per_slab = slab // HEAD_DIM

    @pl.when(kt == 0)
    def _():
        acc_ref[...] = jnp.zeros_like(acc_ref)
        m_ref[...] = jnp.full(m_ref.shape, NEG_BIG, F32)
        l_ref[...] = jnp.zeros_like(l_ref)

    bias = jnp.where(mask_ref[...].astype(I32) != 0, 0.0, NEG_BIG)
    lane_s = lax.broadcasted_iota(I32, (tq, slab), 1) // HEAD_DIM

    for sb in range(d // slab):
        vs = v_ref[:, sb * slab:(sb + 1) * slab]
        lane_v = lax.broadcasted_iota(I32, vs.shape, 1) // HEAD_DIM
        alpha_l = jnp.zeros((tq, slab), F32)
        pv = jnp.zeros((tq, slab), F32)
        for i in range(heads_per_slab):
            hd = sb * heads_per_slab + i
            blk = hd // 2
            qblk = q_ref[:, blk * LANES:(blk + 1) * LANES]
            qm = jnp.where(_lane_head_mask(qblk.shape, hd % 2), qblk, jnp.zeros_like(qblk))
            s = _dot_nt(qm, k_ref[:, blk * LANES:(blk + 1) * LANES]) + bias
            m_old = m_ref[:, hd:hd + 1]
            m_new = jnp.maximum(m_old, jnp.max(s, axis=1, keepdims=True))
            alpha = jnp.exp(m_old - m_new)
            p = jnp.exp(s - m_new)
            l_ref[:, hd:hd + 1] = l_ref[:, hd:hd + 1] * alpha + jnp.sum(p, axis=1, keepdims=True)
            m_ref[:, hd:hd + 1] = m_new
            vm = jnp.where(lane_v == i, vs, jnp.zeros_like(vs))
            pv = pv + _dot(p.astype(MXU_DTYPE), vm)
            alpha_l = jnp.where(lane_s == i, alpha, alpha_l)
        acc_ref[:, sb * slab:(sb + 1) * slab] = acc_ref[:, sb * slab:(sb + 1) * slab] * alpha_l + pv

    last = kt == ((qb_tab[step] + 1) * tq - 1) // tk

    @pl.when(last)
    def _():
        lane_h = lax.broadcasted_iota(I32, (tq, d), 1) // HEAD_DIM
        l_l = jnp.zeros((tq, d), F32)
        for hd in range(N_HEADS):
            l_l = jnp.where(lane_h == hd, l_ref[:, hd:hd + 1], l_l)
        o_ref[...] = (acc_ref[...] / l_l).astype(o_ref.dtype)


def _dsa_attend(q, k, v, mask):
    n, d = q.shape
    tq = min(ATT_TQ, n)
    tk = min(ATT_TK, n)
    pairs = [(qb, kt) for qb in range(n // tq) for kt in range(((qb + 1) * tq - 1) // tk + 1)]
    qb_tab = jnp.asarray([p[0] for p in pairs], I32)
    kt_tab = jnp.asarray([p[1] for p in pairs], I32)
    grid_spec = pltpu.PrefetchScalarGridSpec(
        num_scalar_prefetch=2,
        grid=(len(pairs),),
        in_specs=[
            pl.BlockSpec((tq, d), lambda i, qt, kt: (qt[i], 0)),
            pl.BlockSpec((tk, d), lambda i, qt, kt: (kt[i], 0)),
            pl.BlockSpec((tk, d), lambda i, qt, kt: (kt[i], 0)),
            pl.BlockSpec((tq, tk), lambda i, qt, kt: (qt[i], kt[i])),
        ],
        out_specs=pl.BlockSpec((tq, d), lambda i, qt, kt: (qt[i], 0)),
        scratch_shapes=[pltpu.VMEM((tq, d), F32), pltpu.VMEM((tq, LANES), F32),
                        pltpu.VMEM((tq, LANES), F32)],
    )
    return pl.pallas_call(
        functools.partial(_attn_kernel, tq=tq, tk=tk),
        out_shape=jax.ShapeDtypeStruct((n, d), MXU_DTYPE),
        grid_spec=grid_spec,
        compiler_params=_cparams(("arbitrary",)),
        name="dsa_attend",
    )(qb_tab, kt_tab, q, k, v, mask)


def _out_kernel(x_ref, a_ref, w_ref, o_ref):
    o_ref[...] = x_ref[...] + _dot(a_ref[...], w_ref[...])


def _out_residual(x, a, w):
    n, d = x.shape
    tm = min(ROW_TM, n)
    return pl.pallas_call(
        _out_kernel,
        out_shape=jax.ShapeDtypeStruct((n, d), F32),
        grid=(n // tm,),
        in_specs=[pl.BlockSpec((tm, d), lambda i: (i, 0)),
                  pl.BlockSpec((tm, d), lambda i: (i, 0)),
                  pl.BlockSpec((d, d), lambda i: (0, 0))],
        out_specs=pl.BlockSpec((tm, d), lambda i: (i, 0)),
        compiler_params=_cparams(("parallel",)),
        name="attn_out",
    )(x, a, w)


def _rope_lane_tables(length):
    inv_freq = ROPE_THETA ** (-jnp.arange(0, HEAD_DIM, 2, dtype=F32) / HEAD_DIM)
    ang = jnp.arange(length, dtype=F32)[:, None] * inv_freq[None, :]
    lane = np.arange(LANES)
    cos_t = jnp.cos(ang)[:, lane % 32]
    sin_t = jnp.sin(ang)[:, lane % 32] * jnp.asarray(np.where(lane < 64, -1.0, 1.0), F32)
    return cos_t, sin_t


def kernel(x, s5_lambda_re, s5_lambda_im, s5_log_dt, s5_b_re, s5_b_im, s5_c_re, s5_c_im, s5_d, s5_w_glu, dsa_w_in, dsa_q_norm, dsa_k_norm, dsa_w_o, ffn_w_gate_up, ffn_w_down, norm_mix, norm_ffn):
    bsz, length, d = x.shape
    depth = norm_mix.shape[0]
    topk = min(TOPK_MAX, length // 4)
    nchunk = length // S5_CHUNK
    cos_t, sin_t = _rope_lane_tables(length)
    outs = []
    for b in range(bsz):
        xs = x[b].astype(F32)
        for i in range(depth):
            j = i // 2
            wgu = ffn_w_gate_up[i].astype(MXU_DTYPE)
            wd = ffn_w_down[i].astype(MXU_DTYPE)
            if i % 2 == 0:
                tables = _s5_tables(s5_lambda_re[j], s5_lambda_im[j], s5_log_dt[j], s5_b_re[j],
                                    s5_b_im[j], s5_c_re[j], s5_c_im[j])
                xt = xs.reshape(nchunk, S5_CHUNK, d).transpose(1, 0, 2).reshape(length, d)
                h_t = _norm(xt, norm_mix[i]).reshape(S5_CHUNK, nchunk, d)
                g_t = _s5_scan(h_t, tables, s5_d[j]).reshape(length, d)
                xt = _glu_residual(xt, g_t, s5_w_glu[j].astype(MXU_DTYPE))
                xt = _ffn(xt, norm_ffn[i], wgu, wd)
                xs = xt.reshape(S5_CHUNK, nchunk, d).transpose(1, 0, 2).reshape(length, d)
            else:
                q, k, v, qi, ki, wi = _dsa_project(xs, norm_mix[i], dsa_w_in[j], dsa_q_norm[j],
                                                   dsa_k_norm[j], cos_t, sin_t)
                mask = _dsa_select(qi, wi, ki, topk)
                att = _dsa_attend(q, k, v, mask)
                xs = _out_residual(xs, att, dsa_w_o[j].astype(MXU_DTYPE))
                xs = _ffn(xs, norm_ffn[i], wgu, wd)
        outs.append(xs)
    return jnp.stack(outs, axis=0).astype(x.dtype)
```

```python
import functools
import math

import jax
import jax.numpy as jnp
import numpy as np
from jax import lax
from jax.experimental import pallas as pl
from jax.experimental.pallas import tpu as pltpu

F32 = jnp.float32
BF16 = jnp.bfloat16
I32 = jnp.int32
MXU_DTYPE = BF16

D_MODEL = 1024
S5_GROUP = 16
S5_STATE = 64
N_HEADS = 16
HEAD_DIM = 64
IDX_HEADS = 8
IDX_DIM = 64
TOPK_MAX = 256
ROPE_THETA = 10000.0
EPS = 1e-6

LANES = 128
SUBLANES = 8
MXU_DIM = 256
VMEM_LIMIT = 56 * 1024 * 1024

S5_CHUNK = 16
S5_SLAB_GROUPS = LANES // S5_GROUP
NEG_BIG = -1e30

ROW_TM = 512
NORM_TM = 1024
S5_TC = 512
PROJ_TM = 512
SEL_TQ = 256
SEL_TK = 512
SEL_RB = 64
SEL_UNROLL = 8
ATT_TQ = 512
ATT_TK = 512


def _cparams(sem):
    return pltpu.CompilerParams(dimension_semantics=sem, vmem_limit_bytes=VMEM_LIMIT)


def _rms(x, gain=None):
    y = x * lax.rsqrt(jnp.mean(x * x, axis=-1, keepdims=True) + EPS)
    return y if gain is None else y * gain


def _dot(a, b):
    return jnp.dot(a, b, preferred_element_type=F32)


def _dot_nt(a, b):
    return lax.dot_general(a, b, (((1,), (1,)), ((), ())), preferred_element_type=F32)


def _ffn_kernel(x_ref, g_ref, wgu_ref, wd_ref, o_ref, acc_ref, *, d_ff, fc):
    x = x_ref[...]
    h = _rms(x, g_ref[...]).astype(MXU_DTYPE)
    for c in range(d_ff // fc):
        g = _dot(h, wgu_ref[:, c * fc:(c + 1) * fc])
        u = _dot(h, wgu_ref[:, d_ff + c * fc:d_ff + (c + 1) * fc])
        a = (g * jax.nn.sigmoid(g) * u).astype(MXU_DTYPE)
        d = _dot(a, wd_ref[c * fc:(c + 1) * fc, :])
        if c == 0:
            acc_ref[...] = d
        else:
            acc_ref[...] += d
    o_ref[...] = x + acc_ref[...]


def _ffn(x, gain, wgu, wd):
    n, d = x.shape
    d_ff = wd.shape[0]
    tm = min(ROW_TM, n)
    fc = MXU_DIM
    return pl.pallas_call(
        functools.partial(_ffn_kernel, d_ff=d_ff, fc=fc),
        out_shape=jax.ShapeDtypeStruct((n, d), F32),
        grid=(n // tm,),
        in_specs=[
            pl.BlockSpec((tm, d), lambda i: (i, 0)),
            pl.BlockSpec((1, d), lambda i: (0, 0)),
            pl.BlockSpec((d, 2 * d_ff), lambda i: (0, 0)),
            pl.BlockSpec((d_ff, d), lambda i: (0, 0)),
        ],
        out_specs=pl.BlockSpec((tm, d), lambda i: (i, 0)),
        scratch_shapes=[pltpu.VMEM((tm, d), F32)],
        compiler_params=_cparams(("parallel",)),
        name="ffn",
    )(x, gain.reshape(1, d), wgu, wd)


def _norm_kernel(x_ref, g_ref, o_ref):
    o_ref[...] = _rms(x_ref[...], g_ref[...]).astype(o_ref.dtype)


def _norm(x, gain):
    n, d = x.shape
    tm = min(NORM_TM, n)
    return pl.pallas_call(
        _norm_kernel,
        out_shape=jax.ShapeDtypeStruct((n, d), MXU_DTYPE),
        grid=(n // tm,),
        in_specs=[pl.BlockSpec((tm, d), lambda i: (i, 0)),
                  pl.BlockSpec((1, d), lambda i: (0, 0))],
        out_specs=pl.BlockSpec((tm, d), lambda i: (i, 0)),
        compiler_params=_cparams(("parallel",)),
        name="norm",
    )(x, gain.reshape(1, d))


def _s5_tables(lam_re, lam_im, log_dt, b_re, b_im, c_re, c_im):
    hp = lax.Precision.HIGHEST
    g, p = lam_re.shape
    h = S5_GROUP
    nsl = g // S5_SLAB_GROUPS
    sg = S5_SLAB_GROUPS
    t = S5_CHUNK
    lam_re, lam_im, log_dt = lam_re.astype(F32), lam_im.astype(F32), log_dt.astype(F32)
    b_re, b_im, c_re, c_im = (a.astype(F32) for a in (b_re, b_im, c_re, c_im))
    dt = jnp.exp(log_dt)[:, None]

    def apow(k):
        k = jnp.asarray(k, F32).reshape((-1, 1, 1))
        mag = jnp.exp(lam_re[None] * dt[None] * k)
        ang = lam_im[None] * dt[None] * k
        return mag * jnp.cos(ang), mag * jnp.sin(ang)

    ar, ai = apow([1.0])
    ar, ai = ar[0], ai[0]
    den = lam_re * lam_re + lam_im * lam_im
    nr, ni = ar - 1.0, ai
    qr = (nr * lam_re + ni * lam_im) / den
    qi = (ni * lam_re - nr * lam_im) / den
    bbr = qr[..., None] * b_re - qi[..., None] * b_im
    bbi = qr[..., None] * b_im + qi[..., None] * b_re

    pr, pi = apow(np.arange(t + 1))
    mr = c_re[None] * pr[:, :, None, :] - c_im[None] * pi[:, :, None, :]
    mi = c_re[None] * pi[:, :, None, :] + c_im[None] * pr[:, :, None, :]

    kk = (jnp.einsum('tghp,gpk->tghk', mr[:t], bbr, precision=hp)
          - jnp.einsum('tghp,gpk->tghk', mi[:t], bbi, precision=hp))
    kp = jnp.concatenate([jnp.zeros_like(kk[:1]), kk], axis=0)
    eye = jnp.eye(sg, dtype=F32)

    dl = np.arange(t // 2)[:, None, None]
    sl = np.arange(2)[None, :, None]
    jl = np.arange(2)[None, None, :]
    idx = 2 * dl + jl - sl + 1
    kg = kp[idx]
    kg = kg.reshape(t // 2, 2, 2, nsl, sg, h, h)
    tp = jnp.einsum('dljbgok,gm->bdlgkjmo', kg, eye)
    tp = tp.reshape(nsl, t // 2, 2 * sg * h, 2 * sg * h)

    prs, pis = pr[t - 1::-1][:t], pi[t - 1::-1][:t]
    er = prs[..., None] * bbr[None] - pis[..., None] * bbi[None]
    ei = prs[..., None] * bbi[None] + pis[..., None] * bbr[None]
    bf = jnp.stack([er, ei], axis=2)
    bf = bf.reshape(t // 2, 2, nsl, sg, 2, p, h)
    bz = jnp.einsum('zlbgrpk,gm->bzlgkrmp', bf, eye)
    bz = bz.reshape(nsl, t // 2, 2 * sg * h, 2 * sg * p)

    cf = jnp.stack([mr[1:], -mi[1:]], axis=2)
    cf = cf.reshape(t // 2, 2, nsl, sg, 2, h, p)
    cz = jnp.einsum('ijbgrop,gm->birgpjmo', cf, eye)
    cz = cz.reshape(nsl, t // 2, 2 * sg * p, 2 * sg * h)

    def slab_state(re, im):
        k = re.shape[0]
        x = jnp.stack([re, im], axis=1).reshape(k, 2, nsl, sg * p)
        return x.transpose(2, 0, 1, 3).reshape(nsl, k, 2 * sg * p)

    ad = slab_state(*apow([t * 1.0, t * 2.0, t * 4.0]))
    ap8 = slab_state(*apow(t * (np.arange(SUBLANES) + 1.0)))
    return (tp.astype(MXU_DTYPE), bz.astype(MXU_DTYPE), cz.astype(MXU_DTYPE), ad, ap8)


def _s5_kernel(h_ref, tp_ref, bz_ref, cz_ref, ad_ref, ap8_ref, dsk_ref, o_ref,
               carry_ref, z_ref, xp_ref, *, tc):
    half = z_ref.shape[1] // 2
    npair = S5_CHUNK // 2

    @pl.when(pl.program_id(1) == 0)
    def _():
        carry_ref[...] = jnp.zeros_like(carry_ref)

    u = [jnp.concatenate([h_ref[2 * s], h_ref[2 * s + 1]], axis=1) for s in range(npair)]

    z = _dot(u[0], bz_ref[0])
    for s in range(1, npair):
        z = z + _dot(u[s], bz_ref[s])
    z_ref[...] = z

    row = lax.broadcasted_iota(I32, (SUBLANES, half), 0)
    pr8, pi8 = ap8_ref[:, :half], ap8_ref[:, half:]

    def tile_step(t, carry):
        r0 = pl.multiple_of(t * SUBLANES, SUBLANES)
        zt = z_ref[pl.ds(r0, SUBLANES), :]
        xr, xi = zt[:, :half], zt[:, half:]
        for k, d in enumerate((1, 2, 4)):
            a = ad_ref[k:k + 1, :]
            a_r, a_i = a[:, :half], a[:, half:]
            sr = jnp.where(row >= d, pltpu.roll(xr, d, 0), 0.0)
            si = jnp.where(row >= d, pltpu.roll(xi, d, 0), 0.0)
            xr, xi = xr + a_r * sr - a_i * si, xi + a_r * si + a_i * sr
        cr, ci = carry[:, :half], carry[:, half:]
        xr, xi = xr + pr8 * cr - pi8 * ci, xi + pr8 * ci + pi8 * cr
        xpr = jnp.where(row >= 1, pltpu.roll(xr, 1, 0), cr)
        xpi = jnp.where(row >= 1, pltpu.roll(xi, 1, 0), ci)
        xp_ref[pl.ds(r0, SUBLANES), :] = jnp.concatenate([xpr, xpi], axis=1)
        return jnp.concatenate([xr[SUBLANES - 1:], xi[SUBLANES - 1:]], axis=1)

    carry_ref[...] = lax.fori_loop(0, tc // SUBLANES, tile_step, carry_ref[...])

    xp = xp_ref[...].astype(MXU_DTYPE)
    dsk = dsk_ref[...]
    for i in range(npair):
        y = _dot(xp, cz_ref[i])
        for s in range(i + 1):
            y = y + _dot(u[s], tp_ref[i - s])
        for jl in range(2):
            j = 2 * i + jl
            yj = y[:, jl * LANES:(jl + 1) * LANES] + dsk * h_ref[j].astype(F32)
            o_ref[j] = jax.nn.gelu(yj).astype(o_ref.dtype)


def _s5_scan(h_t, tables, d_skip):
    tp, bz, cz, ad, ap8 = tables
    t, c, d = h_t.shape
    nsl = d // LANES
    tc = min(S5_TC, c)
    st = bz.shape[-1]
    dsk = d_skip.astype(F32).reshape(nsl, 1, LANES)
    return pl.pallas_call(
        functools.partial(_s5_kernel, tc=tc),
        out_shape=jax.ShapeDtypeStruct((t, c, d), MXU_DTYPE),
        grid=(nsl, c // tc),
        in_specs=[
            pl.BlockSpec((t, tc, LANES), lambda b, i: (0, i, b)),
            pl.BlockSpec((None,) + tp.shape[1:], lambda b, i: (b, 0, 0, 0)),
            pl.BlockSpec((None,) + bz.shape[1:], lambda b, i: (b, 0, 0, 0)),
            pl.BlockSpec((None,) + cz.shape[1:], lambda b, i: (b, 0, 0, 0)),
            pl.BlockSpec((None,) + ad.shape[1:], lambda b, i: (b, 0, 0)),
            pl.BlockSpec((None,) + ap8.shape[1:], lambda b, i: (b, 0, 0)),
            pl.BlockSpec((None, 1, LANES), lambda b, i: (b, 0, 0)),
        ],
        out_specs=pl.BlockSpec((t, tc, LANES), lambda b, i: (0, i, b)),
        scratch_shapes=[pltpu.VMEM((1, st), F32), pltpu.VMEM((tc, st), F32),
                        pltpu.VMEM((tc, st), F32)],
        compiler_params=_cparams(("arbitrary", "arbitrary")),
        name="s5_scan",
    )(h_t, tp, bz, cz, ad, ap8, dsk)


def _glu_kernel(x_ref, g_ref, w_ref, o_ref):
    d = x_ref.shape[1]
    vg = _dot(g_ref[...], w_ref[...])
    o_ref[...] = x_ref[...] + vg[:, :d] * jax.nn.sigmoid(vg[:, d:])


def _glu_residual(x, g, w):
    n, d = x.shape
    tm = min(ROW_TM, n)
    return pl.pallas_call(
        _glu_kernel,
        out_shape=jax.ShapeDtypeStruct((n, d), F32),
        grid=(n // tm,),
        in_specs=[pl.BlockSpec((tm, d), lambda i: (i, 0)),
                  pl.BlockSpec((tm, d), lambda i: (i, 0)),
                  pl.BlockSpec((d, 2 * d), lambda i: (0, 0))],
        out_specs=pl.BlockSpec((tm, d), lambda i: (i, 0)),
        compiler_params=_cparams(("parallel",)),
        name="glu",
    )(x, g, w)


def _head_perm(n_heads):
    n = np.arange(n_heads * HEAD_DIM)
    pb, r = n // LANES, n % LANES
    half, r2 = r // 64, r % 64
    hl, dp = r2 // 32, r2 % 32
    return (2 * pb + hl) * HEAD_DIM + 32 * half + dp


def _lane_head_mask(shape, hl):
    lane = lax.broadcasted_iota(I32, shape, len(shape) - 1)
    return ((lane % 64) // 32) == hl


def _proj_kernel(x_ref, g_ref, w_ref, wvt_ref, gq_ref, gk_ref, cos_ref, sin_ref, hm_ref,
                 q_ref, k_ref, vt_ref, qi_ref, ki_ref, wi_ref, *, d, dqi, att_scale, w_scale):
    h = _rms(x_ref[...], g_ref[...]).astype(MXU_DTYPE)
    cos, sin = cos_ref[...], sin_ref[...]
    hm = hm_ref[...]

    def rope(t):
        return t * cos + pltpu.roll(t, 64, 1) * sin

    def headnorm_rope(col0, gain_ref, out_ref, scale):
        for sb in range(d // MXU_DIM):
            c0 = sb * MXU_DIM
            t = _dot(h, w_ref[:, col0 + c0:col0 + c0 + MXU_DIM])
            sq = t * t
            hi = sq.astype(MXU_DTYPE)
            lo = (sq - hi.astype(F32)).astype(MXU_DTYPE)
            ss = _dot(hi, hm) + _dot(lo, hm)
            tn = t * lax.rsqrt(ss * (1.0 / HEAD_DIM) + EPS) * gain_ref[:, c0:c0 + MXU_DIM]
            for b in range(MXU_DIM // LANES):
                r = rope(tn[:, b * LANES:(b + 1) * LANES])
                if scale != 1.0:
                    r = r * scale
                out_ref[:, c0 + b * LANES:c0 + (b + 1) * LANES] = r.astype(out_ref.dtype)

    headnorm_rope(0, gq_ref, q_ref, att_scale)
    headnorm_rope(d, gk_ref, k_ref, 1.0)
    vt_ref[...] = _dot_nt(wvt_ref[...], h).astype(vt_ref.dtype)
    c0 = 2 * d
    t = _dot(h, w_ref[:, c0:c0 + dqi])
    for b in range(dqi // LANES):
        qi_ref[:, b * LANES:(b + 1) * LANES] = rope(t[:, b * LANES:(b + 1) * LANES]).astype(qi_ref.dtype)
    c0 += dqi
    t = _dot(h, w_ref[:, c0:c0 + LANES])
    ms = jnp.sum(t * t, axis=-1, keepdims=True) * (0.5 / IDX_DIM)
    ki_ref[...] = rope(t * lax.rsqrt(ms + EPS)).astype(ki_ref.dtype)
    c0 += LANES
    wi_ref[...] = _dot(h, w_ref[:, c0:c0 + LANES]) * w_scale


def _dsa_project(x, gain, w_in, q_gain, k_gain, cos_t, sin_t):
    n, d = x.shape
    dqi = IDX_HEADS * IDX_DIM
    pq = _head_perm(N_HEADS)
    pqi = _head_perm(IDX_HEADS)
    wq = w_in[:, 0:d][:, pq]
    wk = w_in[:, d:2 * d][:, pq]
    wvt = w_in[:, 2 * d:3 * d].T.astype(MXU_DTYPE)
    wqi = w_in[:, 3 * d:3 * d + dqi][:, pqi]
    lane = np.arange(LANES)
    wki = w_in[:, 3 * d + dqi:3 * d + dqi + IDX_DIM][:, 32 * (lane // 64) + lane % 32]
    wwi = jnp.pad(w_in[:, 3 * d + dqi + IDX_DIM:], ((0, 0), (0, LANES - IDX_HEADS)))
    w_all = jnp.concatenate([wq, wk, wqi, wki, wwi], axis=1).astype(MXU_DTYPE)
    dcol = (pq % HEAD_DIM)
    gq = q_gain.astype(F32)[dcol].reshape(1, d)
    gk = k_gain.astype(F32)[dcol].reshape(1, d)
    l2 = np.arange(MXU_DIM)
    hm = ((l2[:, None] // LANES == l2[None, :] // LANES)
          & ((l2[:, None] % 64) // 32 == (l2[None, :] % 64) // 32))
    hm = jnp.asarray(hm, MXU_DTYPE)
    tm = min(PROJ_TM, n)
    nw = w_all.shape[1]
    outs = pl.pallas_call(
        functools.partial(_proj_kernel, d=d, dqi=dqi, att_scale=HEAD_DIM ** -0.5,
                          w_scale=(IDX_HEADS ** -0.5) * (IDX_DIM ** -0.5)),
        out_shape=[jax.ShapeDtypeStruct((n, d), MXU_DTYPE)] * 2
        + [jax.ShapeDtypeStruct((d, n), MXU_DTYPE),
           jax.ShapeDtypeStruct((n, dqi), MXU_DTYPE),
           jax.ShapeDtypeStruct((n, LANES), MXU_DTYPE),
           jax.ShapeDtypeStruct((n, LANES), F32)],
        grid=(n // tm,),
        in_specs=[
            pl.BlockSpec((tm, d), lambda i: (i, 0)),
            pl.BlockSpec((1, d), lambda i: (0, 0)),
            pl.BlockSpec((d, nw), lambda i: (0, 0)),
            pl.BlockSpec((d, d), lambda i: (0, 0)),
            pl.BlockSpec((1, d), lambda i: (0, 0)),
            pl.BlockSpec((1, d), lambda i: (0, 0)),
            pl.BlockSpec((tm, LANES), lambda i: (i, 0)),
            pl.BlockSpec((tm, LANES), lambda i: (i, 0)),
            pl.BlockSpec((MXU_DIM, MXU_DIM), lambda i: (0, 0)),
        ],
        out_specs=[pl.BlockSpec((tm, d), lambda i: (i, 0))] * 2
        + [pl.BlockSpec((d, tm), lambda i: (0, i)),
           pl.BlockSpec((tm, dqi), lambda i: (i, 0)),
           pl.BlockSpec((tm, LANES), lambda i: (i, 0)),
           pl.BlockSpec((tm, LANES), lambda i: (i, 0))],
        compiler_params=_cparams(("parallel",)),
        name="dsa_proj",
    )(x, gain.reshape(1, d), w_all, wvt, gq, gk, cos_t, sin_t, hm)
    return outs


KEY_NEG_INF = -2139095041
KEY_POS_INF = 2139095040


def _key_to_f32(key):
    bits = key ^ ((key >> 31) & 0x7FFFFFFF)
    return lax.bitcast_convert_type(bits, F32)


def _select_kernel(qi_ref, wi_ref, ki_ref, mask_ref, s_ref, lo_ref, hi_ref, clo_ref, chi_ref,
                   *, tq, tk, topk, rb):
    qb = pl.program_id(0)
    n_all = mask_ref.shape[1]
    nkt = (qb * tq) // tk + 1
    ncol = nkt * (tk // LANES)

    qm = []
    for hh in range(IDX_HEADS):
        blk = qi_ref[:, (hh // 2) * LANES:(hh // 2 + 1) * LANES]
        qm.append(jnp.where(_lane_head_mask(blk.shape, hh % 2), blk, jnp.zeros_like(blk)))
    qm = jnp.concatenate(qm, axis=0)
    wv = wi_ref[...]
    row_pos = qb * tq + lax.broadcasted_iota(I32, (tq, tk), 0)
    col_in = lax.broadcasted_iota(I32, (tq, tk), 1)

    def score_tile(kt, _):
        k0 = pl.multiple_of(kt * tk, tk)
        s = _dot_nt(qm, ki_ref[pl.ds(k0, tk), :])
        acc = wv[:, 0:1] * jnp.maximum(s[0:tq], 0.0)
        for hh in range(1, IDX_HEADS):
            acc = acc + wv[:, hh:hh + 1] * jnp.maximum(s[hh * tq:(hh + 1) * tq], 0.0)
        acc = jnp.where(col_in + k0 <= row_pos, acc, -jnp.inf)
        s_ref[:, pl.ds(k0, tk)] = acc
        return 0

    lax.fori_loop(0, nkt, score_tile, 0)

    lo_ref[...] = jnp.full(lo_ref.shape, KEY_NEG_INF, I32)
    hi_ref[...] = jnp.full(hi_ref.shape, KEY_POS_INF, I32)
    clo_ref[...] = jnp.full(clo_ref.shape, 1, I32) * (ncol * LANES)
    chi_ref[...] = jnp.zeros(chi_ref.shape, I32)

    def count_rows(r0, pred):
        def body(j, acc):
            c0 = pl.multiple_of(j * LANES, LANES)
            return acc + pred(s_ref[r0:r0 + rb, pl.ds(c0, LANES)], j).astype(I32)
        acc = lax.fori_loop(0, ncol, body, jnp.zeros((rb, LANES), I32))
        return jnp.broadcast_to(jnp.sum(acc, axis=1, keepdims=True), (rb, LANES))

    def bisect_step(carry):
        it, _ = carry
        pending = jnp.zeros((), I32)
        for r0 in range(0, tq, rb):
            lo, hi = lo_ref[r0:r0 + rb, :], hi_ref[r0:r0 + rb, :]
            mid = (lo & hi) + ((lo ^ hi) >> 1)
            active = mid != lo
            cand = _key_to_f32(mid)
            cnt = count_rows(r0, lambda blk, j: blk >= cand)
            ge = cnt >= topk
            up = active & ge
            dn = active & jnp.logical_not(ge)
            hit = active & (cnt == topk)
            lo_ref[r0:r0 + rb, :] = jnp.where(up, mid, lo)
            hi_ref[r0:r0 + rb, :] = jnp.where(hit, mid + 1, jnp.where(dn, mid, hi))
            clo_ref[r0:r0 + rb, :] = jnp.where(up, cnt, clo_ref[r0:r0 + rb, :])
            chi_ref[r0:r0 + rb, :] = jnp.where(dn, cnt, chi_ref[r0:r0 + rb, :])
            pending = jnp.maximum(pending, jnp.max(active.astype(I32)))
        return it + 1, pending

    lax.while_loop(lambda c: (c[0] < 40) & (c[1] > 0), bisect_step,
                   (jnp.zeros((), I32), jnp.ones((), I32)))

    tie = (clo_ref[...] > topk) & (lo_ref[...] > KEY_NEG_INF)
    any_tie = jnp.max(tie.astype(I32))

    @pl.when(any_tie == 0)
    def _():
        hi_ref[...] = jnp.full(hi_ref.shape, n_all, I32)

    @pl.when(any_tie > 0)
    def _():
        need = topk - chi_ref[...]
        chi_ref[...] = need
        hi_ref[...] = jnp.full(hi_ref.shape, ncol * LANES - 1, I32)
        clo_ref[...] = jnp.full(clo_ref.shape, -1, I32)
        lane = lax.broadcasted_iota(I32, (rb, LANES), 1)

        def tie_step(_, c):
            for r0 in range(0, tq, rb):
                jl, jh = clo_ref[r0:r0 + rb, :], hi_ref[r0:r0 + rb, :]
                mid = jl + ((jh - jl) >> 1)
                active = (jh - jl) > 1
                thr = _key_to_f32(lo_ref[r0:r0 + rb, :])
                cnt = count_rows(r0, lambda blk, j: (blk == thr) & (lane + j * LANES <= mid))
                ok = cnt >= chi_ref[r0:r0 + rb, :]
                hi_ref[r0:r0 + rb, :] = jnp.where(active & ok, mid, jh)
                clo_ref[r0:r0 + rb, :] = jnp.where(active & jnp.logical_not(ok), mid, jl)
            return c

        lax.fori_loop(0, int(math.ceil(math.log2(n_all))) + 1, tie_step, 0)
        keep_all = jnp.logical_not(tie)
        hi_ref[...] = jnp.where(keep_all, n_all, hi_ref[...])

    lane_k = lax.broadcasted_iota(I32, (tq, tk), 1)
    thr_col = _key_to_f32(lo_ref[:, 0:1])
    last_col = hi_ref[:, 0:1]

    def mask_tile(kt, _):
        k0 = pl.multiple_of(kt * tk, tk)
        s = s_ref[:, pl.ds(k0, tk)]
        sel = ((s > thr_col) | ((s == thr_col) & (lane_k + k0 <= last_col))) & (s > -jnp.inf)
        mask_ref[:, pl.ds(k0, tk)] = jnp.where(sel, 1, 0).astype(mask_ref.dtype)
        return 0

    lax.fori_loop(0, nkt, mask_tile, 0)

    def zero_tile(kt, _):
        k0 = pl.multiple_of(kt * tk, tk)
        mask_ref[:, pl.ds(k0, tk)] = jnp.zeros((tq, tk), mask_ref.dtype)
        return 0

    lax.fori_loop(nkt, n_all // tk, zero_tile, 0)


def _dsa_select(qi, wi, ki, topk):
    n = qi.shape[0]
    tq = min(SEL_TQ, n)
    tk = min(SEL_TK, n)
    rb = min(SEL_RB, tq)
    return pl.pallas_call(
        functools.partial(_select_kernel, tq=tq, tk=tk, topk=topk, rb=rb),
        out_shape=jax.ShapeDtypeStruct((n, n), jnp.int8),
        grid=(n // tq,),
        in_specs=[pl.BlockSpec((tq, qi.shape[1]), lambda i: (i, 0)),
                  pl.BlockSpec((tq, LANES), lambda i: (i, 0)),
                  pl.BlockSpec((n, LANES), lambda i: (0, 0))],
        out_specs=pl.BlockSpec((tq, n), lambda i: (i, 0)),
        scratch_shapes=[pltpu.VMEM((tq, n), F32)] + [pltpu.VMEM((tq, LANES), I32)] * 4,
        compiler_params=_cparams(("parallel",)),
        name="dsa_select",
    )(qi, wi, ki)


def _attn_kernel(qb_tab, kt_tab, q_ref, k_ref, v_ref, mask_ref, o_ref, acc_ref, m_ref, l_ref,
                 *, tq, tk):
    step = pl.program_id(0)
    kt = kt_tab[step]
    d = q_ref.shape[1]
    slab = MXU_DIM
    heads_per_slab = slab // HEAD_DIM

    @pl.when(kt == 0)
    def _():
        acc_ref[...] = jnp.zeros_like(acc_ref)
        m_ref[...] = jnp.full(m_ref.shape, NEG_BIG, F32)
        l_ref[...] = jnp.zeros_like(l_ref)

    bias = jnp.where(mask_ref[...].astype(I32) != 0, 0.0, NEG_BIG)
    lane_s = lax.broadcasted_iota(I32, (tq, slab), 1) // HEAD_DIM

    for sb in range(d // slab):
        vs = v_ref[:, sb * slab:(sb + 1) * slab]
        lane_v = lax.broadcasted_iota(I32, vs.shape, 1) // HEAD_DIM
        alpha_l = jnp.zeros((tq, slab), F32)
        pv = jnp.zeros((tq, slab), F32)
        for i in range(heads_per_slab):
            hd = sb * heads_per_slab + i
            blk = hd // 2
            qblk = q_ref[:, blk * LANES:(blk + 1) * LANES]
            qm = jnp.where(_lane_head_mask(qblk.shape, hd % 2), qblk, jnp.zeros_like(qblk))
            s = _dot_nt(qm, k_ref[:, blk * LANES:(blk + 1) * LANES]) + bias
            m_old = m_ref[:, hd:hd + 1]
            m_new = jnp.maximum(m_old, jnp.max(s, axis=1, keepdims=True))
            alpha = jnp.exp(m_old - m_new)
            p = jnp.exp(s - m_new)
            l_ref[:, hd:hd + 1] = l_ref[:, hd:hd + 1] * alpha + jnp.sum(p, axis=1, keepdims=True)
            m_ref[:, hd:hd + 1] = m_new
            vm = jnp.where(lane_v == i, vs, jnp.zeros_like(vs))
            pv = pv + _dot(p.astype(MXU_DTYPE), vm)
            alpha_l = jnp.where(lane_s == i, alpha, alpha_l)
        acc_ref[:, sb * slab:(sb + 1) * slab] = acc_ref[:, sb * slab:(sb + 1) * slab] * alpha_l + pv

    last = kt == ((qb_tab[step] + 1) * tq - 1) // tk

    @pl.when(last)
    def _():
        lane_h = lax.broadcasted_iota(I32, (tq, d), 1) // HEAD_DIM
        l_l = jnp.zeros((tq, d), F32)
        for hd in range(N_HEADS):
            l_l = jnp.where(lane_h == hd, l_ref[:, hd:hd + 1], l_l)
        o_ref[...] = (acc_ref[...] / l_l).astype(o_ref.dtype)


def _dsa_attend(q, k, v, mask):
    n, d = q.shape
    tq = min(ATT_TQ, n)
    tk = min(ATT_TK, n)
    pairs = [(qb, kt) for qb in range(n // tq) for kt in range(((qb + 1) * tq - 1) // tk + 1)]
    qb_tab = jnp.asarray([p[0] for p in pairs], I32)
    kt_tab = jnp.asarray([p[1] for p in pairs], I32)
    grid_spec = pltpu.PrefetchScalarGridSpec(
        num_scalar_prefetch=2,
        grid=(len(pairs),),
        in_specs=[
            pl.BlockSpec((tq, d), lambda i, qt, kt: (qt[i], 0)),
            pl.BlockSpec((tk, d), lambda i, qt, kt: (kt[i], 0)),
            pl.BlockSpec((tk, d), lambda i, qt, kt: (kt[i], 0)),
            pl.BlockSpec((tq, tk), lambda i, qt, kt: (qt[i], kt[i])),
        ],
        out_specs=pl.BlockSpec((tq, d), lambda i, qt, kt: (qt[i], 0)),
        scratch_shapes=[pltpu.VMEM((tq, d), F32), pltpu.VMEM((tq, LANES), F32),
                        pltpu.VMEM((tq, LANES), F32)],
    )
    return pl.pallas_call(
        functools.partial(_attn_kernel, tq=tq, tk=tk),
        out_shape=jax.ShapeDtypeStruct((n, d), MXU_DTYPE),
        grid_spec=grid_spec,
        compiler_params=_cparams(("arbitrary",)),
        name="dsa_attend",
    )(qb_tab, kt_tab, q, k, v, mask)


def _out_kernel(x_ref, a_ref, w_ref, o_ref):
    o_ref[...] = x_ref[...] + _dot(a_ref[...], w_ref[...])


def _out_residual(x, a, w):
    n, d = x.shape
    tm = min(ROW_TM, n)
    return pl.pallas_call(
        _out_kernel,
        out_shape=jax.ShapeDtypeStruct((n, d), F32),
        grid=(n // tm,),
        in_specs=[pl.BlockSpec((tm, d), lambda i: (i, 0)),
                  pl.BlockSpec((tm, d), lambda i: (i, 0)),
                  pl.BlockSpec((d, d), lambda i: (0, 0))],
        out_specs=pl.BlockSpec((tm, d), lambda i: (i, 0)),
        compiler_params=_cparams(("parallel",)),
        name="attn_out",
    )(x, a, w)


def _select_t_kernel(qi_ref, wit_ref, ki_ref, mask_ref, s_ref, lo_ref, hi_ref, clo_ref, chi_ref,
                     *, tq, tk, topk, unroll):
    qb = pl.program_id(0)
    n_all = mask_ref.shape[0]
    nkt = (qb * tq + tq - 1) // tk + 1
    nrow = nkt * tk
    rows_it = SUBLANES * unroll

    qm = []
    for hh in range(IDX_HEADS):
        blk = qi_ref[:, (hh // 2) * LANES:(hh // 2 + 1) * LANES]
        qm.append(jnp.where(_lane_head_mask(blk.shape, hh % 2), blk, jnp.zeros_like(blk)))
    qm = jnp.concatenate(qm, axis=0)
    wt = wit_ref[...]
    key_in = lax.broadcasted_iota(I32, (tk, tq), 0)
    q_pos = qb * tq + lax.broadcasted_iota(I32, (tk, tq), 1)

    def score_tile(kt, _):
        k0 = pl.multiple_of(kt * tk, tk)
        s = _dot_nt(ki_ref[pl.ds(k0, tk), :], qm)
        acc = wt[0:1, :] * jnp.maximum(s[:, 0:tq], 0.0)
        for hh in range(1, IDX_HEADS):
            acc = acc + wt[hh:hh + 1, :] * jnp.maximum(s[:, hh * tq:(hh + 1) * tq], 0.0)
        s_ref[pl.ds(k0, tk), :] = jnp.where(key_in + k0 <= q_pos, acc, -jnp.inf)
        return 0

    lax.fori_loop(0, nkt, score_tile, 0)

    lo_ref[...] = jnp.full(lo_ref.shape, KEY_NEG_INF, I32)
    hi_ref[...] = jnp.full(hi_ref.shape, KEY_POS_INF, I32)
    clo_ref[...] = jnp.full(clo_ref.shape, 1, I32) * nrow
    chi_ref[...] = jnp.zeros(chi_ref.shape, I32)

    def count(pred):
        def body(i, acc):
            r0 = pl.multiple_of(i * rows_it, rows_it)
            blk = s_ref[pl.ds(r0, rows_it), :].reshape(unroll, SUBLANES, tq)
            return acc + jnp.sum(pred(blk, r0).astype(I32), axis=0)
        acc = lax.fori_loop(0, nrow // rows_it, body, jnp.zeros((SUBLANES, tq), I32))
        return jnp.broadcast_to(jnp.sum(acc, axis=0, keepdims=True), (SUBLANES, tq))

    def bisect_step(carry):
        it, _ = carry
        lo, hi = lo_ref[...], hi_ref[...]
        mid = (lo & hi) + ((lo ^ hi) >> 1)
        active = mid != lo
        cand = _key_to_f32(mid)
        cnt = count(lambda blk, r0: blk >= cand[None])
        ge = cnt >= topk
        up = active & ge
        dn = active & jnp.logical_not(ge)
        hit = active & (cnt == topk)
        lo_ref[...] = jnp.where(up, mid, lo)
        hi_ref[...] = jnp.where(hit, mid + 1, jnp.where(dn, mid, hi))
        clo_ref[...] = jnp.where(up, cnt, clo_ref[...])
        chi_ref[...] = jnp.where(dn, cnt, chi_ref[...])
        return it + 1, jnp.max(active.astype(I32))

    lax.while_loop(lambda c: (c[0] < 40) & (c[1] > 0), bisect_step,
                   (jnp.zeros((), I32), jnp.ones((), I32)))

    tie = (clo_ref[...] > topk) & (lo_ref[...] > KEY_NEG_INF)
    any_tie = jnp.max(tie.astype(I32))

    @pl.when(any_tie == 0)
    def _():
        hi_ref[...] = jnp.full(hi_ref.shape, n_all, I32)

    @pl.when(any_tie > 0)
    def _():
        chi_ref[...] = topk - chi_ref[...]
        hi_ref[...] = jnp.full(hi_ref.shape, 1, I32) * (nrow - 1)
        clo_ref[...] = jnp.full(clo_ref.shape, -1, I32)
        sub = (lax.broadcasted_iota(I32, (unroll, SUBLANES, tq), 0) * SUBLANES
               + lax.broadcasted_iota(I32, (unroll, SUBLANES, tq), 1))
        thr = _key_to_f32(lo_ref[...])

        def tie_step(_, c):
            jl, jh = clo_ref[...], hi_ref[...]
            mid = jl + ((jh - jl) >> 1)
            active = (jh - jl) > 1
            cnt = count(lambda blk, r0: (blk == thr[None]) & (sub + r0 <= mid[None]))
            ok = cnt >= chi_ref[...]
            hi_ref[...] = jnp.where(active & ok, mid, jh)
            clo_ref[...] = jnp.where(active & jnp.logical_not(ok), mid, jl)
            return c

        lax.fori_loop(0, int(math.ceil(math.log2(n_all))) + 1, tie_step, 0)
        hi_ref[...] = jnp.where(tie, hi_ref[...], n_all)

    thr_row = _key_to_f32(lo_ref[0:1, :])
    last_row = hi_ref[0:1, :]

    def mask_tile(kt, _):
        k0 = pl.multiple_of(kt * tk, tk)
        s = s_ref[pl.ds(k0, tk), :]
        sel = ((s > thr_row) | ((s == thr_row) & (key_in + k0 <= last_row))) & (s > -jnp.inf)
        mask_ref[pl.ds(k0, tk), :] = jnp.where(sel, 1, 0).astype(mask_ref.dtype)
        return 0

    lax.fori_loop(0, nkt, mask_tile, 0)

    def zero_tile(kt, _):
        k0 = pl.multiple_of(kt * tk, tk)
        mask_ref[pl.ds(k0, tk), :] = jnp.zeros((tk, tq), mask_ref.dtype)
        return 0

    lax.fori_loop(nkt, n_all // tk, zero_tile, 0)


def _dsa_select_t(qi, wit, ki, topk):
    n = qi.shape[0]
    tq = min(SEL_TQ, n)
    tk = min(SEL_TK, n)
    return pl.pallas_call(
        functools.partial(_select_t_kernel, tq=tq, tk=tk, topk=topk, unroll=SEL_UNROLL),
        out_shape=jax.ShapeDtypeStruct((n, n), jnp.int8),
        grid=(n // tq,),
        in_specs=[pl.BlockSpec((tq, qi.shape[1]), lambda i: (i, 0)),
                  pl.BlockSpec((IDX_HEADS, tq), lambda i: (0, i)),
                  pl.BlockSpec((n, LANES), lambda i: (0, 0))],
        out_specs=pl.BlockSpec((n, tq), lambda i: (0, i)),
        scratch_shapes=[pltpu.VMEM((n, tq), F32)] + [pltpu.VMEM((SUBLANES, tq), I32)] * 4,
        compiler_params=_cparams(("parallel",)),
        name="dsa_select",
    )(qi, wit, ki)


def _attn_t_kernel(qb_tab, kt_tab, q_ref, k_ref, vt_ref, mask_ref, o_ref, acc_ref, m_ref, l_ref,
                   *, tq, tk):
    step = pl.program_id(0)
    kt = kt_tab[step]

    @pl.when(kt == 0)
    def _():
        acc_ref[...] = jnp.zeros_like(acc_ref)
        m_ref[...] = jnp.full(m_ref.shape, NEG_BIG, F32)
        l_ref[...] = jnp.zeros_like(l_ref)

    bias = jnp.where(mask_ref[...].astype(I32) != 0, 0.0, NEG_BIG)

    for hd in range(N_HEADS):
        blk = hd // 2
        qblk = q_ref[:, blk * LANES:(blk + 1) * LANES]
        qm = jnp.where(_lane_head_mask(qblk.shape, hd % 2), qblk, jnp.zeros_like(qblk))
        s = _dot_nt(k_ref[:, blk * LANES:(blk + 1) * LANES], qm) + bias
        m_old = m_ref[hd:hd + 1, :]
        m_new = jnp.maximum(m_old, jnp.max(s, axis=0, keepdims=True))
        alpha = jnp.exp(m_old - m_new)
        p = jnp.exp(s - m_new)
        l_ref[hd:hd + 1, :] = l_ref[hd:hd + 1, :] * alpha + jnp.sum(p, axis=0, keepdims=True)
        m_ref[hd:hd + 1, :] = m_new
        r0 = hd * HEAD_DIM
        pv = _dot(vt_ref[r0:r0 + HEAD_DIM, :], p.astype(MXU_DTYPE))
        acc_ref[r0:r0 + HEAD_DIM, :] = acc_ref[r0:r0 + HEAD_DIM, :] * alpha + pv

    last = kt == ((qb_tab[step] + 1) * tq - 1) // tk

    @pl.when(last)
    def _():
        for hd in range(N_HEADS):
            r0 = hd * HEAD_DIM
            o_ref[r0:r0 + HEAD_DIM, :] = (acc_ref[r0:r0 + HEAD_DIM, :]
                                          / l_ref[hd:hd + 1, :]).astype(o_ref.dtype)


def _dsa_attend_t(q, k, vt, mask_t):
    n, d = q.shape
    tq = min(ATT_TQ, n)
    tk = min(ATT_TK, n)
    pairs = [(qb, kt) for qb in range(n // tq) for kt in range(((qb + 1) * tq - 1) // tk + 1)]
    qb_tab = jnp.asarray([p[0] for p in pairs], I32)
    kt_tab = jnp.asarray([p[1] for p in pairs], I32)
    grid_spec = pltpu.PrefetchScalarGridSpec(
        num_scalar_prefetch=2,
        grid=(len(pairs),),
        in_specs=[
            pl.BlockSpec((tq, d), lambda i, qt, kt: (qt[i], 0)),
            pl.BlockSpec((tk, d), lambda i, qt, kt: (kt[i], 0)),
            pl.BlockSpec((d, tk), lambda i, qt, kt: (0, kt[i])),
            pl.BlockSpec((tk, tq), lambda i, qt, kt: (kt[i], qt[i])),
        ],
        out_specs=pl.BlockSpec((d, tq), lambda i, qt, kt: (0, qt[i])),
        scratch_shapes=[pltpu.VMEM((d, tq), F32), pltpu.VMEM((N_HEADS, tq), F32),
                        pltpu.VMEM((N_HEADS, tq), F32)],
    )
    return pl.pallas_call(
        functools.partial(_attn_t_kernel, tq=tq, tk=tk),
        out_shape=jax.ShapeDtypeStruct((d, n), MXU_DTYPE),
        grid_spec=grid_spec,
        compiler_params=_cparams(("arbitrary",)),
        name="dsa_attend",
    )(qb_tab, kt_tab, q, k, vt, mask_t)


def _out_t_kernel(x_ref, at_ref, w_ref, o_ref):
    o_ref[...] = x_ref[...] + lax.dot_general(
        at_ref[...], w_ref[...], (((0,), (0,)), ((), ())), preferred_element_type=F32)


def _out_residual_t(x, at, w):
    n, d = x.shape
    tm = min(ROW_TM, n)
    return pl.pallas_call(
        _out_t_kernel,
        out_shape=jax.ShapeDtypeStruct((n, d), F32),
        grid=(n // tm,),
        in_specs=[pl.BlockSpec((tm, d), lambda i: (i, 0)),
                  pl.BlockSpec((d, tm), lambda i: (0, i)),
                  pl.BlockSpec((d, d), lambda i: (0, 0))],
        out_specs=pl.BlockSpec((tm, d), lambda i: (i, 0)),
        compiler_params=_cparams(("parallel",)),
        name="attn_out",
    )(x, at, w)


def _rope_lane_tables(length):
    inv_freq = ROPE_THETA ** (-jnp.arange(0, HEAD_DIM, 2, dtype=F32) / HEAD_DIM)
    ang = jnp.arange(length, dtype=F32)[:, None] * inv_freq[None, :]
    lane = np.arange(LANES)
    cos_t = jnp.cos(ang)[:, lane % 32]
    sin_t = jnp.sin(ang)[:, lane % 32] * jnp.asarray(np.where(lane < 64, -1.0, 1.0), F32)
    return cos_t, sin_t


def kernel(x, s5_lambda_re, s5_lambda_im, s5_log_dt, s5_b_re, s5_b_im, s5_c_re, s5_c_im, s5_d, s5_w_glu, dsa_w_in, dsa_q_norm, dsa_k_norm, dsa_w_o, ffn_w_gate_up, ffn_w_down, norm_mix, norm_ffn):
    bsz, length, d = x.shape
    depth = norm_mix.shape[0]
    topk = min(TOPK_MAX, length // 4)
    nchunk = length // S5_CHUNK
    cos_t, sin_t = _rope_lane_tables(length)
    outs = []
    for b in range(bsz):
        xs = x[b].astype(F32)
        for i in range(depth):
            j = i // 2
            wgu = ffn_w_gate_up[i].astype(MXU_DTYPE)
            wd = ffn_w_down[i].astype(MXU_DTYPE)
            if i % 2 == 0:
                tables = _s5_tables(s5_lambda_re[j], s5_lambda_im[j], s5_log_dt[j], s5_b_re[j],
                                    s5_b_im[j], s5_c_re[j], s5_c_im[j])
                xt = xs.reshape(nchunk, S5_CHUNK, d).transpose(1, 0, 2).reshape(length, d)
                h_t = _norm(xt, norm_mix[i]).reshape(S5_CHUNK, nchunk, d)
                g_t = _s5_scan(h_t, tables, s5_d[j]).reshape(length, d)
                xt = _glu_residual(xt, g_t, s5_w_glu[j].astype(MXU_DTYPE))
                xt = _ffn(xt, norm_ffn[i], wgu, wd)
                xs = xt.reshape(S5_CHUNK, nchunk, d).transpose(1, 0, 2).reshape(length, d)
            else:
                q, k, vt, qi, ki, wi = _dsa_project(xs, norm_mix[i], dsa_w_in[j], dsa_q_norm[j],
                                                    dsa_k_norm[j], cos_t, sin_t)
                mask_t = _dsa_select_t(qi, wi[:, :IDX_HEADS].T, ki, topk)
                att_t = _dsa_attend_t(q, k, vt, mask_t)
                xs = _out_residual_t(xs, att_t, dsa_w_o[j].astype(MXU_DTYPE))
                xs = _ffn(xs, norm_ffn[i], wgu, wd)
        outs.append(xs)
    return jnp.stack(outs, axis=0).astype(x.dtype)
```

```python
import functools
import math

import jax
import jax.numpy as jnp
import numpy as np
from jax import lax
from jax.experimental import pallas as pl
from jax.experimental.pallas import tpu as pltpu

F32 = jnp.float32
BF16 = jnp.bfloat16
I32 = jnp.int32
MXU_DTYPE = BF16

D_MODEL = 1024
S5_GROUP = 16
S5_STATE = 64
N_HEADS = 16
HEAD_DIM = 64
IDX_HEADS = 8
IDX_DIM = 64
TOPK_MAX = 256
ROPE_THETA = 10000.0
EPS = 1e-6

LANES = 128
SUBLANES = 8
MXU_DIM = 256
VMEM_LIMIT = 56 * 1024 * 1024

S5_CHUNK = 16
S5_SLAB_GROUPS = LANES // S5_GROUP
NEG_BIG = -1e30
LOG2E = math.log2(math.e)
VT_PAD = 16
VT_ROWS = HEAD_DIM + VT_PAD

ROW_TM = 512
NORM_TM = 1024
S5_TC = 512
PROJ_TM = 512
SEL_TQ = 256
SEL_TK = 512
SEL_RB = 64
SEL_UNROLL = 32
SEL_UNROLL16 = 16
ATT_TQ = 512
ATT_TK = 512
ATT_QS = 256


def _cparams(sem, flags=None):
    return pltpu.CompilerParams(dimension_semantics=sem, vmem_limit_bytes=VMEM_LIMIT, flags=flags)


def _rms(x, gain=None):
    y = x * lax.rsqrt(jnp.mean(x * x, axis=-1, keepdims=True) + EPS)
    return y if gain is None else y * gain


def _dot(a, b):
    return jnp.dot(a, b, preferred_element_type=F32)


def _dot_nt(a, b):
    return lax.dot_general(a, b, (((1,), (1,)), ((), ())), preferred_element_type=F32)


def _ffn_kernel(x_ref, g_ref, wgu_ref, wd_ref, o_ref, acc_ref, *, d_ff, fc):
    x = x_ref[...]
    h = _rms(x, g_ref[...]).astype(MXU_DTYPE)
    for c in range(d_ff // fc):
        g = _dot(h, wgu_ref[:, c * fc:(c + 1) * fc])
        u = _dot(h, wgu_ref[:, d_ff + c * fc:d_ff + (c + 1) * fc])
        a = (g * jax.nn.sigmoid(g) * u).astype(MXU_DTYPE)
        d = _dot(a, wd_ref[c * fc:(c + 1) * fc, :])
        if c == 0:
            acc_ref[...] = d
        else:
            acc_ref[...] += d
    o_ref[...] = x + acc_ref[...]


def _ffn(x, gain, wgu, wd):
    n, d = x.shape
    d_ff = wd.shape[0]
    tm = min(ROW_TM, n)
    fc = MXU_DIM
    return pl.pallas_call(
        functools.partial(_ffn_kernel, d_ff=d_ff, fc=fc),
        out_shape=jax.ShapeDtypeStruct((n, d), F32),
        grid=(n // tm,),
        in_specs=[
            pl.BlockSpec((tm, d), lambda i: (i, 0)),
            pl.BlockSpec((1, d), lambda i: (0, 0)),
            pl.BlockSpec((d, 2 * d_ff), lambda i: (0, 0)),
            pl.BlockSpec((d_ff, d), lambda i: (0, 0)),
        ],
        out_specs=pl.BlockSpec((tm, d), lambda i: (i, 0)),
        scratch_shapes=[pltpu.VMEM((tm, d), F32)],
        compiler_params=_cparams(("parallel",)),
        name="ffn",
    )(x, gain.reshape(1, d), wgu, wd)


def _norm_kernel(x_ref, g_ref, o_ref):
    o_ref[...] = _rms(x_ref[...], g_ref[...]).astype(o_ref.dtype)


def _norm(x, gain):
    n, d = x.shape
    tm = min(NORM_TM, n)
    return pl.pallas_call(
        _norm_kernel,
        out_shape=jax.ShapeDtypeStruct((n, d), MXU_DTYPE),
        grid=(n // tm,),
        in_specs=[pl.BlockSpec((tm, d), lambda i: (i, 0)),
                  pl.BlockSpec((1, d), lambda i: (0, 0))],
        out_specs=pl.BlockSpec((tm, d), lambda i: (i, 0)),
        compiler_params=_cparams(("parallel",)),
        name="norm",
    )(x, gain.reshape(1, d))


def _s5_tables(lam_re, lam_im, log_dt, b_re, b_im, c_re, c_im):
    hp = lax.Precision.HIGHEST
    g, p = lam_re.shape
    h = S5_GROUP
    nsl = g // S5_SLAB_GROUPS
    sg = S5_SLAB_GROUPS
    t = S5_CHUNK
    lam_re, lam_im, log_dt = lam_re.astype(F32), lam_im.astype(F32), log_dt.astype(F32)
    b_re, b_im, c_re, c_im = (a.astype(F32) for a in (b_re, b_im, c_re, c_im))
    dt = jnp.exp(log_dt)[:, None]

    def apow(k):
        k = jnp.asarray(k, F32).reshape((-1, 1, 1))
        mag = jnp.exp(lam_re[None] * dt[None] * k)
        ang = lam_im[None] * dt[None] * k
        return mag * jnp.cos(ang), mag * jnp.sin(ang)

    ar, ai = apow([1.0])
    ar, ai = ar[0], ai[0]
    den = lam_re * lam_re + lam_im * lam_im
    nr, ni = ar - 1.0, ai
    qr = (nr * lam_re + ni * lam_im) / den
    qi = (ni * lam_re - nr * lam_im) / den
    bbr = qr[..., None] * b_re - qi[..., None] * b_im
    bbi = qr[..., None] * b_im + qi[..., None] * b_re

    pr, pi = apow(np.arange(t + 1))
    mr = c_re[None] * pr[:, :, None, :] - c_im[None] * pi[:, :, None, :]
    mi = c_re[None] * pi[:, :, None, :] + c_im[None] * pr[:, :, None, :]

    kk = (jnp.einsum('tghp,gpk->tghk', mr[:t], bbr, precision=hp)
          - jnp.einsum('tghp,gpk->tghk', mi[:t], bbi, precision=hp))
    kp = jnp.concatenate([jnp.zeros_like(kk[:1]), kk], axis=0)
    eye = jnp.eye(sg, dtype=F32)

    dl = np.arange(t // 2)[:, None, None]
    sl = np.arange(2)[None, :, None]
    jl = np.arange(2)[None, None, :]
    idx = 2 * dl + jl - sl + 1
    kg = kp[idx]
    kg = kg.reshape(t // 2, 2, 2, nsl, sg, h, h)
    tp = jnp.einsum('dljbgok,gm->bdlgkjmo', kg, eye)
    tp = tp.reshape(nsl, t // 2, 2 * sg * h, 2 * sg * h)

    prs, pis = pr[t - 1::-1][:t], pi[t - 1::-1][:t]
    er = prs[..., None] * bbr[None] - pis[..., None] * bbi[None]
    ei = prs[..., None] * bbi[None] + pis[..., None] * bbr[None]
    bf = jnp.stack([er, ei], axis=2)
    bf = bf.reshape(t // 2, 2, nsl, sg, 2, p, h)
    bz = jnp.einsum('zlbgrpk,gm->bzlgkrmp', bf, eye)
    bz = bz.reshape(nsl, t // 2, 2 * sg * h, 2 * sg * p)

    cf = jnp.stack([mr[1:], -mi[1:]], axis=2)
    cf = cf.reshape(t // 2, 2, nsl, sg, 2, h, p)
    cz = jnp.einsum('ijbgrop,gm->birgpjmo', cf, eye)
    cz = cz.reshape(nsl, t // 2, 2 * sg * p, 2 * sg * h)

    def slab_state(re, im):
        k = re.shape[0]
        x = jnp.stack([re, im], axis=1).reshape(k, 2, nsl, sg * p)
        return x.transpose(2, 0, 1, 3).reshape(nsl, k, 2 * sg * p)

    ad = slab_state(*apow([t * 1.0, t * 2.0, t * 4.0]))
    ap8 = slab_state(*apow(t * (np.arange(SUBLANES) + 1.0)))
    return (tp.astype(MXU_DTYPE), bz.astype(MXU_DTYPE), cz.astype(MXU_DTYPE), ad, ap8)


def _s5_kernel(h_ref, tp_ref, bz_ref, cz_ref, ad_ref, ap8_ref, dsk_ref, o_ref,
               carry_ref, z_ref, xp_ref, *, tc):
    half = z_ref.shape[1] // 2
    npair = S5_CHUNK // 2

    @pl.when(pl.program_id(1) == 0)
    def _():
        carry_ref[...] = jnp.zeros_like(carry_ref)

    u = [jnp.concatenate([h_ref[2 * s], h_ref[2 * s + 1]], axis=1) for s in range(npair)]

    z = _dot(u[0], bz_ref[0])
    for s in range(1, npair):
        z = z + _dot(u[s], bz_ref[s])
    z_ref[...] = z

    row = lax.broadcasted_iota(I32, (SUBLANES, half), 0)
    pr8, pi8 = ap8_ref[:, :half], ap8_ref[:, half:]

    def tile_step(t, carry):
        r0 = pl.multiple_of(t * SUBLANES, SUBLANES)
        zt = z_ref[pl.ds(r0, SUBLANES), :]
        xr, xi = zt[:, :half], zt[:, half:]
        for k, d in enumerate((1, 2, 4)):
            a = ad_ref[k:k + 1, :]
            a_r, a_i = a[:, :half], a[:, half:]
            sr = jnp.where(row >= d, pltpu.roll(xr, d, 0), 0.0)
            si = jnp.where(row >= d, pltpu.roll(xi, d, 0), 0.0)
            xr, xi = xr + a_r * sr - a_i * si, xi + a_r * si + a_i * sr
        cr, ci = carry[:, :half], carry[:, half:]
        xr, xi = xr + pr8 * cr - pi8 * ci, xi + pr8 * ci + pi8 * cr
        xpr = jnp.where(row >= 1, pltpu.roll(xr, 1, 0), cr)
        xpi = jnp.where(row >= 1, pltpu.roll(xi, 1, 0), ci)
        xp_ref[pl.ds(r0, SUBLANES), :] = jnp.concatenate([xpr, xpi], axis=1)
        return jnp.concatenate([xr[SUBLANES - 1:], xi[SUBLANES - 1:]], axis=1)

    carry_ref[...] = lax.fori_loop(0, tc // SUBLANES, tile_step, carry_ref[...])

    xp = xp_ref[...].astype(MXU_DTYPE)
    dsk = dsk_ref[...]
    for i in range(npair):
        y = _dot(xp, cz_ref[i])
        for s in range(i + 1):
            y = y + _dot(u[s], tp_ref[i - s])
        for jl in range(2):
            j = 2 * i + jl
            yj = y[:, jl * LANES:(jl + 1) * LANES] + dsk * h_ref[j].astype(F32)
            o_ref[j] = jax.nn.gelu(yj).astype(o_ref.dtype)


def _s5_scan(h_t, tables, d_skip):
    tp, bz, cz, ad, ap8 = tables
    t, c, d = h_t.shape
    nsl = d // LANES
    tc = min(S5_TC, c)
    st = bz.shape[-1]
    dsk = d_skip.astype(F32).reshape(nsl, 1, LANES)
    return pl.pallas_call(
        functools.partial(_s5_kernel, tc=tc),
        out_shape=jax.ShapeDtypeStruct((t, c, d), MXU_DTYPE),
        grid=(nsl, c // tc),
        in_specs=[
            pl.BlockSpec((t, tc, LANES), lambda b, i: (0, i, b)),
            pl.BlockSpec((None,) + tp.shape[1:], lambda b, i: (b, 0, 0, 0)),
            pl.BlockSpec((None,) + bz.shape[1:], lambda b, i: (b, 0, 0, 0)),
            pl.BlockSpec((None,) + cz.shape[1:], lambda b, i: (b, 0, 0, 0)),
            pl.BlockSpec((None,) + ad.shape[1:], lambda b, i: (b, 0, 0)),
            pl.BlockSpec((None,) + ap8.shape[1:], lambda b, i: (b, 0, 0)),
            pl.BlockSpec((None, 1, LANES), lambda b, i: (b, 0, 0)),
        ],
        out_specs=pl.BlockSpec((t, tc, LANES), lambda b, i: (0, i, b)),
        scratch_shapes=[pltpu.VMEM((1, st), F32), pltpu.VMEM((tc, st), F32),
                        pltpu.VMEM((tc, st), F32)],
        compiler_params=_cparams(("arbitrary", "arbitrary")),
        name="s5_scan",
    )(h_t, tp, bz, cz, ad, ap8, dsk)


def _glu_kernel(x_ref, g_ref, w_ref, o_ref):
    d = x_ref.shape[1]
    vg = _dot(g_ref[...], w_ref[...])
    o_ref[...] = x_ref[...] + vg[:, :d] * jax.nn.sigmoid(vg[:, d:])


def _glu_residual(x, g, w):
    n, d = x.shape
    tm = min(ROW_TM, n)
    return pl.pallas_call(
        _glu_kernel,
        out_shape=jax.ShapeDtypeStruct((n, d), F32),
        grid=(n // tm,),
        in_specs=[pl.BlockSpec((tm, d), lambda i: (i, 0)),
                  pl.BlockSpec((tm, d), lambda i: (i, 0)),
                  pl.BlockSpec((d, 2 * d), lambda i: (0, 0))],
        out_specs=pl.BlockSpec((tm, d), lambda i: (i, 0)),
        compiler_params=_cparams(("parallel",)),
        name="glu",
    )(x, g, w)


def _head_perm(n_heads):
    n = np.arange(n_heads * HEAD_DIM)
    pb, r = n // LANES, n % LANES
    half, r2 = r // 64, r % 64
    hl, dp = r2 // 32, r2 % 32
    return (2 * pb + hl) * HEAD_DIM + 32 * half + dp


def _lane_head_mask(shape, hl):
    lane = lax.broadcasted_iota(I32, shape, len(shape) - 1)
    return ((lane % 64) // 32) == hl


def _proj_kernel(x_ref, g_ref, w_ref, wvt_ref, gq_ref, gk_ref, cos_ref, sin_ref, hm_ref,
                 q_ref, k_ref, vt_ref, qi_ref, ki_ref, wi_ref, *, d, dqi, att_scale, w_scale):
    h = _rms(x_ref[...], g_ref[...]).astype(MXU_DTYPE)
    cos, sin = cos_ref[...], sin_ref[...]
    hm = hm_ref[...]

    def rope(t):
        return t * cos + pltpu.roll(t, 64, 1) * sin

    def headnorm_rope(col0, gain_ref, out_ref, scale):
        for sb in range(d // MXU_DIM):
            c0 = sb * MXU_DIM
            t = _dot(h, w_ref[:, col0 + c0:col0 + c0 + MXU_DIM])
            sq = t * t
            hi = sq.astype(MXU_DTYPE)
            lo = (sq - hi.astype(F32)).astype(MXU_DTYPE)
            ss = _dot(hi, hm) + _dot(lo, hm)
            tn = t * lax.rsqrt(ss * (1.0 / HEAD_DIM) + EPS) * gain_ref[:, c0:c0 + MXU_DIM]
            for b in range(MXU_DIM // LANES):
                r = rope(tn[:, b * LANES:(b + 1) * LANES])
                if scale != 1.0:
                    r = r * scale
                out_ref[:, c0 + b * LANES:c0 + (b + 1) * LANES] = r.astype(out_ref.dtype)

    headnorm_rope(0, gq_ref, q_ref, att_scale)
    headnorm_rope(d, gk_ref, k_ref, 1.0)
    vt = _dot_nt(wvt_ref[...], h)
    row = lax.broadcasted_iota(I32, vt.shape, 0)
    vt_ref[...] = jnp.where(row % VT_ROWS >= HEAD_DIM, 1.0, vt).astype(vt_ref.dtype)
    c0 = 2 * d
    t = _dot(h, w_ref[:, c0:c0 + dqi])
    for b in range(dqi // LANES):
        qi_ref[:, b * LANES:(b + 1) * LANES] = rope(t[:, b * LANES:(b + 1) * LANES]).astype(qi_ref.dtype)
    c0 += dqi
    t = _dot(h, w_ref[:, c0:c0 + LANES])
    ms = jnp.sum(t * t, axis=-1, keepdims=True) * (0.5 / IDX_DIM)
    ki_ref[...] = rope(t * lax.rsqrt(ms + EPS)).astype(ki_ref.dtype)
    c0 += LANES
    wi_ref[...] = _dot(h, w_ref[:, c0:c0 + LANES]) * w_scale


def _dsa_project(x, gain, w_in, q_gain, k_gain, cos_t, sin_t):
    n, d = x.shape
    dqi = IDX_HEADS * IDX_DIM
    pq = _head_perm(N_HEADS)
    pqi = _head_perm(IDX_HEADS)
    wq = w_in[:, 0:d][:, pq]
    wk = w_in[:, d:2 * d][:, pq]
    wvt = w_in[:, 2 * d:3 * d].T.reshape(N_HEADS, HEAD_DIM, d)
    wvt = jnp.pad(wvt, ((0, 0), (0, VT_PAD), (0, 0))).reshape(N_HEADS * VT_ROWS, d).astype(MXU_DTYPE)
    dvt = N_HEADS * VT_ROWS
    wqi = w_in[:, 3 * d:3 * d + dqi][:, pqi]
    lane = np.arange(LANES)
    wki = w_in[:, 3 * d + dqi:3 * d + dqi + IDX_DIM][:, 32 * (lane // 64) + lane % 32]
    wwi = jnp.pad(w_in[:, 3 * d + dqi + IDX_DIM:], ((0, 0), (0, LANES - IDX_HEADS)))
    w_all = jnp.concatenate([wq, wk, wqi, wki, wwi], axis=1).astype(MXU_DTYPE)
    dcol = (pq % HEAD_DIM)
    gq = q_gain.astype(F32)[dcol].reshape(1, d)
    gk = k_gain.astype(F32)[dcol].reshape(1, d)
    l2 = np.arange(MXU_DIM)
    hm = ((l2[:, None] // LANES == l2[None, :] // LANES)
          & ((l2[:, None] % 64) // 32 == (l2[None, :] % 64) // 32))
    hm = jnp.asarray(hm, MXU_DTYPE)
    tm = min(PROJ_TM, n)
    nw = w_all.shape[1]
    outs = pl.pallas_call(
        functools.partial(_proj_kernel, d=d, dqi=dqi, att_scale=HEAD_DIM ** -0.5 * LOG2E,
                          w_scale=(IDX_HEADS ** -0.5) * (IDX_DIM ** -0.5)),
        out_shape=[jax.ShapeDtypeStruct((n, d), MXU_DTYPE)] * 2
        + [jax.ShapeDtypeStruct((dvt, n), MXU_DTYPE),
           jax.ShapeDtypeStruct((n, dqi), MXU_DTYPE),
           jax.ShapeDtypeStruct((n, LANES), MXU_DTYPE),
           jax.ShapeDtypeStruct((n, LANES), F32)],
        grid=(n // tm,),
        in_specs=[
            pl.BlockSpec((tm, d), lambda i: (i, 0)),
            pl.BlockSpec((1, d), lambda i: (0, 0)),
            pl.BlockSpec((d, nw), lambda i: (0, 0)),
            pl.BlockSpec((dvt, d), lambda i: (0, 0)),
            pl.BlockSpec((1, d), lambda i: (0, 0)),
            pl.BlockSpec((1, d), lambda i: (0, 0)),
            pl.BlockSpec((tm, LANES), lambda i: (i, 0)),
            pl.BlockSpec((tm, LANES), lambda i: (i, 0)),
            pl.BlockSpec((MXU_DIM, MXU_DIM), lambda i: (0, 0)),
        ],
        out_specs=[pl.BlockSpec((tm, d), lambda i: (i, 0))] * 2
        + [pl.BlockSpec((dvt, tm), lambda i: (0, i)),
           pl.BlockSpec((tm, dqi), lambda i: (i, 0)),
           pl.BlockSpec((tm, LANES), lambda i: (i, 0)),
           pl.BlockSpec((tm, LANES), lambda i: (i, 0))],
        compiler_params=_cparams(("parallel",)),
        name="dsa_proj",
    )(x, gain.reshape(1, d), w_all, wvt, gq, gk, cos_t, sin_t, hm)
    return outs


KEY_NEG_INF = -2139095041
KEY_POS_INF = 2139095040
KEY16_NEG_INF = -32641
KEY16_POS_INF = 32640


def _key_to_f32(key):
    bits = key ^ ((key >> 31) & 0x7FFFFFFF)
    return lax.bitcast_convert_type(bits, F32)


def _select_kernel(qi_ref, wi_ref, ki_ref, mask_ref, s_ref, lo_ref, hi_ref, clo_ref, chi_ref,
                   *, tq, tk, topk, rb):
    qb = pl.program_id(0)
    n_all = mask_ref.shape[1]
    nkt = (qb * tq) // tk + 1
    ncol = nkt * (tk // LANES)

    qm = []
    for hh in range(IDX_HEADS):
        blk = qi_ref[:, (hh // 2) * LANES:(hh // 2 + 1) * LANES]
        qm.append(jnp.where(_lane_head_mask(blk.shape, hh % 2), blk, jnp.zeros_like(blk)))
    qm = jnp.concatenate(qm, axis=0)
    wv = wi_ref[...]
    row_pos = qb * tq + lax.broadcasted_iota(I32, (tq, tk), 0)
    col_in = lax.broadcasted_iota(I32, (tq, tk), 1)

    def score_tile(kt, _):
        k0 = pl.multiple_of(kt * tk, tk)
        s = _dot_nt(qm, ki_ref[pl.ds(k0, tk), :])
        acc = wv[:, 0:1] * jnp.maximum(s[0:tq], 0.0)
        for hh in range(1, IDX_HEADS):
            acc = acc + wv[:, hh:hh + 1] * jnp.maximum(s[hh * tq:(hh + 1) * tq], 0.0)
        acc = jnp.where(col_in + k0 <= row_pos, acc, -jnp.inf)
        s_ref[:, pl.ds(k0, tk)] = acc
        return 0

    lax.fori_loop(0, nkt, score_tile, 0)

    lo_ref[...] = jnp.full(lo_ref.shape, KEY_NEG_INF, I32)
    hi_ref[...] = jnp.full(hi_ref.shape, KEY_POS_INF, I32)
    clo_ref[...] = jnp.full(clo_ref.shape, 1, I32) * (ncol * LANES)
    chi_ref[...] = jnp.zeros(chi_ref.shape, I32)

    def count_rows(r0, pred):
        def body(j, acc):
            c0 = pl.multiple_of(j * LANES, LANES)
            return acc + pred(s_ref[r0:r0 + rb, pl.ds(c0, LANES)], j).astype(I32)
        acc = lax.fori_loop(0, ncol, body, jnp.zeros((rb, LANES), I32))
        return jnp.broadcast_to(jnp.sum(acc, axis=1, keepdims=True), (rb, LANES))

    def bisect_step(carry):
        it, _ = carry
        pending = jnp.zeros((), I32)
        for r0 in range(0, tq, rb):
            lo, hi = lo_ref[r0:r0 + rb, :], hi_ref[r0:r0 + rb, :]
            mid = (lo & hi) + ((lo ^ hi) >> 1)
            active = mid != lo
            cand = _key_to_f32(mid)
            cnt = count_rows(r0, lambda blk, j: blk >= cand)
            ge = cnt >= topk
            up = active & ge
            dn = active & jnp.logical_not(ge)
            hit = active & (cnt == topk)
            lo_ref[r0:r0 + rb, :] = jnp.where(up, mid, lo)
            hi_ref[r0:r0 + rb, :] = jnp.where(hit, mid + 1, jnp.where(dn, mid, hi))
            clo_ref[r0:r0 + rb, :] = jnp.where(up, cnt, clo_ref[r0:r0 + rb, :])
            chi_ref[r0:r0 + rb, :] = jnp.where(dn, cnt, chi_ref[r0:r0 + rb, :])
            pending = jnp.maximum(pending, jnp.max(active.astype(I32)))
        return it + 1, pending

    lax.while_loop(lambda c: (c[0] < 40) & (c[1] > 0), bisect_step,
                   (jnp.zeros((), I32), jnp.ones((), I32)))

    tie = (clo_ref[...] > topk) & (lo_ref[...] > KEY_NEG_INF)
    any_tie = jnp.max(tie.astype(I32))

    @pl.when(any_tie == 0)
    def _():
        hi_ref[...] = jnp.full(hi_ref.shape, n_all, I32)

    @pl.when(any_tie > 0)
    def _():
        need = topk - chi_ref[...]
        chi_ref[...] = need
        hi_ref[...] = jnp.full(hi_ref.shape, ncol * LANES - 1, I32)
        clo_ref[...] = jnp.full(clo_ref.shape, -1, I32)
        lane = lax.broadcasted_iota(I32, (rb, LANES), 1)

        def tie_step(_, c):
            for r0 in range(0, tq, rb):
                jl, jh = clo_ref[r0:r0 + rb, :], hi_ref[r0:r0 + rb, :]
                mid = jl + ((jh - jl) >> 1)
                active = (jh - jl) > 1
                thr = _key_to_f32(lo_ref[r0:r0 + rb, :])
                cnt = count_rows(r0, lambda blk, j: (blk == thr) & (lane + j * LANES <= mid))
                ok = cnt >= chi_ref[r0:r0 + rb, :]
                hi_ref[r0:r0 + rb, :] = jnp.where(active & ok, mid, jh)
                clo_ref[r0:r0 + rb, :] = jnp.where(active & jnp.logical_not(ok), mid, jl)
            return c

        lax.fori_loop(0, int(math.ceil(math.log2(n_all))) + 1, tie_step, 0)
        keep_all = jnp.logical_not(tie)
        hi_ref[...] = jnp.where(keep_all, n_all, hi_ref[...])

    lane_k = lax.broadcasted_iota(I32, (tq, tk), 1)
    thr_col = _key_to_f32(lo_ref[:, 0:1])
    last_col = hi_ref[:, 0:1]

    def mask_tile(kt, _):
        k0 = pl.multiple_of(kt * tk, tk)
        s = s_ref[:, pl.ds(k0, tk)]
        sel = ((s > thr_col) | ((s == thr_col) & (lane_k + k0 <= last_col))) & (s > -jnp.inf)
        mask_ref[:, pl.ds(k0, tk)] = jnp.where(sel, 1, 0).astype(mask_ref.dtype)
        return 0

    lax.fori_loop(0, nkt, mask_tile, 0)

    def zero_tile(kt, _):
        k0 = pl.multiple_of(kt * tk, tk)
        mask_ref[:, pl.ds(k0, tk)] = jnp.zeros((tq, tk), mask_ref.dtype)
        return 0

    lax.fori_loop(nkt, n_all // tk, zero_tile, 0)


def _dsa_select(qi, wi, ki, topk):
    n = qi.shape[0]
    tq = min(SEL_TQ, n)
    tk = min(SEL_TK, n)
    rb = min(SEL_RB, tq)
    return pl.pallas_call(
        functools.partial(_select_kernel, tq=tq, tk=tk, topk=topk, rb=rb),
        out_shape=jax.ShapeDtypeStruct((n, n), jnp.int8),
        grid=(n // tq,),
        in_specs=[pl.BlockSpec((tq, qi.shape[1]), lambda i: (i, 0)),
                  pl.BlockSpec((tq, LANES), lambda i: (i, 0)),
                  pl.BlockSpec((n, LANES), lambda i: (0, 0))],
        out_specs=pl.BlockSpec((tq, n), lambda i: (i, 0)),
        scratch_shapes=[pltpu.VMEM((tq, n), F32)] + [pltpu.VMEM((tq, LANES), I32)] * 4,
        compiler_params=_cparams(("parallel",)),
        name="dsa_select",
    )(qi, wi, ki)


def _attn_kernel(qb_tab, kt_tab, q_ref, k_ref, v_ref, mask_ref, o_ref, acc_ref, m_ref, l_ref,
                 *, tq, tk):
    step = pl.program_id(0)
    kt = kt_tab[step]
    d = q_ref.shape[1]
    slab = MXU_DIM
    heads_per_slab = slab // HEAD_DIM

    @pl.when(kt == 0)
    def _():
        acc_ref[...] = jnp.zeros_like(acc_ref)
        m_ref[...] = jnp.full(m_ref.shape, NEG_BIG, F32)
        l_ref[...] = jnp.zeros_like(l_ref)

    bias = jnp.where(mask_ref[...].astype(I32) != 0, 0.0, NEG_BIG)
    lane_s = lax.broadcasted_iota(I32, (tq, slab), 1) // HEAD_DIM

    for sb in range(d // slab):
        vs = v_ref[:, sb * slab:(sb + 1) * slab]
        lane_v = lax.broadcasted_iota(I32, vs.shape, 1) // HEAD_DIM
        alpha_l = jnp.zeros((tq, slab), F32)
        pv = jnp.zeros((tq, slab), F32)
        for i in range(heads_per_slab):
            hd = sb * heads_per_slab + i
            blk = hd // 2
            qblk = q_ref[:, blk * LANES:(blk + 1) * LANES]
            qm = jnp.where(_lane_head_mask(qblk.shape, hd % 2), qblk, jnp.zeros_like(qblk))
            s = _dot_nt(qm, k_ref[:, blk * LANES:(blk + 1) * LANES]) + bias
            m_old = m_ref[:, hd:hd + 1]
            m_new = jnp.maximum(m_old, jnp.max(s, axis=1, keepdims=True))
            alpha = jnp.exp(m_old - m_new)
            p = jnp.exp(s - m_new)
            l_ref[:, hd:hd + 1] = l_ref[:, hd:hd + 1] * alpha + jnp.sum(p, axis=1, keepdims=True)
            m_ref[:, hd:hd + 1] = m_new
            vm = jnp.where(lane_v == i, vs, jnp.zeros_like(vs))
            pv = pv + _dot(p.astype(MXU_DTYPE), vm)
            alpha_l = jnp.where(lane_s == i, alpha, alpha_l)
        acc_ref[:, sb * slab:(sb + 1) * slab] = acc_ref[:, sb * slab:(sb + 1) * slab] * alpha_l + pv

    last = kt == ((qb_tab[step] + 1) * tq - 1) // tk

    @pl.when(last)
    def _():
        lane_h = lax.broadcasted_iota(I32, (tq, d), 1) // HEAD_DIM
        l_l = jnp.zeros((tq, d), F32)
        for hd in range(N_HEADS):
            l_l = jnp.where(lane_h == hd, l_ref[:, hd:hd + 1], l_l)
        o_ref[...] = (acc_ref[...] / l_l).astype(o_ref.dtype)


def _dsa_attend(q, k, v, mask):
    n, d = q.shape
    tq = min(ATT_TQ, n)
    tk = min(ATT_TK, n)
    pairs = [(qb, kt) for qb in range(n // tq) for kt in range(((qb + 1) * tq - 1) // tk + 1)]
    qb_tab = jnp.asarray([p[0] for p in pairs], I32)
    kt_tab = jnp.asarray([p[1] for p in pairs], I32)
    grid_spec = pltpu.PrefetchScalarGridSpec(
        num_scalar_prefetch=2,
        grid=(len(pairs),),
        in_specs=[
            pl.BlockSpec((tq, d), lambda i, qt, kt: (qt[i], 0)),
            pl.BlockSpec((tk, d), lambda i, qt, kt: (kt[i], 0)),
            pl.BlockSpec((tk, d), lambda i, qt, kt: (kt[i], 0)),
            pl.BlockSpec((tq, tk), lambda i, qt, kt: (qt[i], kt[i])),
        ],
        out_specs=pl.BlockSpec((tq, d), lambda i, qt, kt: (qt[i], 0)),
        scratch_shapes=[pltpu.VMEM((tq, d), F32), pltpu.VMEM((tq, LANES), F32),
                        pltpu.VMEM((tq, LANES), F32)],
    )
    return pl.pallas_call(
        functools.partial(_attn_kernel, tq=tq, tk=tk),
        out_shape=jax.ShapeDtypeStruct((n, d), MXU_DTYPE),
        grid_spec=grid_spec,
        compiler_params=_cparams(("arbitrary",)),
        name="dsa_attend",
    )(qb_tab, kt_tab, q, k, v, mask)


def _out_kernel(x_ref, a_ref, w_ref, o_ref):
    o_ref[...] = x_ref[...] + _dot(a_ref[...], w_ref[...])


def _out_residual(x, a, w):
    n, d = x.shape
    tm = min(ROW_TM, n)
    return pl.pallas_call(
        _out_kernel,
        out_shape=jax.ShapeDtypeStruct((n, d), F32),
        grid=(n // tm,),
        in_specs=[pl.BlockSpec((tm, d), lambda i: (i, 0)),
                  pl.BlockSpec((tm, d), lambda i: (i, 0)),
                  pl.BlockSpec((d, d), lambda i: (0, 0))],
        out_specs=pl.BlockSpec((tm, d), lambda i: (i, 0)),
        compiler_params=_cparams(("parallel",)),
        name="attn_out",
    )(x, a, w)


def _select_t_kernel(qi_ref, wit_ref, ki_ref, mask_ref, s_ref, s16_ref, lo_ref, hi_ref, clo_ref,
                     chi_ref, *, tq, tk, topk, unroll, unroll16):
    qb = pl.program_id(0)
    n_all = mask_ref.shape[0]
    nkt = (qb * tq + tq - 1) // tk + 1
    nrow = nkt * tk
    rows_it = SUBLANES * unroll

    qm = []
    for hh in range(IDX_HEADS):
        blk = qi_ref[:, (hh // 2) * LANES:(hh // 2 + 1) * LANES]
        qm.append(jnp.where(_lane_head_mask(blk.shape, hh % 2), blk, jnp.zeros_like(blk)))
    qm = jnp.concatenate(qm, axis=0)
    wt = wit_ref[...]
    key_in = lax.broadcasted_iota(I32, (tk, tq), 0)
    q_pos = qb * tq + lax.broadcasted_iota(I32, (tk, tq), 1)

    def score_tile(kt, _):
        k0 = pl.multiple_of(kt * tk, tk)
        s = _dot_nt(ki_ref[pl.ds(k0, tk), :], qm)
        acc = wt[0:1, :] * jnp.maximum(s[:, 0:tq], 0.0)
        for hh in range(1, IDX_HEADS):
            acc = acc + wt[hh:hh + 1, :] * jnp.maximum(s[:, hh * tq:(hh + 1) * tq], 0.0)
        sc = jnp.where(key_in + k0 <= q_pos, acc, -jnp.inf)
        s_ref[pl.ds(k0, tk), :] = sc
        hi_bits = lax.bitcast_convert_type(sc, I32) & -65536
        s16_ref[pl.ds(k0, tk), :] = lax.bitcast_convert_type(hi_bits, F32).astype(BF16)
        return 0

    lax.fori_loop(0, nkt, score_tile, 0)

    lo_ref[...] = jnp.full(lo_ref.shape, KEY16_NEG_INF, I32)
    hi_ref[...] = jnp.full(hi_ref.shape, KEY16_POS_INF + 1, I32)
    clo_ref[...] = jnp.full(clo_ref.shape, 1, I32) * nrow
    chi_ref[...] = jnp.zeros(chi_ref.shape, I32)
    rows16 = 2 * SUBLANES * unroll16
    one16 = jnp.ones((), BF16)
    zero16 = jnp.zeros((), BF16)

    def count16(cand):
        def body(i, acc):
            r0 = pl.multiple_of(i * rows16, rows16)
            blk = s16_ref[pl.ds(r0, rows16), :].reshape(unroll16, 2 * SUBLANES, tq)
            ones = jnp.where(blk >= cand[None], one16, zero16)
            part = ones[0]
            for u in range(1, unroll16):
                part = part + ones[u]
            return acc + part.astype(F32)
        acc = lax.fori_loop(0, nrow // rows16, body, jnp.zeros((2 * SUBLANES, tq), F32))
        tot = jnp.sum(acc, axis=0, keepdims=True).astype(I32)
        return jnp.broadcast_to(tot, (SUBLANES, tq))

    def coarse_step(carry):
        it, _ = carry
        lo, hi = lo_ref[...], hi_ref[...]
        mid = (lo + hi) >> 1
        active = mid != lo
        bits16 = (mid ^ ((mid >> 31) & 0x7FFF)) & 0xFFFF
        cand = lax.bitcast_convert_type(bits16 << 16, F32)
        cand = jnp.concatenate([cand, cand], axis=0).astype(BF16)
        cnt = count16(cand)
        ge = cnt >= topk
        up = active & ge
        dn = active & jnp.logical_not(ge)
        lo_ref[...] = jnp.where(up, mid, lo)
        hi_ref[...] = jnp.where(dn, mid, hi)
        clo_ref[...] = jnp.where(up, cnt, clo_ref[...])
        chi_ref[...] = jnp.where(dn, cnt, chi_ref[...])
        return it + 1, jnp.max(active.astype(I32))

    lax.while_loop(lambda c: (c[0] < 20) & (c[1] > 0), coarse_step,
                   (jnp.zeros((), I32), jnp.ones((), I32)))

    lo16 = lo_ref[...]
    none_finite = lo16 == KEY16_NEG_INF
    lo_ref[...] = jnp.where(none_finite, KEY_NEG_INF, lo16 << 16)
    hi_ref[...] = jnp.where(none_finite, KEY_NEG_INF + 1, (lo16 + 1) << 16)

    def count(pred):
        def body(i, acc):
            r0 = pl.multiple_of(i * rows_it, rows_it)
            blk = s_ref[pl.ds(r0, rows_it), :].reshape(unroll, SUBLANES, tq)
            return acc + jnp.sum(pred(blk, r0).astype(I32), axis=0)
        acc = lax.fori_loop(0, nrow // rows_it, body, jnp.zeros((SUBLANES, tq), I32))
        return jnp.broadcast_to(jnp.sum(acc, axis=0, keepdims=True), (SUBLANES, tq))

    def bisect_step(carry):
        it, _ = carry
        lo, hi = lo_ref[...], hi_ref[...]
        mid = (lo & hi) + ((lo ^ hi) >> 1)
        active = mid != lo
        cand = _key_to_f32(mid)
        cnt = count(lambda blk, r0: blk >= cand[None])
        ge = cnt >= topk
        up = active & ge
        dn = active & jnp.logical_not(ge)
        hit = active & (cnt == topk)
        lo_ref[...] = jnp.where(up, mid, lo)
        hi_ref[...] = jnp.where(hit, mid + 1, jnp.where(dn, mid, hi))
        clo_ref[...] = jnp.where(up, cnt, clo_ref[...])
        chi_ref[...] = jnp.where(dn, cnt, chi_ref[...])
        return it + 1, jnp.max(active.astype(I32))

    lax.while_loop(lambda c: (c[0] < 40) & (c[1] > 0), bisect_step,
                   (jnp.zeros((), I32), jnp.ones((), I32)))

    tie = (clo_ref[...] > topk) & (lo_ref[...] > KEY_NEG_INF)
    any_tie = jnp.max(tie.astype(I32))

    @pl.when(any_tie == 0)
    def _():
        hi_ref[...] = jnp.full(hi_ref.shape, n_all, I32)

    @pl.when(any_tie > 0)
    def _():
        chi_ref[...] = topk - chi_ref[...]
        hi_ref[...] = jnp.full(hi_ref.shape, 1, I32) * (nrow - 1)
        clo_ref[...] = jnp.full(clo_ref.shape, -1, I32)
        sub = (lax.broadcasted_iota(I32, (unroll, SUBLANES, tq), 0) * SUBLANES
               + lax.broadcasted_iota(I32, (unroll, SUBLANES, tq), 1))
        thr = _key_to_f32(lo_ref[...])

        def tie_step(_, c):
            jl, jh = clo_ref[...], hi_ref[...]
            mid = jl + ((jh - jl) >> 1)
            active = (jh - jl) > 1
            cnt = count(lambda blk, r0: (blk == thr[None]) & (sub + r0 <= mid[None]))
            ok = cnt >= chi_ref[...]
            hi_ref[...] = jnp.where(active & ok, mid, jh)
            clo_ref[...] = jnp.where(active & jnp.logical_not(ok), mid, jl)
            return c

        lax.fori_loop(0, int(math.ceil(math.log2(n_all))) + 1, tie_step, 0)
        hi_ref[...] = jnp.where(tie, hi_ref[...], n_all)

    thr_row = _key_to_f32(lo_ref[0:1, :])
    last_row = hi_ref[0:1, :]

    def mask_tile(kt, _):
        k0 = pl.multiple_of(kt * tk, tk)
        s = s_ref[pl.ds(k0, tk), :]
        sel = ((s > thr_row) | ((s == thr_row) & (key_in + k0 <= last_row))) & (s > -jnp.inf)
        mask_ref[pl.ds(k0, tk), :] = jnp.where(sel, 1, 0).astype(mask_ref.dtype)
        return 0

    lax.fori_loop(0, nkt, mask_tile, 0)

    def zero_tile(kt, _):
        k0 = pl.multiple_of(kt * tk, tk)
        mask_ref[pl.ds(k0, tk), :] = jnp.zeros((tk, tq), mask_ref.dtype)
        return 0

    lax.fori_loop(nkt, n_all // tk, zero_tile, 0)


def _dsa_select_t(qi, wit, ki, topk):
    n = qi.shape[0]
    tq = min(SEL_TQ, n)
    tk = min(SEL_TK, n)
    return pl.pallas_call(
        functools.partial(_select_t_kernel, tq=tq, tk=tk, topk=topk, unroll=SEL_UNROLL,
                          unroll16=SEL_UNROLL16),
        out_shape=jax.ShapeDtypeStruct((n, n), jnp.int8),
        grid=(n // tq,),
        in_specs=[pl.BlockSpec((tq, qi.shape[1]), lambda i: (i, 0)),
                  pl.BlockSpec((IDX_HEADS, tq), lambda i: (0, i)),
                  pl.BlockSpec((n, LANES), lambda i: (0, 0))],
        out_specs=pl.BlockSpec((n, tq), lambda i: (0, i)),
        scratch_shapes=[pltpu.VMEM((n, tq), F32), pltpu.VMEM((n, tq), BF16)]
        + [pltpu.VMEM((SUBLANES, tq), I32)] * 4,
        compiler_params=_cparams(("parallel",)),
        name="dsa_select",
    )(qi, wit, ki)


def _attn_t_kernel(qb_tab, kt_tab, q_ref, k_ref, vt_ref, mask_ref, o_ref, acc_ref, m_ref,
                   *, tq, tk, qs):
    step = pl.program_id(0)
    kt = kt_tab[step]

    @pl.when(kt == 0)
    def _():
        acc_ref[...] = jnp.zeros_like(acc_ref)
        m_ref[...] = jnp.full(m_ref.shape, NEG_BIG, F32)

    bias = jnp.where(mask_ref[...].astype(I32) != 0, 0.0, NEG_BIG)

    for hd in range(N_HEADS):
        blk = hd // 2
        r0 = hd * VT_ROWS
        qblk = q_ref[:, blk * LANES:(blk + 1) * LANES]
        qm = jnp.where(_lane_head_mask(qblk.shape, hd % 2), qblk, jnp.zeros_like(qblk))
        s = [_dot_nt(k_ref[k0:k0 + qs, blk * LANES:(blk + 1) * LANES], qm) + bias[k0:k0 + qs, :]
             for k0 in range(0, tk, qs)]
        m_old = m_ref[hd:hd + 1, :]
        m_new = m_old
        for sp in s:
            m_new = jnp.maximum(m_new, jnp.max(sp, axis=0, keepdims=True))
        alpha = jnp.exp2(m_old - m_new)
        p = jnp.concatenate([jnp.exp2(sp - m_new).astype(MXU_DTYPE) for sp in s], axis=0)
        m_ref[hd:hd + 1, :] = m_new
        pv = _dot(vt_ref[r0:r0 + VT_ROWS, :], p)
        acc_ref[r0:r0 + VT_ROWS, :] = acc_ref[r0:r0 + VT_ROWS, :] * alpha + pv

    last = kt == ((qb_tab[step] + 1) * tq - 1) // tk

    @pl.when(last)
    def _():
        for hd in range(N_HEADS):
            r0 = hd * VT_ROWS
            o_ref[hd * HEAD_DIM:(hd + 1) * HEAD_DIM, :] = (
                acc_ref[r0:r0 + HEAD_DIM, :] / acc_ref[r0 + HEAD_DIM:r0 + HEAD_DIM + 1, :]
            ).astype(o_ref.dtype)


def _dsa_attend_t(q, k, vt, mask_t):
    n, d = q.shape
    tq = min(ATT_TQ, n)
    tk = min(ATT_TK, n)
    pairs = [(qb, kt) for qb in range(n // tq) for kt in range(((qb + 1) * tq - 1) // tk + 1)]
    qb_tab = jnp.asarray([p[0] for p in pairs], I32)
    kt_tab = jnp.asarray([p[1] for p in pairs], I32)
    grid_spec = pltpu.PrefetchScalarGridSpec(
        num_scalar_prefetch=2,
        grid=(len(pairs),),
        in_specs=[
            pl.BlockSpec((tq, d), lambda i, qt, kt: (qt[i], 0)),
            pl.BlockSpec((tk, d), lambda i, qt, kt: (kt[i], 0)),
            pl.BlockSpec((vt.shape[0], tk), lambda i, qt, kt: (0, kt[i])),
            pl.BlockSpec((tk, tq), lambda i, qt, kt: (kt[i], qt[i])),
        ],
        out_specs=pl.BlockSpec((d, tq), lambda i, qt, kt: (0, qt[i])),
        scratch_shapes=[pltpu.VMEM((vt.shape[0], tq), F32), pltpu.VMEM((N_HEADS, tq), F32)],
    )
    return pl.pallas_call(
        functools.partial(_attn_t_kernel, tq=tq, tk=tk, qs=min(ATT_QS, tq)),
        out_shape=jax.ShapeDtypeStruct((d, n), MXU_DTYPE),
        grid_spec=grid_spec,
        compiler_params=_cparams(("arbitrary",)),
        name="dsa_attend",
    )(qb_tab, kt_tab, q, k, vt, mask_t)


def _out_t_kernel(x_ref, at_ref, w_ref, o_ref):
    o_ref[...] = x_ref[...] + lax.dot_general(
        at_ref[...], w_ref[...], (((0,), (0,)), ((), ())), preferred_element_type=F32)


def _out_residual_t(x, at, w):
    n, d = x.shape
    tm = min(ROW_TM, n)
    return pl.pallas_call(
        _out_t_kernel,
        out_shape=jax.ShapeDtypeStruct((n, d), F32),
        grid=(n // tm,),
        in_specs=[pl.BlockSpec((tm, d), lambda i: (i, 0)),
                  pl.BlockSpec((d, tm), lambda i: (0, i)),
                  pl.BlockSpec((d, d), lambda i: (0, 0))],
        out_specs=pl.BlockSpec((tm, d), lambda i: (i, 0)),
        compiler_params=_cparams(("parallel",)),
        name="attn_out",
    )(x, at, w)


def _rope_lane_tables(length):
    inv_freq = ROPE_THETA ** (-jnp.arange(0, HEAD_DIM, 2, dtype=F32) / HEAD_DIM)
    ang = jnp.arange(length, dtype=F32)[:, None] * inv_freq[None, :]
    lane = np.arange(LANES)
    cos_t = jnp.cos(ang)[:, lane % 32]
    sin_t = jnp.sin(ang)[:, lane % 32] * jnp.asarray(np.where(lane < 64, -1.0, 1.0), F32)
    return cos_t, sin_t


def kernel(x, s5_lambda_re, s5_lambda_im, s5_log_dt, s5_b_re, s5_b_im, s5_c_re, s5_c_im, s5_d, s5_w_glu, dsa_w_in, dsa_q_norm, dsa_k_norm, dsa_w_o, ffn_w_gate_up, ffn_w_down, norm_mix, norm_ffn):
    bsz, length, d = x.shape
    depth = norm_mix.shape[0]
    topk = min(TOPK_MAX, length // 4)
    nchunk = length // S5_CHUNK
    cos_t, sin_t = _rope_lane_tables(length)
    outs = []
    for b in range(bsz):
        xs = x[b].astype(F32)
        for i in range(depth):
            j = i // 2
            wgu = ffn_w_gate_up[i].astype(MXU_DTYPE)
            wd = ffn_w_down[i].astype(MXU_DTYPE)
            if i % 2 == 0:
                tables = _s5_tables(s5_lambda_re[j], s5_lambda_im[j], s5_log_dt[j], s5_b_re[j],
                                    s5_b_im[j], s5_c_re[j], s5_c_im[j])
                xt = xs.reshape(nchunk, S5_CHUNK, d).transpose(1, 0, 2).reshape(length, d)
                h_t = _norm(xt, norm_mix[i]).reshape(S5_CHUNK, nchunk, d)
                g_t = _s5_scan(h_t, tables, s5_d[j]).reshape(length, d)
                xt = _glu_residual(xt, g_t, s5_w_glu[j].astype(MXU_DTYPE))
                xt = _ffn(xt, norm_ffn[i], wgu, wd)
                xs = xt.reshape(S5_CHUNK, nchunk, d).transpose(1, 0, 2).reshape(length, d)
            else:
                q, k, vt, qi, ki, wi = _dsa_project(xs, norm_mix[i], dsa_w_in[j], dsa_q_norm[j],
                                                    dsa_k_norm[j], cos_t, sin_t)
                mask_t = _dsa_select_t(qi, wi[:, :IDX_HEADS].T, ki, topk)
                att_t = _dsa_attend_t(q, k, vt, mask_t)
                xs = _out_residual_t(xs, att_t, dsa_w_o[j].astype(MXU_DTYPE))
                xs = _ffn(xs, norm_ffn[i], wgu, wd)
        outs.append(xs)
    return jnp.stack(outs, axis=0).astype(x.dtype)
```

```python
import functools
import math

import jax
import jax.numpy as jnp
import numpy as np
from jax import lax
from jax.experimental import pallas as pl
from jax.experimental.pallas import tpu as pltpu

F32 = jnp.float32
BF16 = jnp.bfloat16
I32 = jnp.int32
MXU_DTYPE = BF16

D_MODEL = 1024
S5_GROUP = 16
S5_STATE = 64
N_HEADS = 16
HEAD_DIM = 64
IDX_HEADS = 8
IDX_DIM = 64
TOPK_MAX = 256
ROPE_THETA = 10000.0
EPS = 1e-6

LANES = 128
SUBLANES = 8
MXU_DIM = 256
VMEM_LIMIT = 56 * 1024 * 1024

S5_CHUNK = 16
S5_SLAB_GROUPS = LANES // S5_GROUP
NEG_BIG = -1e30
LOG2E = math.log2(math.e)
VT_PAD = 16
VT_ROWS = HEAD_DIM + VT_PAD

ROW_TM = 512
NORM_TM = 1024
S5_TC = 512
PROJ_TM = 512
SEL_TQ = 256
SEL_TK = 512
SEL_RB = 64
SEL_UNROLL = 32
SEL_UNROLL16 = 16
ATT_TQ = 512
ATT_TK = 512
ATT_QS = 512


def _cparams(sem, flags=None):
    return pltpu.CompilerParams(dimension_semantics=sem, vmem_limit_bytes=VMEM_LIMIT, flags=flags)


def _rms(x, gain=None):
    y = x * lax.rsqrt(jnp.mean(x * x, axis=-1, keepdims=True) + EPS)
    return y if gain is None else y * gain


def _dot(a, b):
    return jnp.dot(a, b, preferred_element_type=F32)


def _dot_nt(a, b):
    return lax.dot_general(a, b, (((1,), (1,)), ((), ())), preferred_element_type=F32)


def _ffn_kernel(x_ref, g_ref, wgu_ref, wd_ref, o_ref, acc_ref, *, d_ff, fc):
    x = x_ref[...]
    h = _rms(x, g_ref[...]).astype(MXU_DTYPE)
    for c in range(d_ff // fc):
        g = _dot(h, wgu_ref[:, c * fc:(c + 1) * fc])
        u = _dot(h, wgu_ref[:, d_ff + c * fc:d_ff + (c + 1) * fc])
        a = (g * jax.nn.sigmoid(g) * u).astype(MXU_DTYPE)
        d = _dot(a, wd_ref[c * fc:(c + 1) * fc, :])
        if c == 0:
            acc_ref[...] = d
        else:
            acc_ref[...] += d
    o_ref[...] = x + acc_ref[...]


def _ffn(x, gain, wgu, wd):
    n, d = x.shape
    d_ff = wd.shape[0]
    tm = min(ROW_TM, n)
    fc = MXU_DIM
    return pl.pallas_call(
        functools.partial(_ffn_kernel, d_ff=d_ff, fc=fc),
        out_shape=jax.ShapeDtypeStruct((n, d), F32),
        grid=(n // tm,),
        in_specs=[
            pl.BlockSpec((tm, d), lambda i: (i, 0)),
            pl.BlockSpec((1, d), lambda i: (0, 0)),
            pl.BlockSpec((d, 2 * d_ff), lambda i: (0, 0)),
            pl.BlockSpec((d_ff, d), lambda i: (0, 0)),
        ],
        out_specs=pl.BlockSpec((tm, d), lambda i: (i, 0)),
        scratch_shapes=[pltpu.VMEM((tm, d), F32)],
        compiler_params=_cparams(("parallel",)),
        name="ffn",
    )(x, gain.reshape(1, d), wgu, wd)


def _norm_kernel(x_ref, g_ref, o_ref):
    o_ref[...] = _rms(x_ref[...], g_ref[...]).astype(o_ref.dtype)


def _norm(x, gain):
    n, d = x.shape
    tm = min(NORM_TM, n)
    return pl.pallas_call(
        _norm_kernel,
        out_shape=jax.ShapeDtypeStruct((n, d), MXU_DTYPE),
        grid=(n // tm,),
        in_specs=[pl.BlockSpec((tm, d), lambda i: (i, 0)),
                  pl.BlockSpec((1, d), lambda i: (0, 0))],
        out_specs=pl.BlockSpec((tm, d), lambda i: (i, 0)),
        compiler_params=_cparams(("parallel",)),
        name="norm",
    )(x, gain.reshape(1, d))


def _s5_tables(lam_re, lam_im, log_dt, b_re, b_im, c_re, c_im):
    hp = lax.Precision.HIGHEST
    g, p = lam_re.shape
    h = S5_GROUP
    nsl = g // S5_SLAB_GROUPS
    sg = S5_SLAB_GROUPS
    t = S5_CHUNK
    lam_re, lam_im, log_dt = lam_re.astype(F32), lam_im.astype(F32), log_dt.astype(F32)
    b_re, b_im, c_re, c_im = (a.astype(F32) for a in (b_re, b_im, c_re, c_im))
    dt = jnp.exp(log_dt)[:, None]

    def apow(k):
        k = jnp.asarray(k, F32).reshape((-1, 1, 1))
        mag = jnp.exp(lam_re[None] * dt[None] * k)
        ang = lam_im[None] * dt[None] * k
        return mag * jnp.cos(ang), mag * jnp.sin(ang)

    ar, ai = apow([1.0])
    ar, ai = ar[0], ai[0]
    den = lam_re * lam_re + lam_im * lam_im
    nr, ni = ar - 1.0, ai
    qr = (nr * lam_re + ni * lam_im) / den
    qi = (ni * lam_re - nr * lam_im) / den
    bbr = qr[..., None] * b_re - qi[..., None] * b_im
    bbi = qr[..., None] * b_im + qi[..., None] * b_re

    pr, pi = apow(np.arange(t + 1))
    mr = c_re[None] * pr[:, :, None, :] - c_im[None] * pi[:, :, None, :]
    mi = c_re[None] * pi[:, :, None, :] + c_im[None] * pr[:, :, None, :]

    kk = (jnp.einsum('tghp,gpk->tghk', mr[:t], bbr, precision=hp)
          - jnp.einsum('tghp,gpk->tghk', mi[:t], bbi, precision=hp))
    kp = jnp.concatenate([jnp.zeros_like(kk[:1]), kk], axis=0)
    eye = jnp.eye(sg, dtype=F32)

    dl = np.arange(t // 2)[:, None, None]
    sl = np.arange(2)[None, :, None]
    jl = np.arange(2)[None, None, :]
    idx = 2 * dl + jl - sl + 1
    kg = kp[idx]
    kg = kg.reshape(t // 2, 2, 2, nsl, sg, h, h)
    tp = jnp.einsum('dljbgok,gm->bdlgkjmo', kg, eye)
    tp = tp.reshape(nsl, t // 2, 2 * sg * h, 2 * sg * h)

    prs, pis = pr[t - 1::-1][:t], pi[t - 1::-1][:t]
    er = prs[..., None] * bbr[None] - pis[..., None] * bbi[None]
    ei = prs[..., None] * bbi[None] + pis[..., None] * bbr[None]
    bf = jnp.stack([er, ei], axis=2)
    bf = bf.reshape(t // 2, 2, nsl, sg, 2, p, h)
    bz = jnp.einsum('zlbgrpk,gm->bzlgkrmp', bf, eye)
    bz = bz.reshape(nsl, t // 2, 2 * sg * h, 2 * sg * p)

    cf = jnp.stack([mr[1:], -mi[1:]], axis=2)
    cf = cf.reshape(t // 2, 2, nsl, sg, 2, h, p)
    cz = jnp.einsum('ijbgrop,gm->birgpjmo', cf, eye)
    cz = cz.reshape(nsl, t // 2, 2 * sg * p, 2 * sg * h)

    def slab_state(re, im):
        k = re.shape[0]
        x = jnp.stack([re, im], axis=1).reshape(k, 2, nsl, sg * p)
        return x.transpose(2, 0, 1, 3).reshape(nsl, k, 2 * sg * p)

    ad = slab_state(*apow([t * 1.0, t * 2.0, t * 4.0]))
    ap8 = slab_state(*apow(t * (np.arange(SUBLANES) + 1.0)))
    return (tp.astype(MXU_DTYPE), bz.astype(MXU_DTYPE), cz.astype(MXU_DTYPE), ad, ap8)


def _s5_kernel(h_ref, tp_ref, bz_ref, cz_ref, ad_ref, ap8_ref, dsk_ref, o_ref,
               carry_ref, z_ref, xp_ref, *, tc):
    half = z_ref.shape[1] // 2
    npair = S5_CHUNK // 2

    @pl.when(pl.program_id(1) == 0)
    def _():
        carry_ref[...] = jnp.zeros_like(carry_ref)

    u = [jnp.concatenate([h_ref[2 * s], h_ref[2 * s + 1]], axis=1) for s in range(npair)]

    z = _dot(u[0], bz_ref[0])
    for s in range(1, npair):
        z = z + _dot(u[s], bz_ref[s])
    z_ref[...] = z

    row = lax.broadcasted_iota(I32, (SUBLANES, half), 0)
    pr8, pi8 = ap8_ref[:, :half], ap8_ref[:, half:]

    def tile_step(t, carry):
        r0 = pl.multiple_of(t * SUBLANES, SUBLANES)
        zt = z_ref[pl.ds(r0, SUBLANES), :]
        xr, xi = zt[:, :half], zt[:, half:]
        for k, d in enumerate((1, 2, 4)):
            a = ad_ref[k:k + 1, :]
            a_r, a_i = a[:, :half], a[:, half:]
            sr = jnp.where(row >= d, pltpu.roll(xr, d, 0), 0.0)
            si = jnp.where(row >= d, pltpu.roll(xi, d, 0), 0.0)
            xr, xi = xr + a_r * sr - a_i * si, xi + a_r * si + a_i * sr
        cr, ci = carry[:, :half], carry[:, half:]
        xr, xi = xr + pr8 * cr - pi8 * ci, xi + pr8 * ci + pi8 * cr
        xpr = jnp.where(row >= 1, pltpu.roll(xr, 1, 0), cr)
        xpi = jnp.where(row >= 1, pltpu.roll(xi, 1, 0), ci)
        xp_ref[pl.ds(r0, SUBLANES), :] = jnp.concatenate([xpr, xpi], axis=1)
        return jnp.concatenate([xr[SUBLANES - 1:], xi[SUBLANES - 1:]], axis=1)

    carry_ref[...] = lax.fori_loop(0, tc // SUBLANES, tile_step, carry_ref[...])

    xp = xp_ref[...].astype(MXU_DTYPE)
    dsk = dsk_ref[...]
    for i in range(npair):
        y = _dot(xp, cz_ref[i])
        for s in range(i + 1):
            y = y + _dot(u[s], tp_ref[i - s])
        for jl in range(2):
            j = 2 * i + jl
            yj = y[:, jl * LANES:(jl + 1) * LANES] + dsk * h_ref[j].astype(F32)
            o_ref[j] = jax.nn.gelu(yj).astype(o_ref.dtype)


def _s5_scan(h_t, tables, d_skip):
    tp, bz, cz, ad, ap8 = tables
    t, c, d = h_t.shape
    nsl = d // LANES
    tc = min(S5_TC, c)
    st = bz.shape[-1]
    dsk = d_skip.astype(F32).reshape(nsl, 1, LANES)
    return pl.pallas_call(
        functools.partial(_s5_kernel, tc=tc),
        out_shape=jax.ShapeDtypeStruct((t, c, d), MXU_DTYPE),
        grid=(nsl, c // tc),
        in_specs=[
            pl.BlockSpec((t, tc, LANES), lambda b, i: (0, i, b)),
            pl.BlockSpec((None,) + tp.shape[1:], lambda b, i: (b, 0, 0, 0)),
            pl.BlockSpec((None,) + bz.shape[1:], lambda b, i: (b, 0, 0, 0)),
            pl.BlockSpec((None,) + cz.shape[1:], lambda b, i: (b, 0, 0, 0)),
            pl.BlockSpec((None,) + ad.shape[1:], lambda b, i: (b, 0, 0)),
            pl.BlockSpec((None,) + ap8.shape[1:], lambda b, i: (b, 0, 0)),
            pl.BlockSpec((None, 1, LANES), lambda b, i: (b, 0, 0)),
        ],
        out_specs=pl.BlockSpec((t, tc, LANES), lambda b, i: (0, i, b)),
        scratch_shapes=[pltpu.VMEM((1, st), F32), pltpu.VMEM((tc, st), F32),
                        pltpu.VMEM((tc, st), F32)],
        compiler_params=_cparams(("arbitrary", "arbitrary")),
        name="s5_scan",
    )(h_t, tp, bz, cz, ad, ap8, dsk)


def _glu_kernel(x_ref, g_ref, w_ref, o_ref):
    d = x_ref.shape[1]
    vg = _dot(g_ref[...], w_ref[...])
    o_ref[...] = x_ref[...] + vg[:, :d] * jax.nn.sigmoid(vg[:, d:])


def _glu_residual(x, g, w):
    n, d = x.shape
    tm = min(ROW_TM, n)
    return pl.pallas_call(
        _glu_kernel,
        out_shape=jax.ShapeDtypeStruct((n, d), F32),
        grid=(n // tm,),
        in_specs=[pl.BlockSpec((tm, d), lambda i: (i, 0)),
                  pl.BlockSpec((tm, d), lambda i: (i, 0)),
                  pl.BlockSpec((d, 2 * d), lambda i: (0, 0))],
        out_specs=pl.BlockSpec((tm, d), lambda i: (i, 0)),
        compiler_params=_cparams(("parallel",)),
        name="glu",
    )(x, g, w)


def _head_perm(n_heads):
    n = np.arange(n_heads * HEAD_DIM)
    pb, r = n // LANES, n % LANES
    half, r2 = r // 64, r % 64
    hl, dp = r2 // 32, r2 % 32
    return (2 * pb + hl) * HEAD_DIM + 32 * half + dp


def _lane_head_mask(shape, hl):
    lane = lax.broadcasted_iota(I32, shape, len(shape) - 1)
    return ((lane % 64) // 32) == hl


def _proj_kernel(x_ref, g_ref, w_ref, wvt_ref, gq_ref, gk_ref, cos_ref, sin_ref, hm_ref,
                 q_ref, k_ref, vt_ref, qi_ref, ki_ref, wi_ref, *, d, dqi, att_scale, w_scale):
    h = _rms(x_ref[...], g_ref[...]).astype(MXU_DTYPE)
    cos, sin = cos_ref[...], sin_ref[...]
    hm = hm_ref[...]

    def rope(t):
        return t * cos + pltpu.roll(t, 64, 1) * sin

    def headnorm_rope(col0, gain_ref, out_ref, scale):
        for sb in range(d // MXU_DIM):
            c0 = sb * MXU_DIM
            t = _dot(h, w_ref[:, col0 + c0:col0 + c0 + MXU_DIM])
            sq = t * t
            hi = sq.astype(MXU_DTYPE)
            lo = (sq - hi.astype(F32)).astype(MXU_DTYPE)
            ss = _dot(hi, hm) + _dot(lo, hm)
            tn = t * lax.rsqrt(ss * (1.0 / HEAD_DIM) + EPS) * gain_ref[:, c0:c0 + MXU_DIM]
            for b in range(MXU_DIM // LANES):
                r = rope(tn[:, b * LANES:(b + 1) * LANES])
                if scale != 1.0:
                    r = r * scale
                out_ref[:, c0 + b * LANES:c0 + (b + 1) * LANES] = r.astype(out_ref.dtype)

    headnorm_rope(0, gq_ref, q_ref, att_scale)
    headnorm_rope(d, gk_ref, k_ref, 1.0)
    vt = _dot_nt(wvt_ref[...], h)
    row = lax.broadcasted_iota(I32, vt.shape, 0)
    vt_ref[...] = jnp.where(row % VT_ROWS >= HEAD_DIM, 1.0, vt).astype(vt_ref.dtype)
    c0 = 2 * d
    t = _dot(h, w_ref[:, c0:c0 + dqi])
    for b in range(dqi // LANES):
        qi_ref[:, b * LANES:(b + 1) * LANES] = rope(t[:, b * LANES:(b + 1) * LANES]).astype(qi_ref.dtype)
    c0 += dqi
    t = _dot(h, w_ref[:, c0:c0 + LANES])
    ms = jnp.sum(t * t, axis=-1, keepdims=True) * (0.5 / IDX_DIM)
    ki_ref[...] = rope(t * lax.rsqrt(ms + EPS)).astype(ki_ref.dtype)
    c0 += LANES
    wi_ref[...] = _dot(h, w_ref[:, c0:c0 + LANES]) * w_scale


def _dsa_project(x, gain, w_in, q_gain, k_gain, cos_t, sin_t):
    n, d = x.shape
    dqi = IDX_HEADS * IDX_DIM
    pq = _head_perm(N_HEADS)
    pqi = _head_perm(IDX_HEADS)
    wq = w_in[:, 0:d][:, pq]
    wk = w_in[:, d:2 * d][:, pq]
    wvt = w_in[:, 2 * d:3 * d].T.reshape(N_HEADS, HEAD_DIM, d)
    wvt = jnp.pad(wvt, ((0, 0), (0, VT_PAD), (0, 0))).reshape(N_HEADS * VT_ROWS, d).astype(MXU_DTYPE)
    dvt = N_HEADS * VT_ROWS
    wqi = w_in[:, 3 * d:3 * d + dqi][:, pqi]
    lane = np.arange(LANES)
    wki = w_in[:, 3 * d + dqi:3 * d + dqi + IDX_DIM][:, 32 * (lane // 64) + lane % 32]
    wwi = jnp.pad(w_in[:, 3 * d + dqi + IDX_DIM:], ((0, 0), (0, LANES - IDX_HEADS)))
    w_all = jnp.concatenate([wq, wk, wqi, wki, wwi], axis=1).astype(MXU_DTYPE)
    dcol = (pq % HEAD_DIM)
    gq = q_gain.astype(F32)[dcol].reshape(1, d)
    gk = k_gain.astype(F32)[dcol].reshape(1, d)
    l2 = np.arange(MXU_DIM)
    hm = ((l2[:, None] // LANES == l2[None, :] // LANES)
          & ((l2[:, None] % 64) // 32 == (l2[None, :] % 64) // 32))
    hm = jnp.asarray(hm, MXU_DTYPE)
    tm = min(PROJ_TM, n)
    nw = w_all.shape[1]
    outs = pl.pallas_call(
        functools.partial(_proj_kernel, d=d, dqi=dqi, att_scale=HEAD_DIM ** -0.5 * LOG2E,
                          w_scale=(IDX_HEADS ** -0.5) * (IDX_DIM ** -0.5)),
        out_shape=[jax.ShapeDtypeStruct((n, d), MXU_DTYPE)] * 2
        + [jax.ShapeDtypeStruct((dvt, n), MXU_DTYPE),
           jax.ShapeDtypeStruct((n, dqi), MXU_DTYPE),
           jax.ShapeDtypeStruct((n, LANES), MXU_DTYPE),
           jax.ShapeDtypeStruct((n, LANES), F32)],
        grid=(n // tm,),
        in_specs=[
            pl.BlockSpec((tm, d), lambda i: (i, 0)),
            pl.BlockSpec((1, d), lambda i: (0, 0)),
            pl.BlockSpec((d, nw), lambda i: (0, 0)),
            pl.BlockSpec((dvt, d), lambda i: (0, 0)),
            pl.BlockSpec((1, d), lambda i: (0, 0)),
            pl.BlockSpec((1, d), lambda i: (0, 0)),
            pl.BlockSpec((tm, LANES), lambda i: (i, 0)),
            pl.BlockSpec((tm, LANES), lambda i: (i, 0)),
            pl.BlockSpec((MXU_DIM, MXU_DIM), lambda i: (0, 0)),
        ],
        out_specs=[pl.BlockSpec((tm, d), lambda i: (i, 0))] * 2
        + [pl.BlockSpec((dvt, tm), lambda i: (0, i)),
           pl.BlockSpec((tm, dqi), lambda i: (i, 0)),
           pl.BlockSpec((tm, LANES), lambda i: (i, 0)),
           pl.BlockSpec((tm, LANES), lambda i: (i, 0))],
        compiler_params=_cparams(("parallel",)),
        name="dsa_proj",
    )(x, gain.reshape(1, d), w_all, wvt, gq, gk, cos_t, sin_t, hm)
    return outs


KEY_NEG_INF = -2139095041
KEY_POS_INF = 2139095040
KEY16_NEG_INF = -32641
KEY16_POS_INF = 32640


def _key_to_f32(key):
    bits = key ^ ((key >> 31) & 0x7FFFFFFF)
    return lax.bitcast_convert_type(bits, F32)


def _select_kernel(qi_ref, wi_ref, ki_ref, mask_ref, s_ref, lo_ref, hi_ref, clo_ref, chi_ref,
                   *, tq, tk, topk, rb):
    qb = pl.program_id(0)
    n_all = mask_ref.shape[1]
    nkt = (qb * tq) // tk + 1
    ncol = nkt * (tk // LANES)

    qm = []
    for hh in range(IDX_HEADS):
        blk = qi_ref[:, (hh // 2) * LANES:(hh // 2 + 1) * LANES]
        qm.append(jnp.where(_lane_head_mask(blk.shape, hh % 2), blk, jnp.zeros_like(blk)))
    qm = jnp.concatenate(qm, axis=0)
    wv = wi_ref[...]
    row_pos = qb * tq + lax.broadcasted_iota(I32, (tq, tk), 0)
    col_in = lax.broadcasted_iota(I32, (tq, tk), 1)

    def score_tile(kt, _):
        k0 = pl.multiple_of(kt * tk, tk)
        s = _dot_nt(qm, ki_ref[pl.ds(k0, tk), :])
        acc = wv[:, 0:1] * jnp.maximum(s[0:tq], 0.0)
        for hh in range(1, IDX_HEADS):
            acc = acc + wv[:, hh:hh + 1] * jnp.maximum(s[hh * tq:(hh + 1) * tq], 0.0)
        acc = jnp.where(col_in + k0 <= row_pos, acc, -jnp.inf)
        s_ref[:, pl.ds(k0, tk)] = acc
        return 0

    lax.fori_loop(0, nkt, score_tile, 0)

    lo_ref[...] = jnp.full(lo_ref.shape, KEY_NEG_INF, I32)
    hi_ref[...] = jnp.full(hi_ref.shape, KEY_POS_INF, I32)
    clo_ref[...] = jnp.full(clo_ref.shape, 1, I32) * (ncol * LANES)
    chi_ref[...] = jnp.zeros(chi_ref.shape, I32)

    def count_rows(r0, pred):
        def body(j, acc):
            c0 = pl.multiple_of(j * LANES, LANES)
            return acc + pred(s_ref[r0:r0 + rb, pl.ds(c0, LANES)], j).astype(I32)
        acc = lax.fori_loop(0, ncol, body, jnp.zeros((rb, LANES), I32))
        return jnp.broadcast_to(jnp.sum(acc, axis=1, keepdims=True), (rb, LANES))

    def bisect_step(carry):
        it, _ = carry
        pending = jnp.zeros((), I32)
        for r0 in range(0, tq, rb):
            lo, hi = lo_ref[r0:r0 + rb, :], hi_ref[r0:r0 + rb, :]
            mid = (lo & hi) + ((lo ^ hi) >> 1)
            active = mid != lo
            cand = _key_to_f32(mid)
            cnt = count_rows(r0, lambda blk, j: blk >= cand)
            ge = cnt >= topk
            up = active & ge
            dn = active & jnp.logical_not(ge)
            hit = active & (cnt == topk)
            lo_ref[r0:r0 + rb, :] = jnp.where(up, mid, lo)
            hi_ref[r0:r0 + rb, :] = jnp.where(hit, mid + 1, jnp.where(dn, mid, hi))
            clo_ref[r0:r0 + rb, :] = jnp.where(up, cnt, clo_ref[r0:r0 + rb, :])
            chi_ref[r0:r0 + rb, :] = jnp.where(dn, cnt, chi_ref[r0:r0 + rb, :])
            pending = jnp.maximum(pending, jnp.max(active.astype(I32)))
        return it + 1, pending

    lax.while_loop(lambda c: (c[0] < 40) & (c[1] > 0), bisect_step,
                   (jnp.zeros((), I32), jnp.ones((), I32)))

    tie = (clo_ref[...] > topk) & (lo_ref[...] > KEY_NEG_INF)
    any_tie = jnp.max(tie.astype(I32))

    @pl.when(any_tie == 0)
    def _():
        hi_ref[...] = jnp.full(hi_ref.shape, n_all, I32)

    @pl.when(any_tie > 0)
    def _():
        need = topk - chi_ref[...]
        chi_ref[...] = need
        hi_ref[...] = jnp.full(hi_ref.shape, ncol * LANES - 1, I32)
        clo_ref[...] = jnp.full(clo_ref.shape, -1, I32)
        lane = lax.broadcasted_iota(I32, (rb, LANES), 1)

        def tie_step(_, c):
            for r0 in range(0, tq, rb):
                jl, jh = clo_ref[r0:r0 + rb, :], hi_ref[r0:r0 + rb, :]
                mid = jl + ((jh - jl) >> 1)
                active = (jh - jl) > 1
                thr = _key_to_f32(lo_ref[r0:r0 + rb, :])
                cnt = count_rows(r0, lambda blk, j: (blk == thr) & (lane + j * LANES <= mid))
                ok = cnt >= chi_ref[r0:r0 + rb, :]
                hi_ref[r0:r0 + rb, :] = jnp.where(active & ok, mid, jh)
                clo_ref[r0:r0 + rb, :] = jnp.where(active & jnp.logical_not(ok), mid, jl)
            return c

        lax.fori_loop(0, int(math.ceil(math.log2(n_all))) + 1, tie_step, 0)
        keep_all = jnp.logical_not(tie)
        hi_ref[...] = jnp.where(keep_all, n_all, hi_ref[...])

    lane_k = lax.broadcasted_iota(I32, (tq, tk), 1)
    thr_col = _key_to_f32(lo_ref[:, 0:1])
    last_col = hi_ref[:, 0:1]

    def mask_tile(kt, _):
        k0 = pl.multiple_of(kt * tk, tk)
        s = s_ref[:, pl.ds(k0, tk)]
        sel = ((s > thr_col) | ((s == thr_col) & (lane_k + k0 <= last_col))) & (s > -jnp.inf)
        mask_ref[:, pl.ds(k0, tk)] = jnp.where(sel, 1, 0).astype(mask_ref.dtype)
        return 0

    lax.fori_loop(0, nkt, mask_tile, 0)

    def zero_tile(kt, _):
        k0 = pl.multiple_of(kt * tk, tk)
        mask_ref[:, pl.ds(k0, tk)] = jnp.zeros((tq, tk), mask_ref.dtype)
        return 0

    lax.fori_loop(nkt, n_all // tk, zero_tile, 0)


def _dsa_select(qi, wi, ki, topk):
    n = qi.shape[0]
    tq = min(SEL_TQ, n)
    tk = min(SEL_TK, n)
    rb = min(SEL_RB, tq)
    return pl.pallas_call(
        functools.partial(_select_kernel, tq=tq, tk=tk, topk=topk, rb=rb),
        out_shape=jax.ShapeDtypeStruct((n, n), jnp.int8),
        grid=(n // tq,),
        in_specs=[pl.BlockSpec((tq, qi.shape[1]), lambda i: (i, 0)),
                  pl.BlockSpec((tq, LANES), lambda i: (i, 0)),
                  pl.BlockSpec((n, LANES), lambda i: (0, 0))],
        out_specs=pl.BlockSpec((tq, n), lambda i: (i, 0)),
        scratch_shapes=[pltpu.VMEM((tq, n), F32)] + [pltpu.VMEM((tq, LANES), I32)] * 4,
        compiler_params=_cparams(("parallel",)),
        name="dsa_select",
    )(qi, wi, ki)


def _attn_kernel(qb_tab, kt_tab, q_ref, k_ref, v_ref, mask_ref, o_ref, acc_ref, m_ref, l_ref,
                 *, tq, tk):
    step = pl.program_id(0)
    kt = kt_tab[step]
    d = q_ref.shape[1]
    slab = MXU_DIM
    heads_per_slab = slab // HEAD_DIM

    @pl.when(kt == 0)
    def _():
        acc_ref[...] = jnp.zeros_like(acc_ref)
        m_ref[...] = jnp.full(m_ref.shape, NEG_BIG, F32)
        l_ref[...] = jnp.zeros_like(l_ref)

    bias = jnp.where(mask_ref[...].astype(I32) != 0, 0.0, NEG_BIG)
    lane_s = lax.broadcasted_iota(I32, (tq, slab), 1) // HEAD_DIM

    for sb in range(d // slab):
        vs = v_ref[:, sb * slab:(sb + 1) * slab]
        lane_v = lax.broadcasted_iota(I32, vs.shape, 1) // HEAD_DIM
        alpha_l = jnp.zeros((tq, slab), F32)
        pv = jnp.zeros((tq, slab), F32)
        for i in range(heads_per_slab):
            hd = sb * heads_per_slab + i
            blk = hd // 2
            qblk = q_ref[:, blk * LANES:(blk + 1) * LANES]
            qm = jnp.where(_lane_head_mask(qblk.shape, hd % 2), qblk, jnp.zeros_like(qblk))
            s = _dot_nt(qm, k_ref[:, blk * LANES:(blk + 1) * LANES]) + bias
            m_old = m_ref[:, hd:hd + 1]
            m_new = jnp.maximum(m_old, jnp.max(s, axis=1, keepdims=True))
            alpha = jnp.exp(m_old - m_new)
            p = jnp.exp(s - m_new)
            l_ref[:, hd:hd + 1] = l_ref[:, hd:hd + 1] * alpha + jnp.sum(p, axis=1, keepdims=True)
            m_ref[:, hd:hd + 1] = m_new
            vm = jnp.where(lane_v == i, vs, jnp.zeros_like(vs))
            pv = pv + _dot(p.astype(MXU_DTYPE), vm)
            alpha_l = jnp.where(lane_s == i, alpha, alpha_l)
        acc_ref[:, sb * slab:(sb + 1) * slab] = acc_ref[:, sb * slab:(sb + 1) * slab] * alpha_l + pv

    last = kt == ((qb_tab[step] + 1) * tq - 1) // tk

    @pl.when(last)
    def _():
        lane_h = lax.broadcasted_iota(I32, (tq, d), 1) // HEAD_DIM
        l_l = jnp.zeros((tq, d), F32)
        for hd in range(N_HEADS):
            l_l = jnp.where(lane_h == hd, l_ref[:, hd:hd + 1], l_l)
        o_ref[...] = (acc_ref[...] / l_l).astype(o_ref.dtype)


def _dsa_attend(q, k, v, mask):
    n, d = q.shape
    tq = min(ATT_TQ, n)
    tk = min(ATT_TK, n)
    pairs = [(qb, kt) for qb in range(n // tq) for kt in range(((qb + 1) * tq - 1) // tk + 1)]
    qb_tab = jnp.asarray([p[0] for p in pairs], I32)
    kt_tab = jnp.asarray([p[1] for p in pairs], I32)
    grid_spec = pltpu.PrefetchScalarGridSpec(
        num_scalar_prefetch=2,
        grid=(len(pairs),),
        in_specs=[
            pl.BlockSpec((tq, d), lambda i, qt, kt: (qt[i], 0)),
            pl.BlockSpec((tk, d), lambda i, qt, kt: (kt[i], 0)),
            pl.BlockSpec((tk, d), lambda i, qt, kt: (kt[i], 0)),
            pl.BlockSpec((tq, tk), lambda i, qt, kt: (qt[i], kt[i])),
        ],
        out_specs=pl.BlockSpec((tq, d), lambda i, qt, kt: (qt[i], 0)),
        scratch_shapes=[pltpu.VMEM((tq, d), F32), pltpu.VMEM((tq, LANES), F32),
                        pltpu.VMEM((tq, LANES), F32)],
    )
    return pl.pallas_call(
        functools.partial(_attn_kernel, tq=tq, tk=tk),
        out_shape=jax.ShapeDtypeStruct((n, d), MXU_DTYPE),
        grid_spec=grid_spec,
        compiler_params=_cparams(("arbitrary",)),
        name="dsa_attend",
    )(qb_tab, kt_tab, q, k, v, mask)


def _out_kernel(x_ref, a_ref, w_ref, o_ref):
    o_ref[...] = x_ref[...] + _dot(a_ref[...], w_ref[...])


def _out_residual(x, a, w):
    n, d = x.shape
    tm = min(ROW_TM, n)
    return pl.pallas_call(
        _out_kernel,
        out_shape=jax.ShapeDtypeStruct((n, d), F32),
        grid=(n // tm,),
        in_specs=[pl.BlockSpec((tm, d), lambda i: (i, 0)),
                  pl.BlockSpec((tm, d), lambda i: (i, 0)),
                  pl.BlockSpec((d, d), lambda i: (0, 0))],
        out_specs=pl.BlockSpec((tm, d), lambda i: (i, 0)),
        compiler_params=_cparams(("parallel",)),
        name="attn_out",
    )(x, a, w)


def _select_t_kernel(qi_ref, wit_ref, ki_ref, mask_ref, s_ref, s16_ref, lo_ref, hi_ref, clo_ref,
                     chi_ref, *, tq, tk, topk, unroll, unroll16):
    qb = pl.program_id(0)
    n_all = mask_ref.shape[0]
    nkt = (qb * tq + tq - 1) // tk + 1
    nrow = nkt * tk
    rows_it = SUBLANES * unroll

    qm = []
    for hh in range(IDX_HEADS):
        blk = qi_ref[:, (hh // 2) * LANES:(hh // 2 + 1) * LANES]
        qm.append(jnp.where(_lane_head_mask(blk.shape, hh % 2), blk, jnp.zeros_like(blk)))
    qm = jnp.concatenate(qm, axis=0)
    wt = wit_ref[...]
    key_in = lax.broadcasted_iota(I32, (tk, tq), 0)
    q_pos = qb * tq + lax.broadcasted_iota(I32, (tk, tq), 1)

    def score_tile(kt, _):
        k0 = pl.multiple_of(kt * tk, tk)
        s = _dot_nt(ki_ref[pl.ds(k0, tk), :], qm)
        acc = wt[0:1, :] * jnp.maximum(s[:, 0:tq], 0.0)
        for hh in range(1, IDX_HEADS):
            acc = acc + wt[hh:hh + 1, :] * jnp.maximum(s[:, hh * tq:(hh + 1) * tq], 0.0)
        sc = jnp.where(key_in + k0 <= q_pos, acc, -jnp.inf)
        s_ref[pl.ds(k0, tk), :] = sc
        hi_bits = lax.bitcast_convert_type(sc, I32) & -65536
        s16_ref[pl.ds(k0, tk), :] = lax.bitcast_convert_type(hi_bits, F32).astype(BF16)
        return 0

    lax.fori_loop(0, nkt, score_tile, 0)

    lo_ref[...] = jnp.full(lo_ref.shape, KEY16_NEG_INF, I32)
    hi_ref[...] = jnp.full(hi_ref.shape, KEY16_POS_INF + 1, I32)
    clo_ref[...] = jnp.full(clo_ref.shape, 1, I32) * nrow
    chi_ref[...] = jnp.zeros(chi_ref.shape, I32)
    rows16 = 2 * SUBLANES * unroll16
    one16 = jnp.ones((), BF16)
    zero16 = jnp.zeros((), BF16)

    def count16(cand):
        def body(i, acc):
            r0 = pl.multiple_of(i * rows16, rows16)
            blk = s16_ref[pl.ds(r0, rows16), :].reshape(unroll16, 2 * SUBLANES, tq)
            ones = jnp.where(blk >= cand[None], one16, zero16)
            part = ones[0]
            for u in range(1, unroll16):
                part = part + ones[u]
            return acc + part.astype(F32)
        acc = lax.fori_loop(0, nrow // rows16, body, jnp.zeros((2 * SUBLANES, tq), F32))
        tot = jnp.sum(acc, axis=0, keepdims=True).astype(I32)
        return jnp.broadcast_to(tot, (SUBLANES, tq))

    def coarse_step(carry):
        it, _ = carry
        lo, hi = lo_ref[...], hi_ref[...]
        mid = (lo + hi) >> 1
        active = mid != lo
        bits16 = (mid ^ ((mid >> 31) & 0x7FFF)) & 0xFFFF
        cand = lax.bitcast_convert_type(bits16 << 16, F32)
        cand = jnp.concatenate([cand, cand], axis=0).astype(BF16)
        cnt = count16(cand)
        ge = cnt >= topk
        up = active & ge
        dn = active & jnp.logical_not(ge)
        lo_ref[...] = jnp.where(up, mid, lo)
        hi_ref[...] = jnp.where(dn, mid, hi)
        clo_ref[...] = jnp.where(up, cnt, clo_ref[...])
        chi_ref[...] = jnp.where(dn, cnt, chi_ref[...])
        return it + 1, jnp.max(active.astype(I32))

    lax.while_loop(lambda c: (c[0] < 20) & (c[1] > 0), coarse_step,
                   (jnp.zeros((), I32), jnp.ones((), I32)))

    lo16 = lo_ref[...]
    none_finite = lo16 == KEY16_NEG_INF
    lo_ref[...] = jnp.where(none_finite, KEY_NEG_INF, lo16 << 16)
    hi_ref[...] = jnp.where(none_finite, KEY_NEG_INF + 1, (lo16 + 1) << 16)

    def count(pred):
        def body(i, acc):
            r0 = pl.multiple_of(i * rows_it, rows_it)
            blk = s_ref[pl.ds(r0, rows_it), :].reshape(unroll, SUBLANES, tq)
            return acc + jnp.sum(pred(blk, r0).astype(I32), axis=0)
        acc = lax.fori_loop(0, nrow // rows_it, body, jnp.zeros((SUBLANES, tq), I32))
        return jnp.broadcast_to(jnp.sum(acc, axis=0, keepdims=True), (SUBLANES, tq))

    def bisect_step(carry):
        it, _ = carry
        lo, hi = lo_ref[...], hi_ref[...]
        mid = (lo & hi) + ((lo ^ hi) >> 1)
        active = mid != lo
        cand = _key_to_f32(mid)
        cnt = count(lambda blk, r0: blk >= cand[None])
        ge = cnt >= topk
        up = active & ge
        dn = active & jnp.logical_not(ge)
        hit = active & (cnt == topk)
        lo_ref[...] = jnp.where(up, mid, lo)
        hi_ref[...] = jnp.where(hit, mid + 1, jnp.where(dn, mid, hi))
        clo_ref[...] = jnp.where(up, cnt, clo_ref[...])
        chi_ref[...] = jnp.where(dn, cnt, chi_ref[...])
        return it + 1, jnp.max(active.astype(I32))

    lax.while_loop(lambda c: (c[0] < 40) & (c[1] > 0), bisect_step,
                   (jnp.zeros((), I32), jnp.ones((), I32)))

    tie = (clo_ref[...] > topk) & (lo_ref[...] > KEY_NEG_INF)
    any_tie = jnp.max(tie.astype(I32))

    @pl.when(any_tie == 0)
    def _():
        hi_ref[...] = jnp.full(hi_ref.shape, n_all, I32)

    @pl.when(any_tie > 0)
    def _():
        chi_ref[...] = topk - chi_ref[...]
        hi_ref[...] = jnp.full(hi_ref.shape, 1, I32) * (nrow - 1)
        clo_ref[...] = jnp.full(clo_ref.shape, -1, I32)
        sub = (lax.broadcasted_iota(I32, (unroll, SUBLANES, tq), 0) * SUBLANES
               + lax.broadcasted_iota(I32, (unroll, SUBLANES, tq), 1))
        thr = _key_to_f32(lo_ref[...])

        def tie_step(_, c):
            jl, jh = clo_ref[...], hi_ref[...]
            mid = jl + ((jh - jl) >> 1)
            active = (jh - jl) > 1
            cnt = count(lambda blk, r0: (blk == thr[None]) & (sub + r0 <= mid[None]))
            ok = cnt >= chi_ref[...]
            hi_ref[...] = jnp.where(active & ok, mid, jh)
            clo_ref[...] = jnp.where(active & jnp.logical_not(ok), mid, jl)
            return c

        lax.fori_loop(0, int(math.ceil(math.log2(n_all))) + 1, tie_step, 0)
        hi_ref[...] = jnp.where(tie, hi_ref[...], n_all)

    thr_row = _key_to_f32(lo_ref[0:1, :])
    last_row = hi_ref[0:1, :]

    def mask_tile(kt, _):
        k0 = pl.multiple_of(kt * tk, tk)
        s = s_ref[pl.ds(k0, tk), :]
        sel = ((s > thr_row) | ((s == thr_row) & (key_in + k0 <= last_row))) & (s > -jnp.inf)
        mask_ref[pl.ds(k0, tk), :] = jnp.where(sel, 1, 0).astype(mask_ref.dtype)
        return 0

    lax.fori_loop(0, nkt, mask_tile, 0)

    def zero_tile(kt, _):
        k0 = pl.multiple_of(kt * tk, tk)
        mask_ref[pl.ds(k0, tk), :] = jnp.zeros((tk, tq), mask_ref.dtype)
        return 0

    lax.fori_loop(nkt, n_all // tk, zero_tile, 0)


def _dsa_select_t(qi, wit, ki, topk):
    n = qi.shape[0]
    tq = min(SEL_TQ, n)
    tk = min(SEL_TK, n)
    return pl.pallas_call(
        functools.partial(_select_t_kernel, tq=tq, tk=tk, topk=topk, unroll=SEL_UNROLL,
                          unroll16=SEL_UNROLL16),
        out_shape=jax.ShapeDtypeStruct((n, n), jnp.int8),
        grid=(n // tq,),
        in_specs=[pl.BlockSpec((tq, qi.shape[1]), lambda i: (i, 0)),
                  pl.BlockSpec((IDX_HEADS, tq), lambda i: (0, i)),
                  pl.BlockSpec((n, LANES), lambda i: (0, 0))],
        out_specs=pl.BlockSpec((n, tq), lambda i: (0, i)),
        scratch_shapes=[pltpu.VMEM((n, tq), F32), pltpu.VMEM((n, tq), BF16)]
        + [pltpu.VMEM((SUBLANES, tq), I32)] * 4,
        compiler_params=_cparams(("parallel",)),
        name="dsa_select",
    )(qi, wit, ki)


def _attn_t_kernel(qb_tab, kt_tab, q_ref, k_ref, vt_ref, mask_ref, o_ref, acc_ref, m_ref,
                   *, tq, tk, qs):
    step = pl.program_id(0)
    kt = kt_tab[step]

    @pl.when(kt == 0)
    def _():
        acc_ref[...] = jnp.zeros_like(acc_ref)
        m_ref[...] = jnp.full(m_ref.shape, NEG_BIG, F32)

    bias = jnp.where(mask_ref[...].astype(I32) != 0, 0.0, NEG_BIG)

    def logits(hd):
        blk = hd // 2
        qblk = q_ref[:, blk * LANES:(blk + 1) * LANES]
        qm = jnp.where(_lane_head_mask(qblk.shape, hd % 2), qblk, jnp.zeros_like(qblk))
        return [_dot_nt(k_ref[k0:k0 + qs, blk * LANES:(blk + 1) * LANES], qm) + bias[k0:k0 + qs, :]
                for k0 in range(0, tk, qs)]

    s_next = logits(0)
    for hd in range(N_HEADS):
        r0 = hd * VT_ROWS
        s = s_next
        if hd + 1 < N_HEADS:
            s_next = logits(hd + 1)
        m_old = m_ref[hd:hd + 1, :]
        m_new = m_old
        for sp in s:
            m_new = jnp.maximum(m_new, jnp.max(sp, axis=0, keepdims=True))
        alpha = jnp.exp2(m_old - m_new)
        p = jnp.concatenate([jnp.exp2(sp - m_new).astype(MXU_DTYPE) for sp in s], axis=0)
        m_ref[hd:hd + 1, :] = m_new
        pv = _dot(vt_ref[r0:r0 + VT_ROWS, :], p)
        acc_ref[r0:r0 + VT_ROWS, :] = acc_ref[r0:r0 + VT_ROWS, :] * alpha + pv

    last = kt == ((qb_tab[step] + 1) * tq - 1) // tk

    @pl.when(last)
    def _():
        for hd in range(N_HEADS):
            r0 = hd * VT_ROWS
            o_ref[hd * HEAD_DIM:(hd + 1) * HEAD_DIM, :] = (
                acc_ref[r0:r0 + HEAD_DIM, :] / acc_ref[r0 + HEAD_DIM:r0 + HEAD_DIM + 1, :]
            ).astype(o_ref.dtype)


def _dsa_attend_t(q, k, vt, mask_t):
    n, d = q.shape
    tq = min(ATT_TQ, n)
    tk = min(ATT_TK, n)
    pairs = [(qb, kt) for qb in range(n // tq) for kt in range(((qb + 1) * tq - 1) // tk + 1)]
    qb_tab = jnp.asarray([p[0] for p in pairs], I32)
    kt_tab = jnp.asarray([p[1] for p in pairs], I32)
    grid_spec = pltpu.PrefetchScalarGridSpec(
        num_scalar_prefetch=2,
        grid=(len(pairs),),
        in_specs=[
            pl.BlockSpec((tq, d), lambda i, qt, kt: (qt[i], 0)),
            pl.BlockSpec((tk, d), lambda i, qt, kt: (kt[i], 0)),
            pl.BlockSpec((vt.shape[0], tk), lambda i, qt, kt: (0, kt[i])),
            pl.BlockSpec((tk, tq), lambda i, qt, kt: (kt[i], qt[i])),
        ],
        out_specs=pl.BlockSpec((d, tq), lambda i, qt, kt: (0, qt[i])),
        scratch_shapes=[pltpu.VMEM((vt.shape[0], tq), F32), pltpu.VMEM((N_HEADS, tq), F32)],
    )
    return pl.pallas_call(
        functools.partial(_attn_t_kernel, tq=tq, tk=tk, qs=min(ATT_QS, tq)),
        out_shape=jax.ShapeDtypeStruct((d, n), MXU_DTYPE),
        grid_spec=grid_spec,
        compiler_params=_cparams(("arbitrary",)),
        name="dsa_attend",
    )(qb_tab, kt_tab, q, k, vt, mask_t)


def _out_t_kernel(x_ref, at_ref, w_ref, o_ref):
    o_ref[...] = x_ref[...] + lax.dot_general(
        at_ref[...], w_ref[...], (((0,), (0,)), ((), ())), preferred_element_type=F32)


def _out_residual_t(x, at, w):
    n, d = x.shape
    tm = min(ROW_TM, n)
    return pl.pallas_call(
        _out_t_kernel,
        out_shape=jax.ShapeDtypeStruct((n, d), F32),
        grid=(n // tm,),
        in_specs=[pl.BlockSpec((tm, d), lambda i: (i, 0)),
                  pl.BlockSpec((d, tm), lambda i: (0, i)),
                  pl.BlockSpec((d, d), lambda i: (0, 0))],
        out_specs=pl.BlockSpec((tm, d), lambda i: (i, 0)),
        compiler_params=_cparams(("parallel",)),
        name="attn_out",
    )(x, at, w)


def _rope_lane_tables(length):
    inv_freq = ROPE_THETA ** (-jnp.arange(0, HEAD_DIM, 2, dtype=F32) / HEAD_DIM)
    ang = jnp.arange(length, dtype=F32)[:, None] * inv_freq[None, :]
    lane = np.arange(LANES)
    cos_t = jnp.cos(ang)[:, lane % 32]
    sin_t = jnp.sin(ang)[:, lane % 32] * jnp.asarray(np.where(lane < 64, -1.0, 1.0), F32)
    return cos_t, sin_t


def kernel(x, s5_lambda_re, s5_lambda_im, s5_log_dt, s5_b_re, s5_b_im, s5_c_re, s5_c_im, s5_d, s5_w_glu, dsa_w_in, dsa_q_norm, dsa_k_norm, dsa_w_o, ffn_w_gate_up, ffn_w_down, norm_mix, norm_ffn):
    bsz, length, d = x.shape
    depth = norm_mix.shape[0]
    topk = min(TOPK_MAX, length // 4)
    nchunk = length // S5_CHUNK
    cos_t, sin_t = _rope_lane_tables(length)
    outs = []
    for b in range(bsz):
        xs = x[b].astype(F32)
        for i in range(depth):
            j = i // 2
            wgu = ffn_w_gate_up[i].astype(MXU_DTYPE)
            wd = ffn_w_down[i].astype(MXU_DTYPE)
            if i % 2 == 0:
                tables = _s5_tables(s5_lambda_re[j], s5_lambda_im[j], s5_log_dt[j], s5_b_re[j],
                                    s5_b_im[j], s5_c_re[j], s5_c_im[j])
                xt = xs.reshape(nchunk, S5_CHUNK, d).transpose(1, 0, 2).reshape(length, d)
                h_t = _norm(xt, norm_mix[i]).reshape(S5_CHUNK, nchunk, d)
                g_t = _s5_scan(h_t, tables, s5_d[j]).reshape(length, d)
                xt = _glu_residual(xt, g_t, s5_w_glu[j].astype(MXU_DTYPE))
                xt = _ffn(xt, norm_ffn[i], wgu, wd)
                xs = xt.reshape(S5_CHUNK, nchunk, d).transpose(1, 0, 2).reshape(length, d)
            else:
                q, k, vt, qi, ki, wi = _dsa_project(xs, norm_mix[i], dsa_w_in[j], dsa_q_norm[j],
                                                    dsa_k_norm[j], cos_t, sin_t)
                mask_t = _dsa_select_t(qi, wi[:, :IDX_HEADS].T, ki, topk)
                att_t = _dsa_attend_t(q, k, vt, mask_t)
                xs = _out_residual_t(xs, att_t, dsa_w_o[j].astype(MXU_DTYPE))
                xs = _ffn(xs, norm_ffn[i], wgu, wd)
        outs.append(xs)
    return jnp.stack(outs, axis=0).astype(x.dtype)
```

```python
import functools
import math

import jax
import jax.numpy as jnp
import numpy as np
from jax import lax
from jax.experimental import pallas as pl
from jax.experimental.pallas import tpu as pltpu

F32 = jnp.float32
BF16 = jnp.bfloat16
I32 = jnp.int32
MXU_DTYPE = BF16

D_MODEL = 1024
S5_GROUP = 16
S5_STATE = 64
N_HEADS = 16
HEAD_DIM = 64
IDX_HEADS = 8
IDX_DIM = 64
TOPK_MAX = 256
ROPE_THETA = 10000.0
EPS = 1e-6

LANES = 128
SUBLANES = 8
MXU_DIM = 256
VMEM_LIMIT = 56 * 1024 * 1024

S5_CHUNK = 16
S5_SLAB_GROUPS = LANES // S5_GROUP
NEG_BIG = -1e30
LOG2E = math.log2(math.e)
VT_PAD = 16
VT_ROWS = HEAD_DIM + VT_PAD

ROW_TM = 512
NORM_TM = 1024
S5_TC = 512
PROJ_TM = 512
SEL_TQ = 256
SEL_TK = 512
SEL_RB = 64
SEL_UNROLL = 32
SEL_UNROLL16 = 16
ATT_TQ = 512
ATT_TK = 512
ATT_QS = 512


def _cparams(sem, flags=None):
    return pltpu.CompilerParams(dimension_semantics=sem, vmem_limit_bytes=VMEM_LIMIT, flags=flags)


def _rms(x, gain=None):
    y = x * lax.rsqrt(jnp.mean(x * x, axis=-1, keepdims=True) + EPS)
    return y if gain is None else y * gain


def _dot(a, b):
    return jnp.dot(a, b, preferred_element_type=F32)


def _dot_nt(a, b):
    return lax.dot_general(a, b, (((1,), (1,)), ((), ())), preferred_element_type=F32)


def _ffn_kernel(x_ref, g_ref, wgu_ref, wd_ref, o_ref, acc_ref, *, d_ff, fc):
    x = x_ref[...]
    h = _rms(x, g_ref[...]).astype(MXU_DTYPE)
    for c in range(d_ff // fc):
        g = _dot(h, wgu_ref[:, c * fc:(c + 1) * fc])
        u = _dot(h, wgu_ref[:, d_ff + c * fc:d_ff + (c + 1) * fc])
        a = (g * jax.nn.sigmoid(g) * u).astype(MXU_DTYPE)
        d = _dot(a, wd_ref[c * fc:(c + 1) * fc, :])
        if c == 0:
            acc_ref[...] = d
        else:
            acc_ref[...] += d
    o_ref[...] = x + acc_ref[...]


def _ffn(x, gain, wgu, wd):
    n, d = x.shape
    d_ff = wd.shape[0]
    tm = min(ROW_TM, n)
    fc = MXU_DIM
    return pl.pallas_call(
        functools.partial(_ffn_kernel, d_ff=d_ff, fc=fc),
        out_shape=jax.ShapeDtypeStruct((n, d), F32),
        grid=(n // tm,),
        in_specs=[
            pl.BlockSpec((tm, d), lambda i: (i, 0)),
            pl.BlockSpec((1, d), lambda i: (0, 0)),
            pl.BlockSpec((d, 2 * d_ff), lambda i: (0, 0)),
            pl.BlockSpec((d_ff, d), lambda i: (0, 0)),
        ],
        out_specs=pl.BlockSpec((tm, d), lambda i: (i, 0)),
        scratch_shapes=[pltpu.VMEM((tm, d), F32)],
        compiler_params=_cparams(("parallel",)),
        name="ffn",
    )(x, gain.reshape(1, d), wgu, wd)


def _norm_kernel(x_ref, g_ref, o_ref):
    o_ref[...] = _rms(x_ref[...], g_ref[...]).astype(o_ref.dtype)


def _norm(x, gain):
    n, d = x.shape
    tm = min(NORM_TM, n)
    return pl.pallas_call(
        _norm_kernel,
        out_shape=jax.ShapeDtypeStruct((n, d), MXU_DTYPE),
        grid=(n // tm,),
        in_specs=[pl.BlockSpec((tm, d), lambda i: (i, 0)),
                  pl.BlockSpec((1, d), lambda i: (0, 0))],
        out_specs=pl.BlockSpec((tm, d), lambda i: (i, 0)),
        compiler_params=_cparams(("parallel",)),
        name="norm",
    )(x, gain.reshape(1, d))


def _s5_tables(lam_re, lam_im, log_dt, b_re, b_im, c_re, c_im):
    hp = lax.Precision.HIGHEST
    g, p = lam_re.shape
    h = S5_GROUP
    nsl = g // S5_SLAB_GROUPS
    sg = S5_SLAB_GROUPS
    t = S5_CHUNK
    lam_re, lam_im, log_dt = lam_re.astype(F32), lam_im.astype(F32), log_dt.astype(F32)
    b_re, b_im, c_re, c_im = (a.astype(F32) for a in (b_re, b_im, c_re, c_im))
    dt = jnp.exp(log_dt)[:, None]

    def apow(k):
        k = jnp.asarray(k, F32).reshape((-1, 1, 1))
        mag = jnp.exp(lam_re[None] * dt[None] * k)
        ang = lam_im[None] * dt[None] * k
        return mag * jnp.cos(ang), mag * jnp.sin(ang)

    ar, ai = apow([1.0])
    ar, ai = ar[0], ai[0]
    den = lam_re * lam_re + lam_im * lam_im
    nr, ni = ar - 1.0, ai
    qr = (nr * lam_re + ni * lam_im) / den
    qi = (ni * lam_re - nr * lam_im) / den
    bbr = qr[..., None] * b_re - qi[..., None] * b_im
    bbi = qr[..., None] * b_im + qi[..., None] * b_re

    pr, pi = apow(np.arange(t + 1))
    mr = c_re[None] * pr[:, :, None, :] - c_im[None] * pi[:, :, None, :]
    mi = c_re[None] * pi[:, :, None, :] + c_im[None] * pr[:, :, None, :]

    kk = (jnp.einsum('tghp,gpk->tghk', mr[:t], bbr, precision=hp)
          - jnp.einsum('tghp,gpk->tghk', mi[:t], bbi, precision=hp))
    kp = jnp.concatenate([jnp.zeros_like(kk[:1]), kk], axis=0)
    eye = jnp.eye(sg, dtype=F32)

    dl = np.arange(t // 2)[:, None, None]
    sl = np.arange(2)[None, :, None]
    jl = np.arange(2)[None, None, :]
    idx = 2 * dl + jl - sl + 1
    kg = kp[idx]
    kg = kg.reshape(t // 2, 2, 2, nsl, sg, h, h)
    tp = jnp.einsum('dljbgok,gm->bdlgkjmo', kg, eye)
    tp = tp.reshape(nsl, t // 2, 2 * sg * h, 2 * sg * h)

    prs, pis = pr[t - 1::-1][:t], pi[t - 1::-1][:t]
    er = prs[..., None] * bbr[None] - pis[..., None] * bbi[None]
    ei = prs[..., None] * bbi[None] + pis[..., None] * bbr[None]
    bf = jnp.stack([er, ei], axis=2)
    bf = bf.reshape(t // 2, 2, nsl, sg, 2, p, h)
    bz = jnp.einsum('zlbgrpk,gm->bzlgkrmp', bf, eye)
    bz = bz.reshape(nsl, t // 2, 2 * sg * h, 2 * sg * p)

    cf = jnp.stack([mr[1:], -mi[1:]], axis=2)
    cf = cf.reshape(t // 2, 2, nsl, sg, 2, h, p)
    cz = jnp.einsum('ijbgrop,gm->birgpjmo', cf, eye)
    cz = cz.reshape(nsl, t // 2, 2 * sg * p, 2 * sg * h)

    def slab_state(re, im):
        k = re.shape[0]
        x = jnp.stack([re, im], axis=1).reshape(k, 2, nsl, sg * p)
        return x.transpose(2, 0, 1, 3).reshape(nsl, k, 2 * sg * p)

    ad = slab_state(*apow([t * 1.0, t * 2.0, t * 4.0]))
    ap8 = slab_state(*apow(t * (np.arange(SUBLANES) + 1.0)))
    return (tp.astype(MXU_DTYPE), bz.astype(MXU_DTYPE), cz.astype(MXU_DTYPE), ad, ap8)


def _s5_kernel(h_ref, tp_ref, bz_ref, cz_ref, ad_ref, ap8_ref, dsk_ref, o_ref,
               carry_ref, z_ref, xp_ref, *, tc):
    half = z_ref.shape[1] // 2
    npair = S5_CHUNK // 2

    @pl.when(pl.program_id(1) == 0)
    def _():
        carry_ref[...] = jnp.zeros_like(carry_ref)

    u = [jnp.concatenate([h_ref[2 * s], h_ref[2 * s + 1]], axis=1) for s in range(npair)]

    z = _dot(u[0], bz_ref[0])
    for s in range(1, npair):
        z = z + _dot(u[s], bz_ref[s])
    z_ref[...] = z

    row = lax.broadcasted_iota(I32, (SUBLANES, half), 0)
    pr8, pi8 = ap8_ref[:, :half], ap8_ref[:, half:]

    def tile_step(t, carry):
        r0 = pl.multiple_of(t * SUBLANES, SUBLANES)
        zt = z_ref[pl.ds(r0, SUBLANES), :]
        xr, xi = zt[:, :half], zt[:, half:]
        for k, d in enumerate((1, 2, 4)):
            a = ad_ref[k:k + 1, :]
            a_r, a_i = a[:, :half], a[:, half:]
            sr = jnp.where(row >= d, pltpu.roll(xr, d, 0), 0.0)
            si = jnp.where(row >= d, pltpu.roll(xi, d, 0), 0.0)
            xr, xi = xr + a_r * sr - a_i * si, xi + a_r * si + a_i * sr
        cr, ci = carry[:, :half], carry[:, half:]
        xr, xi = xr + pr8 * cr - pi8 * ci, xi + pr8 * ci + pi8 * cr
        xpr = jnp.where(row >= 1, pltpu.roll(xr, 1, 0), cr)
        xpi = jnp.where(row >= 1, pltpu.roll(xi, 1, 0), ci)
        xp_ref[pl.ds(r0, SUBLANES), :] = jnp.concatenate([xpr, xpi], axis=1)
        return jnp.concatenate([xr[SUBLANES - 1:], xi[SUBLANES - 1:]], axis=1)

    carry_ref[...] = lax.fori_loop(0, tc // SUBLANES, tile_step, carry_ref[...])

    xp = xp_ref[...].astype(MXU_DTYPE)
    dsk = dsk_ref[...]
    for i in range(npair):
        y = _dot(xp, cz_ref[i])
        for s in range(i + 1):
            y = y + _dot(u[s], tp_ref[i - s])
        for jl in range(2):
            j = 2 * i + jl
            yj = y[:, jl * LANES:(jl + 1) * LANES] + dsk * h_ref[j].astype(F32)
            o_ref[j] = jax.nn.gelu(yj).astype(o_ref.dtype)


def _s5_scan(h_t, tables, d_skip):
    tp, bz, cz, ad, ap8 = tables
    t, c, d = h_t.shape
    nsl = d // LANES
    tc = min(S5_TC, c)
    st = bz.shape[-1]
    dsk = d_skip.astype(F32).reshape(nsl, 1, LANES)
    return pl.pallas_call(
        functools.partial(_s5_kernel, tc=tc),
        out_shape=jax.ShapeDtypeStruct((t, c, d), MXU_DTYPE),
        grid=(nsl, c // tc),
        in_specs=[
            pl.BlockSpec((t, tc, LANES), lambda b, i: (0, i, b)),
            pl.BlockSpec((None,) + tp.shape[1:], lambda b, i: (b, 0, 0, 0)),
            pl.BlockSpec((None,) + bz.shape[1:], lambda b, i: (b, 0, 0, 0)),
            pl.BlockSpec((None,) + cz.shape[1:], lambda b, i: (b, 0, 0, 0)),
            pl.BlockSpec((None,) + ad.shape[1:], lambda b, i: (b, 0, 0)),
            pl.BlockSpec((None,) + ap8.shape[1:], lambda b, i: (b, 0, 0)),
            pl.BlockSpec((None, 1, LANES), lambda b, i: (b, 0, 0)),
        ],
        out_specs=pl.BlockSpec((t, tc, LANES), lambda b, i: (0, i, b)),
        scratch_shapes=[pltpu.VMEM((1, st), F32), pltpu.VMEM((tc, st), F32),
                        pltpu.VMEM((tc, st), F32)],
        compiler_params=_cparams(("arbitrary", "arbitrary")),
        name="s5_scan",
    )(h_t, tp, bz, cz, ad, ap8, dsk)


def _glu_kernel(x_ref, g_ref, w_ref, o_ref):
    d = x_ref.shape[1]
    vg = _dot(g_ref[...], w_ref[...])
    o_ref[...] = x_ref[...] + vg[:, :d] * jax.nn.sigmoid(vg[:, d:])


def _glu_residual(x, g, w):
    n, d = x.shape
    tm = min(ROW_TM, n)
    return pl.pallas_call(
        _glu_kernel,
        out_shape=jax.ShapeDtypeStruct((n, d), F32),
        grid=(n // tm,),
        in_specs=[pl.BlockSpec((tm, d), lambda i: (i, 0)),
                  pl.BlockSpec((tm, d), lambda i: (i, 0)),
                  pl.BlockSpec((d, 2 * d), lambda i: (0, 0))],
        out_specs=pl.BlockSpec((tm, d), lambda i: (i, 0)),
        compiler_params=_cparams(("parallel",)),
        name="glu",
    )(x, g, w)


def _head_perm(n_heads):
    n = np.arange(n_heads * HEAD_DIM)
    pb, r = n // LANES, n % LANES
    half, r2 = r // 64, r % 64
    hl, dp = r2 // 32, r2 % 32
    return (2 * pb + hl) * HEAD_DIM + 32 * half + dp


def _lane_head_mask(shape, hl):
    lane = lax.broadcasted_iota(I32, shape, len(shape) - 1)
    return ((lane % 64) // 32) == hl


def _proj_kernel(x_ref, g_ref, w_ref, wvt_ref, gq_ref, gk_ref, cos_ref, sin_ref, hm_ref,
                 q_ref, k_ref, vt_ref, qi_ref, ki_ref, wi_ref, *, d, dqi, att_scale, w_scale):
    h = _rms(x_ref[...], g_ref[...]).astype(MXU_DTYPE)
    cos, sin = cos_ref[...], sin_ref[...]
    hm = hm_ref[...]

    def rope(t):
        return t * cos + pltpu.roll(t, 64, 1) * sin

    def headnorm_rope(col0, gain_ref, out_ref, scale):
        for sb in range(d // MXU_DIM):
            c0 = sb * MXU_DIM
            t = _dot(h, w_ref[:, col0 + c0:col0 + c0 + MXU_DIM])
            sq = t * t
            hi = sq.astype(MXU_DTYPE)
            lo = (sq - hi.astype(F32)).astype(MXU_DTYPE)
            ss = _dot(hi, hm) + _dot(lo, hm)
            tn = t * lax.rsqrt(ss * (1.0 / HEAD_DIM) + EPS) * gain_ref[:, c0:c0 + MXU_DIM]
            for b in range(MXU_DIM // LANES):
                r = rope(tn[:, b * LANES:(b + 1) * LANES])
                if scale != 1.0:
                    r = r * scale
                out_ref[:, c0 + b * LANES:c0 + (b + 1) * LANES] = r.astype(out_ref.dtype)

    headnorm_rope(0, gq_ref, q_ref, att_scale)
    headnorm_rope(d, gk_ref, k_ref, 1.0)
    vt = _dot_nt(wvt_ref[...], h)
    row = lax.broadcasted_iota(I32, vt.shape, 0)
    vt_ref[...] = jnp.where(row % VT_ROWS >= HEAD_DIM, 1.0, vt).astype(vt_ref.dtype)
    c0 = 2 * d
    t = _dot(h, w_ref[:, c0:c0 + dqi])
    for b in range(dqi // LANES):
        qi_ref[:, b * LANES:(b + 1) * LANES] = rope(t[:, b * LANES:(b + 1) * LANES]).astype(qi_ref.dtype)
    c0 += dqi
    t = _dot(h, w_ref[:, c0:c0 + LANES])
    ms = jnp.sum(t * t, axis=-1, keepdims=True) * (0.5 / IDX_DIM)
    ki_ref[...] = rope(t * lax.rsqrt(ms + EPS)).astype(ki_ref.dtype)
    c0 += LANES
    wi_ref[...] = _dot(h, w_ref[:, c0:c0 + LANES]) * w_scale


def _dsa_project(x, gain, w_in, q_gain, k_gain, cos_t, sin_t):
    n, d = x.shape
    dqi = IDX_HEADS * IDX_DIM
    pq = _head_perm(N_HEADS)
    pqi = _head_perm(IDX_HEADS)
    wq = w_in[:, 0:d][:, pq]
    wk = w_in[:, d:2 * d][:, pq]
    wvt = w_in[:, 2 * d:3 * d].T.reshape(N_HEADS, HEAD_DIM, d)
    wvt = jnp.pad(wvt, ((0, 0), (0, VT_PAD), (0, 0))).reshape(N_HEADS * VT_ROWS, d).astype(MXU_DTYPE)
    dvt = N_HEADS * VT_ROWS
    wqi = w_in[:, 3 * d:3 * d + dqi][:, pqi]
    lane = np.arange(LANES)
    wki = w_in[:, 3 * d + dqi:3 * d + dqi + IDX_DIM][:, 32 * (lane // 64) + lane % 32]
    wwi = jnp.pad(w_in[:, 3 * d + dqi + IDX_DIM:], ((0, 0), (0, LANES - IDX_HEADS)))
    w_all = jnp.concatenate([wq, wk, wqi, wki, wwi], axis=1).astype(MXU_DTYPE)
    dcol = (pq % HEAD_DIM)
    gq = q_gain.astype(F32)[dcol].reshape(1, d)
    gk = k_gain.astype(F32)[dcol].reshape(1, d)
    l2 = np.arange(MXU_DIM)
    hm = ((l2[:, None] // LANES == l2[None, :] // LANES)
          & ((l2[:, None] % 64) // 32 == (l2[None, :] % 64) // 32))
    hm = jnp.asarray(hm, MXU_DTYPE)
    tm = min(PROJ_TM, n)
    nw = w_all.shape[1]
    outs = pl.pallas_call(
        functools.partial(_proj_kernel, d=d, dqi=dqi, att_scale=HEAD_DIM ** -0.5 * LOG2E,
                          w_scale=(IDX_HEADS ** -0.5) * (IDX_DIM ** -0.5)),
        out_shape=[jax.ShapeDtypeStruct((n, d), MXU_DTYPE)] * 2
        + [jax.ShapeDtypeStruct((dvt, n), MXU_DTYPE),
           jax.ShapeDtypeStruct((n, dqi), MXU_DTYPE),
           jax.ShapeDtypeStruct((n, LANES), MXU_DTYPE),
           jax.ShapeDtypeStruct((n, LANES), F32)],
        grid=(n // tm,),
        in_specs=[
            pl.BlockSpec((tm, d), lambda i: (i, 0)),
            pl.BlockSpec((1, d), lambda i: (0, 0)),
            pl.BlockSpec((d, nw), lambda i: (0, 0)),
            pl.BlockSpec((dvt, d), lambda i: (0, 0)),
            pl.BlockSpec((1, d), lambda i: (0, 0)),
            pl.BlockSpec((1, d), lambda i: (0, 0)),
            pl.BlockSpec((tm, LANES), lambda i: (i, 0)),
            pl.BlockSpec((tm, LANES), lambda i: (i, 0)),
            pl.BlockSpec((MXU_DIM, MXU_DIM), lambda i: (0, 0)),
        ],
        out_specs=[pl.BlockSpec((tm, d), lambda i: (i, 0))] * 2
        + [pl.BlockSpec((dvt, tm), lambda i: (0, i)),
           pl.BlockSpec((tm, dqi), lambda i: (i, 0)),
           pl.BlockSpec((tm, LANES), lambda i: (i, 0)),
           pl.BlockSpec((tm, LANES), lambda i: (i, 0))],
        compiler_params=_cparams(("parallel",)),
        name="dsa_proj",
    )(x, gain.reshape(1, d), w_all, wvt, gq, gk, cos_t, sin_t, hm)
    return outs


KEY_NEG_INF = -2139095041
KEY_POS_INF = 2139095040
KEY16_NEG_INF = -32641
KEY16_POS_INF = 32640


def _key_to_f32(key):
    bits = key ^ ((key >> 31) & 0x7FFFFFFF)
    return lax.bitcast_convert_type(bits, F32)


def _select_kernel(qi_ref, wi_ref, ki_ref, mask_ref, s_ref, lo_ref, hi_ref, clo_ref, chi_ref,
                   *, tq, tk, topk, rb):
    qb = pl.program_id(0)
    n_all = mask_ref.shape[1]
    nkt = (qb * tq) // tk + 1
    ncol = nkt * (tk // LANES)

    qm = []
    for hh in range(IDX_HEADS):
        blk = qi_ref[:, (hh // 2) * LANES:(hh // 2 + 1) * LANES]
        qm.append(jnp.where(_lane_head_mask(blk.shape, hh % 2), blk, jnp.zeros_like(blk)))
    qm = jnp.concatenate(qm, axis=0)
    wv = wi_ref[...]
    row_pos = qb * tq + lax.broadcasted_iota(I32, (tq, tk), 0)
    col_in = lax.broadcasted_iota(I32, (tq, tk), 1)

    def score_tile(kt, _):
        k0 = pl.multiple_of(kt * tk, tk)
        s = _dot_nt(qm, ki_ref[pl.ds(k0, tk), :])
        acc = wv[:, 0:1] * jnp.maximum(s[0:tq], 0.0)
        for hh in range(1, IDX_HEADS):
            acc = acc + wv[:, hh:hh + 1] * jnp.maximum(s[hh * tq:(hh + 1) * tq], 0.0)
        acc = jnp.where(col_in + k0 <= row_pos, acc, -jnp.inf)
        s_ref[:, pl.ds(k0, tk)] = acc
        return 0

    lax.fori_loop(0, nkt, score_tile, 0)

    lo_ref[...] = jnp.full(lo_ref.shape, KEY_NEG_INF, I32)
    hi_ref[...] = jnp.full(hi_ref.shape, KEY_POS_INF, I32)
    clo_ref[...] = jnp.full(clo_ref.shape, 1, I32) * (ncol * LANES)
    chi_ref[...] = jnp.zeros(chi_ref.shape, I32)

    def count_rows(r0, pred):
        def body(j, acc):
            c0 = pl.multiple_of(j * LANES, LANES)
            return acc + pred(s_ref[r0:r0 + rb, pl.ds(c0, LANES)], j).astype(I32)
        acc = lax.fori_loop(0, ncol, body, jnp.zeros((rb, LANES), I32))
        return jnp.broadcast_to(jnp.sum(acc, axis=1, keepdims=True), (rb, LANES))

    def bisect_step(carry):
        it, _ = carry
        pending = jnp.zeros((), I32)
        for r0 in range(0, tq, rb):
            lo, hi = lo_ref[r0:r0 + rb, :], hi_ref[r0:r0 + rb, :]
            mid = (lo & hi) + ((lo ^ hi) >> 1)
            active = mid != lo
            cand = _key_to_f32(mid)
            cnt = count_rows(r0, lambda blk, j: blk >= cand)
            ge = cnt >= topk
            up = active & ge
            dn = active & jnp.logical_not(ge)
            hit = active & (cnt == topk)
            lo_ref[r0:r0 + rb, :] = jnp.where(up, mid, lo)
            hi_ref[r0:r0 + rb, :] = jnp.where(hit, mid + 1, jnp.where(dn, mid, hi))
            clo_ref[r0:r0 + rb, :] = jnp.where(up, cnt, clo_ref[r0:r0 + rb, :])
            chi_ref[r0:r0 + rb, :] = jnp.where(dn, cnt, chi_ref[r0:r0 + rb, :])
            pending = jnp.maximum(pending, jnp.max(active.astype(I32)))
        return it + 1, pending

    lax.while_loop(lambda c: (c[0] < 40) & (c[1] > 0), bisect_step,
                   (jnp.zeros((), I32), jnp.ones((), I32)))

    tie = (clo_ref[...] > topk) & (lo_ref[...] > KEY_NEG_INF)
    any_tie = jnp.max(tie.astype(I32))

    @pl.when(any_tie == 0)
    def _():
        hi_ref[...] = jnp.full(hi_ref.shape, n_all, I32)

    @pl.when(any_tie > 0)
    def _():
        need = topk - chi_ref[...]
        chi_ref[...] = need
        hi_ref[...] = jnp.full(hi_ref.shape, ncol * LANES - 1, I32)
        clo_ref[...] = jnp.full(clo_ref.shape, -1, I32)
        lane = lax.broadcasted_iota(I32, (rb, LANES), 1)

        def tie_step(_, c):
            for r0 in range(0, tq, rb):
                jl, jh = clo_ref[r0:r0 + rb, :], hi_ref[r0:r0 + rb, :]
                mid = jl + ((jh - jl) >> 1)
                active = (jh - jl) > 1
                thr = _key_to_f32(lo_ref[r0:r0 + rb, :])
                cnt = count_rows(r0, lambda blk, j: (blk == thr) & (lane + j * LANES <= mid))
                ok = cnt >= chi_ref[r0:r0 + rb, :]
                hi_ref[r0:r0 + rb, :] = jnp.where(active & ok, mid, jh)
                clo_ref[r0:r0 + rb, :] = jnp.where(active & jnp.logical_not(ok), mid, jl)
            return c

        lax.fori_loop(0, int(math.ceil(math.log2(n_all))) + 1, tie_step, 0)
        keep_all = jnp.logical_not(tie)
        hi_ref[...] = jnp.where(keep_all, n_all, hi_ref[...])

    lane_k = lax.broadcasted_iota(I32, (tq, tk), 1)
    thr_col = _key_to_f32(lo_ref[:, 0:1])
    last_col = hi_ref[:, 0:1]

    def mask_tile(kt, _):
        k0 = pl.multiple_of(kt * tk, tk)
        s = s_ref[:, pl.ds(k0, tk)]
        sel = ((s > thr_col) | ((s == thr_col) & (lane_k + k0 <= last_col))) & (s > -jnp.inf)
        mask_ref[:, pl.ds(k0, tk)] = jnp.where(sel, 1, 0).astype(mask_ref.dtype)
        return 0

    lax.fori_loop(0, nkt, mask_tile, 0)

    def zero_tile(kt, _):
        k0 = pl.multiple_of(kt * tk, tk)
        mask_ref[:, pl.ds(k0, tk)] = jnp.zeros((tq, tk), mask_ref.dtype)
        return 0

    lax.fori_loop(nkt, n_all // tk, zero_tile, 0)


def _dsa_select(qi, wi, ki, topk):
    n = qi.shape[0]
    tq = min(SEL_TQ, n)
    tk = min(SEL_TK, n)
    rb = min(SEL_RB, tq)
    return pl.pallas_call(
        functools.partial(_select_kernel, tq=tq, tk=tk, topk=topk, rb=rb),
        out_shape=jax.ShapeDtypeStruct((n, n), jnp.int8),
        grid=(n // tq,),
        in_specs=[pl.BlockSpec((tq, qi.shape[1]), lambda i: (i, 0)),
                  pl.BlockSpec((tq, LANES), lambda i: (i, 0)),
                  pl.BlockSpec((n, LANES), lambda i: (0, 0))],
        out_specs=pl.BlockSpec((tq, n), lambda i: (i, 0)),
        scratch_shapes=[pltpu.VMEM((tq, n), F32)] + [pltpu.VMEM((tq, LANES), I32)] * 4,
        compiler_params=_cparams(("parallel",)),
        name="dsa_select",
    )(qi, wi, ki)


def _attn_kernel(qb_tab, kt_tab, q_ref, k_ref, v_ref, mask_ref, o_ref, acc_ref, m_ref, l_ref,
                 *, tq, tk):
    step = pl.program_id(0)
    kt = kt_tab[step]
    d = q_ref.shape[1]
    slab = MXU_DIM
    heads_per_slab = slab // HEAD_DIM

    @pl.when(kt == 0)
    def _():
        acc_ref[...] = jnp.zeros_like(acc_ref)
        m_ref[...] = jnp.full(m_ref.shape, NEG_BIG, F32)
        l_ref[...] = jnp.zeros_like(l_ref)

    bias = jnp.where(mask_ref[...].astype(I32) != 0, 0.0, NEG_BIG)
    lane_s = lax.broadcasted_iota(I32, (tq, slab), 1) // HEAD_DIM

    for sb in range(d // slab):
        vs = v_ref[:, sb * slab:(sb + 1) * slab]
        lane_v = lax.broadcasted_iota(I32, vs.shape, 1) // HEAD_DIM
        alpha_l = jnp.zeros((tq, slab), F32)
        pv = jnp.zeros((tq, slab), F32)
        for i in range(heads_per_slab):
            hd = sb * heads_per_slab + i
            blk = hd // 2
            qblk = q_ref[:, blk * LANES:(blk + 1) * LANES]
            qm = jnp.where(_lane_head_mask(qblk.shape, hd % 2), qblk, jnp.zeros_like(qblk))
            s = _dot_nt(qm, k_ref[:, blk * LANES:(blk + 1) * LANES]) + bias
            m_old = m_ref[:, hd:hd + 1]
            m_new = jnp.maximum(m_old, jnp.max(s, axis=1, keepdims=True))
            alpha = jnp.exp(m_old - m_new)
            p = jnp.exp(s - m_new)
            l_ref[:, hd:hd + 1] = l_ref[:, hd:hd + 1] * alpha + jnp.sum(p, axis=1, keepdims=True)
            m_ref[:, hd:hd + 1] = m_new
            vm = jnp.where(lane_v == i, vs, jnp.zeros_like(vs))
            pv = pv + _dot(p.astype(MXU_DTYPE), vm)
            alpha_l = jnp.where(lane_s == i, alpha, alpha_l)
        acc_ref[:, sb * slab:(sb + 1) * slab] = acc_ref[:, sb * slab:(sb + 1) * slab] * alpha_l + pv

    last = kt == ((qb_tab[step] + 1) * tq - 1) // tk

    @pl.when(last)
    def _():
        lane_h = lax.broadcasted_iota(I32, (tq, d), 1) // HEAD_DIM
        l_l = jnp.zeros((tq, d), F32)
        for hd in range(N_HEADS):
            l_l = jnp.where(lane_h == hd, l_ref[:, hd:hd + 1], l_l)
        o_ref[...] = (acc_ref[...] / l_l).astype(o_ref.dtype)


def _dsa_attend(q, k, v, mask):
    n, d = q.shape
    tq = min(ATT_TQ, n)
    tk = min(ATT_TK, n)
    pairs = [(qb, kt) for qb in range(n // tq) for kt in range(((qb + 1) * tq - 1) // tk + 1)]
    qb_tab = jnp.asarray([p[0] for p in pairs], I32)
    kt_tab = jnp.asarray([p[1] for p in pairs], I32)
    grid_spec = pltpu.PrefetchScalarGridSpec(
        num_scalar_prefetch=2,
        grid=(len(pairs),),
        in_specs=[
            pl.BlockSpec((tq, d), lambda i, qt, kt: (qt[i], 0)),
            pl.BlockSpec((tk, d), lambda i, qt, kt: (kt[i], 0)),
            pl.BlockSpec((tk, d), lambda i, qt, kt: (kt[i], 0)),
            pl.BlockSpec((tq, tk), lambda i, qt, kt: (qt[i], kt[i])),
        ],
        out_specs=pl.BlockSpec((tq, d), lambda i, qt, kt: (qt[i], 0)),
        scratch_shapes=[pltpu.VMEM((tq, d), F32), pltpu.VMEM((tq, LANES), F32),
                        pltpu.VMEM((tq, LANES), F32)],
    )
    return pl.pallas_call(
        functools.partial(_attn_kernel, tq=tq, tk=tk),
        out_shape=jax.ShapeDtypeStruct((n, d), MXU_DTYPE),
        grid_spec=grid_spec,
        compiler_params=_cparams(("arbitrary",)),
        name="dsa_attend",
    )(qb_tab, kt_tab, q, k, v, mask)


def _out_kernel(x_ref, a_ref, w_ref, o_ref):
    o_ref[...] = x_ref[...] + _dot(a_ref[...], w_ref[...])


def _out_residual(x, a, w):
    n, d = x.shape
    tm = min(ROW_TM, n)
    return pl.pallas_call(
        _out_kernel,
        out_shape=jax.ShapeDtypeStruct((n, d), F32),
        grid=(n // tm,),
        in_specs=[pl.BlockSpec((tm, d), lambda i: (i, 0)),
                  pl.BlockSpec((tm, d), lambda i: (i, 0)),
                  pl.BlockSpec((d, d), lambda i: (0, 0))],
        out_specs=pl.BlockSpec((tm, d), lambda i: (i, 0)),
        compiler_params=_cparams(("parallel",)),
        name="attn_out",
    )(x, a, w)


def _select_t_kernel(qi_ref, wit_ref, ki_ref, mask_ref, s_ref, s16_ref, lo_ref, hi_ref, clo_ref,
                     chi_ref, *, tq, tk, topk, unroll, unroll16):
    qb = pl.program_id(0)
    n_all = mask_ref.shape[0]
    nkt = (qb * tq + tq - 1) // tk + 1
    nrow = nkt * tk
    rows_it = SUBLANES * unroll

    qm = []
    for hh in range(IDX_HEADS):
        blk = qi_ref[:, (hh // 2) * LANES:(hh // 2 + 1) * LANES]
        qm.append(jnp.where(_lane_head_mask(blk.shape, hh % 2), blk, jnp.zeros_like(blk)))
    qm = jnp.concatenate(qm, axis=0)
    wt = wit_ref[...]
    key_in = lax.broadcasted_iota(I32, (tk, tq), 0)
    q_pos = qb * tq + lax.broadcasted_iota(I32, (tk, tq), 1)

    def score_tile(kt, _):
        k0 = pl.multiple_of(kt * tk, tk)
        s = _dot_nt(ki_ref[pl.ds(k0, tk), :], qm)
        acc = wt[0:1, :] * jnp.maximum(s[:, 0:tq], 0.0)
        for hh in range(1, IDX_HEADS):
            acc = acc + wt[hh:hh + 1, :] * jnp.maximum(s[:, hh * tq:(hh + 1) * tq], 0.0)
        sc = jnp.where(key_in + k0 <= q_pos, acc, -jnp.inf)
        s_ref[pl.ds(k0, tk), :] = sc
        hi_bits = lax.bitcast_convert_type(sc, I32) & -65536
        s16_ref[pl.ds(k0, tk), :] = lax.bitcast_convert_type(hi_bits, F32).astype(BF16)
        return 0

    lax.fori_loop(0, nkt, score_tile, 0)

    lo_ref[...] = jnp.full(lo_ref.shape, KEY16_NEG_INF, I32)
    hi_ref[...] = jnp.full(hi_ref.shape, KEY16_POS_INF + 1, I32)
    clo_ref[...] = jnp.full(clo_ref.shape, 1, I32) * nrow
    chi_ref[...] = jnp.zeros(chi_ref.shape, I32)
    rows16 = 2 * SUBLANES * unroll16
    one16 = jnp.ones((), BF16)
    zero16 = jnp.zeros((), BF16)

    def count16(cand):
        def body(i, acc):
            r0 = pl.multiple_of(i * rows16, rows16)
            blk = s16_ref[pl.ds(r0, rows16), :].reshape(unroll16, 2 * SUBLANES, tq)
            ones = jnp.where(blk >= cand[None], one16, zero16)
            part = ones[0]
            for u in range(1, unroll16):
                part = part + ones[u]
            return acc + part.astype(F32)
        acc = lax.fori_loop(0, nrow // rows16, body, jnp.zeros((2 * SUBLANES, tq), F32))
        tot = jnp.sum(acc, axis=0, keepdims=True).astype(I32)
        return jnp.broadcast_to(tot, (SUBLANES, tq))

    def coarse_step(carry):
        it, _ = carry
        lo, hi = lo_ref[...], hi_ref[...]
        mid = (lo + hi) >> 1
        active = mid != lo
        bits16 = (mid ^ ((mid >> 31) & 0x7FFF)) & 0xFFFF
        cand = lax.bitcast_convert_type(bits16 << 16, F32)
        cand = jnp.concatenate([cand, cand], axis=0).astype(BF16)
        cnt = count16(cand)
        ge = cnt >= topk
        up = active & ge
        dn = active & jnp.logical_not(ge)
        lo_ref[...] = jnp.where(up, mid, lo)
        hi_ref[...] = jnp.where(dn, mid, hi)
        clo_ref[...] = jnp.where(up, cnt, clo_ref[...])
        chi_ref[...] = jnp.where(dn, cnt, chi_ref[...])
        return it + 1, jnp.max(active.astype(I32))

    lax.while_loop(lambda c: (c[0] < 20) & (c[1] > 0), coarse_step,
                   (jnp.zeros((), I32), jnp.ones((), I32)))

    lo16 = lo_ref[...]
    none_finite = lo16 == KEY16_NEG_INF
    lo_ref[...] = jnp.where(none_finite, KEY_NEG_INF, lo16 << 16)
    hi_ref[...] = jnp.where(none_finite, KEY_NEG_INF + 1, (lo16 + 1) << 16)

    def count(pred):
        def body(i, acc):
            r0 = pl.multiple_of(i * rows_it, rows_it)
            blk = s_ref[pl.ds(r0, rows_it), :].reshape(unroll, SUBLANES, tq)
            return acc + jnp.sum(pred(blk, r0).astype(I32), axis=0)
        acc = lax.fori_loop(0, nrow // rows_it, body, jnp.zeros((SUBLANES, tq), I32))
        return jnp.broadcast_to(jnp.sum(acc, axis=0, keepdims=True), (SUBLANES, tq))

    def bisect_step(carry):
        it, _ = carry
        lo, hi = lo_ref[...], hi_ref[...]
        mid = (lo & hi) + ((lo ^ hi) >> 1)
        active = mid != lo
        cand = _key_to_f32(mid)
        cnt = count(lambda blk, r0: blk >= cand[None])
        ge = cnt >= topk
        up = active & ge
        dn = active & jnp.logical_not(ge)
        hit = active & (cnt == topk)
        lo_ref[...] = jnp.where(up, mid, lo)
        hi_ref[...] = jnp.where(hit, mid + 1, jnp.where(dn, mid, hi))
        clo_ref[...] = jnp.where(up, cnt, clo_ref[...])
        chi_ref[...] = jnp.where(dn, cnt, chi_ref[...])
        return it + 1, jnp.max(active.astype(I32))

    lax.while_loop(lambda c: (c[0] < 40) & (c[1] > 0), bisect_step,
                   (jnp.zeros((), I32), jnp.ones((), I32)))

    tie = (clo_ref[...] > topk) & (lo_ref[...] > KEY_NEG_INF)
    any_tie = jnp.max(tie.astype(I32))

    @pl.when(any_tie == 0)
    def _():
        hi_ref[...] = jnp.full(hi_ref.shape, n_all, I32)

    @pl.when(any_tie > 0)
    def _():
        chi_ref[...] = topk - chi_ref[...]
        hi_ref[...] = jnp.full(hi_ref.shape, 1, I32) * (nrow - 1)
        clo_ref[...] = jnp.full(clo_ref.shape, -1, I32)
        sub = (lax.broadcasted_iota(I32, (unroll, SUBLANES, tq), 0) * SUBLANES
               + lax.broadcasted_iota(I32, (unroll, SUBLANES, tq), 1))
        thr = _key_to_f32(lo_ref[...])

        def tie_step(_, c):
            jl, jh = clo_ref[...], hi_ref[...]
            mid = jl + ((jh - jl) >> 1)
            active = (jh - jl) > 1
            cnt = count(lambda blk, r0: (blk == thr[None]) & (sub + r0 <= mid[None]))
            ok = cnt >= chi_ref[...]
            hi_ref[...] = jnp.where(active & ok, mid, jh)
            clo_ref[...] = jnp.where(active & jnp.logical_not(ok), mid, jl)
            return c

        lax.fori_loop(0, int(math.ceil(math.log2(n_all))) + 1, tie_step, 0)
        hi_ref[...] = jnp.where(tie, hi_ref[...], n_all)

    thr_row = _key_to_f32(lo_ref[0:1, :])
    last_row = hi_ref[0:1, :]

    def mask_tile(kt, _):
        k0 = pl.multiple_of(kt * tk, tk)
        s = s_ref[pl.ds(k0, tk), :]
        sel = ((s > thr_row) | ((s == thr_row) & (key_in + k0 <= last_row))) & (s > -jnp.inf)
        mask_ref[pl.ds(k0, tk), :] = jnp.where(sel, 1, 0).astype(mask_ref.dtype)
        return 0

    lax.fori_loop(0, nkt, mask_tile, 0)

    def zero_tile(kt, _):
        k0 = pl.multiple_of(kt * tk, tk)
        mask_ref[pl.ds(k0, tk), :] = jnp.zeros((tk, tq), mask_ref.dtype)
        return 0

    lax.fori_loop(nkt, n_all // tk, zero_tile, 0)


def _dsa_select_t(qi, wit, ki, topk):
    n = qi.shape[0]
    tq = min(SEL_TQ, n)
    tk = min(SEL_TK, n)
    return pl.pallas_call(
        functools.partial(_select_t_kernel, tq=tq, tk=tk, topk=topk, unroll=SEL_UNROLL,
                          unroll16=SEL_UNROLL16),
        out_shape=jax.ShapeDtypeStruct((n, n), jnp.int8),
        grid=(n // tq,),
        in_specs=[pl.BlockSpec((tq, qi.shape[1]), lambda i: (i, 0)),
                  pl.BlockSpec((IDX_HEADS, tq), lambda i: (0, i)),
                  pl.BlockSpec((n, LANES), lambda i: (0, 0))],
        out_specs=pl.BlockSpec((n, tq), lambda i: (0, i)),
        scratch_shapes=[pltpu.VMEM((n, tq), F32), pltpu.VMEM((n, tq), BF16)]
        + [pltpu.VMEM((SUBLANES, tq), I32)] * 4,
        compiler_params=_cparams(("parallel",)),
        name="dsa_select",
    )(qi, wit, ki)


def _attn_t_kernel(qb_tab, kt_tab, q_ref, k_ref, vt_ref, mask_ref, o_ref, acc_ref, m_ref,
                   *, tq, tk, qs):
    step = pl.program_id(0)
    kt = kt_tab[step]

    @pl.when(kt == 0)
    def _():
        acc_ref[...] = jnp.zeros_like(acc_ref)
        m_ref[...] = jnp.full(m_ref.shape, NEG_BIG, MXU_DTYPE).astype(F32)

    bias = jnp.where(mask_ref[...].astype(I32) != 0, 0.0, NEG_BIG).astype(MXU_DTYPE)

    def logits(hd):
        blk = hd // 2
        qblk = q_ref[:, blk * LANES:(blk + 1) * LANES]
        qm = jnp.where(_lane_head_mask(qblk.shape, hd % 2), qblk, jnp.zeros_like(qblk))
        return _dot_nt(k_ref[:, blk * LANES:(blk + 1) * LANES], qm).astype(MXU_DTYPE) + bias

    s_next = logits(0)
    for hd in range(N_HEADS):
        r0 = hd * VT_ROWS
        s = s_next
        if hd + 1 < N_HEADS:
            s_next = logits(hd + 1)
        m_old = m_ref[hd:hd + 1, :]
        m_new = jnp.maximum(m_old, jnp.max(s, axis=0, keepdims=True).astype(F32))
        alpha = jnp.exp2(m_old - m_new)
        p = jnp.exp2(s - m_new.astype(MXU_DTYPE))
        m_ref[hd:hd + 1, :] = m_new
        pv = _dot(vt_ref[r0:r0 + VT_ROWS, :], p)
        acc_ref[r0:r0 + VT_ROWS, :] = acc_ref[r0:r0 + VT_ROWS, :] * alpha + pv

    last = kt == ((qb_tab[step] + 1) * tq - 1) // tk

    @pl.when(last)
    def _():
        for hd in range(N_HEADS):
            r0 = hd * VT_ROWS
            o_ref[hd * HEAD_DIM:(hd + 1) * HEAD_DIM, :] = (
                acc_ref[r0:r0 + HEAD_DIM, :] / acc_ref[r0 + HEAD_DIM:r0 + HEAD_DIM + 1, :]
            ).astype(o_ref.dtype)


def _dsa_attend_t(q, k, vt, mask_t):
    n, d = q.shape
    tq = min(ATT_TQ, n)
    tk = min(ATT_TK, n)
    pairs = [(qb, kt) for qb in range(n // tq) for kt in range(((qb + 1) * tq - 1) // tk + 1)]
    qb_tab = jnp.asarray([p[0] for p in pairs], I32)
    kt_tab = jnp.asarray([p[1] for p in pairs], I32)
    grid_spec = pltpu.PrefetchScalarGridSpec(
        num_scalar_prefetch=2,
        grid=(len(pairs),),
        in_specs=[
            pl.BlockSpec((tq, d), lambda i, qt, kt: (qt[i], 0)),
            pl.BlockSpec((tk, d), lambda i, qt, kt: (kt[i], 0)),
            pl.BlockSpec((vt.shape[0], tk), lambda i, qt, kt: (0, kt[i])),
            pl.BlockSpec((tk, tq), lambda i, qt, kt: (kt[i], qt[i])),
        ],
        out_specs=pl.BlockSpec((d, tq), lambda i, qt, kt: (0, qt[i])),
        scratch_shapes=[pltpu.VMEM((vt.shape[0], tq), F32), pltpu.VMEM((N_HEADS, tq), F32)],
    )
    return pl.pallas_call(
        functools.partial(_attn_t_kernel, tq=tq, tk=tk, qs=min(ATT_QS, tq)),
        out_shape=jax.ShapeDtypeStruct((d, n), MXU_DTYPE),
        grid_spec=grid_spec,
        compiler_params=_cparams(("arbitrary",)),
        name="dsa_attend",
    )(qb_tab, kt_tab, q, k, vt, mask_t)


def _out_t_kernel(x_ref, at_ref, w_ref, o_ref):
    o_ref[...] = x_ref[...] + lax.dot_general(
        at_ref[...], w_ref[...], (((0,), (0,)), ((), ())), preferred_element_type=F32)


def _out_residual_t(x, at, w):
    n, d = x.shape
    tm = min(ROW_TM, n)
    return pl.pallas_call(
        _out_t_kernel,
        out_shape=jax.ShapeDtypeStruct((n, d), F32),
        grid=(n // tm,),
        in_specs=[pl.BlockSpec((tm, d), lambda i: (i, 0)),
                  pl.BlockSpec((d, tm), lambda i: (0, i)),
                  pl.BlockSpec((d, d), lambda i: (0, 0))],
        out_specs=pl.BlockSpec((tm, d), lambda i: (i, 0)),
        compiler_params=_cparams(("parallel",)),
        name="attn_out",
    )(x, at, w)


def _rope_lane_tables(length):
    inv_freq = ROPE_THETA ** (-jnp.arange(0, HEAD_DIM, 2, dtype=F32) / HEAD_DIM)
    ang = jnp.arange(length, dtype=F32)[:, None] * inv_freq[None, :]
    lane = np.arange(LANES)
    cos_t = jnp.cos(ang)[:, lane % 32]
    sin_t = jnp.sin(ang)[:, lane % 32] * jnp.asarray(np.where(lane < 64, -1.0, 1.0), F32)
    return cos_t, sin_t


def kernel(x, s5_lambda_re, s5_lambda_im, s5_log_dt, s5_b_re, s5_b_im, s5_c_re, s5_c_im, s5_d, s5_w_glu, dsa_w_in, dsa_q_norm, dsa_k_norm, dsa_w_o, ffn_w_gate_up, ffn_w_down, norm_mix, norm_ffn):
    bsz, length, d = x.shape
    depth = norm_mix.shape[0]
    topk = min(TOPK_MAX, length // 4)
    nchunk = length // S5_CHUNK
    cos_t, sin_t = _rope_lane_tables(length)
    outs = []
    for b in range(bsz):
        xs = x[b].astype(F32)
        for i in range(depth):
            j = i // 2
            wgu = ffn_w_gate_up[i].astype(MXU_DTYPE)
            wd = ffn_w_down[i].astype(MXU_DTYPE)
            if i % 2 == 0:
                tables = _s5_tables(s5_lambda_re[j], s5_lambda_im[j], s5_log_dt[j], s5_b_re[j],
                                    s5_b_im[j], s5_c_re[j], s5_c_im[j])
                xt = xs.reshape(nchunk, S5_CHUNK, d).transpose(1, 0, 2).reshape(length, d)
                h_t = _norm(xt, norm_mix[i]).reshape(S5_CHUNK, nchunk, d)
                g_t = _s5_scan(h_t, tables, s5_d[j]).reshape(length, d)
                xt = _glu_residual(xt, g_t, s5_w_glu[j].astype(MXU_DTYPE))
                xt = _ffn(xt, norm_ffn[i], wgu, wd)
                xs = xt.reshape(S5_CHUNK, nchunk, d).transpose(1, 0, 2).reshape(length, d)
            else:
                q, k, vt, qi, ki, wi = _dsa_project(xs, norm_mix[i], dsa_w_in[j], dsa_q_norm[j],
                                                    dsa_k_norm[j], cos_t, sin_t)
                mask_t = _dsa_select_t(qi, wi[:, :IDX_HEADS].T, ki, topk)
                att_t = _dsa_attend_t(q, k, vt, mask_t)
                xs = _out_residual_t(xs, att_t, dsa_w_o[j].astype(MXU_DTYPE))
                xs = _ffn(xs, norm_ffn[i], wgu, wd)
        outs.append(xs)
    return jnp.stack(outs, axis=0).astype(x.dtype)
```

```python
import functools
import math

import jax
import jax.numpy as jnp
import numpy as np
from jax import lax
from jax.experimental import pallas as pl
from jax.experimental.pallas import tpu as pltpu

F32 = jnp.float32
BF16 = jnp.bfloat16
I32 = jnp.int32
MXU_DTYPE = BF16

D_MODEL = 1024
S5_GROUP = 16
S5_STATE = 64
N_HEADS = 16
HEAD_DIM = 64
IDX_HEADS = 8
IDX_DIM = 64
TOPK_MAX = 256
ROPE_THETA = 10000.0
EPS = 1e-6

LANES = 128
SUBLANES = 8
MXU_DIM = 256
VMEM_LIMIT = 56 * 1024 * 1024

S5_CHUNK = 16
S5_SLAB_GROUPS = LANES // S5_GROUP
NEG_BIG = -1e30
LOG2E = math.log2(math.e)
VT_PAD = 16
VT_ROWS = HEAD_DIM + VT_PAD

ROW_TM = 512
NORM_TM = 1024
S5_TC = 512
PROJ_TM = 512
SEL_TQ = 256
SEL_TK = 512
SEL_RB = 64
SEL_UNROLL = 64
SEL_UNROLL16 = 32
ATT_TQ = 512
ATT_TK = 512
ATT_QS = 512


def _cparams(sem, flags=None):
    return pltpu.CompilerParams(dimension_semantics=sem, vmem_limit_bytes=VMEM_LIMIT, flags=flags)


def _rms(x, gain=None):
    y = x * lax.rsqrt(jnp.mean(x * x, axis=-1, keepdims=True) + EPS)
    return y if gain is None else y * gain


def _dot(a, b):
    return jnp.dot(a, b, preferred_element_type=F32)


def _dot_nt(a, b):
    return lax.dot_general(a, b, (((1,), (1,)), ((), ())), preferred_element_type=F32)


def _ffn_kernel(x_ref, g_ref, wgu_ref, wd_ref, o_ref, acc_ref, *, d_ff, fc):
    x = x_ref[...]
    h = _rms(x, g_ref[...]).astype(MXU_DTYPE)
    for c in range(d_ff // fc):
        g = _dot(h, wgu_ref[:, c * fc:(c + 1) * fc])
        u = _dot(h, wgu_ref[:, d_ff + c * fc:d_ff + (c + 1) * fc])
        a = (g * jax.nn.sigmoid(g) * u).astype(MXU_DTYPE)
        d = _dot(a, wd_ref[c * fc:(c + 1) * fc, :])
        if c == 0:
            acc_ref[...] = d
        else:
            acc_ref[...] += d
    o_ref[...] = x + acc_ref[...]


def _ffn(x, gain, wgu, wd):
    n, d = x.shape
    d_ff = wd.shape[0]
    tm = min(ROW_TM, n)
    fc = MXU_DIM
    return pl.pallas_call(
        functools.partial(_ffn_kernel, d_ff=d_ff, fc=fc),
        out_shape=jax.ShapeDtypeStruct((n, d), F32),
        grid=(n // tm,),
        in_specs=[
            pl.BlockSpec((tm, d), lambda i: (i, 0)),
            pl.BlockSpec((1, d), lambda i: (0, 0)),
            pl.BlockSpec((d, 2 * d_ff), lambda i: (0, 0)),
            pl.BlockSpec((d_ff, d), lambda i: (0, 0)),
        ],
        out_specs=pl.BlockSpec((tm, d), lambda i: (i, 0)),
        scratch_shapes=[pltpu.VMEM((tm, d), F32)],
        compiler_params=_cparams(("parallel",)),
        name="ffn",
    )(x, gain.reshape(1, d), wgu, wd)


def _norm_kernel(x_ref, g_ref, o_ref):
    o_ref[...] = _rms(x_ref[...], g_ref[...]).astype(o_ref.dtype)


def _norm(x, gain):
    n, d = x.shape
    tm = min(NORM_TM, n)
    return pl.pallas_call(
        _norm_kernel,
        out_shape=jax.ShapeDtypeStruct((n, d), MXU_DTYPE),
        grid=(n // tm,),
        in_specs=[pl.BlockSpec((tm, d), lambda i: (i, 0)),
                  pl.BlockSpec((1, d), lambda i: (0, 0))],
        out_specs=pl.BlockSpec((tm, d), lambda i: (i, 0)),
        compiler_params=_cparams(("parallel",)),
        name="norm",
    )(x, gain.reshape(1, d))


def _s5_tables(lam_re, lam_im, log_dt, b_re, b_im, c_re, c_im):
    hp = lax.Precision.HIGHEST
    g, p = lam_re.shape
    h = S5_GROUP
    nsl = g // S5_SLAB_GROUPS
    sg = S5_SLAB_GROUPS
    t = S5_CHUNK
    lam_re, lam_im, log_dt = lam_re.astype(F32), lam_im.astype(F32), log_dt.astype(F32)
    b_re, b_im, c_re, c_im = (a.astype(F32) for a in (b_re, b_im, c_re, c_im))
    dt = jnp.exp(log_dt)[:, None]

    def apow(k):
        k = jnp.asarray(k, F32).reshape((-1, 1, 1))
        mag = jnp.exp(lam_re[None] * dt[None] * k)
        ang = lam_im[None] * dt[None] * k
        return mag * jnp.cos(ang), mag * jnp.sin(ang)

    ar, ai = apow([1.0])
    ar, ai = ar[0], ai[0]
    den = lam_re * lam_re + lam_im * lam_im
    nr, ni = ar - 1.0, ai
    qr = (nr * lam_re + ni * lam_im) / den
    qi = (ni * lam_re - nr * lam_im) / den
    bbr = qr[..., None] * b_re - qi[..., None] * b_im
    bbi = qr[..., None] * b_im + qi[..., None] * b_re

    pr, pi = apow(np.arange(t + 1))
    mr = c_re[None] * pr[:, :, None, :] - c_im[None] * pi[:, :, None, :]
    mi = c_re[None] * pi[:, :, None, :] + c_im[None] * pr[:, :, None, :]

    kk = (jnp.einsum('tghp,gpk->tghk', mr[:t], bbr, precision=hp)
          - jnp.einsum('tghp,gpk->tghk', mi[:t], bbi, precision=hp))
    kp = jnp.concatenate([jnp.zeros_like(kk[:1]), kk], axis=0)
    eye = jnp.eye(sg, dtype=F32)

    dl = np.arange(t // 2)[:, None, None]
    sl = np.arange(2)[None, :, None]
    jl = np.arange(2)[None, None, :]
    idx = 2 * dl + jl - sl + 1
    kg = kp[idx]
    kg = kg.reshape(t // 2, 2, 2, nsl, sg, h, h)
    tp = jnp.einsum('dljbgok,gm->bdlgkjmo', kg, eye)
    tp = tp.reshape(nsl, t // 2, 2 * sg * h, 2 * sg * h)

    prs, pis = pr[t - 1::-1][:t], pi[t - 1::-1][:t]
    er = prs[..., None] * bbr[None] - pis[..., None] * bbi[None]
    ei = prs[..., None] * bbi[None] + pis[..., None] * bbr[None]
    bf = jnp.stack([er, ei], axis=2)
    bf = bf.reshape(t // 2, 2, nsl, sg, 2, p, h)
    bz = jnp.einsum('zlbgrpk,gm->bzlgkrmp', bf, eye)
    bz = bz.reshape(nsl, t // 2, 2 * sg * h, 2 * sg * p)

    cf = jnp.stack([mr[1:], -mi[1:]], axis=2)
    cf = cf.reshape(t // 2, 2, nsl, sg, 2, h, p)
    cz = jnp.einsum('ijbgrop,gm->birgpjmo', cf, eye)
    cz = cz.reshape(nsl, t // 2, 2 * sg * p, 2 * sg * h)

    def slab_state(re, im):
        k = re.shape[0]
        x = jnp.stack([re, im], axis=1).reshape(k, 2, nsl, sg * p)
        return x.transpose(2, 0, 1, 3).reshape(nsl, k, 2 * sg * p)

    ad = slab_state(*apow([t * 1.0, t * 2.0, t * 4.0]))
    ap8 = slab_state(*apow(t * (np.arange(SUBLANES) + 1.0)))
    return (tp.astype(MXU_DTYPE), bz.astype(MXU_DTYPE), cz.astype(MXU_DTYPE), ad, ap8)


def _s5_kernel(h_ref, tp_ref, bz_ref, cz_ref, ad_ref, ap8_ref, dsk_ref, o_ref,
               carry_ref, z_ref, xp_ref, *, tc):
    half = z_ref.shape[1] // 2
    npair = S5_CHUNK // 2

    @pl.when(pl.program_id(1) == 0)
    def _():
        carry_ref[...] = jnp.zeros_like(carry_ref)

    u = [jnp.concatenate([h_ref[2 * s], h_ref[2 * s + 1]], axis=1) for s in range(npair)]

    z = _dot(u[0], bz_ref[0])
    for s in range(1, npair):
        z = z + _dot(u[s], bz_ref[s])
    z_ref[...] = z

    row = lax.broadcasted_iota(I32, (SUBLANES, half), 0)
    pr8, pi8 = ap8_ref[:, :half], ap8_ref[:, half:]

    def tile_step(t, carry):
        r0 = pl.multiple_of(t * SUBLANES, SUBLANES)
        zt = z_ref[pl.ds(r0, SUBLANES), :]
        xr, xi = zt[:, :half], zt[:, half:]
        for k, d in enumerate((1, 2, 4)):
            a = ad_ref[k:k + 1, :]
            a_r, a_i = a[:, :half], a[:, half:]
            sr = jnp.where(row >= d, pltpu.roll(xr, d, 0), 0.0)
            si = jnp.where(row >= d, pltpu.roll(xi, d, 0), 0.0)
            xr, xi = xr + a_r * sr - a_i * si, xi + a_r * si + a_i * sr
        cr, ci = carry[:, :half], carry[:, half:]
        xr, xi = xr + pr8 * cr - pi8 * ci, xi + pr8 * ci + pi8 * cr
        xpr = jnp.where(row >= 1, pltpu.roll(xr, 1, 0), cr)
        xpi = jnp.where(row >= 1, pltpu.roll(xi, 1, 0), ci)
        xp_ref[pl.ds(r0, SUBLANES), :] = jnp.concatenate([xpr, xpi], axis=1)
        return jnp.concatenate([xr[SUBLANES - 1:], xi[SUBLANES - 1:]], axis=1)

    carry_ref[...] = lax.fori_loop(0, tc // SUBLANES, tile_step, carry_ref[...])

    xp = xp_ref[...].astype(MXU_DTYPE)
    dsk = dsk_ref[...]
    for i in range(npair):
        y = _dot(xp, cz_ref[i])
        for s in range(i + 1):
            y = y + _dot(u[s], tp_ref[i - s])
        for jl in range(2):
            j = 2 * i + jl
            yj = y[:, jl * LANES:(jl + 1) * LANES] + dsk * h_ref[j].astype(F32)
            o_ref[j] = jax.nn.gelu(yj).astype(o_ref.dtype)


def _s5_scan(h_t, tables, d_skip):
    tp, bz, cz, ad, ap8 = tables
    t, c, d = h_t.shape
    nsl = d // LANES
    tc = min(S5_TC, c)
    st = bz.shape[-1]
    dsk = d_skip.astype(F32).reshape(nsl, 1, LANES)
    return pl.pallas_call(
        functools.partial(_s5_kernel, tc=tc),
        out_shape=jax.ShapeDtypeStruct((t, c, d), MXU_DTYPE),
        grid=(nsl, c // tc),
        in_specs=[
            pl.BlockSpec((t, tc, LANES), lambda b, i: (0, i, b)),
            pl.BlockSpec((None,) + tp.shape[1:], lambda b, i: (b, 0, 0, 0)),
            pl.BlockSpec((None,) + bz.shape[1:], lambda b, i: (b, 0, 0, 0)),
            pl.BlockSpec((None,) + cz.shape[1:], lambda b, i: (b, 0, 0, 0)),
            pl.BlockSpec((None,) + ad.shape[1:], lambda b, i: (b, 0, 0)),
            pl.BlockSpec((None,) + ap8.shape[1:], lambda b, i: (b, 0, 0)),
            pl.BlockSpec((None, 1, LANES), lambda b, i: (b, 0, 0)),
        ],
        out_specs=pl.BlockSpec((t, tc, LANES), lambda b, i: (0, i, b)),
        scratch_shapes=[pltpu.VMEM((1, st), F32), pltpu.VMEM((tc, st), F32),
                        pltpu.VMEM((tc, st), F32)],
        compiler_params=_cparams(("arbitrary", "arbitrary")),
        name="s5_scan",
    )(h_t, tp, bz, cz, ad, ap8, dsk)


def _glu_kernel(x_ref, g_ref, w_ref, o_ref):
    d = x_ref.shape[1]
    vg = _dot(g_ref[...], w_ref[...])
    o_ref[...] = x_ref[...] + vg[:, :d] * jax.nn.sigmoid(vg[:, d:])


def _glu_residual(x, g, w):
    n, d = x.shape
    tm = min(ROW_TM, n)
    return pl.pallas_call(
        _glu_kernel,
        out_shape=jax.ShapeDtypeStruct((n, d), F32),
        grid=(n // tm,),
        in_specs=[pl.BlockSpec((tm, d), lambda i: (i, 0)),
                  pl.BlockSpec((tm, d), lambda i: (i, 0)),
                  pl.BlockSpec((d, 2 * d), lambda i: (0, 0))],
        out_specs=pl.BlockSpec((tm, d), lambda i: (i, 0)),
        compiler_params=_cparams(("parallel",)),
        name="glu",
    )(x, g, w)


def _head_perm(n_heads):
    n = np.arange(n_heads * HEAD_DIM)
    pb, r = n // LANES, n % LANES
    half, r2 = r // 64, r % 64
    hl, dp = r2 // 32, r2 % 32
    return (2 * pb + hl) * HEAD_DIM + 32 * half + dp


def _lane_head_mask(shape, hl):
    lane = lax.broadcasted_iota(I32, shape, len(shape) - 1)
    return ((lane % 64) // 32) == hl


def _proj_kernel(x_ref, g_ref, w_ref, wvt_ref, gq_ref, gk_ref, cos_ref, sin_ref, hm_ref,
                 q_ref, k_ref, vt_ref, qi_ref, ki_ref, wi_ref, *, d, dqi, att_scale, w_scale):
    h = _rms(x_ref[...], g_ref[...]).astype(MXU_DTYPE)
    cos, sin = cos_ref[...], sin_ref[...]
    hm = hm_ref[...]

    def rope(t):
        return t * cos + pltpu.roll(t, 64, 1) * sin

    def headnorm_rope(col0, gain_ref, out_ref, scale):
        for sb in range(d // MXU_DIM):
            c0 = sb * MXU_DIM
            t = _dot(h, w_ref[:, col0 + c0:col0 + c0 + MXU_DIM])
            sq = t * t
            hi = sq.astype(MXU_DTYPE)
            lo = (sq - hi.astype(F32)).astype(MXU_DTYPE)
            ss = _dot(hi, hm) + _dot(lo, hm)
            tn = t * lax.rsqrt(ss * (1.0 / HEAD_DIM) + EPS) * gain_ref[:, c0:c0 + MXU_DIM]
            for b in range(MXU_DIM // LANES):
                r = rope(tn[:, b * LANES:(b + 1) * LANES])
                if scale != 1.0:
                    r = r * scale
                out_ref[:, c0 + b * LANES:c0 + (b + 1) * LANES] = r.astype(out_ref.dtype)

    headnorm_rope(0, gq_ref, q_ref, att_scale)
    headnorm_rope(d, gk_ref, k_ref, 1.0)
    vt = _dot_nt(wvt_ref[...], h)
    row = lax.broadcasted_iota(I32, vt.shape, 0)
    vt_ref[...] = jnp.where(row % VT_ROWS >= HEAD_DIM, 1.0, vt).astype(vt_ref.dtype)
    c0 = 2 * d
    t = _dot(h, w_ref[:, c0:c0 + dqi])
    for b in range(dqi // LANES):
        qi_ref[:, b * LANES:(b + 1) * LANES] = rope(t[:, b * LANES:(b + 1) * LANES]).astype(qi_ref.dtype)
    c0 += dqi
    t = _dot(h, w_ref[:, c0:c0 + LANES])
    ms = jnp.sum(t * t, axis=-1, keepdims=True) * (0.5 / IDX_DIM)
    ki_ref[...] = rope(t * lax.rsqrt(ms + EPS)).astype(ki_ref.dtype)
    c0 += LANES
    wi_ref[...] = _dot(h, w_ref[:, c0:c0 + LANES]) * w_scale


def _dsa_project(x, gain, w_in, q_gain, k_gain, cos_t, sin_t):
    n, d = x.shape
    dqi = IDX_HEADS * IDX_DIM
    pq = _head_perm(N_HEADS)
    pqi = _head_perm(IDX_HEADS)
    wq = w_in[:, 0:d][:, pq]
    wk = w_in[:, d:2 * d][:, pq]
    wvt = w_in[:, 2 * d:3 * d].T.reshape(N_HEADS, HEAD_DIM, d)
    wvt = jnp.pad(wvt, ((0, 0), (0, VT_PAD), (0, 0))).reshape(N_HEADS * VT_ROWS, d).astype(MXU_DTYPE)
    dvt = N_HEADS * VT_ROWS
    wqi = w_in[:, 3 * d:3 * d + dqi][:, pqi]
    lane = np.arange(LANES)
    wki = w_in[:, 3 * d + dqi:3 * d + dqi + IDX_DIM][:, 32 * (lane // 64) + lane % 32]
    wwi = jnp.pad(w_in[:, 3 * d + dqi + IDX_DIM:], ((0, 0), (0, LANES - IDX_HEADS)))
    w_all = jnp.concatenate([wq, wk, wqi, wki, wwi], axis=1).astype(MXU_DTYPE)
    dcol = (pq % HEAD_DIM)
    gq = q_gain.astype(F32)[dcol].reshape(1, d)
    gk = k_gain.astype(F32)[dcol].reshape(1, d)
    l2 = np.arange(MXU_DIM)
    hm = ((l2[:, None] // LANES == l2[None, :] // LANES)
          & ((l2[:, None] % 64) // 32 == (l2[None, :] % 64) // 32))
    hm = jnp.asarray(hm, MXU_DTYPE)
    tm = min(PROJ_TM, n)
    nw = w_all.shape[1]
    outs = pl.pallas_call(
        functools.partial(_proj_kernel, d=d, dqi=dqi, att_scale=HEAD_DIM ** -0.5 * LOG2E,
                          w_scale=(IDX_HEADS ** -0.5) * (IDX_DIM ** -0.5)),
        out_shape=[jax.ShapeDtypeStruct((n, d), MXU_DTYPE)] * 2
        + [jax.ShapeDtypeStruct((dvt, n), MXU_DTYPE),
           jax.ShapeDtypeStruct((n, dqi), MXU_DTYPE),
           jax.ShapeDtypeStruct((n, LANES), MXU_DTYPE),
           jax.ShapeDtypeStruct((n, LANES), F32)],
        grid=(n // tm,),
        in_specs=[
            pl.BlockSpec((tm, d), lambda i: (i, 0)),
            pl.BlockSpec((1, d), lambda i: (0, 0)),
            pl.BlockSpec((d, nw), lambda i: (0, 0)),
            pl.BlockSpec((dvt, d), lambda i: (0, 0)),
            pl.BlockSpec((1, d), lambda i: (0, 0)),
            pl.BlockSpec((1, d), lambda i: (0, 0)),
            pl.BlockSpec((tm, LANES), lambda i: (i, 0)),
            pl.BlockSpec((tm, LANES), lambda i: (i, 0)),
            pl.BlockSpec((MXU_DIM, MXU_DIM), lambda i: (0, 0)),
        ],
        out_specs=[pl.BlockSpec((tm, d), lambda i: (i, 0))] * 2
        + [pl.BlockSpec((dvt, tm), lambda i: (0, i)),
           pl.BlockSpec((tm, dqi), lambda i: (i, 0)),
           pl.BlockSpec((tm, LANES), lambda i: (i, 0)),
           pl.BlockSpec((tm, LANES), lambda i: (i, 0))],
        compiler_params=_cparams(("parallel",)),
        name="dsa_proj",
    )(x, gain.reshape(1, d), w_all, wvt, gq, gk, cos_t, sin_t, hm)
    return outs


KEY_NEG_INF = -2139095041
KEY_POS_INF = 2139095040
KEY16_NEG_INF = -32641
KEY16_POS_INF = 32640


def _key_to_f32(key):
    bits = key ^ ((key >> 31) & 0x7FFFFFFF)
    return lax.bitcast_convert_type(bits, F32)


def _select_kernel(qi_ref, wi_ref, ki_ref, mask_ref, s_ref, lo_ref, hi_ref, clo_ref, chi_ref,
                   *, tq, tk, topk, rb):
    qb = pl.program_id(0)
    n_all = mask_ref.shape[1]
    nkt = (qb * tq) // tk + 1
    ncol = nkt * (tk // LANES)

    qm = []
    for hh in range(IDX_HEADS):
        blk = qi_ref[:, (hh // 2) * LANES:(hh // 2 + 1) * LANES]
        qm.append(jnp.where(_lane_head_mask(blk.shape, hh % 2), blk, jnp.zeros_like(blk)))
    qm = jnp.concatenate(qm, axis=0)
    wv = wi_ref[...]
    row_pos = qb * tq + lax.broadcasted_iota(I32, (tq, tk), 0)
    col_in = lax.broadcasted_iota(I32, (tq, tk), 1)

    def score_tile(kt, _):
        k0 = pl.multiple_of(kt * tk, tk)
        s = _dot_nt(qm, ki_ref[pl.ds(k0, tk), :])
        acc = wv[:, 0:1] * jnp.maximum(s[0:tq], 0.0)
        for hh in range(1, IDX_HEADS):
            acc = acc + wv[:, hh:hh + 1] * jnp.maximum(s[hh * tq:(hh + 1) * tq], 0.0)
        acc = jnp.where(col_in + k0 <= row_pos, acc, -jnp.inf)
        s_ref[:, pl.ds(k0, tk)] = acc
        return 0

    lax.fori_loop(0, nkt, score_tile, 0)

    lo_ref[...] = jnp.full(lo_ref.shape, KEY_NEG_INF, I32)
    hi_ref[...] = jnp.full(hi_ref.shape, KEY_POS_INF, I32)
    clo_ref[...] = jnp.full(clo_ref.shape, 1, I32) * (ncol * LANES)
    chi_ref[...] = jnp.zeros(chi_ref.shape, I32)

    def count_rows(r0, pred):
        def body(j, acc):
            c0 = pl.multiple_of(j * LANES, LANES)
            return acc + pred(s_ref[r0:r0 + rb, pl.ds(c0, LANES)], j).astype(I32)
        acc = lax.fori_loop(0, ncol, body, jnp.zeros((rb, LANES), I32))
        return jnp.broadcast_to(jnp.sum(acc, axis=1, keepdims=True), (rb, LANES))

    def bisect_step(carry):
        it, _ = carry
        pending = jnp.zeros((), I32)
        for r0 in range(0, tq, rb):
            lo, hi = lo_ref[r0:r0 + rb, :], hi_ref[r0:r0 + rb, :]
            mid = (lo & hi) + ((lo ^ hi) >> 1)
            active = mid != lo
            cand = _key_to_f32(mid)
            cnt = count_rows(r0, lambda blk, j: blk >= cand)
            ge = cnt >= topk
            up = active & ge
            dn = active & jnp.logical_not(ge)
            hit = active & (cnt == topk)
            lo_ref[r0:r0 + rb, :] = jnp.where(up, mid, lo)
            hi_ref[r0:r0 + rb, :] = jnp.where(hit, mid + 1, jnp.where(dn, mid, hi))
            clo_ref[r0:r0 + rb, :] = jnp.where(up, cnt, clo_ref[r0:r0 + rb, :])
            chi_ref[r0:r0 + rb, :] = jnp.where(dn, cnt, chi_ref[r0:r0 + rb, :])
            pending = jnp.maximum(pending, jnp.max(active.astype(I32)))
        return it + 1, pending

    lax.while_loop(lambda c: (c[0] < 40) & (c[1] > 0), bisect_step,
                   (jnp.zeros((), I32), jnp.ones((), I32)))

    tie = (clo_ref[...] > topk) & (lo_ref[...] > KEY_NEG_INF)
    any_tie = jnp.max(tie.astype(I32))

    @pl.when(any_tie == 0)
    def _():
        hi_ref[...] = jnp.full(hi_ref.shape, n_all, I32)

    @pl.when(any_tie > 0)
    def _():
        need = topk - chi_ref[...]
        chi_ref[...] = need
        hi_ref[...] = jnp.full(hi_ref.shape, ncol * LANES - 1, I32)
        clo_ref[...] = jnp.full(clo_ref.shape, -1, I32)
        lane = lax.broadcasted_iota(I32, (rb, LANES), 1)

        def tie_step(_, c):
            for r0 in range(0, tq, rb):
                jl, jh = clo_ref[r0:r0 + rb, :], hi_ref[r0:r0 + rb, :]
                mid = jl + ((jh - jl) >> 1)
                active = (jh - jl) > 1
                thr = _key_to_f32(lo_ref[r0:r0 + rb, :])
                cnt = count_rows(r0, lambda blk, j: (blk == thr) & (lane + j * LANES <= mid))
                ok = cnt >= chi_ref[r0:r0 + rb, :]
                hi_ref[r0:r0 + rb, :] = jnp.where(active & ok, mid, jh)
                clo_ref[r0:r0 + rb, :] = jnp.where(active & jnp.logical_not(ok), mid, jl)
            return c

        lax.fori_loop(0, int(math.ceil(math.log2(n_all))) + 1, tie_step, 0)
        keep_all = jnp.logical_not(tie)
        hi_ref[...] = jnp.where(keep_all, n_all, hi_ref[...])

    lane_k = lax.broadcasted_iota(I32, (tq, tk), 1)
    thr_col = _key_to_f32(lo_ref[:, 0:1])
    last_col = hi_ref[:, 0:1]

    def mask_tile(kt, _):
        k0 = pl.multiple_of(kt * tk, tk)
        s = s_ref[:, pl.ds(k0, tk)]
        sel = ((s > thr_col) | ((s == thr_col) & (lane_k + k0 <= last_col))) & (s > -jnp.inf)
        mask_ref[:, pl.ds(k0, tk)] = jnp.where(sel, 1, 0).astype(mask_ref.dtype)
        return 0

    lax.fori_loop(0, nkt, mask_tile, 0)

    def zero_tile(kt, _):
        k0 = pl.multiple_of(kt * tk, tk)
        mask_ref[:, pl.ds(k0, tk)] = jnp.zeros((tq, tk), mask_ref.dtype)
        return 0

    lax.fori_loop(nkt, n_all // tk, zero_tile, 0)


def _dsa_select(qi, wi, ki, topk):
    n = qi.shape[0]
    tq = min(SEL_TQ, n)
    tk = min(SEL_TK, n)
    rb = min(SEL_RB, tq)
    return pl.pallas_call(
        functools.partial(_select_kernel, tq=tq, tk=tk, topk=topk, rb=rb),
        out_shape=jax.ShapeDtypeStruct((n, n), jnp.int8),
        grid=(n // tq,),
        in_specs=[pl.BlockSpec((tq, qi.shape[1]), lambda i: (i, 0)),
                  pl.BlockSpec((tq, LANES), lambda i: (i, 0)),
                  pl.BlockSpec((n, LANES), lambda i: (0, 0))],
        out_specs=pl.BlockSpec((tq, n), lambda i: (i, 0)),
        scratch_shapes=[pltpu.VMEM((tq, n), F32)] + [pltpu.VMEM((tq, LANES), I32)] * 4,
        compiler_params=_cparams(("parallel",)),
        name="dsa_select",
    )(qi, wi, ki)


def _attn_kernel(qb_tab, kt_tab, q_ref, k_ref, v_ref, mask_ref, o_ref, acc_ref, m_ref, l_ref,
                 *, tq, tk):
    step = pl.program_id(0)
    kt = kt_tab[step]
    d = q_ref.shape[1]
    slab = MXU_DIM
    heads_per_slab = slab // HEAD_DIM

    @pl.when(kt == 0)
    def _():
        acc_ref[...] = jnp.zeros_like(acc_ref)
        m_ref[...] = jnp.full(m_ref.shape, NEG_BIG, F32)
        l_ref[...] = jnp.zeros_like(l_ref)

    bias = jnp.where(mask_ref[...].astype(I32) != 0, 0.0, NEG_BIG)
    lane_s = lax.broadcasted_iota(I32, (tq, slab), 1) // HEAD_DIM

    for sb in range(d // slab):
        vs = v_ref[:, sb * slab:(sb + 1) * slab]
        lane_v = lax.broadcasted_iota(I32, vs.shape, 1) // HEAD_DIM
        alpha_l = jnp.zeros((tq, slab), F32)
        pv = jnp.zeros((tq, slab), F32)
        for i in range(heads_per_slab):
            hd = sb * heads_per_slab + i
            blk = hd // 2
            qblk = q_ref[:, blk * LANES:(blk + 1) * LANES]
            qm = jnp.where(_lane_head_mask(qblk.shape, hd % 2), qblk, jnp.zeros_like(qblk))
            s = _dot_nt(qm, k_ref[:, blk * LANES:(blk + 1) * LANES]) + bias
            m_old = m_ref[:, hd:hd + 1]
            m_new = jnp.maximum(m_old, jnp.max(s, axis=1, keepdims=True))
            alpha = jnp.exp(m_old - m_new)
            p = jnp.exp(s - m_new)
            l_ref[:, hd:hd + 1] = l_ref[:, hd:hd + 1] * alpha + jnp.sum(p, axis=1, keepdims=True)
            m_ref[:, hd:hd + 1] = m_new
            vm = jnp.where(lane_v == i, vs, jnp.zeros_like(vs))
            pv = pv + _dot(p.astype(MXU_DTYPE), vm)
            alpha_l = jnp.where(lane_s == i, alpha, alpha_l)
        acc_ref[:, sb * slab:(sb + 1) * slab] = acc_ref[:, sb * slab:(sb + 1) * slab] * alpha_l + pv

    last = kt == ((qb_tab[step] + 1) * tq - 1) // tk

    @pl.when(last)
    def _():
        lane_h = lax.broadcasted_iota(I32, (tq, d), 1) // HEAD_DIM
        l_l = jnp.zeros((tq, d), F32)
        for hd in range(N_HEADS):
            l_l = jnp.where(lane_h == hd, l_ref[:, hd:hd + 1], l_l)
        o_ref[...] = (acc_ref[...] / l_l).astype(o_ref.dtype)


def _dsa_attend(q, k, v, mask):
    n, d = q.shape
    tq = min(ATT_TQ, n)
    tk = min(ATT_TK, n)
    pairs = [(qb, kt) for qb in range(n // tq) for kt in range(((qb + 1) * tq - 1) // tk + 1)]
    qb_tab = jnp.asarray([p[0] for p in pairs], I32)
    kt_tab = jnp.asarray([p[1] for p in pairs], I32)
    grid_spec = pltpu.PrefetchScalarGridSpec(
        num_scalar_prefetch=2,
        grid=(len(pairs),),
        in_specs=[
            pl.BlockSpec((tq, d), lambda i, qt, kt: (qt[i], 0)),
            pl.BlockSpec((tk, d), lambda i, qt, kt: (kt[i], 0)),
            pl.BlockSpec((tk, d), lambda i, qt, kt: (kt[i], 0)),
            pl.BlockSpec((tq, tk), lambda i, qt, kt: (qt[i], kt[i])),
        ],
        out_specs=pl.BlockSpec((tq, d), lambda i, qt, kt: (qt[i], 0)),
        scratch_shapes=[pltpu.VMEM((tq, d), F32), pltpu.VMEM((tq, LANES), F32),
                        pltpu.VMEM((tq, LANES), F32)],
    )
    return pl.pallas_call(
        functools.partial(_attn_kernel, tq=tq, tk=tk),
        out_shape=jax.ShapeDtypeStruct((n, d), MXU_DTYPE),
        grid_spec=grid_spec,
        compiler_params=_cparams(("arbitrary",)),
        name="dsa_attend",
    )(qb_tab, kt_tab, q, k, v, mask)


def _out_kernel(x_ref, a_ref, w_ref, o_ref):
    o_ref[...] = x_ref[...] + _dot(a_ref[...], w_ref[...])


def _out_residual(x, a, w):
    n, d = x.shape
    tm = min(ROW_TM, n)
    return pl.pallas_call(
        _out_kernel,
        out_shape=jax.ShapeDtypeStruct((n, d), F32),
        grid=(n // tm,),
        in_specs=[pl.BlockSpec((tm, d), lambda i: (i, 0)),
                  pl.BlockSpec((tm, d), lambda i: (i, 0)),
                  pl.BlockSpec((d, d), lambda i: (0, 0))],
        out_specs=pl.BlockSpec((tm, d), lambda i: (i, 0)),
        compiler_params=_cparams(("parallel",)),
        name="attn_out",
    )(x, a, w)


def _select_t_kernel(qi_ref, wit_ref, ki_ref, mask_ref, s_ref, s16_ref, lo_ref, hi_ref, clo_ref,
                     chi_ref, *, tq, tk, topk, unroll, unroll16):
    qb = pl.program_id(0)
    n_all = mask_ref.shape[0]
    nkt = (qb * tq + tq - 1) // tk + 1
    nrow = nkt * tk
    rows_it = SUBLANES * unroll

    qm = []
    for hh in range(IDX_HEADS):
        blk = qi_ref[:, (hh // 2) * LANES:(hh // 2 + 1) * LANES]
        qm.append(jnp.where(_lane_head_mask(blk.shape, hh % 2), blk, jnp.zeros_like(blk)))
    qm = jnp.concatenate(qm, axis=0)
    wt = wit_ref[...]
    key_in = lax.broadcasted_iota(I32, (tk, tq), 0)
    q_pos = qb * tq + lax.broadcasted_iota(I32, (tk, tq), 1)

    def score_tile(kt, _):
        k0 = pl.multiple_of(kt * tk, tk)
        s = _dot_nt(ki_ref[pl.ds(k0, tk), :], qm)
        acc = wt[0:1, :] * jnp.maximum(s[:, 0:tq], 0.0)
        for hh in range(1, IDX_HEADS):
            acc = acc + wt[hh:hh + 1, :] * jnp.maximum(s[:, hh * tq:(hh + 1) * tq], 0.0)
        sc = jnp.where(key_in + k0 <= q_pos, acc, -jnp.inf)
        s_ref[pl.ds(k0, tk), :] = sc
        hi_bits = lax.bitcast_convert_type(sc, I32) & -65536
        s16_ref[pl.ds(k0, tk), :] = lax.bitcast_convert_type(hi_bits, F32).astype(BF16)
        return 0

    lax.fori_loop(0, nkt, score_tile, 0)

    lo_ref[...] = jnp.full(lo_ref.shape, KEY16_NEG_INF, I32)
    hi_ref[...] = jnp.full(hi_ref.shape, KEY16_POS_INF + 1, I32)
    clo_ref[...] = jnp.full(clo_ref.shape, 1, I32) * nrow
    chi_ref[...] = jnp.zeros(chi_ref.shape, I32)
    rows16 = 2 * SUBLANES * unroll16
    one16 = jnp.ones((), BF16)
    zero16 = jnp.zeros((), BF16)

    def count16(cand):
        def body(i, acc):
            r0 = pl.multiple_of(i * rows16, rows16)
            blk = s16_ref[pl.ds(r0, rows16), :].reshape(unroll16, 2 * SUBLANES, tq)
            ones = jnp.where(blk >= cand[None], one16, zero16)
            part = ones[0]
            for u in range(1, unroll16):
                part = part + ones[u]
            return acc + part.astype(F32)
        acc = lax.fori_loop(0, nrow // rows16, body, jnp.zeros((2 * SUBLANES, tq), F32))
        tot = jnp.sum(acc, axis=0, keepdims=True).astype(I32)
        return jnp.broadcast_to(tot, (SUBLANES, tq))

    def coarse_step(_, carry):
        lo, hi = lo_ref[...], hi_ref[...]
        mid = (lo + hi) >> 1
        active = mid != lo
        bits16 = (mid ^ ((mid >> 31) & 0x7FFF)) & 0xFFFF
        cand = lax.bitcast_convert_type(bits16 << 16, F32)
        cand = jnp.concatenate([cand, cand], axis=0).astype(BF16)
        cnt = count16(cand)
        ge = cnt >= topk
        up = active & ge
        dn = active & jnp.logical_not(ge)
        lo_ref[...] = jnp.where(up, mid, lo)
        hi_ref[...] = jnp.where(dn, mid, hi)
        clo_ref[...] = jnp.where(up, cnt, clo_ref[...])
        chi_ref[...] = jnp.where(dn, cnt, chi_ref[...])
        return carry

    lax.fori_loop(0, 16, coarse_step, 0)

    lo16 = lo_ref[...]
    none_finite = lo16 == KEY16_NEG_INF
    lo_ref[...] = jnp.where(none_finite, KEY_NEG_INF, lo16 << 16)
    hi_ref[...] = jnp.where(none_finite, KEY_NEG_INF + 1, (lo16 + 1) << 16)

    def count(pred):
        def body(i, acc):
            r0 = pl.multiple_of(i * rows_it, rows_it)
            blk = s_ref[pl.ds(r0, rows_it), :].reshape(unroll, SUBLANES, tq)
            return acc + jnp.sum(pred(blk, r0).astype(I32), axis=0)
        acc = lax.fori_loop(0, nrow // rows_it, body, jnp.zeros((SUBLANES, tq), I32))
        return jnp.broadcast_to(jnp.sum(acc, axis=0, keepdims=True), (SUBLANES, tq))

    def bisect_step(carry):
        it, _ = carry
        lo, hi = lo_ref[...], hi_ref[...]
        mid = (lo & hi) + ((lo ^ hi) >> 1)
        active = mid != lo
        cand = _key_to_f32(mid)
        cnt = count(lambda blk, r0: blk >= cand[None])
        ge = cnt >= topk
        up = active & ge
        dn = active & jnp.logical_not(ge)
        hit = active & (cnt == topk)
        lo_ref[...] = jnp.where(up, mid, lo)
        hi_ref[...] = jnp.where(hit, mid + 1, jnp.where(dn, mid, hi))
        clo_ref[...] = jnp.where(up, cnt, clo_ref[...])
        chi_ref[...] = jnp.where(dn, cnt, chi_ref[...])
        return it + 1, jnp.max(active.astype(I32))

    lax.while_loop(lambda c: (c[0] < 40) & (c[1] > 0), bisect_step,
                   (jnp.zeros((), I32), jnp.ones((), I32)))

    tie = (clo_ref[...] > topk) & (lo_ref[...] > KEY_NEG_INF)
    any_tie = jnp.max(tie.astype(I32))

    @pl.when(any_tie == 0)
    def _():
        hi_ref[...] = jnp.full(hi_ref.shape, n_all, I32)

    @pl.when(any_tie > 0)
    def _():
        chi_ref[...] = topk - chi_ref[...]
        hi_ref[...] = jnp.full(hi_ref.shape, 1, I32) * (nrow - 1)
        clo_ref[...] = jnp.full(clo_ref.shape, -1, I32)
        sub = (lax.broadcasted_iota(I32, (unroll, SUBLANES, tq), 0) * SUBLANES
               + lax.broadcasted_iota(I32, (unroll, SUBLANES, tq), 1))
        thr = _key_to_f32(lo_ref[...])

        def tie_step(_, c):
            jl, jh = clo_ref[...], hi_ref[...]
            mid = jl + ((jh - jl) >> 1)
            active = (jh - jl) > 1
            cnt = count(lambda blk, r0: (blk == thr[None]) & (sub + r0 <= mid[None]))
            ok = cnt >= chi_ref[...]
            hi_ref[...] = jnp.where(active & ok, mid, jh)
            clo_ref[...] = jnp.where(active & jnp.logical_not(ok), mid, jl)
            return c

        lax.fori_loop(0, int(math.ceil(math.log2(n_all))) + 1, tie_step, 0)
        hi_ref[...] = jnp.where(tie, hi_ref[...], n_all)

    thr_row = _key_to_f32(lo_ref[0:1, :])
    last_row = hi_ref[0:1, :]

    def mask_tile(kt, _):
        k0 = pl.multiple_of(kt * tk, tk)
        s = s_ref[pl.ds(k0, tk), :]
        sel = ((s > thr_row) | ((s == thr_row) & (key_in + k0 <= last_row))) & (s > -jnp.inf)
        mask_ref[pl.ds(k0, tk), :] = jnp.where(sel, 1, 0).astype(mask_ref.dtype)
        return 0

    lax.fori_loop(0, nkt, mask_tile, 0)

    def zero_tile(kt, _):
        k0 = pl.multiple_of(kt * tk, tk)
        mask_ref[pl.ds(k0, tk), :] = jnp.zeros((tk, tq), mask_ref.dtype)
        return 0

    lax.fori_loop(nkt, n_all // tk, zero_tile, 0)


def _dsa_select_t(qi, wit, ki, topk):
    n = qi.shape[0]
    tq = min(SEL_TQ, n)
    tk = min(SEL_TK, n)
    assert tk % (SUBLANES * SEL_UNROLL) == 0 and tk % (2 * SUBLANES * SEL_UNROLL16) == 0
    return pl.pallas_call(
        functools.partial(_select_t_kernel, tq=tq, tk=tk, topk=topk, unroll=SEL_UNROLL,
                          unroll16=SEL_UNROLL16),
        out_shape=jax.ShapeDtypeStruct((n, n), jnp.int8),
        grid=(n // tq,),
        in_specs=[pl.BlockSpec((tq, qi.shape[1]), lambda i: (i, 0)),
                  pl.BlockSpec((IDX_HEADS, tq), lambda i: (0, i)),
                  pl.BlockSpec((n, LANES), lambda i: (0, 0))],
        out_specs=pl.BlockSpec((n, tq), lambda i: (0, i)),
        scratch_shapes=[pltpu.VMEM((n, tq), F32), pltpu.VMEM((n, tq), BF16)]
        + [pltpu.VMEM((SUBLANES, tq), I32)] * 4,
        compiler_params=_cparams(("parallel",)),
        name="dsa_select",
    )(qi, wit, ki)


I16 = jnp.int16
L16_MIN = -32768


def _select16_kernel(qi_ref, wit_ref, ki_ref, mask_ref, h16_ref, l16_ref, lo_ref, hi_ref, clo_ref,
                     chi_ref, base_ref, *, tq, tk, topk, unroll16):
    qb = pl.program_id(0)
    n_all = mask_ref.shape[0]
    nkt = (qb * tq + tq - 1) // tk + 1
    nrow = nkt * tk
    rows16 = 2 * SUBLANES * unroll16
    ntrip = nrow // rows16

    qm = []
    for hh in range(IDX_HEADS):
        blk = qi_ref[:, (hh // 2) * LANES:(hh // 2 + 1) * LANES]
        qm.append(jnp.where(_lane_head_mask(blk.shape, hh % 2), blk, jnp.zeros_like(blk)))
    qm = jnp.concatenate(qm, axis=0)
    wt = wit_ref[...]
    causal_slack = (qb * tq + lax.broadcasted_iota(I32, (tk, tq), 1)
                    - lax.broadcasted_iota(I32, (tk, tq), 0))

    def score_tile(kt, _):
        k0 = pl.multiple_of(kt * tk, tk)
        s = _dot_nt(ki_ref[pl.ds(k0, tk), :], qm)
        acc = wt[0:1, :] * jnp.maximum(s[:, 0:tq], 0.0)
        for hh in range(1, IDX_HEADS):
            acc = acc + wt[hh:hh + 1, :] * jnp.maximum(s[:, hh * tq:(hh + 1) * tq], 0.0)
        acc = jnp.where(acc == 0.0, 0.0, acc)
        sc = jnp.where(k0 <= causal_slack, acc, -jnp.inf)
        bits = lax.bitcast_convert_type(sc, I32)
        key = bits ^ ((bits >> 31) & 0x7FFFFFFF)
        h16_ref[pl.ds(k0, tk), :] = (key >> 16).astype(I16)
        l16_ref[pl.ds(k0, tk), :] = (key ^ 0x8000).astype(I16)
        return 0

    lax.fori_loop(0, nkt, score_tile, 0)

    one16 = jnp.ones((), I16)
    zero16 = jnp.zeros((), I16)
    sub16 = (lax.broadcasted_iota(I32, (unroll16, 2 * SUBLANES, tq), 0) * (2 * SUBLANES)
             + lax.broadcasted_iota(I32, (unroll16, 2 * SUBLANES, tq), 1)).astype(I16)

    def pack16(v):
        return jnp.concatenate([v, v], axis=0).astype(I16)

    def count(pred):
        def body(i, acc):
            r0 = pl.multiple_of(i * rows16, rows16)
            hb = h16_ref[pl.ds(r0, rows16), :].reshape(unroll16, 2 * SUBLANES, tq)
            lb = l16_ref[pl.ds(r0, rows16), :].reshape(unroll16, 2 * SUBLANES, tq)
            ones = jnp.where(pred(hb, lb, r0), one16, zero16)
            part = ones[0]
            for u in range(1, unroll16):
                part = part + ones[u]
            return acc + part.astype(I32)
        acc = lax.fori_loop(0, ntrip, body, jnp.zeros((2 * SUBLANES, tq), I32))
        return jnp.broadcast_to(jnp.sum(acc, axis=0, keepdims=True), (SUBLANES, tq))

    def bisect(count_ge, n_steps):
        def step(_, carry):
            lo, hi = lo_ref[...], hi_ref[...]
            mid = (lo + hi) >> 1
            active = mid != lo
            cnt = count_ge(pack16(mid)) + base_ref[...]
            ge = cnt >= topk
            up = active & ge
            dn = active & jnp.logical_not(ge)
            lo_ref[...] = jnp.where(up, mid, lo)
            hi_ref[...] = jnp.where(dn, mid, hi)
            clo_ref[...] = jnp.where(up, cnt, clo_ref[...])
            chi_ref[...] = jnp.where(dn, cnt, chi_ref[...])
            return carry
        lax.fori_loop(0, n_steps, step, 0)

    lo_ref[...] = jnp.full(lo_ref.shape, KEY16_NEG_INF, I32)
    hi_ref[...] = jnp.full(hi_ref.shape, KEY16_POS_INF + 1, I32)
    clo_ref[...] = jnp.full(clo_ref.shape, 1, I32) * nrow
    chi_ref[...] = jnp.zeros(chi_ref.shape, I32)
    base_ref[...] = jnp.zeros(base_ref.shape, I32)
    bisect(lambda c: count(lambda hb, lb, r0: hb >= c[None]), 16)

    t_hi = lo_ref[...]
    none_finite = t_hi == KEY16_NEG_INF
    t_hi16 = pack16(t_hi)

    def bucket_tile(i, _):
        r0 = pl.multiple_of(i * rows16, rows16)
        hb = h16_ref[pl.ds(r0, rows16), :].reshape(unroll16, 2 * SUBLANES, tq)
        lb = l16_ref[pl.ds(r0, rows16), :].reshape(unroll16, 2 * SUBLANES, tq)
        lb = jnp.where(hb == t_hi16[None], lb, jnp.full((), L16_MIN, I16))
        l16_ref[pl.ds(r0, rows16), :] = lb.reshape(rows16, tq)
        return 0

    lax.fori_loop(0, ntrip, bucket_tile, 0)
    base_ref[...] = chi_ref[...]
    lo_ref[...] = jnp.full(lo_ref.shape, L16_MIN, I32)
    hi_ref[...] = jnp.full(hi_ref.shape, -L16_MIN, I32)
    bisect(lambda c: count(lambda hb, lb, r0: lb >= c[None]), 16)
    t_lo = jnp.where(none_finite, -L16_MIN - 1, lo_ref[...])

    tie = (clo_ref[...] > topk) & jnp.logical_not(none_finite)
    any_tie = jnp.max(tie.astype(I32))
    t_lo16 = pack16(t_lo)

    @pl.when(any_tie == 0)
    def _():
        hi_ref[...] = jnp.full(hi_ref.shape, n_all, I32)

    @pl.when(any_tie > 0)
    def _():
        base_ref[...] = jnp.zeros(base_ref.shape, I32)
        chi_ref[...] = topk - chi_ref[...]
        hi_ref[...] = jnp.full(hi_ref.shape, 1, I32) * (nrow - 1)
        clo_ref[...] = jnp.full(clo_ref.shape, -1, I32)

        def tie_step(_, c):
            jl, jh = clo_ref[...], hi_ref[...]
            mid = jl + ((jh - jl) >> 1)
            active = (jh - jl) > 1
            cnt = count(lambda hb, lb, r0: (hb == t_hi16[None]) & (lb == t_lo16[None])
                        & (sub16 <= pack16(mid - r0)[None]))
            ok = cnt >= chi_ref[...]
            hi_ref[...] = jnp.where(active & ok, mid, jh)
            clo_ref[...] = jnp.where(active & jnp.logical_not(ok), mid, jl)
            return c

        lax.fori_loop(0, int(math.ceil(math.log2(n_all))) + 1, tie_step, 0)
        hi_ref[...] = jnp.where(tie, hi_ref[...], n_all)

    last_row = hi_ref[...]
    g16 = 2 * SUBLANES
    row16 = (lax.broadcasted_iota(I32, (tk // g16, g16, tq), 0) * g16
             + lax.broadcasted_iota(I32, (tk // g16, g16, tq), 1)).astype(I16)

    def mask_tile(kt, _):
        k0 = pl.multiple_of(kt * tk, tk)
        hb = h16_ref[pl.ds(k0, tk), :].reshape(tk // g16, g16, tq)
        lb = l16_ref[pl.ds(k0, tk), :].reshape(tk // g16, g16, tq)
        th, tl = t_hi16[None], t_lo16[None]
        at_thr = (lb == tl) & (row16 <= pack16(last_row - k0)[None])
        sel = ((hb > th) | ((hb == th) & ((lb > tl) | at_thr))) & (hb > KEY16_NEG_INF)
        sel = jnp.where(sel, one16, zero16).reshape(tk, tq)
        mask_ref[pl.ds(k0, tk), :] = sel.astype(mask_ref.dtype)
        return 0

    lax.fori_loop(0, nkt, mask_tile, 0)

    def zero_tile(kt, _):
        k0 = pl.multiple_of(kt * tk, tk)
        mask_ref[pl.ds(k0, tk), :] = jnp.zeros((tk, tq), mask_ref.dtype)
        return 0

    lax.fori_loop(nkt, n_all // tk, zero_tile, 0)


def _dsa_select16(qi, wit, ki, topk):
    n = qi.shape[0]
    tq = min(SEL_TQ, n)
    tk = min(SEL_TK, n)
    assert tk % (2 * SUBLANES * SEL_UNROLL16) == 0 and n < -L16_MIN
    return pl.pallas_call(
        functools.partial(_select16_kernel, tq=tq, tk=tk, topk=topk, unroll16=SEL_UNROLL16),
        out_shape=jax.ShapeDtypeStruct((n, n), jnp.int8),
        grid=(n // tq,),
        in_specs=[pl.BlockSpec((tq, qi.shape[1]), lambda i: (i, 0)),
                  pl.BlockSpec((IDX_HEADS, tq), lambda i: (0, i)),
                  pl.BlockSpec((n, LANES), lambda i: (0, 0))],
        out_specs=pl.BlockSpec((n, tq), lambda i: (0, i)),
        scratch_shapes=[pltpu.VMEM((n, tq), I16), pltpu.VMEM((n, tq), I16)]
        + [pltpu.VMEM((SUBLANES, tq), I32)] * 5,
        compiler_params=_cparams(("parallel",)),
        name="dsa_select",
    )(qi, wit, ki)


def _attn_t_kernel(qb_tab, kt_tab, q_ref, k_ref, vt_ref, mask_ref, o_ref, acc_ref, m_ref,
                   *, tq, tk, qs):
    step = pl.program_id(0)
    kt = kt_tab[step]

    @pl.when(kt == 0)
    def _():
        acc_ref[...] = jnp.zeros_like(acc_ref)
        m_ref[...] = jnp.full(m_ref.shape, NEG_BIG, MXU_DTYPE).astype(F32)

    bias = jnp.where(mask_ref[...].astype(I32) != 0, 0.0, NEG_BIG).astype(MXU_DTYPE)

    def logits(hd):
        blk = hd // 2
        qblk = q_ref[:, blk * LANES:(blk + 1) * LANES]
        qm = jnp.where(_lane_head_mask(qblk.shape, hd % 2), qblk, jnp.zeros_like(qblk))
        return _dot_nt(k_ref[:, blk * LANES:(blk + 1) * LANES], qm).astype(MXU_DTYPE) + bias

    s_next = logits(0)
    for hd in range(N_HEADS):
        r0 = hd * VT_ROWS
        s = s_next
        if hd + 1 < N_HEADS:
            s_next = logits(hd + 1)
        m_old = m_ref[hd:hd + 1, :]
        m_new = jnp.maximum(m_old, jnp.max(s, axis=0, keepdims=True).astype(F32))
        alpha = jnp.exp2(m_old - m_new)
        p = jnp.exp2(s - m_new.astype(MXU_DTYPE))
        m_ref[hd:hd + 1, :] = m_new
        pv = _dot(vt_ref[r0:r0 + VT_ROWS, :], p)
        acc_ref[r0:r0 + VT_ROWS, :] = acc_ref[r0:r0 + VT_ROWS, :] * alpha + pv

    last = kt == ((qb_tab[step] + 1) * tq - 1) // tk

    @pl.when(last)
    def _():
        for hd in range(N_HEADS):
            r0 = hd * VT_ROWS
            o_ref[hd * HEAD_DIM:(hd + 1) * HEAD_DIM, :] = (
                acc_ref[r0:r0 + HEAD_DIM, :] / acc_ref[r0 + HEAD_DIM:r0 + HEAD_DIM + 1, :]
            ).astype(o_ref.dtype)


def _dsa_attend_t(q, k, vt, mask_t):
    n, d = q.shape
    tq = min(ATT_TQ, n)
    tk = min(ATT_TK, n)
    pairs = [(qb, kt) for qb in range(n // tq) for kt in range(((qb + 1) * tq - 1) // tk + 1)]
    qb_tab = jnp.asarray([p[0] for p in pairs], I32)
    kt_tab = jnp.asarray([p[1] for p in pairs], I32)
    grid_spec = pltpu.PrefetchScalarGridSpec(
        num_scalar_prefetch=2,
        grid=(len(pairs),),
        in_specs=[
            pl.BlockSpec((tq, d), lambda i, qt, kt: (qt[i], 0)),
            pl.BlockSpec((tk, d), lambda i, qt, kt: (kt[i], 0)),
            pl.BlockSpec((vt.shape[0], tk), lambda i, qt, kt: (0, kt[i])),
            pl.BlockSpec((tk, tq), lambda i, qt, kt: (kt[i], qt[i])),
        ],
        out_specs=pl.BlockSpec((d, tq), lambda i, qt, kt: (0, qt[i])),
        scratch_shapes=[pltpu.VMEM((vt.shape[0], tq), F32), pltpu.VMEM((N_HEADS, tq), F32)],
    )
    return pl.pallas_call(
        functools.partial(_attn_t_kernel, tq=tq, tk=tk, qs=min(ATT_QS, tq)),
        out_shape=jax.ShapeDtypeStruct((d, n), MXU_DTYPE),
        grid_spec=grid_spec,
        compiler_params=_cparams(("arbitrary",)),
        name="dsa_attend",
    )(qb_tab, kt_tab, q, k, vt, mask_t)


def _out_t_kernel(x_ref, at_ref, w_ref, o_ref):
    o_ref[...] = x_ref[...] + lax.dot_general(
        at_ref[...], w_ref[...], (((0,), (0,)), ((), ())), preferred_element_type=F32)


def _out_residual_t(x, at, w):
    n, d = x.shape
    tm = min(ROW_TM, n)
    return pl.pallas_call(
        _out_t_kernel,
        out_shape=jax.ShapeDtypeStruct((n, d), F32),
        grid=(n // tm,),
        in_specs=[pl.BlockSpec((tm, d), lambda i: (i, 0)),
                  pl.BlockSpec((d, tm), lambda i: (0, i)),
                  pl.BlockSpec((d, d), lambda i: (0, 0))],
        out_specs=pl.BlockSpec((tm, d), lambda i: (i, 0)),
        compiler_params=_cparams(("parallel",)),
        name="attn_out",
    )(x, at, w)


def _rope_lane_tables(length):
    inv_freq = ROPE_THETA ** (-jnp.arange(0, HEAD_DIM, 2, dtype=F32) / HEAD_DIM)
    ang = jnp.arange(length, dtype=F32)[:, None] * inv_freq[None, :]
    lane = np.arange(LANES)
    cos_t = jnp.cos(ang)[:, lane % 32]
    sin_t = jnp.sin(ang)[:, lane % 32] * jnp.asarray(np.where(lane < 64, -1.0, 1.0), F32)
    return cos_t, sin_t


def kernel(x, s5_lambda_re, s5_lambda_im, s5_log_dt, s5_b_re, s5_b_im, s5_c_re, s5_c_im, s5_d, s5_w_glu, dsa_w_in, dsa_q_norm, dsa_k_norm, dsa_w_o, ffn_w_gate_up, ffn_w_down, norm_mix, norm_ffn):
    bsz, length, d = x.shape
    depth = norm_mix.shape[0]
    topk = min(TOPK_MAX, length // 4)
    nchunk = length // S5_CHUNK
    cos_t, sin_t = _rope_lane_tables(length)
    outs = []
    for b in range(bsz):
        xs = x[b].astype(F32)
        for i in range(depth):
            j = i // 2
            wgu = ffn_w_gate_up[i].astype(MXU_DTYPE)
            wd = ffn_w_down[i].astype(MXU_DTYPE)
            if i % 2 == 0:
                tables = _s5_tables(s5_lambda_re[j], s5_lambda_im[j], s5_log_dt[j], s5_b_re[j],
                                    s5_b_im[j], s5_c_re[j], s5_c_im[j])
                xt = xs.reshape(nchunk, S5_CHUNK, d).transpose(1, 0, 2).reshape(length, d)
                h_t = _norm(xt, norm_mix[i]).reshape(S5_CHUNK, nchunk, d)
                g_t = _s5_scan(h_t, tables, s5_d[j]).reshape(length, d)
                xt = _glu_residual(xt, g_t, s5_w_glu[j].astype(MXU_DTYPE))
                xt = _ffn(xt, norm_ffn[i], wgu, wd)
                xs = xt.reshape(S5_CHUNK, nchunk, d).transpose(1, 0, 2).reshape(length, d)
            else:
                q, k, vt, qi, ki, wi = _dsa_project(xs, norm_mix[i], dsa_w_in[j], dsa_q_norm[j],
                                                    dsa_k_norm[j], cos_t, sin_t)
                mask_t = _dsa_select16(qi, wi[:, :IDX_HEADS].T, ki, topk)
                att_t = _dsa_attend_t(q, k, vt, mask_t)
                xs = _out_residual_t(xs, att_t, dsa_w_o[j].astype(MXU_DTYPE))
                xs = _ffn(xs, norm_ffn[i], wgu, wd)
        outs.append(xs)
    return jnp.stack(outs, axis=0).astype(x.dtype)
```

```python
import functools
import math

import jax
import jax.numpy as jnp
import numpy as np
from jax import lax
from jax.experimental import pallas as pl
from jax.experimental.pallas import tpu as pltpu

F32 = jnp.float32
BF16 = jnp.bfloat16
I32 = jnp.int32
MXU_DTYPE = BF16

D_MODEL = 1024
S5_GROUP = 16
S5_STATE = 64
N_HEADS = 16
HEAD_DIM = 64
IDX_HEADS = 8
IDX_DIM = 64
TOPK_MAX = 256
ROPE_THETA = 10000.0
EPS = 1e-6

LANES = 128
SUBLANES = 8
MXU_DIM = 256
VMEM_LIMIT = 56 * 1024 * 1024

S5_CHUNK = 16
S5_SLAB_GROUPS = LANES // S5_GROUP
NEG_BIG = -1e30
LOG2E = math.log2(math.e)
VT_PAD = 16
VT_ROWS = HEAD_DIM + VT_PAD

ROW_TM = 512
NORM_TM = 1024
S5_TC = 512
PROJ_TM = 512
SEL_TQ = 256
SEL_TK = 512
SEL_RB = 64
SEL_UNROLL = 64
SEL_UNROLL16 = 32
ATT_TQ = 512
ATT_TK = 512
ATT_QS = 512


def _cparams(sem, flags=None):
    return pltpu.CompilerParams(dimension_semantics=sem, vmem_limit_bytes=VMEM_LIMIT, flags=flags)


def _rms(x, gain=None):
    y = x * lax.rsqrt(jnp.mean(x * x, axis=-1, keepdims=True) + EPS)
    return y if gain is None else y * gain


def _dot(a, b):
    return jnp.dot(a, b, preferred_element_type=F32)


def _dot_nt(a, b):
    return lax.dot_general(a, b, (((1,), (1,)), ((), ())), preferred_element_type=F32)


def _ffn_kernel(x_ref, g_ref, wgu_ref, wd_ref, o_ref, acc_ref, *, d_ff, fc):
    x = x_ref[...]
    h = _rms(x, g_ref[...]).astype(MXU_DTYPE)
    for c in range(d_ff // fc):
        g = _dot(h, wgu_ref[:, c * fc:(c + 1) * fc])
        u = _dot(h, wgu_ref[:, d_ff + c * fc:d_ff + (c + 1) * fc])
        a = (g * jax.nn.sigmoid(g) * u).astype(MXU_DTYPE)
        d = _dot(a, wd_ref[c * fc:(c + 1) * fc, :])
        if c == 0:
            acc_ref[...] = d
        else:
            acc_ref[...] += d
    o_ref[...] = x + acc_ref[...]


def _ffn(x, gain, wgu, wd):
    n, d = x.shape
    d_ff = wd.shape[0]
    tm = min(ROW_TM, n)
    fc = MXU_DIM
    return pl.pallas_call(
        functools.partial(_ffn_kernel, d_ff=d_ff, fc=fc),
        out_shape=jax.ShapeDtypeStruct((n, d), F32),
        grid=(n // tm,),
        in_specs=[
            pl.BlockSpec((tm, d), lambda i: (i, 0)),
            pl.BlockSpec((1, d), lambda i: (0, 0)),
            pl.BlockSpec((d, 2 * d_ff), lambda i: (0, 0)),
            pl.BlockSpec((d_ff, d), lambda i: (0, 0)),
        ],
        out_specs=pl.BlockSpec((tm, d), lambda i: (i, 0)),
        scratch_shapes=[pltpu.VMEM((tm, d), F32)],
        compiler_params=_cparams(("parallel",)),
        name="ffn",
    )(x, gain.reshape(1, d), wgu, wd)


def _norm_kernel(x_ref, g_ref, o_ref):
    o_ref[...] = _rms(x_ref[...], g_ref[...]).astype(o_ref.dtype)


def _norm(x, gain):
    n, d = x.shape
    tm = min(NORM_TM, n)
    return pl.pallas_call(
        _norm_kernel,
        out_shape=jax.ShapeDtypeStruct((n, d), MXU_DTYPE),
        grid=(n // tm,),
        in_specs=[pl.BlockSpec((tm, d), lambda i: (i, 0)),
                  pl.BlockSpec((1, d), lambda i: (0, 0))],
        out_specs=pl.BlockSpec((tm, d), lambda i: (i, 0)),
        compiler_params=_cparams(("parallel",)),
        name="norm",
    )(x, gain.reshape(1, d))


def _s5_tables(lam_re, lam_im, log_dt, b_re, b_im, c_re, c_im):
    hp = lax.Precision.HIGHEST
    g, p = lam_re.shape
    h = S5_GROUP
    nsl = g // S5_SLAB_GROUPS
    sg = S5_SLAB_GROUPS
    t = S5_CHUNK
    lam_re, lam_im, log_dt = lam_re.astype(F32), lam_im.astype(F32), log_dt.astype(F32)
    b_re, b_im, c_re, c_im = (a.astype(F32) for a in (b_re, b_im, c_re, c_im))
    dt = jnp.exp(log_dt)[:, None]

    def apow(k):
        k = jnp.asarray(k, F32).reshape((-1, 1, 1))
        mag = jnp.exp(lam_re[None] * dt[None] * k)
        ang = lam_im[None] * dt[None] * k
        return mag * jnp.cos(ang), mag * jnp.sin(ang)

    ar, ai = apow([1.0])
    ar, ai = ar[0], ai[0]
    den = lam_re * lam_re + lam_im * lam_im
    nr, ni = ar - 1.0, ai
    qr = (nr * lam_re + ni * lam_im) / den
    qi = (ni * lam_re - nr * lam_im) / den
    bbr = qr[..., None] * b_re - qi[..., None] * b_im
    bbi = qr[..., None] * b_im + qi[..., None] * b_re

    pr, pi = apow(np.arange(t + 1))
    mr = c_re[None] * pr[:, :, None, :] - c_im[None] * pi[:, :, None, :]
    mi = c_re[None] * pi[:, :, None, :] + c_im[None] * pr[:, :, None, :]

    kk = (jnp.einsum('tghp,gpk->tghk', mr[:t], bbr, precision=hp)
          - jnp.einsum('tghp,gpk->tghk', mi[:t], bbi, precision=hp))
    kp = jnp.concatenate([jnp.zeros_like(kk[:1]), kk], axis=0)
    eye = jnp.eye(sg, dtype=F32)

    dl = np.arange(t // 2)[:, None, None]
    sl = np.arange(2)[None, :, None]
    jl = np.arange(2)[None, None, :]
    idx = 2 * dl + jl - sl + 1
    kg = kp[idx]
    kg = kg.reshape(t // 2, 2, 2, nsl, sg, h, h)
    tp = jnp.einsum('dljbgok,gm->bdlgkjmo', kg, eye)
    tp = tp.reshape(nsl, t // 2, 2 * sg * h, 2 * sg * h)

    prs, pis = pr[t - 1::-1][:t], pi[t - 1::-1][:t]
    er = prs[..., None] * bbr[None] - pis[..., None] * bbi[None]
    ei = prs[..., None] * bbi[None] + pis[..., None] * bbr[None]
    bf = jnp.stack([er, ei], axis=2)
    bf = bf.reshape(t // 2, 2, nsl, sg, 2, p, h)
    bz = jnp.einsum('zlbgrpk,gm->bzlgkrmp', bf, eye)
    bz = bz.reshape(nsl, t // 2, 2 * sg * h, 2 * sg * p)

    cf = jnp.stack([mr[1:], -mi[1:]], axis=2)
    cf = cf.reshape(t // 2, 2, nsl, sg, 2, h, p)
    cz = jnp.einsum('ijbgrop,gm->birgpjmo', cf, eye)
    cz = cz.reshape(nsl, t // 2, 2 * sg * p, 2 * sg * h)

    def slab_state(re, im):
        k = re.shape[0]
        x = jnp.stack([re, im], axis=1).reshape(k, 2, nsl, sg * p)
        return x.transpose(2, 0, 1, 3).reshape(nsl, k, 2 * sg * p)

    ad = slab_state(*apow([t * 1.0, t * 2.0, t * 4.0]))
    ap8 = slab_state(*apow(t * (np.arange(SUBLANES) + 1.0)))
    return (tp.astype(MXU_DTYPE), bz.astype(MXU_DTYPE), cz.astype(MXU_DTYPE), ad, ap8)


def _s5_kernel(h_ref, tp_ref, bz_ref, cz_ref, ad_ref, ap8_ref, dsk_ref, o_ref,
               carry_ref, z_ref, xp_ref, *, tc):
    half = z_ref.shape[1] // 2
    npair = S5_CHUNK // 2

    @pl.when(pl.program_id(1) == 0)
    def _():
        carry_ref[...] = jnp.zeros_like(carry_ref)

    u = [jnp.concatenate([h_ref[2 * s], h_ref[2 * s + 1]], axis=1) for s in range(npair)]

    z = _dot(u[0], bz_ref[0])
    for s in range(1, npair):
        z = z + _dot(u[s], bz_ref[s])
    z_ref[...] = z

    row = lax.broadcasted_iota(I32, (SUBLANES, half), 0)
    pr8, pi8 = ap8_ref[:, :half], ap8_ref[:, half:]

    def tile_step(t, carry):
        r0 = pl.multiple_of(t * SUBLANES, SUBLANES)
        zt = z_ref[pl.ds(r0, SUBLANES), :]
        xr, xi = zt[:, :half], zt[:, half:]
        for k, d in enumerate((1, 2, 4)):
            a = ad_ref[k:k + 1, :]
            a_r, a_i = a[:, :half], a[:, half:]
            sr = jnp.where(row >= d, pltpu.roll(xr, d, 0), 0.0)
            si = jnp.where(row >= d, pltpu.roll(xi, d, 0), 0.0)
            xr, xi = xr + a_r * sr - a_i * si, xi + a_r * si + a_i * sr
        cr, ci = carry[:, :half], carry[:, half:]
        xr, xi = xr + pr8 * cr - pi8 * ci, xi + pr8 * ci + pi8 * cr
        xpr = jnp.where(row >= 1, pltpu.roll(xr, 1, 0), cr)
        xpi = jnp.where(row >= 1, pltpu.roll(xi, 1, 0), ci)
        xp_ref[pl.ds(r0, SUBLANES), :] = jnp.concatenate([xpr, xpi], axis=1)
        return jnp.concatenate([xr[SUBLANES - 1:], xi[SUBLANES - 1:]], axis=1)

    carry_ref[...] = lax.fori_loop(0, tc // SUBLANES, tile_step, carry_ref[...])

    xp = xp_ref[...].astype(MXU_DTYPE)
    dsk = dsk_ref[...]
    for i in range(npair):
        y = _dot(xp, cz_ref[i])
        for s in range(i + 1):
            y = y + _dot(u[s], tp_ref[i - s])
        for jl in range(2):
            j = 2 * i + jl
            yj = y[:, jl * LANES:(jl + 1) * LANES] + dsk * h_ref[j].astype(F32)
            o_ref[j] = jax.nn.gelu(yj).astype(o_ref.dtype)


def _s5_scan(h_t, tables, d_skip):
    tp, bz, cz, ad, ap8 = tables
    t, c, d = h_t.shape
    nsl = d // LANES
    tc = min(S5_TC, c)
    st = bz.shape[-1]
    dsk = d_skip.astype(F32).reshape(nsl, 1, LANES)
    return pl.pallas_call(
        functools.partial(_s5_kernel, tc=tc),
        out_shape=jax.ShapeDtypeStruct((t, c, d), MXU_DTYPE),
        grid=(nsl, c // tc),
        in_specs=[
            pl.BlockSpec((t, tc, LANES), lambda b, i: (0, i, b)),
            pl.BlockSpec((None,) + tp.shape[1:], lambda b, i: (b, 0, 0, 0)),
            pl.BlockSpec((None,) + bz.shape[1:], lambda b, i: (b, 0, 0, 0)),
            pl.BlockSpec((None,) + cz.shape[1:], lambda b, i: (b, 0, 0, 0)),
            pl.BlockSpec((None,) + ad.shape[1:], lambda b, i: (b, 0, 0)),
            pl.BlockSpec((None,) + ap8.shape[1:], lambda b, i: (b, 0, 0)),
            pl.BlockSpec((None, 1, LANES), lambda b, i: (b, 0, 0)),
        ],
        out_specs=pl.BlockSpec((t, tc, LANES), lambda b, i: (0, i, b)),
        scratch_shapes=[pltpu.VMEM((1, st), F32), pltpu.VMEM((tc, st), F32),
                        pltpu.VMEM((tc, st), F32)],
        compiler_params=_cparams(("arbitrary", "arbitrary")),
        name="s5_scan",
    )(h_t, tp, bz, cz, ad, ap8, dsk)


def _glu_kernel(x_ref, g_ref, w_ref, o_ref):
    d = x_ref.shape[1]
    vg = _dot(g_ref[...], w_ref[...])
    o_ref[...] = x_ref[...] + vg[:, :d] * jax.nn.sigmoid(vg[:, d:])


def _glu_residual(x, g, w):
    n, d = x.shape
    tm = min(ROW_TM, n)
    return pl.pallas_call(
        _glu_kernel,
        out_shape=jax.ShapeDtypeStruct((n, d), F32),
        grid=(n // tm,),
        in_specs=[pl.BlockSpec((tm, d), lambda i: (i, 0)),
                  pl.BlockSpec((tm, d), lambda i: (i, 0)),
                  pl.BlockSpec((d, 2 * d), lambda i: (0, 0))],
        out_specs=pl.BlockSpec((tm, d), lambda i: (i, 0)),
        compiler_params=_cparams(("parallel",)),
        name="glu",
    )(x, g, w)


def _head_perm(n_heads):
    n = np.arange(n_heads * HEAD_DIM)
    pb, r = n // LANES, n % LANES
    half, r2 = r // 64, r % 64
    hl, dp = r2 // 32, r2 % 32
    return (2 * pb + hl) * HEAD_DIM + 32 * half + dp


def _lane_head_mask(shape, hl):
    lane = lax.broadcasted_iota(I32, shape, len(shape) - 1)
    return ((lane % 64) // 32) == hl


def _proj_kernel(x_ref, g_ref, w_ref, wvt_ref, gq_ref, gk_ref, cos_ref, sin_ref, hm_ref,
                 q_ref, k_ref, vt_ref, qi_ref, ki_ref, wi_ref, *, d, dqi, att_scale, w_scale):
    h = _rms(x_ref[...], g_ref[...]).astype(MXU_DTYPE)
    cos, sin = cos_ref[...], sin_ref[...]
    hm = hm_ref[...]

    def rope(t):
        return t * cos + pltpu.roll(t, 64, 1) * sin

    def headnorm_rope(col0, gain_ref, out_ref, scale):
        for sb in range(d // MXU_DIM):
            c0 = sb * MXU_DIM
            t = _dot(h, w_ref[:, col0 + c0:col0 + c0 + MXU_DIM])
            sq = t * t
            hi = sq.astype(MXU_DTYPE)
            lo = (sq - hi.astype(F32)).astype(MXU_DTYPE)
            ss = _dot(hi, hm) + _dot(lo, hm)
            tn = t * lax.rsqrt(ss * (1.0 / HEAD_DIM) + EPS) * gain_ref[:, c0:c0 + MXU_DIM]
            for b in range(MXU_DIM // LANES):
                r = rope(tn[:, b * LANES:(b + 1) * LANES])
                if scale != 1.0:
                    r = r * scale
                out_ref[:, c0 + b * LANES:c0 + (b + 1) * LANES] = r.astype(out_ref.dtype)

    headnorm_rope(0, gq_ref, q_ref, att_scale)
    headnorm_rope(d, gk_ref, k_ref, 1.0)
    vt = _dot_nt(wvt_ref[...], h)
    row = lax.broadcasted_iota(I32, vt.shape, 0)
    vt_ref[...] = jnp.where(row % VT_ROWS >= HEAD_DIM, 1.0, vt).astype(vt_ref.dtype)
    c0 = 2 * d
    t = _dot(h, w_ref[:, c0:c0 + dqi])
    for b in range(dqi // LANES):
        qi_ref[:, b * LANES:(b + 1) * LANES] = rope(t[:, b * LANES:(b + 1) * LANES]).astype(qi_ref.dtype)
    c0 += dqi
    t = _dot(h, w_ref[:, c0:c0 + LANES])
    ms = jnp.sum(t * t, axis=-1, keepdims=True) * (0.5 / IDX_DIM)
    ki_ref[...] = rope(t * lax.rsqrt(ms + EPS)).astype(ki_ref.dtype)
    c0 += LANES
    wi_ref[...] = _dot(h, w_ref[:, c0:c0 + LANES]) * w_scale


def _dsa_project(x, gain, w_in, q_gain, k_gain, cos_t, sin_t):
    n, d = x.shape
    dqi = IDX_HEADS * IDX_DIM
    pq = _head_perm(N_HEADS)
    pqi = _head_perm(IDX_HEADS)
    wq = w_in[:, 0:d][:, pq]
    wk = w_in[:, d:2 * d][:, pq]
    wvt = w_in[:, 2 * d:3 * d].T.reshape(N_HEADS, HEAD_DIM, d)
    wvt = jnp.pad(wvt, ((0, 0), (0, VT_PAD), (0, 0))).reshape(N_HEADS * VT_ROWS, d).astype(MXU_DTYPE)
    dvt = N_HEADS * VT_ROWS
    wqi = w_in[:, 3 * d:3 * d + dqi][:, pqi]
    lane = np.arange(LANES)
    wki = w_in[:, 3 * d + dqi:3 * d + dqi + IDX_DIM][:, 32 * (lane // 64) + lane % 32]
    wwi = jnp.pad(w_in[:, 3 * d + dqi + IDX_DIM:], ((0, 0), (0, LANES - IDX_HEADS)))
    w_all = jnp.concatenate([wq, wk, wqi, wki, wwi], axis=1).astype(MXU_DTYPE)
    dcol = (pq % HEAD_DIM)
    gq = q_gain.astype(F32)[dcol].reshape(1, d)
    gk = k_gain.astype(F32)[dcol].reshape(1, d)
    l2 = np.arange(MXU_DIM)
    hm = ((l2[:, None] // LANES == l2[None, :] // LANES)
          & ((l2[:, None] % 64) // 32 == (l2[None, :] % 64) // 32))
    hm = jnp.asarray(hm, MXU_DTYPE)
    tm = min(PROJ_TM, n)
    nw = w_all.shape[1]
    outs = pl.pallas_call(
        functools.partial(_proj_kernel, d=d, dqi=dqi, att_scale=HEAD_DIM ** -0.5 * LOG2E,
                          w_scale=(IDX_HEADS ** -0.5) * (IDX_DIM ** -0.5)),
        out_shape=[jax.ShapeDtypeStruct((n, d), MXU_DTYPE)] * 2
        + [jax.ShapeDtypeStruct((dvt, n), MXU_DTYPE),
           jax.ShapeDtypeStruct((n, dqi), MXU_DTYPE),
           jax.ShapeDtypeStruct((n, LANES), MXU_DTYPE),
           jax.ShapeDtypeStruct((n, LANES), F32)],
        grid=(n // tm,),
        in_specs=[
            pl.BlockSpec((tm, d), lambda i: (i, 0)),
            pl.BlockSpec((1, d), lambda i: (0, 0)),
            pl.BlockSpec((d, nw), lambda i: (0, 0)),
            pl.BlockSpec((dvt, d), lambda i: (0, 0)),
            pl.BlockSpec((1, d), lambda i: (0, 0)),
            pl.BlockSpec((1, d), lambda i: (0, 0)),
            pl.BlockSpec((tm, LANES), lambda i: (i, 0)),
            pl.BlockSpec((tm, LANES), lambda i: (i, 0)),
            pl.BlockSpec((MXU_DIM, MXU_DIM), lambda i: (0, 0)),
        ],
        out_specs=[pl.BlockSpec((tm, d), lambda i: (i, 0))] * 2
        + [pl.BlockSpec((dvt, tm), lambda i: (0, i)),
           pl.BlockSpec((tm, dqi), lambda i: (i, 0)),
           pl.BlockSpec((tm, LANES), lambda i: (i, 0)),
           pl.BlockSpec((tm, LANES), lambda i: (i, 0))],
        compiler_params=_cparams(("parallel",)),
        name="dsa_proj",
    )(x, gain.reshape(1, d), w_all, wvt, gq, gk, cos_t, sin_t, hm)
    return outs


KEY_NEG_INF = -2139095041
KEY_POS_INF = 2139095040
KEY16_NEG_INF = -32641
KEY16_POS_INF = 32640


def _key_to_f32(key):
    bits = key ^ ((key >> 31) & 0x7FFFFFFF)
    return lax.bitcast_convert_type(bits, F32)


def _select_kernel(qi_ref, wi_ref, ki_ref, mask_ref, s_ref, lo_ref, hi_ref, clo_ref, chi_ref,
                   *, tq, tk, topk, rb):
    qb = pl.program_id(0)
    n_all = mask_ref.shape[1]
    nkt = (qb * tq) // tk + 1
    ncol = nkt * (tk // LANES)

    qm = []
    for hh in range(IDX_HEADS):
        blk = qi_ref[:, (hh // 2) * LANES:(hh // 2 + 1) * LANES]
        qm.append(jnp.where(_lane_head_mask(blk.shape, hh % 2), blk, jnp.zeros_like(blk)))
    qm = jnp.concatenate(qm, axis=0)
    wv = wi_ref[...]
    row_pos = qb * tq + lax.broadcasted_iota(I32, (tq, tk), 0)
    col_in = lax.broadcasted_iota(I32, (tq, tk), 1)

    def score_tile(kt, _):
        k0 = pl.multiple_of(kt * tk, tk)
        s = _dot_nt(qm, ki_ref[pl.ds(k0, tk), :])
        acc = wv[:, 0:1] * jnp.maximum(s[0:tq], 0.0)
        for hh in range(1, IDX_HEADS):
            acc = acc + wv[:, hh:hh + 1] * jnp.maximum(s[hh * tq:(hh + 1) * tq], 0.0)
        acc = jnp.where(col_in + k0 <= row_pos, acc, -jnp.inf)
        s_ref[:, pl.ds(k0, tk)] = acc
        return 0

    lax.fori_loop(0, nkt, score_tile, 0)

    lo_ref[...] = jnp.full(lo_ref.shape, KEY_NEG_INF, I32)
    hi_ref[...] = jnp.full(hi_ref.shape, KEY_POS_INF, I32)
    clo_ref[...] = jnp.full(clo_ref.shape, 1, I32) * (ncol * LANES)
    chi_ref[...] = jnp.zeros(chi_ref.shape, I32)

    def count_rows(r0, pred):
        def body(j, acc):
            c0 = pl.multiple_of(j * LANES, LANES)
            return acc + pred(s_ref[r0:r0 + rb, pl.ds(c0, LANES)], j).astype(I32)
        acc = lax.fori_loop(0, ncol, body, jnp.zeros((rb, LANES), I32))
        return jnp.broadcast_to(jnp.sum(acc, axis=1, keepdims=True), (rb, LANES))

    def bisect_step(carry):
        it, _ = carry
        pending = jnp.zeros((), I32)
        for r0 in range(0, tq, rb):
            lo, hi = lo_ref[r0:r0 + rb, :], hi_ref[r0:r0 + rb, :]
            mid = (lo & hi) + ((lo ^ hi) >> 1)
            active = mid != lo
            cand = _key_to_f32(mid)
            cnt = count_rows(r0, lambda blk, j: blk >= cand)
            ge = cnt >= topk
            up = active & ge
            dn = active & jnp.logical_not(ge)
            hit = active & (cnt == topk)
            lo_ref[r0:r0 + rb, :] = jnp.where(up, mid, lo)
            hi_ref[r0:r0 + rb, :] = jnp.where(hit, mid + 1, jnp.where(dn, mid, hi))
            clo_ref[r0:r0 + rb, :] = jnp.where(up, cnt, clo_ref[r0:r0 + rb, :])
            chi_ref[r0:r0 + rb, :] = jnp.where(dn, cnt, chi_ref[r0:r0 + rb, :])
            pending = jnp.maximum(pending, jnp.max(active.astype(I32)))
        return it + 1, pending

    lax.while_loop(lambda c: (c[0] < 40) & (c[1] > 0), bisect_step,
                   (jnp.zeros((), I32), jnp.ones((), I32)))

    tie = (clo_ref[...] > topk) & (lo_ref[...] > KEY_NEG_INF)
    any_tie = jnp.max(tie.astype(I32))

    @pl.when(any_tie == 0)
    def _():
        hi_ref[...] = jnp.full(hi_ref.shape, n_all, I32)

    @pl.when(any_tie > 0)
    def _():
        need = topk - chi_ref[...]
        chi_ref[...] = need
        hi_ref[...] = jnp.full(hi_ref.shape, ncol * LANES - 1, I32)
        clo_ref[...] = jnp.full(clo_ref.shape, -1, I32)
        lane = lax.broadcasted_iota(I32, (rb, LANES), 1)

        def tie_step(_, c):
            for r0 in range(0, tq, rb):
                jl, jh = clo_ref[r0:r0 + rb, :], hi_ref[r0:r0 + rb, :]
                mid = jl + ((jh - jl) >> 1)
                active = (jh - jl) > 1
                thr = _key_to_f32(lo_ref[r0:r0 + rb, :])
                cnt = count_rows(r0, lambda blk, j: (blk == thr) & (lane + j * LANES <= mid))
                ok = cnt >= chi_ref[r0:r0 + rb, :]
                hi_ref[r0:r0 + rb, :] = jnp.where(active & ok, mid, jh)
                clo_ref[r0:r0 + rb, :] = jnp.where(active & jnp.logical_not(ok), mid, jl)
            return c

        lax.fori_loop(0, int(math.ceil(math.log2(n_all))) + 1, tie_step, 0)
        keep_all = jnp.logical_not(tie)
        hi_ref[...] = jnp.where(keep_all, n_all, hi_ref[...])

    lane_k = lax.broadcasted_iota(I32, (tq, tk), 1)
    thr_col = _key_to_f32(lo_ref[:, 0:1])
    last_col = hi_ref[:, 0:1]

    def mask_tile(kt, _):
        k0 = pl.multiple_of(kt * tk, tk)
        s = s_ref[:, pl.ds(k0, tk)]
        sel = ((s > thr_col) | ((s == thr_col) & (lane_k + k0 <= last_col))) & (s > -jnp.inf)
        mask_ref[:, pl.ds(k0, tk)] = jnp.where(sel, 1, 0).astype(mask_ref.dtype)
        return 0

    lax.fori_loop(0, nkt, mask_tile, 0)

    def zero_tile(kt, _):
        k0 = pl.multiple_of(kt * tk, tk)
        mask_ref[:, pl.ds(k0, tk)] = jnp.zeros((tq, tk), mask_ref.dtype)
        return 0

    lax.fori_loop(nkt, n_all // tk, zero_tile, 0)


def _dsa_select(qi, wi, ki, topk):
    n = qi.shape[0]
    tq = min(SEL_TQ, n)
    tk = min(SEL_TK, n)
    rb = min(SEL_RB, tq)
    return pl.pallas_call(
        functools.partial(_select_kernel, tq=tq, tk=tk, topk=topk, rb=rb),
        out_shape=jax.ShapeDtypeStruct((n, n), jnp.int8),
        grid=(n // tq,),
        in_specs=[pl.BlockSpec((tq, qi.shape[1]), lambda i: (i, 0)),
                  pl.BlockSpec((tq, LANES), lambda i: (i, 0)),
                  pl.BlockSpec((n, LANES), lambda i: (0, 0))],
        out_specs=pl.BlockSpec((tq, n), lambda i: (i, 0)),
        scratch_shapes=[pltpu.VMEM((tq, n), F32)] + [pltpu.VMEM((tq, LANES), I32)] * 4,
        compiler_params=_cparams(("parallel",)),
        name="dsa_select",
    )(qi, wi, ki)


def _attn_kernel(qb_tab, kt_tab, q_ref, k_ref, v_ref, mask_ref, o_ref, acc_ref, m_ref, l_ref,
                 *, tq, tk):
    step = pl.program_id(0)
    kt = kt_tab[step]
    d = q_ref.shape[1]
    slab = MXU_DIM
    heads_per_slab = slab // HEAD_DIM

    @pl.when(kt == 0)
    def _():
        acc_ref[...] = jnp.zeros_like(acc_ref)
        m_ref[...] = jnp.full(m_ref.shape, NEG_BIG, F32)
        l_ref[...] = jnp.zeros_like(l_ref)

    bias = jnp.where(mask_ref[...].astype(I32) != 0, 0.0, NEG_BIG)
    lane_s = lax.broadcasted_iota(I32, (tq, slab), 1) // HEAD_DIM

    for sb in range(d // slab):
        vs = v_ref[:, sb * slab:(sb + 1) * slab]
        lane_v = lax.broadcasted_iota(I32, vs.shape, 1) // HEAD_DIM
        alpha_l = jnp.zeros((tq, slab), F32)
        pv = jnp.zeros((tq, slab), F32)
        for i in range(heads_per_slab):
            hd = sb * heads_per_slab + i
            blk = hd // 2
            qblk = q_ref[:, blk * LANES:(blk + 1) * LANES]
            qm = jnp.where(_lane_head_mask(qblk.shape, hd % 2), qblk, jnp.zeros_like(qblk))
            s = _dot_nt(qm, k_ref[:, blk * LANES:(blk + 1) * LANES]) + bias
            m_old = m_ref[:, hd:hd + 1]
            m_new = jnp.maximum(m_old, jnp.max(s, axis=1, keepdims=True))
            alpha = jnp.exp(m_old - m_new)
            p = jnp.exp(s - m_new)
            l_ref[:, hd:hd + 1] = l_ref[:, hd:hd + 1] * alpha + jnp.sum(p, axis=1, keepdims=True)
            m_ref[:, hd:hd + 1] = m_new
            vm = jnp.where(lane_v == i, vs, jnp.zeros_like(vs))
            pv = pv + _dot(p.astype(MXU_DTYPE), vm)
            alpha_l = jnp.where(lane_s == i, alpha, alpha_l)
        acc_ref[:, sb * slab:(sb + 1) * slab] = acc_ref[:, sb * slab:(sb + 1) * slab] * alpha_l + pv

    last = kt == ((qb_tab[step] + 1) * tq - 1) // tk

    @pl.when(last)
    def _():
        lane_h = lax.broadcasted_iota(I32, (tq, d), 1) // HEAD_DIM
        l_l = jnp.zeros((tq, d), F32)
        for hd in range(N_HEADS):
            l_l = jnp.where(lane_h == hd, l_ref[:, hd:hd + 1], l_l)
        o_ref[...] = (acc_ref[...] / l_l).astype(o_ref.dtype)


def _dsa_attend(q, k, v, mask):
    n, d = q.shape
    tq = min(ATT_TQ, n)
    tk = min(ATT_TK, n)
    pairs = [(qb, kt) for qb in range(n // tq) for kt in range(((qb + 1) * tq - 1) // tk + 1)]
    qb_tab = jnp.asarray([p[0] for p in pairs], I32)
    kt_tab = jnp.asarray([p[1] for p in pairs], I32)
    grid_spec = pltpu.PrefetchScalarGridSpec(
        num_scalar_prefetch=2,
        grid=(len(pairs),),
        in_specs=[
            pl.BlockSpec((tq, d), lambda i, qt, kt: (qt[i], 0)),
            pl.BlockSpec((tk, d), lambda i, qt, kt: (kt[i], 0)),
            pl.BlockSpec((tk, d), lambda i, qt, kt: (kt[i], 0)),
            pl.BlockSpec((tq, tk), lambda i, qt, kt: (qt[i], kt[i])),
        ],
        out_specs=pl.BlockSpec((tq, d), lambda i, qt, kt: (qt[i], 0)),
        scratch_shapes=[pltpu.VMEM((tq, d), F32), pltpu.VMEM((tq, LANES), F32),
                        pltpu.VMEM((tq, LANES), F32)],
    )
    return pl.pallas_call(
        functools.partial(_attn_kernel, tq=tq, tk=tk),
        out_shape=jax.ShapeDtypeStruct((n, d), MXU_DTYPE),
        grid_spec=grid_spec,
        compiler_params=_cparams(("arbitrary",)),
        name="dsa_attend",
    )(qb_tab, kt_tab, q, k, v, mask)


def _out_kernel(x_ref, a_ref, w_ref, o_ref):
    o_ref[...] = x_ref[...] + _dot(a_ref[...], w_ref[...])


def _out_residual(x, a, w):
    n, d = x.shape
    tm = min(ROW_TM, n)
    return pl.pallas_call(
        _out_kernel,
        out_shape=jax.ShapeDtypeStruct((n, d), F32),
        grid=(n // tm,),
        in_specs=[pl.BlockSpec((tm, d), lambda i: (i, 0)),
                  pl.BlockSpec((tm, d), lambda i: (i, 0)),
                  pl.BlockSpec((d, d), lambda i: (0, 0))],
        out_specs=pl.BlockSpec((tm, d), lambda i: (i, 0)),
        compiler_params=_cparams(("parallel",)),
        name="attn_out",
    )(x, a, w)


def _select_t_kernel(qi_ref, wit_ref, ki_ref, mask_ref, s_ref, s16_ref, lo_ref, hi_ref, clo_ref,
                     chi_ref, *, tq, tk, topk, unroll, unroll16):
    qb = pl.program_id(0)
    n_all = mask_ref.shape[0]
    nkt = (qb * tq + tq - 1) // tk + 1
    nrow = nkt * tk
    rows_it = SUBLANES * unroll

    qm = []
    for hh in range(IDX_HEADS):
        blk = qi_ref[:, (hh // 2) * LANES:(hh // 2 + 1) * LANES]
        qm.append(jnp.where(_lane_head_mask(blk.shape, hh % 2), blk, jnp.zeros_like(blk)))
    qm = jnp.concatenate(qm, axis=0)
    wt = wit_ref[...]
    key_in = lax.broadcasted_iota(I32, (tk, tq), 0)
    q_pos = qb * tq + lax.broadcasted_iota(I32, (tk, tq), 1)

    def score_tile(kt, _):
        k0 = pl.multiple_of(kt * tk, tk)
        s = _dot_nt(ki_ref[pl.ds(k0, tk), :], qm)
        acc = wt[0:1, :] * jnp.maximum(s[:, 0:tq], 0.0)
        for hh in range(1, IDX_HEADS):
            acc = acc + wt[hh:hh + 1, :] * jnp.maximum(s[:, hh * tq:(hh + 1) * tq], 0.0)
        sc = jnp.where(key_in + k0 <= q_pos, acc, -jnp.inf)
        s_ref[pl.ds(k0, tk), :] = sc
        hi_bits = lax.bitcast_convert_type(sc, I32) & -65536
        s16_ref[pl.ds(k0, tk), :] = lax.bitcast_convert_type(hi_bits, F32).astype(BF16)
        return 0

    lax.fori_loop(0, nkt, score_tile, 0)

    lo_ref[...] = jnp.full(lo_ref.shape, KEY16_NEG_INF, I32)
    hi_ref[...] = jnp.full(hi_ref.shape, KEY16_POS_INF + 1, I32)
    clo_ref[...] = jnp.full(clo_ref.shape, 1, I32) * nrow
    chi_ref[...] = jnp.zeros(chi_ref.shape, I32)
    rows16 = 2 * SUBLANES * unroll16
    one16 = jnp.ones((), BF16)
    zero16 = jnp.zeros((), BF16)

    def count16(cand):
        def body(i, acc):
            r0 = pl.multiple_of(i * rows16, rows16)
            blk = s16_ref[pl.ds(r0, rows16), :].reshape(unroll16, 2 * SUBLANES, tq)
            ones = jnp.where(blk >= cand[None], one16, zero16)
            part = ones[0]
            for u in range(1, unroll16):
                part = part + ones[u]
            return acc + part.astype(F32)
        acc = lax.fori_loop(0, nrow // rows16, body, jnp.zeros((2 * SUBLANES, tq), F32))
        tot = jnp.sum(acc, axis=0, keepdims=True).astype(I32)
        return jnp.broadcast_to(tot, (SUBLANES, tq))

    def coarse_step(_, carry):
        lo, hi = lo_ref[...], hi_ref[...]
        mid = (lo + hi) >> 1
        active = mid != lo
        bits16 = (mid ^ ((mid >> 31) & 0x7FFF)) & 0xFFFF
        cand = lax.bitcast_convert_type(bits16 << 16, F32)
        cand = jnp.concatenate([cand, cand], axis=0).astype(BF16)
        cnt = count16(cand)
        ge = cnt >= topk
        up = active & ge
        dn = active & jnp.logical_not(ge)
        lo_ref[...] = jnp.where(up, mid, lo)
        hi_ref[...] = jnp.where(dn, mid, hi)
        clo_ref[...] = jnp.where(up, cnt, clo_ref[...])
        chi_ref[...] = jnp.where(dn, cnt, chi_ref[...])
        return carry

    lax.fori_loop(0, 16, coarse_step, 0)

    lo16 = lo_ref[...]
    none_finite = lo16 == KEY16_NEG_INF
    lo_ref[...] = jnp.where(none_finite, KEY_NEG_INF, lo16 << 16)
    hi_ref[...] = jnp.where(none_finite, KEY_NEG_INF + 1, (lo16 + 1) << 16)

    def count(pred):
        def body(i, acc):
            r0 = pl.multiple_of(i * rows_it, rows_it)
            blk = s_ref[pl.ds(r0, rows_it), :].reshape(unroll, SUBLANES, tq)
            return acc + jnp.sum(pred(blk, r0).astype(I32), axis=0)
        acc = lax.fori_loop(0, nrow // rows_it, body, jnp.zeros((SUBLANES, tq), I32))
        return jnp.broadcast_to(jnp.sum(acc, axis=0, keepdims=True), (SUBLANES, tq))

    def bisect_step(carry):
        it, _ = carry
        lo, hi = lo_ref[...], hi_ref[...]
        mid = (lo & hi) + ((lo ^ hi) >> 1)
        active = mid != lo
        cand = _key_to_f32(mid)
        cnt = count(lambda blk, r0: blk >= cand[None])
        ge = cnt >= topk
        up = active & ge
        dn = active & jnp.logical_not(ge)
        hit = active & (cnt == topk)
        lo_ref[...] = jnp.where(up, mid, lo)
        hi_ref[...] = jnp.where(hit, mid + 1, jnp.where(dn, mid, hi))
        clo_ref[...] = jnp.where(up, cnt, clo_ref[...])
        chi_ref[...] = jnp.where(dn, cnt, chi_ref[...])
        return it + 1, jnp.max(active.astype(I32))

    lax.while_loop(lambda c: (c[0] < 40) & (c[1] > 0), bisect_step,
                   (jnp.zeros((), I32), jnp.ones((), I32)))

    tie = (clo_ref[...] > topk) & (lo_ref[...] > KEY_NEG_INF)
    any_tie = jnp.max(tie.astype(I32))

    @pl.when(any_tie == 0)
    def _():
        hi_ref[...] = jnp.full(hi_ref.shape, n_all, I32)

    @pl.when(any_tie > 0)
    def _():
        chi_ref[...] = topk - chi_ref[...]
        hi_ref[...] = jnp.full(hi_ref.shape, 1, I32) * (nrow - 1)
        clo_ref[...] = jnp.full(clo_ref.shape, -1, I32)
        sub = (lax.broadcasted_iota(I32, (unroll, SUBLANES, tq), 0) * SUBLANES
               + lax.broadcasted_iota(I32, (unroll, SUBLANES, tq), 1))
        thr = _key_to_f32(lo_ref[...])

        def tie_step(_, c):
            jl, jh = clo_ref[...], hi_ref[...]
            mid = jl + ((jh - jl) >> 1)
            active = (jh - jl) > 1
            cnt = count(lambda blk, r0: (blk == thr[None]) & (sub + r0 <= mid[None]))
            ok = cnt >= chi_ref[...]
            hi_ref[...] = jnp.where(active & ok, mid, jh)
            clo_ref[...] = jnp.where(active & jnp.logical_not(ok), mid, jl)
            return c

        lax.fori_loop(0, int(math.ceil(math.log2(n_all))) + 1, tie_step, 0)
        hi_ref[...] = jnp.where(tie, hi_ref[...], n_all)

    thr_row = _key_to_f32(lo_ref[0:1, :])
    last_row = hi_ref[0:1, :]

    def mask_tile(kt, _):
        k0 = pl.multiple_of(kt * tk, tk)
        s = s_ref[pl.ds(k0, tk), :]
        sel = ((s > thr_row) | ((s == thr_row) & (key_in + k0 <= last_row))) & (s > -jnp.inf)
        mask_ref[pl.ds(k0, tk), :] = jnp.where(sel, 1, 0).astype(mask_ref.dtype)
        return 0

    lax.fori_loop(0, nkt, mask_tile, 0)

    def zero_tile(kt, _):
        k0 = pl.multiple_of(kt * tk, tk)
        mask_ref[pl.ds(k0, tk), :] = jnp.zeros((tk, tq), mask_ref.dtype)
        return 0

    lax.fori_loop(nkt, n_all // tk, zero_tile, 0)


def _dsa_select_t(qi, wit, ki, topk):
    n = qi.shape[0]
    tq = min(SEL_TQ, n)
    tk = min(SEL_TK, n)
    assert tk % (SUBLANES * SEL_UNROLL) == 0 and tk % (2 * SUBLANES * SEL_UNROLL16) == 0
    return pl.pallas_call(
        functools.partial(_select_t_kernel, tq=tq, tk=tk, topk=topk, unroll=SEL_UNROLL,
                          unroll16=SEL_UNROLL16),
        out_shape=jax.ShapeDtypeStruct((n, n), jnp.int8),
        grid=(n // tq,),
        in_specs=[pl.BlockSpec((tq, qi.shape[1]), lambda i: (i, 0)),
                  pl.BlockSpec((IDX_HEADS, tq), lambda i: (0, i)),
                  pl.BlockSpec((n, LANES), lambda i: (0, 0))],
        out_specs=pl.BlockSpec((n, tq), lambda i: (0, i)),
        scratch_shapes=[pltpu.VMEM((n, tq), F32), pltpu.VMEM((n, tq), BF16)]
        + [pltpu.VMEM((SUBLANES, tq), I32)] * 4,
        compiler_params=_cparams(("parallel",)),
        name="dsa_select",
    )(qi, wit, ki)


I16 = jnp.int16
L16_MIN = -32768
ROW_NEVER = 32767


def _select16_kernel(qi_ref, wit_ref, ki_ref, mask_ref, h16_ref, l16_ref, lo_ref, hi_ref, clo_ref,
                     chi_ref, base_ref, *, tq, tk, topk, unroll16):
    qb = pl.program_id(0)
    n_all = mask_ref.shape[0]
    nkt = (qb * tq + tq - 1) // tk + 1
    nrow = nkt * tk
    rows16 = 2 * SUBLANES * unroll16
    ntrip = nrow // rows16

    qm = []
    for hh in range(IDX_HEADS):
        blk = qi_ref[:, (hh // 2) * LANES:(hh // 2 + 1) * LANES]
        qm.append(jnp.where(_lane_head_mask(blk.shape, hh % 2), blk, jnp.zeros_like(blk)))
    qm = jnp.concatenate(qm, axis=0)
    wt = wit_ref[...]
    causal_slack = (qb * tq + lax.broadcasted_iota(I32, (tk, tq), 1)
                    - lax.broadcasted_iota(I32, (tk, tq), 0))

    def score_tile(kt, _):
        k0 = pl.multiple_of(kt * tk, tk)
        s = _dot_nt(ki_ref[pl.ds(k0, tk), :], qm)
        acc = wt[0:1, :] * jnp.maximum(s[:, 0:tq], 0.0)
        for hh in range(1, IDX_HEADS):
            acc = acc + wt[hh:hh + 1, :] * jnp.maximum(s[:, hh * tq:(hh + 1) * tq], 0.0)
        acc = jnp.where(acc == 0.0, 0.0, acc)
        sc = jnp.where(k0 <= causal_slack, acc, -jnp.inf)
        bits = lax.bitcast_convert_type(sc, I32)
        key = bits ^ ((bits >> 31) & 0x7FFFFFFF)
        h16_ref[pl.ds(k0, tk), :] = (key >> 16).astype(I16)
        l16_ref[pl.ds(k0, tk), :] = (key ^ 0x8000).astype(I16)
        return 0

    lax.fori_loop(0, nkt, score_tile, 0)

    one16 = jnp.ones((), I16)
    zero16 = jnp.zeros((), I16)
    sub16 = (lax.broadcasted_iota(I32, (unroll16, 2 * SUBLANES, tq), 0) * (2 * SUBLANES)
             + lax.broadcasted_iota(I32, (unroll16, 2 * SUBLANES, tq), 1)).astype(I16)

    def pack16(v):
        return jnp.concatenate([v, v], axis=0).astype(I16)

    def count(pred):
        def body(i, acc):
            r0 = pl.multiple_of(i * rows16, rows16)
            hb = h16_ref[pl.ds(r0, rows16), :].reshape(unroll16, 2 * SUBLANES, tq)
            lb = l16_ref[pl.ds(r0, rows16), :].reshape(unroll16, 2 * SUBLANES, tq)
            ones = jnp.where(pred(hb, lb, r0), one16, zero16)
            part = ones[0]
            for u in range(1, unroll16):
                part = part + ones[u]
            return acc + part.astype(I32)
        acc = lax.fori_loop(0, ntrip, body, jnp.zeros((2 * SUBLANES, tq), I32))
        return jnp.broadcast_to(jnp.sum(acc, axis=0, keepdims=True), (SUBLANES, tq))

    def bisect(count_ge, n_steps):
        def step(_, carry):
            lo, hi = lo_ref[...], hi_ref[...]
            mid = (lo + hi) >> 1
            active = mid != lo
            cnt = count_ge(pack16(mid)) + base_ref[...]
            ge = cnt >= topk
            up = active & ge
            dn = active & jnp.logical_not(ge)
            lo_ref[...] = jnp.where(up, mid, lo)
            hi_ref[...] = jnp.where(dn, mid, hi)
            clo_ref[...] = jnp.where(up, cnt, clo_ref[...])
            chi_ref[...] = jnp.where(dn, cnt, chi_ref[...])
            return carry
        lax.fori_loop(0, n_steps, step, 0)

    lo_ref[...] = jnp.full(lo_ref.shape, KEY16_NEG_INF, I32)
    hi_ref[...] = jnp.full(hi_ref.shape, KEY16_POS_INF + 1, I32)
    clo_ref[...] = jnp.full(clo_ref.shape, 1, I32) * nrow
    chi_ref[...] = jnp.zeros(chi_ref.shape, I32)
    base_ref[...] = jnp.zeros(base_ref.shape, I32)
    bisect(lambda c: count(lambda hb, lb, r0: hb >= c[None]), 16)

    t_hi = lo_ref[...]
    none_finite = t_hi == KEY16_NEG_INF
    t_hi16 = pack16(t_hi)

    def bucket_tile(i, _):
        r0 = pl.multiple_of(i * rows16, rows16)
        hb = h16_ref[pl.ds(r0, rows16), :].reshape(unroll16, 2 * SUBLANES, tq)
        lb = l16_ref[pl.ds(r0, rows16), :].reshape(unroll16, 2 * SUBLANES, tq)
        lb = jnp.where(hb == t_hi16[None], lb, jnp.full((), L16_MIN, I16))
        l16_ref[pl.ds(r0, rows16), :] = lb.reshape(rows16, tq)
        return 0

    lax.fori_loop(0, ntrip, bucket_tile, 0)
    base_ref[...] = chi_ref[...]
    lo_ref[...] = jnp.full(lo_ref.shape, L16_MIN, I32)
    hi_ref[...] = jnp.full(hi_ref.shape, -L16_MIN, I32)
    bisect(lambda c: count(lambda hb, lb, r0: lb >= c[None]), 16)
    t_lo = jnp.where(none_finite, -L16_MIN - 1, lo_ref[...])

    tie = (clo_ref[...] > topk) & jnp.logical_not(none_finite)
    any_tie = jnp.max(tie.astype(I32))
    t_lo16 = pack16(t_lo)

    def rank_tile(i, _):
        r0 = pl.multiple_of(i * rows16, rows16)
        hb = h16_ref[pl.ds(r0, rows16), :].reshape(unroll16, 2 * SUBLANES, tq)
        lb = l16_ref[pl.ds(r0, rows16), :].reshape(unroll16, 2 * SUBLANES, tq)
        in_bucket = hb == t_hi16[None]
        rows = sub16 + jnp.full((2 * SUBLANES, tq), r0, I32).astype(I16)[None]
        rank = jnp.where(in_bucket & (lb == t_lo16[None]), rows,
                         jnp.where(in_bucket & (lb > t_lo16[None]),
                                   jnp.full((), -1, I16), jnp.full((), ROW_NEVER, I16)))
        l16_ref[pl.ds(r0, rows16), :] = rank.reshape(rows16, tq)
        return 0

    lax.fori_loop(0, ntrip, rank_tile, 0)

    @pl.when(any_tie == 0)
    def _():
        hi_ref[...] = jnp.full(hi_ref.shape, n_all, I32)

    @pl.when(any_tie > 0)
    def _():
        chi_ref[...] = topk - base_ref[...]
        hi_ref[...] = jnp.full(hi_ref.shape, 1, I32) * (nrow - 1)
        clo_ref[...] = jnp.full(clo_ref.shape, -1, I32)

        def tie_step(_, c):
            jl, jh = clo_ref[...], hi_ref[...]
            mid = jl + ((jh - jl) >> 1)
            active = (jh - jl) > 1
            mid16 = pack16(mid)
            cnt = count(lambda hb, lb, r0: lb <= mid16[None])
            ok = cnt >= chi_ref[...]
            hi_ref[...] = jnp.where(active & ok, mid, jh)
            clo_ref[...] = jnp.where(active & jnp.logical_not(ok), mid, jl)
            return c

        n_halvings = 0
        for j in range(int(math.ceil(math.log2(n_all // tk))) + 1):
            n_halvings = n_halvings + (((nkt - 1) >> j) > 0).astype(I32)
        lax.fori_loop(0, n_halvings + int(math.log2(tk)) + 1, tie_step, 0)
        hi_ref[...] = jnp.where(tie, hi_ref[...], n_all)

    last16 = pack16(hi_ref[...])
    g16 = 2 * SUBLANES

    def mask_tile(kt, _):
        k0 = pl.multiple_of(kt * tk, tk)
        hb = h16_ref[pl.ds(k0, tk), :].reshape(tk // g16, g16, tq)
        rank = l16_ref[pl.ds(k0, tk), :].reshape(tk // g16, g16, tq)
        sel = ((hb > t_hi16[None]) | (rank <= last16[None])) & (hb > KEY16_NEG_INF)
        sel = jnp.where(sel, one16, zero16).reshape(tk, tq)
        mask_ref[pl.ds(k0, tk), :] = sel.astype(mask_ref.dtype)
        return 0

    lax.fori_loop(0, nkt, mask_tile, 0)

    def zero_tile(kt, _):
        k0 = pl.multiple_of(kt * tk, tk)
        mask_ref[pl.ds(k0, tk), :] = jnp.zeros((tk, tq), mask_ref.dtype)
        return 0

    lax.fori_loop(nkt, n_all // tk, zero_tile, 0)


def _dsa_select16(qi, wit, ki, topk):
    n = qi.shape[0]
    tq = min(SEL_TQ, n)
    tk = min(SEL_TK, n)
    assert tk % (2 * SUBLANES * SEL_UNROLL16) == 0 and n < -L16_MIN
    return pl.pallas_call(
        functools.partial(_select16_kernel, tq=tq, tk=tk, topk=topk, unroll16=SEL_UNROLL16),
        out_shape=jax.ShapeDtypeStruct((n, n), jnp.int8),
        grid=(n // tq,),
        in_specs=[pl.BlockSpec((tq, qi.shape[1]), lambda i: (i, 0)),
                  pl.BlockSpec((IDX_HEADS, tq), lambda i: (0, i)),
                  pl.BlockSpec((n, LANES), lambda i: (0, 0))],
        out_specs=pl.BlockSpec((n, tq), lambda i: (0, i)),
        scratch_shapes=[pltpu.VMEM((n, tq), I16), pltpu.VMEM((n, tq), I16)]
        + [pltpu.VMEM((SUBLANES, tq), I32)] * 5,
        compiler_params=_cparams(("parallel",)),
        name="dsa_select",
    )(qi, wit, ki)


def _attn_t_kernel(qb_tab, kt_tab, q_ref, k_ref, vt_ref, mask_ref, o_ref, acc_ref, m_ref,
                   *, tq, tk, qs):
    step = pl.program_id(0)
    kt = kt_tab[step]

    @pl.when(kt == 0)
    def _():
        acc_ref[...] = jnp.zeros_like(acc_ref)
        m_ref[...] = jnp.full(m_ref.shape, NEG_BIG, MXU_DTYPE).astype(F32)

    bias = jnp.where(mask_ref[...].astype(I32) != 0, 0.0, NEG_BIG).astype(MXU_DTYPE)

    def logits(hd):
        blk = hd // 2
        qblk = q_ref[:, blk * LANES:(blk + 1) * LANES]
        qm = jnp.where(_lane_head_mask(qblk.shape, hd % 2), qblk, jnp.zeros_like(qblk))
        return _dot_nt(k_ref[:, blk * LANES:(blk + 1) * LANES], qm).astype(MXU_DTYPE) + bias

    s_next = logits(0)
    for hd in range(N_HEADS):
        r0 = hd * VT_ROWS
        s = s_next
        if hd + 1 < N_HEADS:
            s_next = logits(hd + 1)
        m_old = m_ref[hd:hd + 1, :]
        m_new = jnp.maximum(m_old, jnp.max(s, axis=0, keepdims=True).astype(F32))
        alpha = jnp.exp2(m_old - m_new)
        p = jnp.exp2(s - m_new.astype(MXU_DTYPE))
        m_ref[hd:hd + 1, :] = m_new
        pv = _dot(vt_ref[r0:r0 + VT_ROWS, :], p)
        acc_ref[r0:r0 + VT_ROWS, :] = acc_ref[r0:r0 + VT_ROWS, :] * alpha + pv

    last = kt == ((qb_tab[step] + 1) * tq - 1) // tk

    @pl.when(last)
    def _():
        for hd in range(N_HEADS):
            r0 = hd * VT_ROWS
            o_ref[hd * HEAD_DIM:(hd + 1) * HEAD_DIM, :] = (
                acc_ref[r0:r0 + HEAD_DIM, :] / acc_ref[r0 + HEAD_DIM:r0 + HEAD_DIM + 1, :]
            ).astype(o_ref.dtype)


def _dsa_attend_t(q, k, vt, mask_t):
    n, d = q.shape
    tq = min(ATT_TQ, n)
    tk = min(ATT_TK, n)
    pairs = [(qb, kt) for qb in range(n // tq) for kt in range(((qb + 1) * tq - 1) // tk + 1)]
    qb_tab = jnp.asarray([p[0] for p in pairs], I32)
    kt_tab = jnp.asarray([p[1] for p in pairs], I32)
    grid_spec = pltpu.PrefetchScalarGridSpec(
        num_scalar_prefetch=2,
        grid=(len(pairs),),
        in_specs=[
            pl.BlockSpec((tq, d), lambda i, qt, kt: (qt[i], 0)),
            pl.BlockSpec((tk, d), lambda i, qt, kt: (kt[i], 0)),
            pl.BlockSpec((vt.shape[0], tk), lambda i, qt, kt: (0, kt[i])),
            pl.BlockSpec((tk, tq), lambda i, qt, kt: (kt[i], qt[i])),
        ],
        out_specs=pl.BlockSpec((d, tq), lambda i, qt, kt: (0, qt[i])),
        scratch_shapes=[pltpu.VMEM((vt.shape[0], tq), F32), pltpu.VMEM((N_HEADS, tq), F32)],
    )
    return pl.pallas_call(
        functools.partial(_attn_t_kernel, tq=tq, tk=tk, qs=min(ATT_QS, tq)),
        out_shape=jax.ShapeDtypeStruct((d, n), MXU_DTYPE),
        grid_spec=grid_spec,
        compiler_params=_cparams(("arbitrary",)),
        name="dsa_attend",
    )(qb_tab, kt_tab, q, k, vt, mask_t)


def _out_t_kernel(x_ref, at_ref, w_ref, o_ref):
    o_ref[...] = x_ref[...] + lax.dot_general(
        at_ref[...], w_ref[...], (((0,), (0,)), ((), ())), preferred_element_type=F32)


def _out_residual_t(x, at, w):
    n, d = x.shape
    tm = min(ROW_TM, n)
    return pl.pallas_call(
        _out_t_kernel,
        out_shape=jax.ShapeDtypeStruct((n, d), F32),
        grid=(n // tm,),
        in_specs=[pl.BlockSpec((tm, d), lambda i: (i, 0)),
                  pl.BlockSpec((d, tm), lambda i: (0, i)),
                  pl.BlockSpec((d, d), lambda i: (0, 0))],
        out_specs=pl.BlockSpec((tm, d), lambda i: (i, 0)),
        compiler_params=_cparams(("parallel",)),
        name="attn_out",
    )(x, at, w)


def _rope_lane_tables(length):
    inv_freq = ROPE_THETA ** (-jnp.arange(0, HEAD_DIM, 2, dtype=F32) / HEAD_DIM)
    ang = jnp.arange(length, dtype=F32)[:, None] * inv_freq[None, :]
    lane = np.arange(LANES)
    cos_t = jnp.cos(ang)[:, lane % 32]
    sin_t = jnp.sin(ang)[:, lane % 32] * jnp.asarray(np.where(lane < 64, -1.0, 1.0), F32)
    return cos_t, sin_t


def kernel(x, s5_lambda_re, s5_lambda_im, s5_log_dt, s5_b_re, s5_b_im, s5_c_re, s5_c_im, s5_d, s5_w_glu, dsa_w_in, dsa_q_norm, dsa_k_norm, dsa_w_o, ffn_w_gate_up, ffn_w_down, norm_mix, norm_ffn):
    bsz, length, d = x.shape
    depth = norm_mix.shape[0]
    topk = min(TOPK_MAX, length // 4)
    nchunk = length // S5_CHUNK
    cos_t, sin_t = _rope_lane_tables(length)
    outs = []
    for b in range(bsz):
        xs = x[b].astype(F32)
        for i in range(depth):
            j = i // 2
            wgu = ffn_w_gate_up[i].astype(MXU_DTYPE)
            wd = ffn_w_down[i].astype(MXU_DTYPE)
            if i % 2 == 0:
                tables = _s5_tables(s5_lambda_re[j], s5_lambda_im[j], s5_log_dt[j], s5_b_re[j],
                                    s5_b_im[j], s5_c_re[j], s5_c_im[j])
                xt = xs.reshape(nchunk, S5_CHUNK, d).transpose(1, 0, 2).reshape(length, d)
                h_t = _norm(xt, norm_mix[i]).reshape(S5_CHUNK, nchunk, d)
                g_t = _s5_scan(h_t, tables, s5_d[j]).reshape(length, d)
                xt = _glu_residual(xt, g_t, s5_w_glu[j].astype(MXU_DTYPE))
                xt = _ffn(xt, norm_ffn[i], wgu, wd)
                xs = xt.reshape(S5_CHUNK, nchunk, d).transpose(1, 0, 2).reshape(length, d)
            else:
                q, k, vt, qi, ki, wi = _dsa_project(xs, norm_mix[i], dsa_w_in[j], dsa_q_norm[j],
                                                    dsa_k_norm[j], cos_t, sin_t)
                mask_t = _dsa_select16(qi, wi[:, :IDX_HEADS].T, ki, topk)
                att_t = _dsa_attend_t(q, k, vt, mask_t)
                xs = _out_residual_t(xs, att_t, dsa_w_o[j].astype(MXU_DTYPE))
                xs = _ffn(xs, norm_ffn[i], wgu, wd)
        outs.append(xs)
    return jnp.stack(outs, axis=0).astype(x.dtype)
```

```python
import functools
import math

import jax
import jax.numpy as jnp
import numpy as np
from jax import lax
from jax.experimental import pallas as pl
from jax.experimental.pallas import tpu as pltpu

F32 = jnp.float32
BF16 = jnp.bfloat16
I32 = jnp.int32
MXU_DTYPE = BF16

D_MODEL = 1024
S5_GROUP = 16
S5_STATE = 64
N_HEADS = 16
HEAD_DIM = 64
IDX_HEADS = 8
IDX_DIM = 64
TOPK_MAX = 256
ROPE_THETA = 10000.0
EPS = 1e-6

LANES = 128
SUBLANES = 8
MXU_DIM = 256
VMEM_LIMIT = 56 * 1024 * 1024

S5_CHUNK = 16
S5_SLAB_GROUPS = LANES // S5_GROUP
NEG_BIG = -1e30
LOG2E = math.log2(math.e)
VT_PAD = 16
VT_ROWS = HEAD_DIM + VT_PAD

ROW_TM = 512
NORM_TM = 1024
S5_TC = 512
PROJ_TM = 512
SEL_TQ = 256
SEL_TK = 512
SEL_RB = 64
SEL_UNROLL = 64
SEL_UNROLL16 = 32
ATT_TQ = 512
ATT_TK = 512
ATT_QS = 512


def _cparams(sem, flags=None):
    return pltpu.CompilerParams(dimension_semantics=sem, vmem_limit_bytes=VMEM_LIMIT, flags=flags)


def _rms(x, gain=None):
    y = x * lax.rsqrt(jnp.mean(x * x, axis=-1, keepdims=True) + EPS)
    return y if gain is None else y * gain


def _dot(a, b):
    return jnp.dot(a, b, preferred_element_type=F32)


def _dot_nt(a, b):
    return lax.dot_general(a, b, (((1,), (1,)), ((), ())), preferred_element_type=F32)


def _ffn_kernel(x_ref, g_ref, wgu_ref, wd_ref, o_ref, acc_ref, *, d_ff, fc):
    x = x_ref[...]
    h = _rms(x, g_ref[...]).astype(MXU_DTYPE)
    for c in range(d_ff // fc):
        g = _dot(h, wgu_ref[:, c * fc:(c + 1) * fc])
        u = _dot(h, wgu_ref[:, d_ff + c * fc:d_ff + (c + 1) * fc])
        a = (g * jax.nn.sigmoid(g) * u).astype(MXU_DTYPE)
        d = _dot(a, wd_ref[c * fc:(c + 1) * fc, :])
        if c == 0:
            acc_ref[...] = d
        else:
            acc_ref[...] += d
    o_ref[...] = x + acc_ref[...]


def _ffn(x, gain, wgu, wd):
    n, d = x.shape
    d_ff = wd.shape[0]
    tm = min(ROW_TM, n)
    fc = MXU_DIM
    return pl.pallas_call(
        functools.partial(_ffn_kernel, d_ff=d_ff, fc=fc),
        out_shape=jax.ShapeDtypeStruct((n, d), F32),
        grid=(n // tm,),
        in_specs=[
            pl.BlockSpec((tm, d), lambda i: (i, 0)),
            pl.BlockSpec((1, d), lambda i: (0, 0)),
            pl.BlockSpec((d, 2 * d_ff), lambda i: (0, 0)),
            pl.BlockSpec((d_ff, d), lambda i: (0, 0)),
        ],
        out_specs=pl.BlockSpec((tm, d), lambda i: (i, 0)),
        scratch_shapes=[pltpu.VMEM((tm, d), F32)],
        compiler_params=_cparams(("parallel",)),
        name="ffn",
    )(x, gain.reshape(1, d), wgu, wd)


def _norm_kernel(x_ref, g_ref, o_ref):
    o_ref[...] = _rms(x_ref[...], g_ref[...]).astype(o_ref.dtype)


def _norm(x, gain):
    n, d = x.shape
    tm = min(NORM_TM, n)
    return pl.pallas_call(
        _norm_kernel,
        out_shape=jax.ShapeDtypeStruct((n, d), MXU_DTYPE),
        grid=(n // tm,),
        in_specs=[pl.BlockSpec((tm, d), lambda i: (i, 0)),
                  pl.BlockSpec((1, d), lambda i: (0, 0))],
        out_specs=pl.BlockSpec((tm, d), lambda i: (i, 0)),
        compiler_params=_cparams(("parallel",)),
        name="norm",
    )(x, gain.reshape(1, d))


def _s5_tables(lam_re, lam_im, log_dt, b_re, b_im, c_re, c_im):
    hp = lax.Precision.HIGHEST
    g, p = lam_re.shape
    h = S5_GROUP
    nsl = g // S5_SLAB_GROUPS
    sg = S5_SLAB_GROUPS
    t = S5_CHUNK
    lam_re, lam_im, log_dt = lam_re.astype(F32), lam_im.astype(F32), log_dt.astype(F32)
    b_re, b_im, c_re, c_im = (a.astype(F32) for a in (b_re, b_im, c_re, c_im))
    dt = jnp.exp(log_dt)[:, None]

    def apow(k):
        k = jnp.asarray(k, F32).reshape((-1, 1, 1))
        mag = jnp.exp(lam_re[None] * dt[None] * k)
        ang = lam_im[None] * dt[None] * k
        return mag * jnp.cos(ang), mag * jnp.sin(ang)

    ar, ai = apow([1.0])
    ar, ai = ar[0], ai[0]
    den = lam_re * lam_re + lam_im * lam_im
    nr, ni = ar - 1.0, ai
    qr = (nr * lam_re + ni * lam_im) / den
    qi = (ni * lam_re - nr * lam_im) / den
    bbr = qr[..., None] * b_re - qi[..., None] * b_im
    bbi = qr[..., None] * b_im + qi[..., None] * b_re

    pr, pi = apow(np.arange(t + 1))
    mr = c_re[None] * pr[:, :, None, :] - c_im[None] * pi[:, :, None, :]
    mi = c_re[None] * pi[:, :, None, :] + c_im[None] * pr[:, :, None, :]

    kk = (jnp.einsum('tghp,gpk->tghk', mr[:t], bbr, precision=hp)
          - jnp.einsum('tghp,gpk->tghk', mi[:t], bbi, precision=hp))
    kp = jnp.concatenate([jnp.zeros_like(kk[:1]), kk], axis=0)
    eye = np.eye(sg, dtype=bool)

    def block_diag(x, g_axis):
        m_axis = x.ndim - 1
        shape = [1] * (x.ndim + 1)
        shape[g_axis] = sg
        shape[m_axis] = sg
        x = jnp.expand_dims(x.astype(MXU_DTYPE), m_axis)
        return jnp.where(eye.reshape(shape), x, jnp.zeros((), MXU_DTYPE))

    dl = np.arange(t // 2)[:, None, None]
    sl = np.arange(2)[None, :, None]
    jl = np.arange(2)[None, None, :]
    idx = 2 * dl + jl - sl + 1
    kg = kp[idx]
    kg = kg.reshape(t // 2, 2, 2, nsl, sg, h, h)
    tp = block_diag(kg.transpose(3, 0, 1, 4, 6, 2, 5), 3)
    tp = tp.reshape(nsl, t // 2, 2 * sg * h, 2 * sg * h)

    prs, pis = pr[t - 1::-1][:t], pi[t - 1::-1][:t]
    er = prs[..., None] * bbr[None] - pis[..., None] * bbi[None]
    ei = prs[..., None] * bbi[None] + pis[..., None] * bbr[None]
    bf = jnp.stack([er, ei], axis=2)
    bf = bf.reshape(t // 2, 2, nsl, sg, 2, p, h)
    bz = block_diag(bf.transpose(2, 0, 1, 3, 6, 4, 5), 3)
    bz = bz.reshape(nsl, t // 2, 2 * sg * h, 2 * sg * p)

    cf = jnp.stack([mr[1:], -mi[1:]], axis=2)
    cf = cf.reshape(t // 2, 2, nsl, sg, 2, h, p)
    cz = block_diag(cf.transpose(2, 0, 4, 3, 6, 1, 5), 3)
    cz = cz.reshape(nsl, t // 2, 2 * sg * p, 2 * sg * h)

    def slab_state(re, im):
        k = re.shape[0]
        x = jnp.stack([re, im], axis=1).reshape(k, 2, nsl, sg * p)
        return x.transpose(2, 0, 1, 3).reshape(nsl, k, 2 * sg * p)

    ad = slab_state(*apow([t * 1.0, t * 2.0, t * 4.0]))
    ap8 = slab_state(*apow(t * (np.arange(SUBLANES) + 1.0)))
    return tp, bz, cz, ad, ap8


def _s5_kernel(h_ref, tp_ref, bz_ref, cz_ref, ad_ref, ap8_ref, dsk_ref, o_ref,
               carry_ref, z_ref, xp_ref, *, tc):
    half = z_ref.shape[1] // 2
    npair = S5_CHUNK // 2

    @pl.when(pl.program_id(1) == 0)
    def _():
        carry_ref[...] = jnp.zeros_like(carry_ref)

    u = [jnp.concatenate([h_ref[2 * s], h_ref[2 * s + 1]], axis=1) for s in range(npair)]

    z = _dot(u[0], bz_ref[0])
    for s in range(1, npair):
        z = z + _dot(u[s], bz_ref[s])
    z_ref[...] = z

    row = lax.broadcasted_iota(I32, (SUBLANES, half), 0)
    pr8, pi8 = ap8_ref[:, :half], ap8_ref[:, half:]

    def tile_step(t, carry):
        r0 = pl.multiple_of(t * SUBLANES, SUBLANES)
        zt = z_ref[pl.ds(r0, SUBLANES), :]
        xr, xi = zt[:, :half], zt[:, half:]
        for k, d in enumerate((1, 2, 4)):
            a = ad_ref[k:k + 1, :]
            a_r, a_i = a[:, :half], a[:, half:]
            sr = jnp.where(row >= d, pltpu.roll(xr, d, 0), 0.0)
            si = jnp.where(row >= d, pltpu.roll(xi, d, 0), 0.0)
            xr, xi = xr + a_r * sr - a_i * si, xi + a_r * si + a_i * sr
        cr, ci = carry[:, :half], carry[:, half:]
        xr, xi = xr + pr8 * cr - pi8 * ci, xi + pr8 * ci + pi8 * cr
        xpr = jnp.where(row >= 1, pltpu.roll(xr, 1, 0), cr)
        xpi = jnp.where(row >= 1, pltpu.roll(xi, 1, 0), ci)
        xp_ref[pl.ds(r0, SUBLANES), :] = jnp.concatenate([xpr, xpi], axis=1)
        return jnp.concatenate([xr[SUBLANES - 1:], xi[SUBLANES - 1:]], axis=1)

    carry_ref[...] = lax.fori_loop(0, tc // SUBLANES, tile_step, carry_ref[...])

    xp = xp_ref[...].astype(MXU_DTYPE)
    dsk = dsk_ref[...]
    for i in range(npair):
        y = _dot(xp, cz_ref[i])
        for s in range(i + 1):
            y = y + _dot(u[s], tp_ref[i - s])
        for jl in range(2):
            j = 2 * i + jl
            yj = y[:, jl * LANES:(jl + 1) * LANES] + dsk * h_ref[j].astype(F32)
            o_ref[j] = jax.nn.gelu(yj).astype(o_ref.dtype)


def _s5_scan(h_t, tables, d_skip):
    tp, bz, cz, ad, ap8 = tables
    t, c, d = h_t.shape
    nsl = d // LANES
    tc = min(S5_TC, c)
    st = bz.shape[-1]
    dsk = d_skip.astype(F32).reshape(nsl, 1, LANES)
    return pl.pallas_call(
        functools.partial(_s5_kernel, tc=tc),
        out_shape=jax.ShapeDtypeStruct((t, c, d), MXU_DTYPE),
        grid=(nsl, c // tc),
        in_specs=[
            pl.BlockSpec((t, tc, LANES), lambda b, i: (0, i, b)),
            pl.BlockSpec((None,) + tp.shape[1:], lambda b, i: (b, 0, 0, 0)),
            pl.BlockSpec((None,) + bz.shape[1:], lambda b, i: (b, 0, 0, 0)),
            pl.BlockSpec((None,) + cz.shape[1:], lambda b, i: (b, 0, 0, 0)),
            pl.BlockSpec((None,) + ad.shape[1:], lambda b, i: (b, 0, 0)),
            pl.BlockSpec((None,) + ap8.shape[1:], lambda b, i: (b, 0, 0)),
            pl.BlockSpec((None, 1, LANES), lambda b, i: (b, 0, 0)),
        ],
        out_specs=pl.BlockSpec((t, tc, LANES), lambda b, i: (0, i, b)),
        scratch_shapes=[pltpu.VMEM((1, st), F32), pltpu.VMEM((tc, st), F32),
                        pltpu.VMEM((tc, st), F32)],
        compiler_params=_cparams(("arbitrary", "arbitrary")),
        name="s5_scan",
    )(h_t, tp, bz, cz, ad, ap8, dsk)


def _plane_perm(rows):
    nc = rows // S5_CHUNK
    p = np.zeros((rows, rows), np.float32)
    c, s = np.meshgrid(np.arange(nc), np.arange(S5_CHUNK), indexing="ij")
    p[(s * nc + c).ravel(), (c * S5_CHUNK + s).ravel()] = 1.0
    return p


def _norm_planes_kernel(x_ref, g_ref, p_ref, o_ref):
    h = _rms(x_ref[...], g_ref[...]).astype(MXU_DTYPE)
    hp = _dot(p_ref[...], h).astype(o_ref.dtype)
    o_ref[...] = hp.reshape(o_ref.shape)


def _norm_planes(x, gain):
    n, d = x.shape
    tm = min(ROW_TM, n)
    nc = tm // S5_CHUNK
    perm = jnp.asarray(_plane_perm(tm), MXU_DTYPE)
    return pl.pallas_call(
        _norm_planes_kernel,
        out_shape=jax.ShapeDtypeStruct((S5_CHUNK, n // S5_CHUNK, d), MXU_DTYPE),
        grid=(n // tm,),
        in_specs=[pl.BlockSpec((tm, d), lambda i: (i, 0)),
                  pl.BlockSpec((1, d), lambda i: (0, 0)),
                  pl.BlockSpec((tm, tm), lambda i: (0, 0))],
        out_specs=pl.BlockSpec((S5_CHUNK, nc, d), lambda i: (0, i, 0)),
        compiler_params=_cparams(("parallel",)),
        name="norm_planes",
    )(x, gain.reshape(1, d), perm)


def _glu_planes_kernel(x_ref, g_ref, pt_ref, w_ref, o_ref):
    d = x_ref.shape[1]
    gp = g_ref[...].reshape(x_ref.shape)
    g = _dot(pt_ref[...], gp).astype(MXU_DTYPE)
    vg = _dot(g, w_ref[...])
    o_ref[...] = x_ref[...] + vg[:, :d] * jax.nn.sigmoid(vg[:, d:])


def _glu_planes(x, g_t, w):
    n, d = x.shape
    tm = min(ROW_TM, n)
    nc = tm // S5_CHUNK
    perm_t = jnp.asarray(_plane_perm(tm).T, MXU_DTYPE)
    return pl.pallas_call(
        _glu_planes_kernel,
        out_shape=jax.ShapeDtypeStruct((n, d), F32),
        grid=(n // tm,),
        in_specs=[pl.BlockSpec((tm, d), lambda i: (i, 0)),
                  pl.BlockSpec((S5_CHUNK, nc, d), lambda i: (0, i, 0)),
                  pl.BlockSpec((tm, tm), lambda i: (0, 0)),
                  pl.BlockSpec((d, 2 * d), lambda i: (0, 0))],
        out_specs=pl.BlockSpec((tm, d), lambda i: (i, 0)),
        compiler_params=_cparams(("parallel",)),
        name="glu",
    )(x, g_t, perm_t, w)


def _glu_kernel(x_ref, g_ref, w_ref, o_ref):
    d = x_ref.shape[1]
    vg = _dot(g_ref[...], w_ref[...])
    o_ref[...] = x_ref[...] + vg[:, :d] * jax.nn.sigmoid(vg[:, d:])


def _glu_residual(x, g, w):
    n, d = x.shape
    tm = min(ROW_TM, n)
    return pl.pallas_call(
        _glu_kernel,
        out_shape=jax.ShapeDtypeStruct((n, d), F32),
        grid=(n // tm,),
        in_specs=[pl.BlockSpec((tm, d), lambda i: (i, 0)),
                  pl.BlockSpec((tm, d), lambda i: (i, 0)),
                  pl.BlockSpec((d, 2 * d), lambda i: (0, 0))],
        out_specs=pl.BlockSpec((tm, d), lambda i: (i, 0)),
        compiler_params=_cparams(("parallel",)),
        name="glu",
    )(x, g, w)


def _head_perm(n_heads):
    n = np.arange(n_heads * HEAD_DIM)
    pb, r = n // LANES, n % LANES
    half, r2 = r // 64, r % 64
    hl, dp = r2 // 32, r2 % 32
    return (2 * pb + hl) * HEAD_DIM + 32 * half + dp


def _lane_head_mask(shape, hl):
    lane = lax.broadcasted_iota(I32, shape, len(shape) - 1)
    return ((lane % 64) // 32) == hl


def _proj_kernel(x_ref, g_ref, w_ref, wvt_ref, gq_ref, gk_ref, cos_ref, sin_ref, hm_ref,
                 q_ref, k_ref, vt_ref, qi_ref, ki_ref, wi_ref, *, d, dqi, att_scale, w_scale):
    h = _rms(x_ref[...], g_ref[...]).astype(MXU_DTYPE)
    cos, sin = cos_ref[...], sin_ref[...]
    hm = hm_ref[...]

    def rope(t):
        return t * cos + pltpu.roll(t, 64, 1) * sin

    def headnorm_rope(col0, gain_ref, out_ref, scale):
        for sb in range(d // MXU_DIM):
            c0 = sb * MXU_DIM
            t = _dot(h, w_ref[:, col0 + c0:col0 + c0 + MXU_DIM])
            sq = t * t
            hi = sq.astype(MXU_DTYPE)
            lo = (sq - hi.astype(F32)).astype(MXU_DTYPE)
            ss = _dot(hi, hm) + _dot(lo, hm)
            tn = t * lax.rsqrt(ss * (1.0 / HEAD_DIM) + EPS) * gain_ref[:, c0:c0 + MXU_DIM]
            for b in range(MXU_DIM // LANES):
                r = rope(tn[:, b * LANES:(b + 1) * LANES])
                if scale != 1.0:
                    r = r * scale
                out_ref[:, c0 + b * LANES:c0 + (b + 1) * LANES] = r.astype(out_ref.dtype)

    headnorm_rope(0, gq_ref, q_ref, att_scale)
    headnorm_rope(d, gk_ref, k_ref, 1.0)
    vt = _dot_nt(wvt_ref[...], h)
    row = lax.broadcasted_iota(I32, vt.shape, 0)
    vt_ref[...] = jnp.where(row % VT_ROWS >= HEAD_DIM, 1.0, vt).astype(vt_ref.dtype)
    c0 = 2 * d
    t = _dot(h, w_ref[:, c0:c0 + dqi])
    for b in range(dqi // LANES):
        qi_ref[:, b * LANES:(b + 1) * LANES] = rope(t[:, b * LANES:(b + 1) * LANES]).astype(qi_ref.dtype)
    c0 += dqi
    t = _dot(h, w_ref[:, c0:c0 + LANES])
    ms = jnp.sum(t * t, axis=-1, keepdims=True) * (0.5 / IDX_DIM)
    ki_ref[...] = rope(t * lax.rsqrt(ms + EPS)).astype(ki_ref.dtype)
    c0 += LANES
    wi_ref[...] = _dot(h, w_ref[:, c0:c0 + LANES]) * w_scale


def _dsa_project(x, gain, w_in, q_gain, k_gain, cos_t, sin_t):
    n, d = x.shape
    dqi = IDX_HEADS * IDX_DIM
    pq = _head_perm(N_HEADS)
    pqi = _head_perm(IDX_HEADS)
    wq = w_in[:, 0:d][:, pq]
    wk = w_in[:, d:2 * d][:, pq]
    wvt = w_in[:, 2 * d:3 * d].T.reshape(N_HEADS, HEAD_DIM, d)
    wvt = jnp.pad(wvt, ((0, 0), (0, VT_PAD), (0, 0))).reshape(N_HEADS * VT_ROWS, d).astype(MXU_DTYPE)
    dvt = N_HEADS * VT_ROWS
    wqi = w_in[:, 3 * d:3 * d + dqi][:, pqi]
    lane = np.arange(LANES)
    wki = w_in[:, 3 * d + dqi:3 * d + dqi + IDX_DIM][:, 32 * (lane // 64) + lane % 32]
    wwi = jnp.pad(w_in[:, 3 * d + dqi + IDX_DIM:], ((0, 0), (0, LANES - IDX_HEADS)))
    w_all = jnp.concatenate([wq, wk, wqi, wki, wwi], axis=1).astype(MXU_DTYPE)
    dcol = (pq % HEAD_DIM)
    gq = q_gain.astype(F32)[dcol].reshape(1, d)
    gk = k_gain.astype(F32)[dcol].reshape(1, d)
    l2 = np.arange(MXU_DIM)
    hm = ((l2[:, None] // LANES == l2[None, :] // LANES)
          & ((l2[:, None] % 64) // 32 == (l2[None, :] % 64) // 32))
    hm = jnp.asarray(hm, MXU_DTYPE)
    tm = min(PROJ_TM, n)
    nw = w_all.shape[1]
    outs = pl.pallas_call(
        functools.partial(_proj_kernel, d=d, dqi=dqi, att_scale=HEAD_DIM ** -0.5 * LOG2E,
                          w_scale=(IDX_HEADS ** -0.5) * (IDX_DIM ** -0.5)),
        out_shape=[jax.ShapeDtypeStruct((n, d), MXU_DTYPE)] * 2
        + [jax.ShapeDtypeStruct((dvt, n), MXU_DTYPE),
           jax.ShapeDtypeStruct((n, dqi), MXU_DTYPE),
           jax.ShapeDtypeStruct((n, LANES), MXU_DTYPE),
           jax.ShapeDtypeStruct((n, LANES), F32)],
        grid=(n // tm,),
        in_specs=[
            pl.BlockSpec((tm, d), lambda i: (i, 0)),
            pl.BlockSpec((1, d), lambda i: (0, 0)),
            pl.BlockSpec((d, nw), lambda i: (0, 0)),
            pl.BlockSpec((dvt, d), lambda i: (0, 0)),
            pl.BlockSpec((1, d), lambda i: (0, 0)),
            pl.BlockSpec((1, d), lambda i: (0, 0)),
            pl.BlockSpec((tm, LANES), lambda i: (i, 0)),
            pl.BlockSpec((tm, LANES), lambda i: (i, 0)),
            pl.BlockSpec((MXU_DIM, MXU_DIM), lambda i: (0, 0)),
        ],
        out_specs=[pl.BlockSpec((tm, d), lambda i: (i, 0))] * 2
        + [pl.BlockSpec((dvt, tm), lambda i: (0, i)),
           pl.BlockSpec((tm, dqi), lambda i: (i, 0)),
           pl.BlockSpec((tm, LANES), lambda i: (i, 0)),
           pl.BlockSpec((tm, LANES), lambda i: (i, 0))],
        compiler_params=_cparams(("parallel",)),
        name="dsa_proj",
    )(x, gain.reshape(1, d), w_all, wvt, gq, gk, cos_t, sin_t, hm)
    return outs


KEY_NEG_INF = -2139095041
KEY_POS_INF = 2139095040
KEY16_NEG_INF = -32641
KEY16_POS_INF = 32640


def _key_to_f32(key):
    bits = key ^ ((key >> 31) & 0x7FFFFFFF)
    return lax.bitcast_convert_type(bits, F32)


def _select_kernel(qi_ref, wi_ref, ki_ref, mask_ref, s_ref, lo_ref, hi_ref, clo_ref, chi_ref,
                   *, tq, tk, topk, rb):
    qb = pl.program_id(0)
    n_all = mask_ref.shape[1]
    nkt = (qb * tq) // tk + 1
    ncol = nkt * (tk // LANES)

    qm = []
    for hh in range(IDX_HEADS):
        blk = qi_ref[:, (hh // 2) * LANES:(hh // 2 + 1) * LANES]
        qm.append(jnp.where(_lane_head_mask(blk.shape, hh % 2), blk, jnp.zeros_like(blk)))
    qm = jnp.concatenate(qm, axis=0)
    wv = wi_ref[...]
    row_pos = qb * tq + lax.broadcasted_iota(I32, (tq, tk), 0)
    col_in = lax.broadcasted_iota(I32, (tq, tk), 1)

    def score_tile(kt, _):
        k0 = pl.multiple_of(kt * tk, tk)
        s = _dot_nt(qm, ki_ref[pl.ds(k0, tk), :])
        acc = wv[:, 0:1] * jnp.maximum(s[0:tq], 0.0)
        for hh in range(1, IDX_HEADS):
            acc = acc + wv[:, hh:hh + 1] * jnp.maximum(s[hh * tq:(hh + 1) * tq], 0.0)
        acc = jnp.where(col_in + k0 <= row_pos, acc, -jnp.inf)
        s_ref[:, pl.ds(k0, tk)] = acc
        return 0

    lax.fori_loop(0, nkt, score_tile, 0)

    lo_ref[...] = jnp.full(lo_ref.shape, KEY_NEG_INF, I32)
    hi_ref[...] = jnp.full(hi_ref.shape, KEY_POS_INF, I32)
    clo_ref[...] = jnp.full(clo_ref.shape, 1, I32) * (ncol * LANES)
    chi_ref[...] = jnp.zeros(chi_ref.shape, I32)

    def count_rows(r0, pred):
        def body(j, acc):
            c0 = pl.multiple_of(j * LANES, LANES)
            return acc + pred(s_ref[r0:r0 + rb, pl.ds(c0, LANES)], j).astype(I32)
        acc = lax.fori_loop(0, ncol, body, jnp.zeros((rb, LANES), I32))
        return jnp.broadcast_to(jnp.sum(acc, axis=1, keepdims=True), (rb, LANES))

    def bisect_step(carry):
        it, _ = carry
        pending = jnp.zeros((), I32)
        for r0 in range(0, tq, rb):
            lo, hi = lo_ref[r0:r0 + rb, :], hi_ref[r0:r0 + rb, :]
            mid = (lo & hi) + ((lo ^ hi) >> 1)
            active = mid != lo
            cand = _key_to_f32(mid)
            cnt = count_rows(r0, lambda blk, j: blk >= cand)
            ge = cnt >= topk
            up = active & ge
            dn = active & jnp.logical_not(ge)
            hit = active & (cnt == topk)
            lo_ref[r0:r0 + rb, :] = jnp.where(up, mid, lo)
            hi_ref[r0:r0 + rb, :] = jnp.where(hit, mid + 1, jnp.where(dn, mid, hi))
            clo_ref[r0:r0 + rb, :] = jnp.where(up, cnt, clo_ref[r0:r0 + rb, :])
            chi_ref[r0:r0 + rb, :] = jnp.where(dn, cnt, chi_ref[r0:r0 + rb, :])
            pending = jnp.maximum(pending, jnp.max(active.astype(I32)))
        return it + 1, pending

    lax.while_loop(lambda c: (c[0] < 40) & (c[1] > 0), bisect_step,
                   (jnp.zeros((), I32), jnp.ones((), I32)))

    tie = (clo_ref[...] > topk) & (lo_ref[...] > KEY_NEG_INF)
    any_tie = jnp.max(tie.astype(I32))

    @pl.when(any_tie == 0)
    def _():
        hi_ref[...] = jnp.full(hi_ref.shape, n_all, I32)

    @pl.when(any_tie > 0)
    def _():
        need = topk - chi_ref[...]
        chi_ref[...] = need
        hi_ref[...] = jnp.full(hi_ref.shape, ncol * LANES - 1, I32)
        clo_ref[...] = jnp.full(clo_ref.shape, -1, I32)
        lane = lax.broadcasted_iota(I32, (rb, LANES), 1)

        def tie_step(_, c):
            for r0 in range(0, tq, rb):
                jl, jh = clo_ref[r0:r0 + rb, :], hi_ref[r0:r0 + rb, :]
                mid = jl + ((jh - jl) >> 1)
                active = (jh - jl) > 1
                thr = _key_to_f32(lo_ref[r0:r0 + rb, :])
                cnt = count_rows(r0, lambda blk, j: (blk == thr) & (lane + j * LANES <= mid))
                ok = cnt >= chi_ref[r0:r0 + rb, :]
                hi_ref[r0:r0 + rb, :] = jnp.where(active & ok, mid, jh)
                clo_ref[r0:r0 + rb, :] = jnp.where(active & jnp.logical_not(ok), mid, jl)
            return c

        lax.fori_loop(0, int(math.ceil(math.log2(n_all))) + 1, tie_step, 0)
        keep_all = jnp.logical_not(tie)
        hi_ref[...] = jnp.where(keep_all, n_all, hi_ref[...])

    lane_k = lax.broadcasted_iota(I32, (tq, tk), 1)
    thr_col = _key_to_f32(lo_ref[:, 0:1])
    last_col = hi_ref[:, 0:1]

    def mask_tile(kt, _):
        k0 = pl.multiple_of(kt * tk, tk)
        s = s_ref[:, pl.ds(k0, tk)]
        sel = ((s > thr_col) | ((s == thr_col) & (lane_k + k0 <= last_col))) & (s > -jnp.inf)
        mask_ref[:, pl.ds(k0, tk)] = jnp.where(sel, 1, 0).astype(mask_ref.dtype)
        return 0

    lax.fori_loop(0, nkt, mask_tile, 0)

    def zero_tile(kt, _):
        k0 = pl.multiple_of(kt * tk, tk)
        mask_ref[:, pl.ds(k0, tk)] = jnp.zeros((tq, tk), mask_ref.dtype)
        return 0

    lax.fori_loop(nkt, n_all // tk, zero_tile, 0)


def _dsa_select(qi, wi, ki, topk):
    n = qi.shape[0]
    tq = min(SEL_TQ, n)
    tk = min(SEL_TK, n)
    rb = min(SEL_RB, tq)
    return pl.pallas_call(
        functools.partial(_select_kernel, tq=tq, tk=tk, topk=topk, rb=rb),
        out_shape=jax.ShapeDtypeStruct((n, n), jnp.int8),
        grid=(n // tq,),
        in_specs=[pl.BlockSpec((tq, qi.shape[1]), lambda i: (i, 0)),
                  pl.BlockSpec((tq, LANES), lambda i: (i, 0)),
                  pl.BlockSpec((n, LANES), lambda i: (0, 0))],
        out_specs=pl.BlockSpec((tq, n), lambda i: (i, 0)),
        scratch_shapes=[pltpu.VMEM((tq, n), F32)] + [pltpu.VMEM((tq, LANES), I32)] * 4,
        compiler_params=_cparams(("parallel",)),
        name="dsa_select",
    )(qi, wi, ki)


def _attn_kernel(qb_tab, kt_tab, q_ref, k_ref, v_ref, mask_ref, o_ref, acc_ref, m_ref, l_ref,
                 *, tq, tk):
    step = pl.program_id(0)
    kt = kt_tab[step]
    d = q_ref.shape[1]
    slab = MXU_DIM
    heads_per_slab = slab // HEAD_DIM

    @pl.when(kt == 0)
    def _():
        acc_ref[...] = jnp.zeros_like(acc_ref)
        m_ref[...] = jnp.full(m_ref.shape, NEG_BIG, F32)
        l_ref[...] = jnp.zeros_like(l_ref)

    bias = jnp.where(mask_ref[...].astype(I32) != 0, 0.0, NEG_BIG)
    lane_s = lax.broadcasted_iota(I32, (tq, slab), 1) // HEAD_DIM

    for sb in range(d // slab):
        vs = v_ref[:, sb * slab:(sb + 1) * slab]
        lane_v = lax.broadcasted_iota(I32, vs.shape, 1) // HEAD_DIM
        alpha_l = jnp.zeros((tq, slab), F32)
        pv = jnp.zeros((tq, slab), F32)
        for i in range(heads_per_slab):
            hd = sb * heads_per_slab + i
            blk = hd // 2
            qblk = q_ref[:, blk * LANES:(blk + 1) * LANES]
            qm = jnp.where(_lane_head_mask(qblk.shape, hd % 2), qblk, jnp.zeros_like(qblk))
            s = _dot_nt(qm, k_ref[:, blk * LANES:(blk + 1) * LANES]) + bias
            m_old = m_ref[:, hd:hd + 1]
            m_new = jnp.maximum(m_old, jnp.max(s, axis=1, keepdims=True))
            alpha = jnp.exp(m_old - m_new)
            p = jnp.exp(s - m_new)
            l_ref[:, hd:hd + 1] = l_ref[:, hd:hd + 1] * alpha + jnp.sum(p, axis=1, keepdims=True)
            m_ref[:, hd:hd + 1] = m_new
            vm = jnp.where(lane_v == i, vs, jnp.zeros_like(vs))
            pv = pv + _dot(p.astype(MXU_DTYPE), vm)
            alpha_l = jnp.where(lane_s == i, alpha, alpha_l)
        acc_ref[:, sb * slab:(sb + 1) * slab] = acc_ref[:, sb * slab:(sb + 1) * slab] * alpha_l + pv

    last = kt == ((qb_tab[step] + 1) * tq - 1) // tk

    @pl.when(last)
    def _():
        lane_h = lax.broadcasted_iota(I32, (tq, d), 1) // HEAD_DIM
        l_l = jnp.zeros((tq, d), F32)
        for hd in range(N_HEADS):
            l_l = jnp.where(lane_h == hd, l_ref[:, hd:hd + 1], l_l)
        o_ref[...] = (acc_ref[...] / l_l).astype(o_ref.dtype)


def _dsa_attend(q, k, v, mask):
    n, d = q.shape
    tq = min(ATT_TQ, n)
    tk = min(ATT_TK, n)
    pairs = [(qb, kt) for qb in range(n // tq) for kt in range(((qb + 1) * tq - 1) // tk + 1)]
    qb_tab = jnp.asarray([p[0] for p in pairs], I32)
    kt_tab = jnp.asarray([p[1] for p in pairs], I32)
    grid_spec = pltpu.PrefetchScalarGridSpec(
        num_scalar_prefetch=2,
        grid=(len(pairs),),
        in_specs=[
            pl.BlockSpec((tq, d), lambda i, qt, kt: (qt[i], 0)),
            pl.BlockSpec((tk, d), lambda i, qt, kt: (kt[i], 0)),
            pl.BlockSpec((tk, d), lambda i, qt, kt: (kt[i], 0)),
            pl.BlockSpec((tq, tk), lambda i, qt, kt: (qt[i], kt[i])),
        ],
        out_specs=pl.BlockSpec((tq, d), lambda i, qt, kt: (qt[i], 0)),
        scratch_shapes=[pltpu.VMEM((tq, d), F32), pltpu.VMEM((tq, LANES), F32),
                        pltpu.VMEM((tq, LANES), F32)],
    )
    return pl.pallas_call(
        functools.partial(_attn_kernel, tq=tq, tk=tk),
        out_shape=jax.ShapeDtypeStruct((n, d), MXU_DTYPE),
        grid_spec=grid_spec,
        compiler_params=_cparams(("arbitrary",)),
        name="dsa_attend",
    )(qb_tab, kt_tab, q, k, v, mask)


def _out_kernel(x_ref, a_ref, w_ref, o_ref):
    o_ref[...] = x_ref[...] + _dot(a_ref[...], w_ref[...])


def _out_residual(x, a, w):
    n, d = x.shape
    tm = min(ROW_TM, n)
    return pl.pallas_call(
        _out_kernel,
        out_shape=jax.ShapeDtypeStruct((n, d), F32),
        grid=(n // tm,),
        in_specs=[pl.BlockSpec((tm, d), lambda i: (i, 0)),
                  pl.BlockSpec((tm, d), lambda i: (i, 0)),
                  pl.BlockSpec((d, d), lambda i: (0, 0))],
        out_specs=pl.BlockSpec((tm, d), lambda i: (i, 0)),
        compiler_params=_cparams(("parallel",)),
        name="attn_out",
    )(x, a, w)


def _select_t_kernel(qi_ref, wit_ref, ki_ref, mask_ref, s_ref, s16_ref, lo_ref, hi_ref, clo_ref,
                     chi_ref, *, tq, tk, topk, unroll, unroll16):
    qb = pl.program_id(0)
    n_all = mask_ref.shape[0]
    nkt = (qb * tq + tq - 1) // tk + 1
    nrow = nkt * tk
    rows_it = SUBLANES * unroll

    qm = []
    for hh in range(IDX_HEADS):
        blk = qi_ref[:, (hh // 2) * LANES:(hh // 2 + 1) * LANES]
        qm.append(jnp.where(_lane_head_mask(blk.shape, hh % 2), blk, jnp.zeros_like(blk)))
    qm = jnp.concatenate(qm, axis=0)
    wt = wit_ref[...]
    key_in = lax.broadcasted_iota(I32, (tk, tq), 0)
    q_pos = qb * tq + lax.broadcasted_iota(I32, (tk, tq), 1)

    def score_tile(kt, _):
        k0 = pl.multiple_of(kt * tk, tk)
        s = _dot_nt(ki_ref[pl.ds(k0, tk), :], qm)
        acc = wt[0:1, :] * jnp.maximum(s[:, 0:tq], 0.0)
        for hh in range(1, IDX_HEADS):
            acc = acc + wt[hh:hh + 1, :] * jnp.maximum(s[:, hh * tq:(hh + 1) * tq], 0.0)
        sc = jnp.where(key_in + k0 <= q_pos, acc, -jnp.inf)
        s_ref[pl.ds(k0, tk), :] = sc
        hi_bits = lax.bitcast_convert_type(sc, I32) & -65536
        s16_ref[pl.ds(k0, tk), :] = lax.bitcast_convert_type(hi_bits, F32).astype(BF16)
        return 0

    lax.fori_loop(0, nkt, score_tile, 0)

    lo_ref[...] = jnp.full(lo_ref.shape, KEY16_NEG_INF, I32)
    hi_ref[...] = jnp.full(hi_ref.shape, KEY16_POS_INF + 1, I32)
    clo_ref[...] = jnp.full(clo_ref.shape, 1, I32) * nrow
    chi_ref[...] = jnp.zeros(chi_ref.shape, I32)
    rows16 = 2 * SUBLANES * unroll16
    one16 = jnp.ones((), BF16)
    zero16 = jnp.zeros((), BF16)

    def count16(cand):
        def body(i, acc):
            r0 = pl.multiple_of(i * rows16, rows16)
            blk = s16_ref[pl.ds(r0, rows16), :].reshape(unroll16, 2 * SUBLANES, tq)
            ones = jnp.where(blk >= cand[None], one16, zero16)
            part = ones[0]
            for u in range(1, unroll16):
                part = part + ones[u]
            return acc + part.astype(F32)
        acc = lax.fori_loop(0, nrow // rows16, body, jnp.zeros((2 * SUBLANES, tq), F32))
        tot = jnp.sum(acc, axis=0, keepdims=True).astype(I32)
        return jnp.broadcast_to(tot, (SUBLANES, tq))

    def coarse_step(_, carry):
        lo, hi = lo_ref[...], hi_ref[...]
        mid = (lo + hi) >> 1
        active = mid != lo
        bits16 = (mid ^ ((mid >> 31) & 0x7FFF)) & 0xFFFF
        cand = lax.bitcast_convert_type(bits16 << 16, F32)
        cand = jnp.concatenate([cand, cand], axis=0).astype(BF16)
        cnt = count16(cand)
        ge = cnt >= topk
        up = active & ge
        dn = active & jnp.logical_not(ge)
        lo_ref[...] = jnp.where(up, mid, lo)
        hi_ref[...] = jnp.where(dn, mid, hi)
        clo_ref[...] = jnp.where(up, cnt, clo_ref[...])
        chi_ref[...] = jnp.where(dn, cnt, chi_ref[...])
        return carry

    lax.fori_loop(0, 16, coarse_step, 0)

    lo16 = lo_ref[...]
    none_finite = lo16 == KEY16_NEG_INF
    lo_ref[...] = jnp.where(none_finite, KEY_NEG_INF, lo16 << 16)
    hi_ref[...] = jnp.where(none_finite, KEY_NEG_INF + 1, (lo16 + 1) << 16)

    def count(pred):
        def body(i, acc):
            r0 = pl.multiple_of(i * rows_it, rows_it)
            blk = s_ref[pl.ds(r0, rows_it), :].reshape(unroll, SUBLANES, tq)
            return acc + jnp.sum(pred(blk, r0).astype(I32), axis=0)
        acc = lax.fori_loop(0, nrow // rows_it, body, jnp.zeros((SUBLANES, tq), I32))
        return jnp.broadcast_to(jnp.sum(acc, axis=0, keepdims=True), (SUBLANES, tq))

    def bisect_step(carry):
        it, _ = carry
        lo, hi = lo_ref[...], hi_ref[...]
        mid = (lo & hi) + ((lo ^ hi) >> 1)
        active = mid != lo
        cand = _key_to_f32(mid)
        cnt = count(lambda blk, r0: blk >= cand[None])
        ge = cnt >= topk
        up = active & ge
        dn = active & jnp.logical_not(ge)
        hit = active & (cnt == topk)
        lo_ref[...] = jnp.where(up, mid, lo)
        hi_ref[...] = jnp.where(hit, mid + 1, jnp.where(dn, mid, hi))
        clo_ref[...] = jnp.where(up, cnt, clo_ref[...])
        chi_ref[...] = jnp.where(dn, cnt, chi_ref[...])
        return it + 1, jnp.max(active.astype(I32))

    lax.while_loop(lambda c: (c[0] < 40) & (c[1] > 0), bisect_step,
                   (jnp.zeros((), I32), jnp.ones((), I32)))

    tie = (clo_ref[...] > topk) & (lo_ref[...] > KEY_NEG_INF)
    any_tie = jnp.max(tie.astype(I32))

    @pl.when(any_tie == 0)
    def _():
        hi_ref[...] = jnp.full(hi_ref.shape, n_all, I32)

    @pl.when(any_tie > 0)
    def _():
        chi_ref[...] = topk - chi_ref[...]
        hi_ref[...] = jnp.full(hi_ref.shape, 1, I32) * (nrow - 1)
        clo_ref[...] = jnp.full(clo_ref.shape, -1, I32)
        sub = (lax.broadcasted_iota(I32, (unroll, SUBLANES, tq), 0) * SUBLANES
               + lax.broadcasted_iota(I32, (unroll, SUBLANES, tq), 1))
        thr = _key_to_f32(lo_ref[...])

        def tie_step(_, c):
            jl, jh = clo_ref[...], hi_ref[...]
            mid = jl + ((jh - jl) >> 1)
            active = (jh - jl) > 1
            cnt = count(lambda blk, r0: (blk == thr[None]) & (sub + r0 <= mid[None]))
            ok = cnt >= chi_ref[...]
            hi_ref[...] = jnp.where(active & ok, mid, jh)
            clo_ref[...] = jnp.where(active & jnp.logical_not(ok), mid, jl)
            return c

        lax.fori_loop(0, int(math.ceil(math.log2(n_all))) + 1, tie_step, 0)
        hi_ref[...] = jnp.where(tie, hi_ref[...], n_all)

    thr_row = _key_to_f32(lo_ref[0:1, :])
    last_row = hi_ref[0:1, :]

    def mask_tile(kt, _):
        k0 = pl.multiple_of(kt * tk, tk)
        s = s_ref[pl.ds(k0, tk), :]
        sel = ((s > thr_row) | ((s == thr_row) & (key_in + k0 <= last_row))) & (s > -jnp.inf)
        mask_ref[pl.ds(k0, tk), :] = jnp.where(sel, 1, 0).astype(mask_ref.dtype)
        return 0

    lax.fori_loop(0, nkt, mask_tile, 0)

    def zero_tile(kt, _):
        k0 = pl.multiple_of(kt * tk, tk)
        mask_ref[pl.ds(k0, tk), :] = jnp.zeros((tk, tq), mask_ref.dtype)
        return 0

    lax.fori_loop(nkt, n_all // tk, zero_tile, 0)


def _dsa_select_t(qi, wit, ki, topk):
    n = qi.shape[0]
    tq = min(SEL_TQ, n)
    tk = min(SEL_TK, n)
    assert tk % (SUBLANES * SEL_UNROLL) == 0 and tk % (2 * SUBLANES * SEL_UNROLL16) == 0
    return pl.pallas_call(
        functools.partial(_select_t_kernel, tq=tq, tk=tk, topk=topk, unroll=SEL_UNROLL,
                          unroll16=SEL_UNROLL16),
        out_shape=jax.ShapeDtypeStruct((n, n), jnp.int8),
        grid=(n // tq,),
        in_specs=[pl.BlockSpec((tq, qi.shape[1]), lambda i: (i, 0)),
                  pl.BlockSpec((IDX_HEADS, tq), lambda i: (0, i)),
                  pl.BlockSpec((n, LANES), lambda i: (0, 0))],
        out_specs=pl.BlockSpec((n, tq), lambda i: (0, i)),
        scratch_shapes=[pltpu.VMEM((n, tq), F32), pltpu.VMEM((n, tq), BF16)]
        + [pltpu.VMEM((SUBLANES, tq), I32)] * 4,
        compiler_params=_cparams(("parallel",)),
        name="dsa_select",
    )(qi, wit, ki)


I16 = jnp.int16
L16_MIN = -32768
ROW_NEVER = 32767


def _select16_kernel(qi_ref, wit_ref, ki_ref, mask_ref, h16_ref, l16_ref, lo_ref, hi_ref, clo_ref,
                     chi_ref, base_ref, *, tq, tk, topk, unroll16):
    qb = pl.program_id(0)
    n_all = mask_ref.shape[0]
    nkt = (qb * tq + tq - 1) // tk + 1
    nrow = nkt * tk
    rows16 = 2 * SUBLANES * unroll16
    ntrip = nrow // rows16

    qm = []
    for hh in range(IDX_HEADS):
        blk = qi_ref[:, (hh // 2) * LANES:(hh // 2 + 1) * LANES]
        qm.append(jnp.where(_lane_head_mask(blk.shape, hh % 2), blk, jnp.zeros_like(blk)))
    qm = jnp.concatenate(qm, axis=0)
    wt = wit_ref[...]
    causal_slack = (qb * tq + lax.broadcasted_iota(I32, (tk, tq), 1)
                    - lax.broadcasted_iota(I32, (tk, tq), 0))

    def score_tile(kt, _):
        k0 = pl.multiple_of(kt * tk, tk)
        s = _dot_nt(ki_ref[pl.ds(k0, tk), :], qm)
        acc = wt[0:1, :] * jnp.maximum(s[:, 0:tq], 0.0)
        for hh in range(1, IDX_HEADS):
            acc = acc + wt[hh:hh + 1, :] * jnp.maximum(s[:, hh * tq:(hh + 1) * tq], 0.0)
        acc = jnp.where(acc == 0.0, 0.0, acc)
        sc = jnp.where(k0 <= causal_slack, acc, -jnp.inf)
        bits = lax.bitcast_convert_type(sc, I32)
        key = bits ^ ((bits >> 31) & 0x7FFFFFFF)
        h16_ref[pl.ds(k0, tk), :] = (key >> 16).astype(I16)
        l16_ref[pl.ds(k0, tk), :] = (key ^ 0x8000).astype(I16)
        return 0

    lax.fori_loop(0, nkt, score_tile, 0)

    one16 = jnp.ones((), I16)
    zero16 = jnp.zeros((), I16)
    sub16 = (lax.broadcasted_iota(I32, (unroll16, 2 * SUBLANES, tq), 0) * (2 * SUBLANES)
             + lax.broadcasted_iota(I32, (unroll16, 2 * SUBLANES, tq), 1)).astype(I16)

    def pack16(v):
        return jnp.concatenate([v, v], axis=0).astype(I16)

    def count(pred):
        def body(i, acc):
            r0 = pl.multiple_of(i * rows16, rows16)
            hb = h16_ref[pl.ds(r0, rows16), :].reshape(unroll16, 2 * SUBLANES, tq)
            lb = l16_ref[pl.ds(r0, rows16), :].reshape(unroll16, 2 * SUBLANES, tq)
            ones = jnp.where(pred(hb, lb, r0), one16, zero16)
            part = ones[0]
            for u in range(1, unroll16):
                part = part + ones[u]
            return acc + part.astype(I32)
        acc = lax.fori_loop(0, ntrip, body, jnp.zeros((2 * SUBLANES, tq), I32))
        return jnp.broadcast_to(jnp.sum(acc, axis=0, keepdims=True), (SUBLANES, tq))

    def bisect(count_ge, n_steps):
        def step(_, carry):
            lo, hi = lo_ref[...], hi_ref[...]
            mid = (lo + hi) >> 1
            active = mid != lo
            cnt = count_ge(pack16(mid)) + base_ref[...]
            ge = cnt >= topk
            up = active & ge
            dn = active & jnp.logical_not(ge)
            lo_ref[...] = jnp.where(up, mid, lo)
            hi_ref[...] = jnp.where(dn, mid, hi)
            clo_ref[...] = jnp.where(up, cnt, clo_ref[...])
            chi_ref[...] = jnp.where(dn, cnt, chi_ref[...])
            return carry
        lax.fori_loop(0, n_steps, step, 0)

    lo_ref[...] = jnp.full(lo_ref.shape, KEY16_NEG_INF, I32)
    hi_ref[...] = jnp.full(hi_ref.shape, KEY16_POS_INF + 1, I32)
    clo_ref[...] = jnp.full(clo_ref.shape, 1, I32) * nrow
    chi_ref[...] = jnp.zeros(chi_ref.shape, I32)
    base_ref[...] = jnp.zeros(base_ref.shape, I32)
    bisect(lambda c: count(lambda hb, lb, r0: hb >= c[None]), 16)

    t_hi = lo_ref[...]
    none_finite = t_hi == KEY16_NEG_INF
    t_hi16 = pack16(t_hi)

    def bucket_tile(i, _):
        r0 = pl.multiple_of(i * rows16, rows16)
        hb = h16_ref[pl.ds(r0, rows16), :].reshape(unroll16, 2 * SUBLANES, tq)
        lb = l16_ref[pl.ds(r0, rows16), :].reshape(unroll16, 2 * SUBLANES, tq)
        lb = jnp.where(hb == t_hi16[None], lb, jnp.full((), L16_MIN, I16))
        l16_ref[pl.ds(r0, rows16), :] = lb.reshape(rows16, tq)
        return 0

    lax.fori_loop(0, ntrip, bucket_tile, 0)
    base_ref[...] = chi_ref[...]
    lo_ref[...] = jnp.full(lo_ref.shape, L16_MIN, I32)
    hi_ref[...] = jnp.full(hi_ref.shape, -L16_MIN, I32)
    bisect(lambda c: count(lambda hb, lb, r0: lb >= c[None]), 16)
    t_lo = jnp.where(none_finite, -L16_MIN - 1, lo_ref[...])

    tie = (clo_ref[...] > topk) & jnp.logical_not(none_finite)
    any_tie = jnp.max(tie.astype(I32))
    t_lo16 = pack16(t_lo)

    def rank_tile(i, _):
        r0 = pl.multiple_of(i * rows16, rows16)
        hb = h16_ref[pl.ds(r0, rows16), :].reshape(unroll16, 2 * SUBLANES, tq)
        lb = l16_ref[pl.ds(r0, rows16), :].reshape(unroll16, 2 * SUBLANES, tq)
        in_bucket = hb == t_hi16[None]
        rows = sub16 + jnp.full((2 * SUBLANES, tq), r0, I32).astype(I16)[None]
        rank = jnp.where(in_bucket & (lb == t_lo16[None]), rows,
                         jnp.where(in_bucket & (lb > t_lo16[None]),
                                   jnp.full((), -1, I16), jnp.full((), ROW_NEVER, I16)))
        l16_ref[pl.ds(r0, rows16), :] = rank.reshape(rows16, tq)
        return 0

    lax.fori_loop(0, ntrip, rank_tile, 0)

    @pl.when(any_tie == 0)
    def _():
        hi_ref[...] = jnp.full(hi_ref.shape, n_all, I32)

    @pl.when(any_tie > 0)
    def _():
        chi_ref[...] = topk - base_ref[...]
        hi_ref[...] = jnp.full(hi_ref.shape, 1, I32) * (nrow - 1)
        clo_ref[...] = jnp.full(clo_ref.shape, -1, I32)

        def tie_step(_, c):
            jl, jh = clo_ref[...], hi_ref[...]
            mid = jl + ((jh - jl) >> 1)
            active = (jh - jl) > 1
            mid16 = pack16(mid)
            cnt = count(lambda hb, lb, r0: lb <= mid16[None])
            ok = cnt >= chi_ref[...]
            hi_ref[...] = jnp.where(active & ok, mid, jh)
            clo_ref[...] = jnp.where(active & jnp.logical_not(ok), mid, jl)
            return c

        n_halvings = 0
        for j in range(int(math.ceil(math.log2(n_all // tk))) + 1):
            n_halvings = n_halvings + (((nkt - 1) >> j) > 0).astype(I32)
        lax.fori_loop(0, n_halvings + int(math.log2(tk)) + 1, tie_step, 0)
        hi_ref[...] = jnp.where(tie, hi_ref[...], n_all)

    last16 = pack16(hi_ref[...])
    g16 = 2 * SUBLANES

    def mask_tile(kt, _):
        k0 = pl.multiple_of(kt * tk, tk)
        hb = h16_ref[pl.ds(k0, tk), :].reshape(tk // g16, g16, tq)
        rank = l16_ref[pl.ds(k0, tk), :].reshape(tk // g16, g16, tq)
        sel = ((hb > t_hi16[None]) | (rank <= last16[None])) & (hb > KEY16_NEG_INF)
        sel = jnp.where(sel, one16, zero16).reshape(tk, tq)
        mask_ref[pl.ds(k0, tk), :] = sel.astype(mask_ref.dtype)
        return 0

    lax.fori_loop(0, nkt, mask_tile, 0)

    def zero_tile(kt, _):
        k0 = pl.multiple_of(kt * tk, tk)
        mask_ref[pl.ds(k0, tk), :] = jnp.zeros((tk, tq), mask_ref.dtype)
        return 0

    lax.fori_loop(nkt, n_all // tk, zero_tile, 0)


def _dsa_select16(qi, wit, ki, topk):
    n = qi.shape[0]
    tq = min(SEL_TQ, n)
    tk = min(SEL_TK, n)
    assert tk % (2 * SUBLANES * SEL_UNROLL16) == 0 and n < -L16_MIN
    return pl.pallas_call(
        functools.partial(_select16_kernel, tq=tq, tk=tk, topk=topk, unroll16=SEL_UNROLL16),
        out_shape=jax.ShapeDtypeStruct((n, n), jnp.int8),
        grid=(n // tq,),
        in_specs=[pl.BlockSpec((tq, qi.shape[1]), lambda i: (i, 0)),
                  pl.BlockSpec((IDX_HEADS, tq), lambda i: (0, i)),
                  pl.BlockSpec((n, LANES), lambda i: (0, 0))],
        out_specs=pl.BlockSpec((n, tq), lambda i: (0, i)),
        scratch_shapes=[pltpu.VMEM((n, tq), I16), pltpu.VMEM((n, tq), I16)]
        + [pltpu.VMEM((SUBLANES, tq), I32)] * 5,
        compiler_params=_cparams(("parallel",)),
        name="dsa_select",
    )(qi, wit, ki)


def _attn_t_kernel(qb_tab, kt_tab, q_ref, k_ref, vt_ref, mask_ref, o_ref, acc_ref, m_ref,
                   *, tq, tk, qs):
    step = pl.program_id(0)
    kt = kt_tab[step]

    @pl.when(kt == 0)
    def _():
        acc_ref[...] = jnp.zeros_like(acc_ref)
        m_ref[...] = jnp.full(m_ref.shape, NEG_BIG, MXU_DTYPE).astype(F32)

    bias = jnp.where(mask_ref[...].astype(I32) != 0, 0.0, NEG_BIG).astype(MXU_DTYPE)

    def logits(hd):
        blk = hd // 2
        qblk = q_ref[:, blk * LANES:(blk + 1) * LANES]
        qm = jnp.where(_lane_head_mask(qblk.shape, hd % 2), qblk, jnp.zeros_like(qblk))
        return _dot_nt(k_ref[:, blk * LANES:(blk + 1) * LANES], qm).astype(MXU_DTYPE) + bias

    s_next = logits(0)
    for hd in range(N_HEADS):
        r0 = hd * VT_ROWS
        s = s_next
        if hd + 1 < N_HEADS:
            s_next = logits(hd + 1)
        m_old = m_ref[hd:hd + 1, :]
        m_new = jnp.maximum(m_old, jnp.max(s, axis=0, keepdims=True).astype(F32))
        alpha = jnp.exp2(m_old - m_new)
        p = jnp.exp2(s - m_new.astype(MXU_DTYPE))
        m_ref[hd:hd + 1, :] = m_new
        pv = _dot(vt_ref[r0:r0 + VT_ROWS, :], p)
        acc_ref[r0:r0 + VT_ROWS, :] = acc_ref[r0:r0 + VT_ROWS, :] * alpha + pv

    last = kt == ((qb_tab[step] + 1) * tq - 1) // tk

    @pl.when(last)
    def _():
        for hd in range(N_HEADS):
            r0 = hd * VT_ROWS
            o_ref[hd * HEAD_DIM:(hd + 1) * HEAD_DIM, :] = (
                acc_ref[r0:r0 + HEAD_DIM, :] / acc_ref[r0 + HEAD_DIM:r0 + HEAD_DIM + 1, :]
            ).astype(o_ref.dtype)


def _dsa_attend_t(q, k, vt, mask_t):
    n, d = q.shape
    tq = min(ATT_TQ, n)
    tk = min(ATT_TK, n)
    pairs = [(qb, kt) for qb in range(n // tq) for kt in range(((qb + 1) * tq - 1) // tk + 1)]
    qb_tab = jnp.asarray([p[0] for p in pairs], I32)
    kt_tab = jnp.asarray([p[1] for p in pairs], I32)
    grid_spec = pltpu.PrefetchScalarGridSpec(
        num_scalar_prefetch=2,
        grid=(len(pairs),),
        in_specs=[
            pl.BlockSpec((tq, d), lambda i, qt, kt: (qt[i], 0)),
            pl.BlockSpec((tk, d), lambda i, qt, kt: (kt[i], 0)),
            pl.BlockSpec((vt.shape[0], tk), lambda i, qt, kt: (0, kt[i])),
            pl.BlockSpec((tk, tq), lambda i, qt, kt: (kt[i], qt[i])),
        ],
        out_specs=pl.BlockSpec((d, tq), lambda i, qt, kt: (0, qt[i])),
        scratch_shapes=[pltpu.VMEM((vt.shape[0], tq), F32), pltpu.VMEM((N_HEADS, tq), F32)],
    )
    return pl.pallas_call(
        functools.partial(_attn_t_kernel, tq=tq, tk=tk, qs=min(ATT_QS, tq)),
        out_shape=jax.ShapeDtypeStruct((d, n), MXU_DTYPE),
        grid_spec=grid_spec,
        compiler_params=_cparams(("arbitrary",)),
        name="dsa_attend",
    )(qb_tab, kt_tab, q, k, vt, mask_t)


def _out_t_kernel(x_ref, at_ref, w_ref, o_ref):
    o_ref[...] = x_ref[...] + lax.dot_general(
        at_ref[...], w_ref[...], (((0,), (0,)), ((), ())), preferred_element_type=F32)


def _out_residual_t(x, at, w):
    n, d = x.shape
    tm = min(ROW_TM, n)
    return pl.pallas_call(
        _out_t_kernel,
        out_shape=jax.ShapeDtypeStruct((n, d), F32),
        grid=(n // tm,),
        in_specs=[pl.BlockSpec((tm, d), lambda i: (i, 0)),
                  pl.BlockSpec((d, tm), lambda i: (0, i)),
                  pl.BlockSpec((d, d), lambda i: (0, 0))],
        out_specs=pl.BlockSpec((tm, d), lambda i: (i, 0)),
        compiler_params=_cparams(("parallel",)),
        name="attn_out",
    )(x, at, w)


def _rope_lane_tables(length):
    inv_freq = ROPE_THETA ** (-jnp.arange(0, HEAD_DIM, 2, dtype=F32) / HEAD_DIM)
    ang = jnp.arange(length, dtype=F32)[:, None] * inv_freq[None, :]
    lane = np.arange(LANES)
    cos_t = jnp.cos(ang)[:, lane % 32]
    sin_t = jnp.sin(ang)[:, lane % 32] * jnp.asarray(np.where(lane < 64, -1.0, 1.0), F32)
    return cos_t, sin_t


def kernel(x, s5_lambda_re, s5_lambda_im, s5_log_dt, s5_b_re, s5_b_im, s5_c_re, s5_c_im, s5_d, s5_w_glu, dsa_w_in, dsa_q_norm, dsa_k_norm, dsa_w_o, ffn_w_gate_up, ffn_w_down, norm_mix, norm_ffn):
    bsz, length, d = x.shape
    depth = norm_mix.shape[0]
    topk = min(TOPK_MAX, length // 4)
    nchunk = length // S5_CHUNK
    cos_t, sin_t = _rope_lane_tables(length)
    outs = []
    for b in range(bsz):
        xs = x[b].astype(F32)
        for i in range(depth):
            j = i // 2
            wgu = ffn_w_gate_up[i].astype(MXU_DTYPE)
            wd = ffn_w_down[i].astype(MXU_DTYPE)
            if i % 2 == 0:
                tables = _s5_tables(s5_lambda_re[j], s5_lambda_im[j], s5_log_dt[j], s5_b_re[j],
                                    s5_b_im[j], s5_c_re[j], s5_c_im[j])
                h_t = _norm_planes(xs, norm_mix[i])
                g_t = _s5_scan(h_t, tables, s5_d[j])
                xs = _glu_planes(xs, g_t, s5_w_glu[j].astype(MXU_DTYPE))
                xs = _ffn(xs, norm_ffn[i], wgu, wd)
            else:
                q, k, vt, qi, ki, wi = _dsa_project(xs, norm_mix[i], dsa_w_in[j], dsa_q_norm[j],
                                                    dsa_k_norm[j], cos_t, sin_t)
                mask_t = _dsa_select16(qi, wi[:, :IDX_HEADS].T, ki, topk)
                att_t = _dsa_attend_t(q, k, vt, mask_t)
                xs = _out_residual_t(xs, att_t, dsa_w_o[j].astype(MXU_DTYPE))
                xs = _ffn(xs, norm_ffn[i], wgu, wd)
        outs.append(xs)
    return jnp.stack(outs, axis=0).astype(x.dtype)
```

```python
import functools
import math

import jax
import jax.numpy as jnp
import numpy as np
from jax import lax
from jax.experimental import pallas as pl
from jax.experimental.pallas import tpu as pltpu

F32 = jnp.float32
BF16 = jnp.bfloat16
I32 = jnp.int32
MXU_DTYPE = BF16

D_MODEL = 1024
S5_GROUP = 16
S5_STATE = 64
N_HEADS = 16
HEAD_DIM = 64
IDX_HEADS = 8
IDX_DIM = 64
TOPK_MAX = 256
ROPE_THETA = 10000.0
EPS = 1e-6

LANES = 128
SUBLANES = 8
MXU_DIM = 256
VMEM_LIMIT = 56 * 1024 * 1024

S5_CHUNK = 16
S5_SLAB_GROUPS = LANES // S5_GROUP
NEG_BIG = -1e30
LOG2E = math.log2(math.e)
VT_PAD = 16
VT_ROWS = HEAD_DIM + VT_PAD

ROW_TM = 512
NORM_TM = 1024
S5_TC = 512
PROJ_TM = 512
SEL_TQ = 256
SEL_TK = 512
SEL_RB = 64
SEL_UNROLL = 64
SEL_UNROLL16 = 32
ATT_TQ = 512
ATT_TK = 512
ATT_QS = 512


def _cparams(sem, flags=None):
    return pltpu.CompilerParams(dimension_semantics=sem, vmem_limit_bytes=VMEM_LIMIT, flags=flags)


def _rms(x, gain=None):
    y = x * lax.rsqrt(jnp.mean(x * x, axis=-1, keepdims=True) + EPS)
    return y if gain is None else y * gain


def _dot(a, b):
    return jnp.dot(a, b, preferred_element_type=F32)


def _dot_nt(a, b):
    return lax.dot_general(a, b, (((1,), (1,)), ((), ())), preferred_element_type=F32)


def _ffn_kernel(x_ref, g_ref, wgu_ref, wd_ref, o_ref, acc_ref, *, d_ff, fc):
    x = x_ref[...]
    h = _rms(x, g_ref[...]).astype(MXU_DTYPE)
    for c in range(d_ff // fc):
        g = _dot(h, wgu_ref[:, c * fc:(c + 1) * fc])
        u = _dot(h, wgu_ref[:, d_ff + c * fc:d_ff + (c + 1) * fc])
        a = (g * jax.nn.sigmoid(g) * u).astype(MXU_DTYPE)
        d = _dot(a, wd_ref[c * fc:(c + 1) * fc, :])
        if c == 0:
            acc_ref[...] = d
        else:
            acc_ref[...] += d
    o_ref[...] = x + acc_ref[...]


def _ffn(x, gain, wgu, wd):
    n, d = x.shape
    d_ff = wd.shape[0]
    tm = min(ROW_TM, n)
    fc = MXU_DIM
    return pl.pallas_call(
        functools.partial(_ffn_kernel, d_ff=d_ff, fc=fc),
        out_shape=jax.ShapeDtypeStruct((n, d), F32),
        grid=(n // tm,),
        in_specs=[
            pl.BlockSpec((tm, d), lambda i: (i, 0)),
            pl.BlockSpec((1, d), lambda i: (0, 0)),
            pl.BlockSpec((d, 2 * d_ff), lambda i: (0, 0)),
            pl.BlockSpec((d_ff, d), lambda i: (0, 0)),
        ],
        out_specs=pl.BlockSpec((tm, d), lambda i: (i, 0)),
        scratch_shapes=[pltpu.VMEM((tm, d), F32)],
        compiler_params=_cparams(("parallel",)),
        name="ffn",
    )(x, gain.reshape(1, d), wgu, wd)


def _norm_kernel(x_ref, g_ref, o_ref):
    o_ref[...] = _rms(x_ref[...], g_ref[...]).astype(o_ref.dtype)


def _norm(x, gain):
    n, d = x.shape
    tm = min(NORM_TM, n)
    return pl.pallas_call(
        _norm_kernel,
        out_shape=jax.ShapeDtypeStruct((n, d), MXU_DTYPE),
        grid=(n // tm,),
        in_specs=[pl.BlockSpec((tm, d), lambda i: (i, 0)),
                  pl.BlockSpec((1, d), lambda i: (0, 0))],
        out_specs=pl.BlockSpec((tm, d), lambda i: (i, 0)),
        compiler_params=_cparams(("parallel",)),
        name="norm",
    )(x, gain.reshape(1, d))


def _s5_tables(lam_re, lam_im, log_dt, b_re, b_im, c_re, c_im):
    hp = lax.Precision.HIGHEST
    g, p = lam_re.shape
    h = S5_GROUP
    nsl = g // S5_SLAB_GROUPS
    sg = S5_SLAB_GROUPS
    t = S5_CHUNK
    lam_re, lam_im, log_dt = lam_re.astype(F32), lam_im.astype(F32), log_dt.astype(F32)
    b_re, b_im, c_re, c_im = (a.astype(F32) for a in (b_re, b_im, c_re, c_im))
    dt = jnp.exp(log_dt)[:, None]

    def apow(k):
        k = jnp.asarray(k, F32).reshape((-1, 1, 1))
        mag = jnp.exp(lam_re[None] * dt[None] * k)
        ang = lam_im[None] * dt[None] * k
        return mag * jnp.cos(ang), mag * jnp.sin(ang)

    ar, ai = apow([1.0])
    ar, ai = ar[0], ai[0]
    den = lam_re * lam_re + lam_im * lam_im
    nr, ni = ar - 1.0, ai
    qr = (nr * lam_re + ni * lam_im) / den
    qi = (ni * lam_re - nr * lam_im) / den
    bbr = qr[..., None] * b_re - qi[..., None] * b_im
    bbi = qr[..., None] * b_im + qi[..., None] * b_re

    pr, pi = apow(np.arange(t + 1))
    mr = c_re[None] * pr[:, :, None, :] - c_im[None] * pi[:, :, None, :]
    mi = c_re[None] * pi[:, :, None, :] + c_im[None] * pr[:, :, None, :]

    kk = (jnp.einsum('tghp,gpk->tghk', mr[:t], bbr, precision=hp)
          - jnp.einsum('tghp,gpk->tghk', mi[:t], bbi, precision=hp))
    kp = jnp.concatenate([jnp.zeros_like(kk[:1]), kk], axis=0)

    def block_diag(x, row_dims, col_dims):
        r0, r2 = row_dims
        c0, c1 = col_dims
        nr, ncc = r0 * sg * r2, c0 * c1
        x2 = x.reshape(nsl, t // 2, nr, ncc).astype(MXU_DTYPE)
        ci = np.arange(c0 * sg * c1)
        src = (ci // (sg * c1)) * c1 + ci % c1
        spread = jnp.asarray(np.arange(ncc)[:, None] == src[None, :], MXU_DTYPE)
        ri = np.arange(nr)
        keep = ((ri // r2) % sg)[:, None] == ((ci // c1) % sg)[None, :]
        y = jnp.einsum('bzrc,cn->bzrn', x2, spread, preferred_element_type=MXU_DTYPE)
        return jnp.where(keep, y, jnp.zeros((), MXU_DTYPE))

    dl = np.arange(t // 2)[:, None, None]
    sl = np.arange(2)[None, :, None]
    jl = np.arange(2)[None, None, :]
    idx = 2 * dl + jl - sl + 1
    kg = kp[idx]
    kg = kg.reshape(t // 2, 2, 2, nsl, sg, h, h)
    tp = block_diag(kg.transpose(3, 0, 1, 4, 6, 2, 5), (2, h), (2, h))

    prs, pis = pr[t - 1::-1][:t], pi[t - 1::-1][:t]
    er = prs[..., None] * bbr[None] - pis[..., None] * bbi[None]
    ei = prs[..., None] * bbi[None] + pis[..., None] * bbr[None]
    bf = jnp.stack([er, ei], axis=2)
    bf = bf.reshape(t // 2, 2, nsl, sg, 2, p, h)
    bz = block_diag(bf.transpose(2, 0, 1, 3, 6, 4, 5), (2, h), (2, p))

    cf = jnp.stack([mr[1:], -mi[1:]], axis=2)
    cf = cf.reshape(t // 2, 2, nsl, sg, 2, h, p)
    cz = block_diag(cf.transpose(2, 0, 4, 3, 6, 1, 5), (2, p), (2, h))

    def slab_state(re, im):
        k = re.shape[0]
        x = jnp.stack([re, im], axis=1).reshape(k, 2, nsl, sg * p)
        return x.transpose(2, 0, 1, 3).reshape(nsl, k, 2 * sg * p)

    ad = slab_state(*apow([t * 1.0, t * 2.0, t * 4.0]))
    ap8 = slab_state(*apow(t * (np.arange(SUBLANES) + 1.0)))
    return tp, bz, cz, ad, ap8


def _s5_kernel(h_ref, tp_ref, bz_ref, cz_ref, ad_ref, ap8_ref, dsk_ref, o_ref,
               carry_ref, z_ref, xp_ref, *, tc):
    half = z_ref.shape[1] // 2
    npair = S5_CHUNK // 2

    @pl.when(pl.program_id(1) == 0)
    def _():
        carry_ref[...] = jnp.zeros_like(carry_ref)

    u = [jnp.concatenate([h_ref[2 * s], h_ref[2 * s + 1]], axis=1) for s in range(npair)]

    z = _dot(u[0], bz_ref[0])
    for s in range(1, npair):
        z = z + _dot(u[s], bz_ref[s])
    z_ref[...] = z

    row = lax.broadcasted_iota(I32, (SUBLANES, half), 0)
    pr8, pi8 = ap8_ref[:, :half], ap8_ref[:, half:]

    def tile_step(t, carry):
        r0 = pl.multiple_of(t * SUBLANES, SUBLANES)
        zt = z_ref[pl.ds(r0, SUBLANES), :]
        xr, xi = zt[:, :half], zt[:, half:]
        for k, d in enumerate((1, 2, 4)):
            a = ad_ref[k:k + 1, :]
            a_r, a_i = a[:, :half], a[:, half:]
            sr = jnp.where(row >= d, pltpu.roll(xr, d, 0), 0.0)
            si = jnp.where(row >= d, pltpu.roll(xi, d, 0), 0.0)
            xr, xi = xr + a_r * sr - a_i * si, xi + a_r * si + a_i * sr
        cr, ci = carry[:, :half], carry[:, half:]
        xr, xi = xr + pr8 * cr - pi8 * ci, xi + pr8 * ci + pi8 * cr
        xpr = jnp.where(row >= 1, pltpu.roll(xr, 1, 0), cr)
        xpi = jnp.where(row >= 1, pltpu.roll(xi, 1, 0), ci)
        xp_ref[pl.ds(r0, SUBLANES), :] = jnp.concatenate([xpr, xpi], axis=1)
        return jnp.concatenate([xr[SUBLANES - 1:], xi[SUBLANES - 1:]], axis=1)

    carry_ref[...] = lax.fori_loop(0, tc // SUBLANES, tile_step, carry_ref[...])

    xp = xp_ref[...].astype(MXU_DTYPE)
    dsk = dsk_ref[...]
    for i in range(npair):
        y = _dot(xp, cz_ref[i])
        for s in range(i + 1):
            y = y + _dot(u[s], tp_ref[i - s])
        for jl in range(2):
            j = 2 * i + jl
            yj = y[:, jl * LANES:(jl + 1) * LANES] + dsk * h_ref[j].astype(F32)
            o_ref[j] = jax.nn.gelu(yj).astype(o_ref.dtype)


def _s5_scan(h_t, tables, d_skip):
    tp, bz, cz, ad, ap8 = tables
    t, c, d = h_t.shape
    nsl = d // LANES
    tc = min(S5_TC, c)
    st = bz.shape[-1]
    dsk = d_skip.astype(F32).reshape(nsl, 1, LANES)
    return pl.pallas_call(
        functools.partial(_s5_kernel, tc=tc),
        out_shape=jax.ShapeDtypeStruct((t, c, d), MXU_DTYPE),
        grid=(nsl, c // tc),
        in_specs=[
            pl.BlockSpec((t, tc, LANES), lambda b, i: (0, i, b)),
            pl.BlockSpec((None,) + tp.shape[1:], lambda b, i: (b, 0, 0, 0)),
            pl.BlockSpec((None,) + bz.shape[1:], lambda b, i: (b, 0, 0, 0)),
            pl.BlockSpec((None,) + cz.shape[1:], lambda b, i: (b, 0, 0, 0)),
            pl.BlockSpec((None,) + ad.shape[1:], lambda b, i: (b, 0, 0)),
            pl.BlockSpec((None,) + ap8.shape[1:], lambda b, i: (b, 0, 0)),
            pl.BlockSpec((None, 1, LANES), lambda b, i: (b, 0, 0)),
        ],
        out_specs=pl.BlockSpec((t, tc, LANES), lambda b, i: (0, i, b)),
        scratch_shapes=[pltpu.VMEM((1, st), F32), pltpu.VMEM((tc, st), F32),
                        pltpu.VMEM((tc, st), F32)],
        compiler_params=_cparams(("arbitrary", "arbitrary")),
        name="s5_scan",
    )(h_t, tp, bz, cz, ad, ap8, dsk)


def _plane_perm(rows):
    nc = rows // S5_CHUNK
    p = np.zeros((rows, rows), np.float32)
    c, s = np.meshgrid(np.arange(nc), np.arange(S5_CHUNK), indexing="ij")
    p[(s * nc + c).ravel(), (c * S5_CHUNK + s).ravel()] = 1.0
    return p


def _norm_planes_kernel(x_ref, g_ref, p_ref, o_ref):
    h = _rms(x_ref[...], g_ref[...]).astype(MXU_DTYPE)
    hp = _dot(p_ref[...], h).astype(o_ref.dtype)
    o_ref[...] = hp.reshape(o_ref.shape)


def _norm_planes(x, gain):
    n, d = x.shape
    tm = min(ROW_TM, n)
    nc = tm // S5_CHUNK
    perm = jnp.asarray(_plane_perm(tm), MXU_DTYPE)
    return pl.pallas_call(
        _norm_planes_kernel,
        out_shape=jax.ShapeDtypeStruct((S5_CHUNK, n // S5_CHUNK, d), MXU_DTYPE),
        grid=(n // tm,),
        in_specs=[pl.BlockSpec((tm, d), lambda i: (i, 0)),
                  pl.BlockSpec((1, d), lambda i: (0, 0)),
                  pl.BlockSpec((tm, tm), lambda i: (0, 0))],
        out_specs=pl.BlockSpec((S5_CHUNK, nc, d), lambda i: (0, i, 0)),
        compiler_params=_cparams(("parallel",)),
        name="norm_planes",
    )(x, gain.reshape(1, d), perm)


def _glu_planes_kernel(x_ref, g_ref, pt_ref, w_ref, o_ref):
    d = x_ref.shape[1]
    gp = g_ref[...].reshape(x_ref.shape)
    g = _dot(pt_ref[...], gp).astype(MXU_DTYPE)
    vg = _dot(g, w_ref[...])
    o_ref[...] = x_ref[...] + vg[:, :d] * jax.nn.sigmoid(vg[:, d:])


def _glu_planes(x, g_t, w):
    n, d = x.shape
    tm = min(ROW_TM, n)
    nc = tm // S5_CHUNK
    perm_t = jnp.asarray(_plane_perm(tm).T, MXU_DTYPE)
    return pl.pallas_call(
        _glu_planes_kernel,
        out_shape=jax.ShapeDtypeStruct((n, d), F32),
        grid=(n // tm,),
        in_specs=[pl.BlockSpec((tm, d), lambda i: (i, 0)),
                  pl.BlockSpec((S5_CHUNK, nc, d), lambda i: (0, i, 0)),
                  pl.BlockSpec((tm, tm), lambda i: (0, 0)),
                  pl.BlockSpec((d, 2 * d), lambda i: (0, 0))],
        out_specs=pl.BlockSpec((tm, d), lambda i: (i, 0)),
        compiler_params=_cparams(("parallel",)),
        name="glu",
    )(x, g_t, perm_t, w)


def _glu_kernel(x_ref, g_ref, w_ref, o_ref):
    d = x_ref.shape[1]
    vg = _dot(g_ref[...], w_ref[...])
    o_ref[...] = x_ref[...] + vg[:, :d] * jax.nn.sigmoid(vg[:, d:])


def _glu_residual(x, g, w):
    n, d = x.shape
    tm = min(ROW_TM, n)
    return pl.pallas_call(
        _glu_kernel,
        out_shape=jax.ShapeDtypeStruct((n, d), F32),
        grid=(n // tm,),
        in_specs=[pl.BlockSpec((tm, d), lambda i: (i, 0)),
                  pl.BlockSpec((tm, d), lambda i: (i, 0)),
                  pl.BlockSpec((d, 2 * d), lambda i: (0, 0))],
        out_specs=pl.BlockSpec((tm, d), lambda i: (i, 0)),
        compiler_params=_cparams(("parallel",)),
        name="glu",
    )(x, g, w)


def _head_perm(n_heads):
    n = np.arange(n_heads * HEAD_DIM)
    pb, r = n // LANES, n % LANES
    half, r2 = r // 64, r % 64
    hl, dp = r2 // 32, r2 % 32
    return (2 * pb + hl) * HEAD_DIM + 32 * half + dp


def _lane_head_mask(shape, hl):
    lane = lax.broadcasted_iota(I32, shape, len(shape) - 1)
    return ((lane % 64) // 32) == hl


def _proj_kernel(x_ref, g_ref, w_ref, wvt_ref, gq_ref, gk_ref, cos_ref, sin_ref, hm_ref,
                 q_ref, k_ref, vt_ref, qi_ref, ki_ref, wi_ref, *, d, dqi, att_scale, w_scale):
    h = _rms(x_ref[...], g_ref[...]).astype(MXU_DTYPE)
    cos, sin = cos_ref[...], sin_ref[...]
    hm = hm_ref[...]

    def rope(t):
        return t * cos + pltpu.roll(t, 64, 1) * sin

    def headnorm_rope(col0, gain_ref, out_ref, scale):
        t_all = _dot(h, w_ref[:, col0:col0 + d])
        for sb in range(d // MXU_DIM):
            c0 = sb * MXU_DIM
            t = t_all[:, c0:c0 + MXU_DIM]
            sq = t * t
            hi = sq.astype(MXU_DTYPE)
            lo = (sq - hi.astype(F32)).astype(MXU_DTYPE)
            ss = _dot(hi, hm) + _dot(lo, hm)
            tn = t * lax.rsqrt(ss * (1.0 / HEAD_DIM) + EPS) * gain_ref[:, c0:c0 + MXU_DIM]
            for b in range(MXU_DIM // LANES):
                r = rope(tn[:, b * LANES:(b + 1) * LANES])
                if scale != 1.0:
                    r = r * scale
                out_ref[:, c0 + b * LANES:c0 + (b + 1) * LANES] = r.astype(out_ref.dtype)

    headnorm_rope(0, gq_ref, q_ref, att_scale)
    headnorm_rope(d, gk_ref, k_ref, 1.0)
    vt = _dot_nt(wvt_ref[...], h)
    row = lax.broadcasted_iota(I32, vt.shape, 0)
    vt_ref[...] = jnp.where(row % VT_ROWS >= HEAD_DIM, 1.0, vt).astype(vt_ref.dtype)
    c0 = 2 * d
    t = _dot(h, w_ref[:, c0:c0 + dqi])
    for b in range(dqi // LANES):
        qi_ref[:, b * LANES:(b + 1) * LANES] = rope(t[:, b * LANES:(b + 1) * LANES]).astype(qi_ref.dtype)
    c0 += dqi
    t = _dot(h, w_ref[:, c0:c0 + LANES])
    ms = jnp.sum(t * t, axis=-1, keepdims=True) * (0.5 / IDX_DIM)
    ki_ref[...] = rope(t * lax.rsqrt(ms + EPS)).astype(ki_ref.dtype)
    c0 += LANES
    wi_ref[...] = _dot(h, w_ref[:, c0:c0 + LANES]) * w_scale


def _dsa_project(x, gain, w_in, q_gain, k_gain, cos_t, sin_t):
    n, d = x.shape
    dqi = IDX_HEADS * IDX_DIM
    pq = _head_perm(N_HEADS)
    pqi = _head_perm(IDX_HEADS)
    wq = w_in[:, 0:d][:, pq]
    wk = w_in[:, d:2 * d][:, pq]
    wvt = w_in[:, 2 * d:3 * d].T.reshape(N_HEADS, HEAD_DIM, d)
    wvt = jnp.pad(wvt, ((0, 0), (0, VT_PAD), (0, 0))).reshape(N_HEADS * VT_ROWS, d).astype(MXU_DTYPE)
    dvt = N_HEADS * VT_ROWS
    wqi = w_in[:, 3 * d:3 * d + dqi][:, pqi]
    lane = np.arange(LANES)
    wki = w_in[:, 3 * d + dqi:3 * d + dqi + IDX_DIM][:, 32 * (lane // 64) + lane % 32]
    wwi = jnp.pad(w_in[:, 3 * d + dqi + IDX_DIM:], ((0, 0), (0, LANES - IDX_HEADS)))
    w_all = jnp.concatenate([wq, wk, wqi, wki, wwi], axis=1).astype(MXU_DTYPE)
    dcol = (pq % HEAD_DIM)
    gq = q_gain.astype(F32)[dcol].reshape(1, d)
    gk = k_gain.astype(F32)[dcol].reshape(1, d)
    l2 = np.arange(MXU_DIM)
    hm = ((l2[:, None] // LANES == l2[None, :] // LANES)
          & ((l2[:, None] % 64) // 32 == (l2[None, :] % 64) // 32))
    hm = jnp.asarray(hm, MXU_DTYPE)
    tm = min(PROJ_TM, n)
    nw = w_all.shape[1]
    outs = pl.pallas_call(
        functools.partial(_proj_kernel, d=d, dqi=dqi, att_scale=HEAD_DIM ** -0.5 * LOG2E,
                          w_scale=(IDX_HEADS ** -0.5) * (IDX_DIM ** -0.5)),
        out_shape=[jax.ShapeDtypeStruct((n, d), MXU_DTYPE)] * 2
        + [jax.ShapeDtypeStruct((dvt, n), MXU_DTYPE),
           jax.ShapeDtypeStruct((n, dqi), MXU_DTYPE),
           jax.ShapeDtypeStruct((n, LANES), MXU_DTYPE),
           jax.ShapeDtypeStruct((n, LANES), F32)],
        grid=(n // tm,),
        in_specs=[
            pl.BlockSpec((tm, d), lambda i: (i, 0)),
            pl.BlockSpec((1, d), lambda i: (0, 0)),
            pl.BlockSpec((d, nw), lambda i: (0, 0)),
            pl.BlockSpec((dvt, d), lambda i: (0, 0)),
            pl.BlockSpec((1, d), lambda i: (0, 0)),
            pl.BlockSpec((1, d), lambda i: (0, 0)),
            pl.BlockSpec((tm, LANES), lambda i: (i, 0)),
            pl.BlockSpec((tm, LANES), lambda i: (i, 0)),
            pl.BlockSpec((MXU_DIM, MXU_DIM), lambda i: (0, 0)),
        ],
        out_specs=[pl.BlockSpec((tm, d), lambda i: (i, 0))] * 2
        + [pl.BlockSpec((dvt, tm), lambda i: (0, i)),
           pl.BlockSpec((tm, dqi), lambda i: (i, 0)),
           pl.BlockSpec((tm, LANES), lambda i: (i, 0)),
           pl.BlockSpec((tm, LANES), lambda i: (i, 0))],
        compiler_params=_cparams(("parallel",)),
        name="dsa_proj",
    )(x, gain.reshape(1, d), w_all, wvt, gq, gk, cos_t, sin_t, hm)
    return outs


KEY_NEG_INF = -2139095041
KEY_POS_INF = 2139095040
KEY16_NEG_INF = -32641
KEY16_POS_INF = 32640


def _key_to_f32(key):
    bits = key ^ ((key >> 31) & 0x7FFFFFFF)
    return lax.bitcast_convert_type(bits, F32)


def _select_kernel(qi_ref, wi_ref, ki_ref, mask_ref, s_ref, lo_ref, hi_ref, clo_ref, chi_ref,
                   *, tq, tk, topk, rb):
    qb = pl.program_id(0)
    n_all = mask_ref.shape[1]
    nkt = (qb * tq) // tk + 1
    ncol = nkt * (tk // LANES)

    qm = []
    for hh in range(IDX_HEADS):
        blk = qi_ref[:, (hh // 2) * LANES:(hh // 2 + 1) * LANES]
        qm.append(jnp.where(_lane_head_mask(blk.shape, hh % 2), blk, jnp.zeros_like(blk)))
    qm = jnp.concatenate(qm, axis=0)
    wv = wi_ref[...]
    row_pos = qb * tq + lax.broadcasted_iota(I32, (tq, tk), 0)
    col_in = lax.broadcasted_iota(I32, (tq, tk), 1)

    def score_tile(kt, _):
        k0 = pl.multiple_of(kt * tk, tk)
        s = _dot_nt(qm, ki_ref[pl.ds(k0, tk), :])
        acc = wv[:, 0:1] * jnp.maximum(s[0:tq], 0.0)
        for hh in range(1, IDX_HEADS):
            acc = acc + wv[:, hh:hh + 1] * jnp.maximum(s[hh * tq:(hh + 1) * tq], 0.0)
        acc = jnp.where(col_in + k0 <= row_pos, acc, -jnp.inf)
        s_ref[:, pl.ds(k0, tk)] = acc
        return 0

    lax.fori_loop(0, nkt, score_tile, 0)

    lo_ref[...] = jnp.full(lo_ref.shape, KEY_NEG_INF, I32)
    hi_ref[...] = jnp.full(hi_ref.shape, KEY_POS_INF, I32)
    clo_ref[...] = jnp.full(clo_ref.shape, 1, I32) * (ncol * LANES)
    chi_ref[...] = jnp.zeros(chi_ref.shape, I32)

    def count_rows(r0, pred):
        def body(j, acc):
            c0 = pl.multiple_of(j * LANES, LANES)
            return acc + pred(s_ref[r0:r0 + rb, pl.ds(c0, LANES)], j).astype(I32)
        acc = lax.fori_loop(0, ncol, body, jnp.zeros((rb, LANES), I32))
        return jnp.broadcast_to(jnp.sum(acc, axis=1, keepdims=True), (rb, LANES))

    def bisect_step(carry):
        it, _ = carry
        pending = jnp.zeros((), I32)
        for r0 in range(0, tq, rb):
            lo, hi = lo_ref[r0:r0 + rb, :], hi_ref[r0:r0 + rb, :]
            mid = (lo & hi) + ((lo ^ hi) >> 1)
            active = mid != lo
            cand = _key_to_f32(mid)
            cnt = count_rows(r0, lambda blk, j: blk >= cand)
            ge = cnt >= topk
            up = active & ge
            dn = active & jnp.logical_not(ge)
            hit = active & (cnt == topk)
            lo_ref[r0:r0 + rb, :] = jnp.where(up, mid, lo)
            hi_ref[r0:r0 + rb, :] = jnp.where(hit, mid + 1, jnp.where(dn, mid, hi))
            clo_ref[r0:r0 + rb, :] = jnp.where(up, cnt, clo_ref[r0:r0 + rb, :])
            chi_ref[r0:r0 + rb, :] = jnp.where(dn, cnt, chi_ref[r0:r0 + rb, :])
            pending = jnp.maximum(pending, jnp.max(active.astype(I32)))
        return it + 1, pending

    lax.while_loop(lambda c: (c[0] < 40) & (c[1] > 0), bisect_step,
                   (jnp.zeros((), I32), jnp.ones((), I32)))

    tie = (clo_ref[...] > topk) & (lo_ref[...] > KEY_NEG_INF)
    any_tie = jnp.max(tie.astype(I32))

    @pl.when(any_tie == 0)
    def _():
        hi_ref[...] = jnp.full(hi_ref.shape, n_all, I32)

    @pl.when(any_tie > 0)
    def _():
        need = topk - chi_ref[...]
        chi_ref[...] = need
        hi_ref[...] = jnp.full(hi_ref.shape, ncol * LANES - 1, I32)
        clo_ref[...] = jnp.full(clo_ref.shape, -1, I32)
        lane = lax.broadcasted_iota(I32, (rb, LANES), 1)

        def tie_step(_, c):
            for r0 in range(0, tq, rb):
                jl, jh = clo_ref[r0:r0 + rb, :], hi_ref[r0:r0 + rb, :]
                mid = jl + ((jh - jl) >> 1)
                active = (jh - jl) > 1
                thr = _key_to_f32(lo_ref[r0:r0 + rb, :])
                cnt = count_rows(r0, lambda blk, j: (blk == thr) & (lane + j * LANES <= mid))
                ok = cnt >= chi_ref[r0:r0 + rb, :]
                hi_ref[r0:r0 + rb, :] = jnp.where(active & ok, mid, jh)
                clo_ref[r0:r0 + rb, :] = jnp.where(active & jnp.logical_not(ok), mid, jl)
            return c

        lax.fori_loop(0, int(math.ceil(math.log2(n_all))) + 1, tie_step, 0)
        keep_all = jnp.logical_not(tie)
        hi_ref[...] = jnp.where(keep_all, n_all, hi_ref[...])

    lane_k = lax.broadcasted_iota(I32, (tq, tk), 1)
    thr_col = _key_to_f32(lo_ref[:, 0:1])
    last_col = hi_ref[:, 0:1]

    def mask_tile(kt, _):
        k0 = pl.multiple_of(kt * tk, tk)
        s = s_ref[:, pl.ds(k0, tk)]
        sel = ((s > thr_col) | ((s == thr_col) & (lane_k + k0 <= last_col))) & (s > -jnp.inf)
        mask_ref[:, pl.ds(k0, tk)] = jnp.where(sel, 1, 0).astype(mask_ref.dtype)
        return 0

    lax.fori_loop(0, nkt, mask_tile, 0)

    def zero_tile(kt, _):
        k0 = pl.multiple_of(kt * tk, tk)
        mask_ref[:, pl.ds(k0, tk)] = jnp.zeros((tq, tk), mask_ref.dtype)
        return 0

    lax.fori_loop(nkt, n_all // tk, zero_tile, 0)


def _dsa_select(qi, wi, ki, topk):
    n = qi.shape[0]
    tq = min(SEL_TQ, n)
    tk = min(SEL_TK, n)
    rb = min(SEL_RB, tq)
    return pl.pallas_call(
        functools.partial(_select_kernel, tq=tq, tk=tk, topk=topk, rb=rb),
        out_shape=jax.ShapeDtypeStruct((n, n), jnp.int8),
        grid=(n // tq,),
        in_specs=[pl.BlockSpec((tq, qi.shape[1]), lambda i: (i, 0)),
                  pl.BlockSpec((tq, LANES), lambda i: (i, 0)),
                  pl.BlockSpec((n, LANES), lambda i: (0, 0))],
        out_specs=pl.BlockSpec((tq, n), lambda i: (i, 0)),
        scratch_shapes=[pltpu.VMEM((tq, n), F32)] + [pltpu.VMEM((tq, LANES), I32)] * 4,
        compiler_params=_cparams(("parallel",)),
        name="dsa_select",
    )(qi, wi, ki)


def _attn_kernel(qb_tab, kt_tab, q_ref, k_ref, v_ref, mask_ref, o_ref, acc_ref, m_ref, l_ref,
                 *, tq, tk):
    step = pl.program_id(0)
    kt = kt_tab[step]
    d = q_ref.shape[1]
    slab = MXU_DIM
    heads_per_slab = slab // HEAD_DIM

    @pl.when(kt == 0)
    def _():
        acc_ref[...] = jnp.zeros_like(acc_ref)
        m_ref[...] = jnp.full(m_ref.shape, NEG_BIG, F32)
        l_ref[...] = jnp.zeros_like(l_ref)

    bias = jnp.where(mask_ref[...].astype(I32) != 0, 0.0, NEG_BIG)
    lane_s = lax.broadcasted_iota(I32, (tq, slab), 1) // HEAD_DIM

    for sb in range(d // slab):
        vs = v_ref[:, sb * slab:(sb + 1) * slab]
        lane_v = lax.broadcasted_iota(I32, vs.shape, 1) // HEAD_DIM
        alpha_l = jnp.zeros((tq, slab), F32)
        pv = jnp.zeros((tq, slab), F32)
        for i in range(heads_per_slab):
            hd = sb * heads_per_slab + i
            blk = hd // 2
            qblk = q_ref[:, blk * LANES:(blk + 1) * LANES]
            qm = jnp.where(_lane_head_mask(qblk.shape, hd % 2), qblk, jnp.zeros_like(qblk))
            s = _dot_nt(qm, k_ref[:, blk * LANES:(blk + 1) * LANES]) + bias
            m_old = m_ref[:, hd:hd + 1]
            m_new = jnp.maximum(m_old, jnp.max(s, axis=1, keepdims=True))
            alpha = jnp.exp(m_old - m_new)
            p = jnp.exp(s - m_new)
            l_ref[:, hd:hd + 1] = l_ref[:, hd:hd + 1] * alpha + jnp.sum(p, axis=1, keepdims=True)
            m_ref[:, hd:hd + 1] = m_new
            vm = jnp.where(lane_v == i, vs, jnp.zeros_like(vs))
            pv = pv + _dot(p.astype(MXU_DTYPE), vm)
            alpha_l = jnp.where(lane_s == i, alpha, alpha_l)
        acc_ref[:, sb * slab:(sb + 1) * slab] = acc_ref[:, sb * slab:(sb + 1) * slab] * alpha_l + pv

    last = kt == ((qb_tab[step] + 1) * tq - 1) // tk

    @pl.when(last)
    def _():
        lane_h = lax.broadcasted_iota(I32, (tq, d), 1) // HEAD_DIM
        l_l = jnp.zeros((tq, d), F32)
        for hd in range(N_HEADS):
            l_l = jnp.where(lane_h == hd, l_ref[:, hd:hd + 1], l_l)
        o_ref[...] = (acc_ref[...] / l_l).astype(o_ref.dtype)


def _dsa_attend(q, k, v, mask):
    n, d = q.shape
    tq = min(ATT_TQ, n)
    tk = min(ATT_TK, n)
    pairs = [(qb, kt) for qb in range(n // tq) for kt in range(((qb + 1) * tq - 1) // tk + 1)]
    qb_tab = jnp.asarray([p[0] for p in pairs], I32)
    kt_tab = jnp.asarray([p[1] for p in pairs], I32)
    grid_spec = pltpu.PrefetchScalarGridSpec(
        num_scalar_prefetch=2,
        grid=(len(pairs),),
        in_specs=[
            pl.BlockSpec((tq, d), lambda i, qt, kt: (qt[i], 0)),
            pl.BlockSpec((tk, d), lambda i, qt, kt: (kt[i], 0)),
            pl.BlockSpec((tk, d), lambda i, qt, kt: (kt[i], 0)),
            pl.BlockSpec((tq, tk), lambda i, qt, kt: (qt[i], kt[i])),
        ],
        out_specs=pl.BlockSpec((tq, d), lambda i, qt, kt: (qt[i], 0)),
        scratch_shapes=[pltpu.VMEM((tq, d), F32), pltpu.VMEM((tq, LANES), F32),
                        pltpu.VMEM((tq, LANES), F32)],
    )
    return pl.pallas_call(
        functools.partial(_attn_kernel, tq=tq, tk=tk),
        out_shape=jax.ShapeDtypeStruct((n, d), MXU_DTYPE),
        grid_spec=grid_spec,
        compiler_params=_cparams(("arbitrary",)),
        name="dsa_attend",
    )(qb_tab, kt_tab, q, k, v, mask)


def _out_kernel(x_ref, a_ref, w_ref, o_ref):
    o_ref[...] = x_ref[...] + _dot(a_ref[...], w_ref[...])


def _out_residual(x, a, w):
    n, d = x.shape
    tm = min(ROW_TM, n)
    return pl.pallas_call(
        _out_kernel,
        out_shape=jax.ShapeDtypeStruct((n, d), F32),
        grid=(n // tm,),
        in_specs=[pl.BlockSpec((tm, d), lambda i: (i, 0)),
                  pl.BlockSpec((tm, d), lambda i: (i, 0)),
                  pl.BlockSpec((d, d), lambda i: (0, 0))],
        out_specs=pl.BlockSpec((tm, d), lambda i: (i, 0)),
        compiler_params=_cparams(("parallel",)),
        name="attn_out",
    )(x, a, w)


def _select_t_kernel(qi_ref, wit_ref, ki_ref, mask_ref, s_ref, s16_ref, lo_ref, hi_ref, clo_ref,
                     chi_ref, *, tq, tk, topk, unroll, unroll16):
    qb = pl.program_id(0)
    n_all = mask_ref.shape[0]
    nkt = (qb * tq + tq - 1) // tk + 1
    nrow = nkt * tk
    rows_it = SUBLANES * unroll

    qm = []
    for hh in range(IDX_HEADS):
        blk = qi_ref[:, (hh // 2) * LANES:(hh // 2 + 1) * LANES]
        qm.append(jnp.where(_lane_head_mask(blk.shape, hh % 2), blk, jnp.zeros_like(blk)))
    qm = jnp.concatenate(qm, axis=0)
    wt = wit_ref[...]
    key_in = lax.broadcasted_iota(I32, (tk, tq), 0)
    q_pos = qb * tq + lax.broadcasted_iota(I32, (tk, tq), 1)

    def score_tile(kt, _):
        k0 = pl.multiple_of(kt * tk, tk)
        s = _dot_nt(ki_ref[pl.ds(k0, tk), :], qm)
        acc = wt[0:1, :] * jnp.maximum(s[:, 0:tq], 0.0)
        for hh in range(1, IDX_HEADS):
            acc = acc + wt[hh:hh + 1, :] * jnp.maximum(s[:, hh * tq:(hh + 1) * tq], 0.0)
        sc = jnp.where(key_in + k0 <= q_pos, acc, -jnp.inf)
        s_ref[pl.ds(k0, tk), :] = sc
        hi_bits = lax.bitcast_convert_type(sc, I32) & -65536
        s16_ref[pl.ds(k0, tk), :] = lax.bitcast_convert_type(hi_bits, F32).astype(BF16)
        return 0

    lax.fori_loop(0, nkt, score_tile, 0)

    lo_ref[...] = jnp.full(lo_ref.shape, KEY16_NEG_INF, I32)
    hi_ref[...] = jnp.full(hi_ref.shape, KEY16_POS_INF + 1, I32)
    clo_ref[...] = jnp.full(clo_ref.shape, 1, I32) * nrow
    chi_ref[...] = jnp.zeros(chi_ref.shape, I32)
    rows16 = 2 * SUBLANES * unroll16
    one16 = jnp.ones((), BF16)
    zero16 = jnp.zeros((), BF16)

    def count16(cand):
        def body(i, acc):
            r0 = pl.multiple_of(i * rows16, rows16)
            blk = s16_ref[pl.ds(r0, rows16), :].reshape(unroll16, 2 * SUBLANES, tq)
            ones = jnp.where(blk >= cand[None], one16, zero16)
            part = ones[0]
            for u in range(1, unroll16):
                part = part + ones[u]
            return acc + part.astype(F32)
        acc = lax.fori_loop(0, nrow // rows16, body, jnp.zeros((2 * SUBLANES, tq), F32))
        tot = jnp.sum(acc, axis=0, keepdims=True).astype(I32)
        return jnp.broadcast_to(tot, (SUBLANES, tq))

    def coarse_step(_, carry):
        lo, hi = lo_ref[...], hi_ref[...]
        mid = (lo + hi) >> 1
        active = mid != lo
        bits16 = (mid ^ ((mid >> 31) & 0x7FFF)) & 0xFFFF
        cand = lax.bitcast_convert_type(bits16 << 16, F32)
        cand = jnp.concatenate([cand, cand], axis=0).astype(BF16)
        cnt = count16(cand)
        ge = cnt >= topk
        up = active & ge
        dn = active & jnp.logical_not(ge)
        lo_ref[...] = jnp.where(up, mid, lo)
        hi_ref[...] = jnp.where(dn, mid, hi)
        clo_ref[...] = jnp.where(up, cnt, clo_ref[...])
        chi_ref[...] = jnp.where(dn, cnt, chi_ref[...])
        return carry

    lax.fori_loop(0, 16, coarse_step, 0)

    lo16 = lo_ref[...]
    none_finite = lo16 == KEY16_NEG_INF
    lo_ref[...] = jnp.where(none_finite, KEY_NEG_INF, lo16 << 16)
    hi_ref[...] = jnp.where(none_finite, KEY_NEG_INF + 1, (lo16 + 1) << 16)

    def count(pred):
        def body(i, acc):
            r0 = pl.multiple_of(i * rows_it, rows_it)
            blk = s_ref[pl.ds(r0, rows_it), :].reshape(unroll, SUBLANES, tq)
            return acc + jnp.sum(pred(blk, r0).astype(I32), axis=0)
        acc = lax.fori_loop(0, nrow // rows_it, body, jnp.zeros((SUBLANES, tq), I32))
        return jnp.broadcast_to(jnp.sum(acc, axis=0, keepdims=True), (SUBLANES, tq))

    def bisect_step(carry):
        it, _ = carry
        lo, hi = lo_ref[...], hi_ref[...]
        mid = (lo & hi) + ((lo ^ hi) >> 1)
        active = mid != lo
        cand = _key_to_f32(mid)
        cnt = count(lambda blk, r0: blk >= cand[None])
        ge = cnt >= topk
        up = active & ge
        dn = active & jnp.logical_not(ge)
        hit = active & (cnt == topk)
        lo_ref[...] = jnp.where(up, mid, lo)
        hi_ref[...] = jnp.where(hit, mid + 1, jnp.where(dn, mid, hi))
        clo_ref[...] = jnp.where(up, cnt, clo_ref[...])
        chi_ref[...] = jnp.where(dn, cnt, chi_ref[...])
        return it + 1, jnp.max(active.astype(I32))

    lax.while_loop(lambda c: (c[0] < 40) & (c[1] > 0), bisect_step,
                   (jnp.zeros((), I32), jnp.ones((), I32)))

    tie = (clo_ref[...] > topk) & (lo_ref[...] > KEY_NEG_INF)
    any_tie = jnp.max(tie.astype(I32))

    @pl.when(any_tie == 0)
    def _():
        hi_ref[...] = jnp.full(hi_ref.shape, n_all, I32)

    @pl.when(any_tie > 0)
    def _():
        chi_ref[...] = topk - chi_ref[...]
        hi_ref[...] = jnp.full(hi_ref.shape, 1, I32) * (nrow - 1)
        clo_ref[...] = jnp.full(clo_ref.shape, -1, I32)
        sub = (lax.broadcasted_iota(I32, (unroll, SUBLANES, tq), 0) * SUBLANES
               + lax.broadcasted_iota(I32, (unroll, SUBLANES, tq), 1))
        thr = _key_to_f32(lo_ref[...])

        def tie_step(_, c):
            jl, jh = clo_ref[...], hi_ref[...]
            mid = jl + ((jh - jl) >> 1)
            active = (jh - jl) > 1
            cnt = count(lambda blk, r0: (blk == thr[None]) & (sub + r0 <= mid[None]))
            ok = cnt >= chi_ref[...]
            hi_ref[...] = jnp.where(active & ok, mid, jh)
            clo_ref[...] = jnp.where(active & jnp.logical_not(ok), mid, jl)
            return c

        lax.fori_loop(0, int(math.ceil(math.log2(n_all))) + 1, tie_step, 0)
        hi_ref[...] = jnp.where(tie, hi_ref[...], n_all)

    thr_row = _key_to_f32(lo_ref[0:1, :])
    last_row = hi_ref[0:1, :]

    def mask_tile(kt, _):
        k0 = pl.multiple_of(kt * tk, tk)
        s = s_ref[pl.ds(k0, tk), :]
        sel = ((s > thr_row) | ((s == thr_row) & (key_in + k0 <= last_row))) & (s > -jnp.inf)
        mask_ref[pl.ds(k0, tk), :] = jnp.where(sel, 1, 0).astype(mask_ref.dtype)
        return 0

    lax.fori_loop(0, nkt, mask_tile, 0)

    def zero_tile(kt, _):
        k0 = pl.multiple_of(kt * tk, tk)
        mask_ref[pl.ds(k0, tk), :] = jnp.zeros((tk, tq), mask_ref.dtype)
        return 0

    lax.fori_loop(nkt, n_all // tk, zero_tile, 0)


def _dsa_select_t(qi, wit, ki, topk):
    n = qi.shape[0]
    tq = min(SEL_TQ, n)
    tk = min(SEL_TK, n)
    assert tk % (SUBLANES * SEL_UNROLL) == 0 and tk % (2 * SUBLANES * SEL_UNROLL16) == 0
    return pl.pallas_call(
        functools.partial(_select_t_kernel, tq=tq, tk=tk, topk=topk, unroll=SEL_UNROLL,
                          unroll16=SEL_UNROLL16),
        out_shape=jax.ShapeDtypeStruct((n, n), jnp.int8),
        grid=(n // tq,),
        in_specs=[pl.BlockSpec((tq, qi.shape[1]), lambda i: (i, 0)),
                  pl.BlockSpec((IDX_HEADS, tq), lambda i: (0, i)),
                  pl.BlockSpec((n, LANES), lambda i: (0, 0))],
        out_specs=pl.BlockSpec((n, tq), lambda i: (0, i)),
        scratch_shapes=[pltpu.VMEM((n, tq), F32), pltpu.VMEM((n, tq), BF16)]
        + [pltpu.VMEM((SUBLANES, tq), I32)] * 4,
        compiler_params=_cparams(("parallel",)),
        name="dsa_select",
    )(qi, wit, ki)


I16 = jnp.int16
L16_MIN = -32768
ROW_NEVER = 32767


def _select16_kernel(qi_ref, wit_ref, ki_ref, mask_ref, h16_ref, l16_ref, lo_ref, hi_ref, clo_ref,
                     chi_ref, base_ref, *, tq, tk, topk, unroll16):
    qb = pl.program_id(0)
    n_all = mask_ref.shape[0]
    nkt = (qb * tq + tq - 1) // tk + 1
    nrow = nkt * tk
    rows16 = 2 * SUBLANES * unroll16
    ntrip = nrow // rows16

    qm = []
    for hh in range(IDX_HEADS):
        blk = qi_ref[:, (hh // 2) * LANES:(hh // 2 + 1) * LANES]
        qm.append(jnp.where(_lane_head_mask(blk.shape, hh % 2), blk, jnp.zeros_like(blk)))
    qm = jnp.concatenate(qm, axis=0)
    wt = wit_ref[...]
    causal_slack = (qb * tq + lax.broadcasted_iota(I32, (tk, tq), 1)
                    - lax.broadcasted_iota(I32, (tk, tq), 0))

    def score_tile(kt, _):
        k0 = pl.multiple_of(kt * tk, tk)
        s = _dot_nt(ki_ref[pl.ds(k0, tk), :], qm)
        acc = wt[0:1, :] * jnp.maximum(s[:, 0:tq], 0.0)
        for hh in range(1, IDX_HEADS):
            acc = acc + wt[hh:hh + 1, :] * jnp.maximum(s[:, hh * tq:(hh + 1) * tq], 0.0)
        acc = jnp.where(acc == 0.0, 0.0, acc)
        sc = jnp.where(k0 <= causal_slack, acc, -jnp.inf)
        bits = lax.bitcast_convert_type(sc, I32)
        key = bits ^ ((bits >> 31) & 0x7FFFFFFF)
        h16_ref[pl.ds(k0, tk), :] = (key >> 16).astype(I16)
        l16_ref[pl.ds(k0, tk), :] = (key ^ 0x8000).astype(I16)
        return 0

    lax.fori_loop(0, nkt, score_tile, 0)

    one16 = jnp.ones((), I16)
    zero16 = jnp.zeros((), I16)
    sub16 = (lax.broadcasted_iota(I32, (unroll16, 2 * SUBLANES, tq), 0) * (2 * SUBLANES)
             + lax.broadcasted_iota(I32, (unroll16, 2 * SUBLANES, tq), 1)).astype(I16)

    def pack16(v):
        return jnp.concatenate([v, v], axis=0).astype(I16)

    def count(pred):
        def body(i, acc):
            r0 = pl.multiple_of(i * rows16, rows16)
            hb = h16_ref[pl.ds(r0, rows16), :].reshape(unroll16, 2 * SUBLANES, tq)
            lb = l16_ref[pl.ds(r0, rows16), :].reshape(unroll16, 2 * SUBLANES, tq)
            ones = jnp.where(pred(hb, lb, r0), one16, zero16)
            part = ones[0]
            for u in range(1, unroll16):
                part = part + ones[u]
            return acc + part.astype(I32)
        acc = lax.fori_loop(0, ntrip, body, jnp.zeros((2 * SUBLANES, tq), I32))
        return jnp.broadcast_to(jnp.sum(acc, axis=0, keepdims=True), (SUBLANES, tq))

    def bisect(count_ge, n_steps):
        def step(_, carry):
            lo, hi = lo_ref[...], hi_ref[...]
            mid = (lo + hi) >> 1
            active = mid != lo
            cnt = count_ge(pack16(mid)) + base_ref[...]
            ge = cnt >= topk
            up = active & ge
            dn = active & jnp.logical_not(ge)
            lo_ref[...] = jnp.where(up, mid, lo)
            hi_ref[...] = jnp.where(dn, mid, hi)
            clo_ref[...] = jnp.where(up, cnt, clo_ref[...])
            chi_ref[...] = jnp.where(dn, cnt, chi_ref[...])
            return carry
        lax.fori_loop(0, n_steps, step, 0)

    lo_ref[...] = jnp.full(lo_ref.shape, KEY16_NEG_INF, I32)
    hi_ref[...] = jnp.full(hi_ref.shape, KEY16_POS_INF + 1, I32)
    clo_ref[...] = jnp.full(clo_ref.shape, 1, I32) * nrow
    chi_ref[...] = jnp.zeros(chi_ref.shape, I32)
    base_ref[...] = jnp.zeros(base_ref.shape, I32)
    bisect(lambda c: count(lambda hb, lb, r0: hb >= c[None]), 16)

    t_hi = lo_ref[...]
    none_finite = t_hi == KEY16_NEG_INF
    t_hi16 = pack16(t_hi)

    def bucket_tile(i, _):
        r0 = pl.multiple_of(i * rows16, rows16)
        hb = h16_ref[pl.ds(r0, rows16), :].reshape(unroll16, 2 * SUBLANES, tq)
        lb = l16_ref[pl.ds(r0, rows16), :].reshape(unroll16, 2 * SUBLANES, tq)
        lb = jnp.where(hb == t_hi16[None], lb, jnp.full((), L16_MIN, I16))
        l16_ref[pl.ds(r0, rows16), :] = lb.reshape(rows16, tq)
        return 0

    lax.fori_loop(0, ntrip, bucket_tile, 0)
    base_ref[...] = chi_ref[...]
    lo_ref[...] = jnp.full(lo_ref.shape, L16_MIN, I32)
    hi_ref[...] = jnp.full(hi_ref.shape, -L16_MIN, I32)
    bisect(lambda c: count(lambda hb, lb, r0: lb >= c[None]), 16)
    t_lo = jnp.where(none_finite, -L16_MIN - 1, lo_ref[...])

    tie = (clo_ref[...] > topk) & jnp.logical_not(none_finite)
    any_tie = jnp.max(tie.astype(I32))
    t_lo16 = pack16(t_lo)

    def rank_tile(i, _):
        r0 = pl.multiple_of(i * rows16, rows16)
        hb = h16_ref[pl.ds(r0, rows16), :].reshape(unroll16, 2 * SUBLANES, tq)
        lb = l16_ref[pl.ds(r0, rows16), :].reshape(unroll16, 2 * SUBLANES, tq)
        in_bucket = hb == t_hi16[None]
        rows = sub16 + jnp.full((2 * SUBLANES, tq), r0, I32).astype(I16)[None]
        rank = jnp.where(in_bucket & (lb == t_lo16[None]), rows,
                         jnp.where(in_bucket & (lb > t_lo16[None]),
                                   jnp.full((), -1, I16), jnp.full((), ROW_NEVER, I16)))
        l16_ref[pl.ds(r0, rows16), :] = rank.reshape(rows16, tq)
        return 0

    lax.fori_loop(0, ntrip, rank_tile, 0)

    @pl.when(any_tie == 0)
    def _():
        hi_ref[...] = jnp.full(hi_ref.shape, n_all, I32)

    @pl.when(any_tie > 0)
    def _():
        chi_ref[...] = topk - base_ref[...]
        hi_ref[...] = jnp.full(hi_ref.shape, 1, I32) * (nrow - 1)
        clo_ref[...] = jnp.full(clo_ref.shape, -1, I32)

        def tie_step(_, c):
            jl, jh = clo_ref[...], hi_ref[...]
            mid = jl + ((jh - jl) >> 1)
            active = (jh - jl) > 1
            mid16 = pack16(mid)
            cnt = count(lambda hb, lb, r0: lb <= mid16[None])
            ok = cnt >= chi_ref[...]
            hi_ref[...] = jnp.where(active & ok, mid, jh)
            clo_ref[...] = jnp.where(active & jnp.logical_not(ok), mid, jl)
            return c

        n_halvings = 0
        for j in range(int(math.ceil(math.log2(n_all // tk))) + 1):
            n_halvings = n_halvings + (((nkt - 1) >> j) > 0).astype(I32)
        lax.fori_loop(0, n_halvings + int(math.log2(tk)) + 1, tie_step, 0)
        hi_ref[...] = jnp.where(tie, hi_ref[...], n_all)

    last16 = pack16(hi_ref[...])
    g16 = 2 * SUBLANES

    def mask_tile(kt, _):
        k0 = pl.multiple_of(kt * tk, tk)
        hb = h16_ref[pl.ds(k0, tk), :].reshape(tk // g16, g16, tq)
        rank = l16_ref[pl.ds(k0, tk), :].reshape(tk // g16, g16, tq)
        sel = ((hb > t_hi16[None]) | (rank <= last16[None])) & (hb > KEY16_NEG_INF)
        sel = jnp.where(sel, one16, zero16).reshape(tk, tq)
        mask_ref[pl.ds(k0, tk), :] = sel.astype(mask_ref.dtype)
        return 0

    lax.fori_loop(0, nkt, mask_tile, 0)

    def zero_tile(kt, _):
        k0 = pl.multiple_of(kt * tk, tk)
        mask_ref[pl.ds(k0, tk), :] = jnp.zeros((tk, tq), mask_ref.dtype)
        return 0

    lax.fori_loop(nkt, n_all // tk, zero_tile, 0)


def _dsa_select16(qi, wit, ki, topk):
    n = qi.shape[0]
    tq = min(SEL_TQ, n)
    tk = min(SEL_TK, n)
    assert tk % (2 * SUBLANES * SEL_UNROLL16) == 0 and n < -L16_MIN
    return pl.pallas_call(
        functools.partial(_select16_kernel, tq=tq, tk=tk, topk=topk, unroll16=SEL_UNROLL16),
        out_shape=jax.ShapeDtypeStruct((n, n), jnp.int8),
        grid=(n // tq,),
        in_specs=[pl.BlockSpec((tq, qi.shape[1]), lambda i: (i, 0)),
                  pl.BlockSpec((IDX_HEADS, tq), lambda i: (0, i)),
                  pl.BlockSpec((n, LANES), lambda i: (0, 0))],
        out_specs=pl.BlockSpec((n, tq), lambda i: (0, i)),
        scratch_shapes=[pltpu.VMEM((n, tq), I16), pltpu.VMEM((n, tq), I16)]
        + [pltpu.VMEM((SUBLANES, tq), I32)] * 5,
        compiler_params=_cparams(("parallel",)),
        name="dsa_select",
    )(qi, wit, ki)


def _attn_t_kernel(qb_tab, kt_tab, q_ref, k_ref, vt_ref, mask_ref, o_ref, acc_ref, m_ref,
                   *, tq, tk, qs):
    step = pl.program_id(0)
    kt = kt_tab[step]

    @pl.when(kt == 0)
    def _():
        acc_ref[...] = jnp.zeros_like(acc_ref)
        m_ref[...] = jnp.full(m_ref.shape, NEG_BIG, MXU_DTYPE).astype(F32)

    bias = jnp.where(mask_ref[...].astype(I32) != 0, 0.0, NEG_BIG).astype(MXU_DTYPE)

    def logits(hd):
        blk = hd // 2
        qblk = q_ref[:, blk * LANES:(blk + 1) * LANES]
        qm = jnp.where(_lane_head_mask(qblk.shape, hd % 2), qblk, jnp.zeros_like(qblk))
        return _dot_nt(k_ref[:, blk * LANES:(blk + 1) * LANES], qm).astype(MXU_DTYPE) + bias

    s_next = logits(0)
    for hd in range(N_HEADS):
        r0 = hd * VT_ROWS
        s = s_next
        if hd + 1 < N_HEADS:
            s_next = logits(hd + 1)
        m_old = m_ref[hd:hd + 1, :]
        m_new = jnp.maximum(m_old, jnp.max(s, axis=0, keepdims=True).astype(F32))
        alpha = jnp.exp2(m_old - m_new)
        p = jnp.exp2(s - m_new.astype(MXU_DTYPE))
        m_ref[hd:hd + 1, :] = m_new
        pv = _dot(vt_ref[r0:r0 + VT_ROWS, :], p)
        acc_ref[r0:r0 + VT_ROWS, :] = acc_ref[r0:r0 + VT_ROWS, :] * alpha + pv

    last = kt == ((qb_tab[step] + 1) * tq - 1) // tk

    @pl.when(last)
    def _():
        for hd in range(N_HEADS):
            r0 = hd * VT_ROWS
            o_ref[hd * HEAD_DIM:(hd + 1) * HEAD_DIM, :] = (
                acc_ref[r0:r0 + HEAD_DIM, :] / acc_ref[r0 + HEAD_DIM:r0 + HEAD_DIM + 1, :]
            ).astype(o_ref.dtype)


def _dsa_attend_t(q, k, vt, mask_t):
    n, d = q.shape
    tq = min(ATT_TQ, n)
    tk = min(ATT_TK, n)
    pairs = [(qb, kt) for qb in range(n // tq) for kt in range(((qb + 1) * tq - 1) // tk + 1)]
    qb_tab = jnp.asarray([p[0] for p in pairs], I32)
    kt_tab = jnp.asarray([p[1] for p in pairs], I32)
    grid_spec = pltpu.PrefetchScalarGridSpec(
        num_scalar_prefetch=2,
        grid=(len(pairs),),
        in_specs=[
            pl.BlockSpec((tq, d), lambda i, qt, kt: (qt[i], 0)),
            pl.BlockSpec((tk, d), lambda i, qt, kt: (kt[i], 0)),
            pl.BlockSpec((vt.shape[0], tk), lambda i, qt, kt: (0, kt[i])),
            pl.BlockSpec((tk, tq), lambda i, qt, kt: (kt[i], qt[i])),
        ],
        out_specs=pl.BlockSpec((d, tq), lambda i, qt, kt: (0, qt[i])),
        scratch_shapes=[pltpu.VMEM((vt.shape[0], tq), F32), pltpu.VMEM((N_HEADS, tq), F32)],
    )
    return pl.pallas_call(
        functools.partial(_attn_t_kernel, tq=tq, tk=tk, qs=min(ATT_QS, tq)),
        out_shape=jax.ShapeDtypeStruct((d, n), MXU_DTYPE),
        grid_spec=grid_spec,
        compiler_params=_cparams(("arbitrary",)),
        name="dsa_attend",
    )(qb_tab, kt_tab, q, k, vt, mask_t)


def _out_t_kernel(x_ref, at_ref, w_ref, o_ref):
    o_ref[...] = x_ref[...] + lax.dot_general(
        at_ref[...], w_ref[...], (((0,), (0,)), ((), ())), preferred_element_type=F32)


def _out_residual_t(x, at, w):
    n, d = x.shape
    tm = min(ROW_TM, n)
    return pl.pallas_call(
        _out_t_kernel,
        out_shape=jax.ShapeDtypeStruct((n, d), F32),
        grid=(n // tm,),
        in_specs=[pl.BlockSpec((tm, d), lambda i: (i, 0)),
                  pl.BlockSpec((d, tm), lambda i: (0, i)),
                  pl.BlockSpec((d, d), lambda i: (0, 0))],
        out_specs=pl.BlockSpec((tm, d), lambda i: (i, 0)),
        compiler_params=_cparams(("parallel",)),
        name="attn_out",
    )(x, at, w)


def _rope_lane_tables(length):
    inv_freq = ROPE_THETA ** (-jnp.arange(0, HEAD_DIM, 2, dtype=F32) / HEAD_DIM)
    ang = jnp.arange(length, dtype=F32)[:, None] * inv_freq[None, :]
    lane = np.arange(LANES)
    cos_t = jnp.cos(ang)[:, lane % 32]
    sin_t = jnp.sin(ang)[:, lane % 32] * jnp.asarray(np.where(lane < 64, -1.0, 1.0), F32)
    return cos_t, sin_t


def kernel(x, s5_lambda_re, s5_lambda_im, s5_log_dt, s5_b_re, s5_b_im, s5_c_re, s5_c_im, s5_d, s5_w_glu, dsa_w_in, dsa_q_norm, dsa_k_norm, dsa_w_o, ffn_w_gate_up, ffn_w_down, norm_mix, norm_ffn):
    bsz, length, d = x.shape
    depth = norm_mix.shape[0]
    topk = min(TOPK_MAX, length // 4)
    nchunk = length // S5_CHUNK
    cos_t, sin_t = _rope_lane_tables(length)
    outs = []
    for b in range(bsz):
        xs = x[b].astype(F32)
        for i in range(depth):
            j = i // 2
            wgu = ffn_w_gate_up[i].astype(MXU_DTYPE)
            wd = ffn_w_down[i].astype(MXU_DTYPE)
            if i % 2 == 0:
                tables = _s5_tables(s5_lambda_re[j], s5_lambda_im[j], s5_log_dt[j], s5_b_re[j],
                                    s5_b_im[j], s5_c_re[j], s5_c_im[j])
                h_t = _norm_planes(xs, norm_mix[i])
                g_t = _s5_scan(h_t, tables, s5_d[j])
                xs = _glu_planes(xs, g_t, s5_w_glu[j].astype(MXU_DTYPE))
                xs = _ffn(xs, norm_ffn[i], wgu, wd)
            else:
                q, k, vt, qi, ki, wi = _dsa_project(xs, norm_mix[i], dsa_w_in[j], dsa_q_norm[j],
                                                    dsa_k_norm[j], cos_t, sin_t)
                mask_t = _dsa_select16(qi, wi[:, :IDX_HEADS].T, ki, topk)
                att_t = _dsa_attend_t(q, k, vt, mask_t)
                xs = _out_residual_t(xs, att_t, dsa_w_o[j].astype(MXU_DTYPE))
                xs = _ffn(xs, norm_ffn[i], wgu, wd)
        outs.append(xs)
    return jnp.stack(outs, axis=0).astype(x.dtype)
```

```python
import functools
import math

import jax
import jax.numpy as jnp
import numpy as np
from jax import lax
from jax.experimental import pallas as pl
from jax.experimental.pallas import tpu as pltpu

F32 = jnp.float32
BF16 = jnp.bfloat16
I32 = jnp.int32
MXU_DTYPE = BF16

D_MODEL = 1024
S5_GROUP = 16
S5_STATE = 64
N_HEADS = 16
HEAD_DIM = 64
IDX_HEADS = 8
IDX_DIM = 64
TOPK_MAX = 256
ROPE_THETA = 10000.0
EPS = 1e-6

LANES = 128
SUBLANES = 8
MXU_DIM = 256
VMEM_LIMIT = 56 * 1024 * 1024

S5_CHUNK = 16
S5_SLAB_GROUPS = LANES // S5_GROUP
NEG_BIG = -1e30
LOG2E = math.log2(math.e)
VT_PAD = 16
VT_ROWS = HEAD_DIM + VT_PAD

ROW_TM = 512
NORM_TM = 1024
S5_TC = 512
PROJ_TM = 512
SEL_TQ = 256
SEL_TK = 512
SEL_RB = 64
SEL_UNROLL = 64
SEL_UNROLL16 = 32
ATT_TQ = 512
ATT_TK = 512
ATT_QS = 512


def _cparams(sem, flags=None):
    return pltpu.CompilerParams(dimension_semantics=sem, vmem_limit_bytes=VMEM_LIMIT, flags=flags)


def _rms(x, gain=None):
    y = x * lax.rsqrt(jnp.mean(x * x, axis=-1, keepdims=True) + EPS)
    return y if gain is None else y * gain


def _dot(a, b):
    return jnp.dot(a, b, preferred_element_type=F32)


def _dot_nt(a, b):
    return lax.dot_general(a, b, (((1,), (1,)), ((), ())), preferred_element_type=F32)


def _ffn_kernel(x_ref, g_ref, wgu_ref, wd_ref, o_ref, acc_ref, *, d_ff, fc):
    x = x_ref[...]
    h = _rms(x, g_ref[...]).astype(MXU_DTYPE)
    for c in range(d_ff // fc):
        g = _dot(h, wgu_ref[:, c * fc:(c + 1) * fc])
        u = _dot(h, wgu_ref[:, d_ff + c * fc:d_ff + (c + 1) * fc])
        a = (g * jax.nn.sigmoid(g) * u).astype(MXU_DTYPE)
        d = _dot(a, wd_ref[c * fc:(c + 1) * fc, :])
        if c == 0:
            acc_ref[...] = d
        else:
            acc_ref[...] += d
    o_ref[...] = x + acc_ref[...]


def _ffn(x, gain, wgu, wd, layer):
    n, d = x.shape
    d_ff = wd.shape[1]
    tm = min(ROW_TM, n)
    fc = MXU_DIM
    return pl.pallas_call(
        functools.partial(_ffn_kernel, d_ff=d_ff, fc=fc),
        out_shape=jax.ShapeDtypeStruct((n, d), F32),
        grid=(n // tm,),
        in_specs=[
            pl.BlockSpec((tm, d), lambda i: (i, 0)),
            pl.BlockSpec((1, d), lambda i: (0, 0)),
            pl.BlockSpec((None, d, 2 * d_ff), lambda i: (layer, 0, 0)),
            pl.BlockSpec((None, d_ff, d), lambda i: (layer, 0, 0)),
        ],
        out_specs=pl.BlockSpec((tm, d), lambda i: (i, 0)),
        scratch_shapes=[pltpu.VMEM((tm, d), F32)],
        compiler_params=_cparams(("parallel",)),
        name="ffn",
    )(x, gain.reshape(1, d), wgu, wd)


def _norm_kernel(x_ref, g_ref, o_ref):
    o_ref[...] = _rms(x_ref[...], g_ref[...]).astype(o_ref.dtype)


def _norm(x, gain):
    n, d = x.shape
    tm = min(NORM_TM, n)
    return pl.pallas_call(
        _norm_kernel,
        out_shape=jax.ShapeDtypeStruct((n, d), MXU_DTYPE),
        grid=(n // tm,),
        in_specs=[pl.BlockSpec((tm, d), lambda i: (i, 0)),
                  pl.BlockSpec((1, d), lambda i: (0, 0))],
        out_specs=pl.BlockSpec((tm, d), lambda i: (i, 0)),
        compiler_params=_cparams(("parallel",)),
        name="norm",
    )(x, gain.reshape(1, d))


def _s5_tables(lam_re, lam_im, log_dt, b_re, b_im, c_re, c_im):
    hp = lax.Precision.HIGHEST
    g, p = lam_re.shape
    h = S5_GROUP
    nsl = g // S5_SLAB_GROUPS
    sg = S5_SLAB_GROUPS
    t = S5_CHUNK
    lam_re, lam_im, log_dt = lam_re.astype(F32), lam_im.astype(F32), log_dt.astype(F32)
    b_re, b_im, c_re, c_im = (a.astype(F32) for a in (b_re, b_im, c_re, c_im))
    dt = jnp.exp(log_dt)[:, None]

    def apow(k):
        k = jnp.asarray(k, F32).reshape((-1, 1, 1))
        mag = jnp.exp(lam_re[None] * dt[None] * k)
        ang = lam_im[None] * dt[None] * k
        return mag * jnp.cos(ang), mag * jnp.sin(ang)

    ar, ai = apow([1.0])
    ar, ai = ar[0], ai[0]
    den = lam_re * lam_re + lam_im * lam_im
    nr, ni = ar - 1.0, ai
    qr = (nr * lam_re + ni * lam_im) / den
    qi = (ni * lam_re - nr * lam_im) / den
    bbr = qr[..., None] * b_re - qi[..., None] * b_im
    bbi = qr[..., None] * b_im + qi[..., None] * b_re

    pr, pi = apow(np.arange(t + 1))
    mr = c_re[None] * pr[:, :, None, :] - c_im[None] * pi[:, :, None, :]
    mi = c_re[None] * pi[:, :, None, :] + c_im[None] * pr[:, :, None, :]

    kk = (jnp.einsum('tghp,gpk->tghk', mr[:t], bbr, precision=hp)
          - jnp.einsum('tghp,gpk->tghk', mi[:t], bbi, precision=hp))
    kp = jnp.concatenate([jnp.zeros_like(kk[:1]), kk], axis=0)

    def block_diag(x, row_dims, col_dims):
        r0, r2 = row_dims
        c0, c1 = col_dims
        nr, ncc = r0 * sg * r2, c0 * c1
        x2 = x.reshape(nsl, t // 2, nr, ncc).astype(MXU_DTYPE)
        ci = np.arange(c0 * sg * c1)
        src = (ci // (sg * c1)) * c1 + ci % c1
        spread = jnp.asarray(np.arange(ncc)[:, None] == src[None, :], MXU_DTYPE)
        ri = np.arange(nr)
        keep = ((ri // r2) % sg)[:, None] == ((ci // c1) % sg)[None, :]
        y = jnp.einsum('bzrc,cn->bzrn', x2, spread, preferred_element_type=MXU_DTYPE)
        return jnp.where(keep, y, jnp.zeros((), MXU_DTYPE))

    dl = np.arange(t // 2)[:, None, None]
    sl = np.arange(2)[None, :, None]
    jl = np.arange(2)[None, None, :]
    idx = 2 * dl + jl - sl + 1
    kg = kp[idx]
    kg = kg.reshape(t // 2, 2, 2, nsl, sg, h, h)
    tp = block_diag(kg.transpose(3, 0, 1, 4, 6, 2, 5), (2, h), (2, h))

    prs, pis = pr[t - 1::-1][:t], pi[t - 1::-1][:t]
    er = prs[..., None] * bbr[None] - pis[..., None] * bbi[None]
    ei = prs[..., None] * bbi[None] + pis[..., None] * bbr[None]
    bf = jnp.stack([er, ei], axis=2)
    bf = bf.reshape(t // 2, 2, nsl, sg, 2, p, h)
    bz = block_diag(bf.transpose(2, 0, 1, 3, 6, 4, 5), (2, h), (2, p))

    cf = jnp.stack([mr[1:], -mi[1:]], axis=2)
    cf = cf.reshape(t // 2, 2, nsl, sg, 2, h, p)
    cz = block_diag(cf.transpose(2, 0, 4, 3, 6, 1, 5), (2, p), (2, h))

    def slab_state(re, im):
        k = re.shape[0]
        x = jnp.stack([re, im], axis=1).reshape(k, 2, nsl, sg * p)
        return x.transpose(2, 0, 1, 3).reshape(nsl, k, 2 * sg * p)

    ad = slab_state(*apow([t * 1.0, t * 2.0, t * 4.0]))
    ap8 = slab_state(*apow(t * (np.arange(SUBLANES) + 1.0)))
    return tp, bz, cz, ad, ap8


def _s5_kernel(h_ref, tp_ref, bz_ref, cz_ref, ad_ref, ap8_ref, dsk_ref, o_ref,
               carry_ref, z_ref, xp_ref, *, tc):
    half = z_ref.shape[1] // 2
    npair = S5_CHUNK // 2

    @pl.when(pl.program_id(1) == 0)
    def _():
        carry_ref[...] = jnp.zeros_like(carry_ref)

    u = [jnp.concatenate([h_ref[2 * s], h_ref[2 * s + 1]], axis=1) for s in range(npair)]

    z = _dot(u[0], bz_ref[0])
    for s in range(1, npair):
        z = z + _dot(u[s], bz_ref[s])
    z_ref[...] = z

    row = lax.broadcasted_iota(I32, (SUBLANES, half), 0)
    pr8, pi8 = ap8_ref[:, :half], ap8_ref[:, half:]

    def tile_step(t, carry):
        r0 = pl.multiple_of(t * SUBLANES, SUBLANES)
        zt = z_ref[pl.ds(r0, SUBLANES), :]
        xr, xi = zt[:, :half], zt[:, half:]
        for k, d in enumerate((1, 2, 4)):
            a = ad_ref[k:k + 1, :]
            a_r, a_i = a[:, :half], a[:, half:]
            sr = jnp.where(row >= d, pltpu.roll(xr, d, 0), 0.0)
            si = jnp.where(row >= d, pltpu.roll(xi, d, 0), 0.0)
            xr, xi = xr + a_r * sr - a_i * si, xi + a_r * si + a_i * sr
        cr, ci = carry[:, :half], carry[:, half:]
        xr, xi = xr + pr8 * cr - pi8 * ci, xi + pr8 * ci + pi8 * cr
        xpr = jnp.where(row >= 1, pltpu.roll(xr, 1, 0), cr)
        xpi = jnp.where(row >= 1, pltpu.roll(xi, 1, 0), ci)
        xp_ref[pl.ds(r0, SUBLANES), :] = jnp.concatenate([xpr, xpi], axis=1)
        return jnp.concatenate([xr[SUBLANES - 1:], xi[SUBLANES - 1:]], axis=1)

    carry_ref[...] = lax.fori_loop(0, tc // SUBLANES, tile_step, carry_ref[...])

    xp = xp_ref[...].astype(MXU_DTYPE)
    dsk = dsk_ref[...]
    for i in range(npair):
        y = _dot(xp, cz_ref[i])
        for s in range(i + 1):
            y = y + _dot(u[s], tp_ref[i - s])
        for jl in range(2):
            j = 2 * i + jl
            yj = y[:, jl * LANES:(jl + 1) * LANES] + dsk * h_ref[j].astype(F32)
            o_ref[j] = jax.nn.gelu(yj).astype(o_ref.dtype)


def _s5_scan(h_t, tables, d_skip, layer):
    tp, bz, cz, ad, ap8 = tables
    t, c, d = h_t.shape
    nsl = d // LANES
    tc = min(S5_TC, c)
    st = bz.shape[-1]
    dsk = d_skip.astype(F32).reshape(nsl, 1, LANES)
    return pl.pallas_call(
        functools.partial(_s5_kernel, tc=tc),
        out_shape=jax.ShapeDtypeStruct((t, c, d), MXU_DTYPE),
        grid=(nsl, c // tc),
        in_specs=[
            pl.BlockSpec((t, tc, LANES), lambda b, i: (0, i, b)),
            pl.BlockSpec((None, None) + tp.shape[2:], lambda b, i: (layer, b, 0, 0, 0)),
            pl.BlockSpec((None, None) + bz.shape[2:], lambda b, i: (layer, b, 0, 0, 0)),
            pl.BlockSpec((None, None) + cz.shape[2:], lambda b, i: (layer, b, 0, 0, 0)),
            pl.BlockSpec((None, None) + ad.shape[2:], lambda b, i: (layer, b, 0, 0)),
            pl.BlockSpec((None, None) + ap8.shape[2:], lambda b, i: (layer, b, 0, 0)),
            pl.BlockSpec((None, 1, LANES), lambda b, i: (b, 0, 0)),
        ],
        out_specs=pl.BlockSpec((t, tc, LANES), lambda b, i: (0, i, b)),
        scratch_shapes=[pltpu.VMEM((1, st), F32), pltpu.VMEM((tc, st), F32),
                        pltpu.VMEM((tc, st), F32)],
        compiler_params=_cparams(("arbitrary", "arbitrary")),
        name="s5_scan",
    )(h_t, tp, bz, cz, ad, ap8, dsk)


def _plane_perm(rows):
    nc = rows // S5_CHUNK
    p = np.zeros((rows, rows), np.float32)
    c, s = np.meshgrid(np.arange(nc), np.arange(S5_CHUNK), indexing="ij")
    p[(s * nc + c).ravel(), (c * S5_CHUNK + s).ravel()] = 1.0
    return p


def _norm_planes_kernel(x_ref, g_ref, p_ref, o_ref):
    h = _rms(x_ref[...], g_ref[...]).astype(MXU_DTYPE)
    hp = _dot(p_ref[...], h).astype(o_ref.dtype)
    o_ref[...] = hp.reshape(o_ref.shape)


def _norm_planes(x, gain):
    n, d = x.shape
    tm = min(ROW_TM, n)
    nc = tm // S5_CHUNK
    perm = jnp.asarray(_plane_perm(tm), MXU_DTYPE)
    return pl.pallas_call(
        _norm_planes_kernel,
        out_shape=jax.ShapeDtypeStruct((S5_CHUNK, n // S5_CHUNK, d), MXU_DTYPE),
        grid=(n // tm,),
        in_specs=[pl.BlockSpec((tm, d), lambda i: (i, 0)),
                  pl.BlockSpec((1, d), lambda i: (0, 0)),
                  pl.BlockSpec((tm, tm), lambda i: (0, 0))],
        out_specs=pl.BlockSpec((S5_CHUNK, nc, d), lambda i: (0, i, 0)),
        compiler_params=_cparams(("parallel",)),
        name="norm_planes",
    )(x, gain.reshape(1, d), perm)


def _glu_planes_kernel(x_ref, g_ref, pt_ref, w_ref, o_ref):
    d = x_ref.shape[1]
    gp = g_ref[...].reshape(x_ref.shape)
    g = _dot(pt_ref[...], gp).astype(MXU_DTYPE)
    vg = _dot(g, w_ref[...])
    o_ref[...] = x_ref[...] + vg[:, :d] * jax.nn.sigmoid(vg[:, d:])


def _glu_planes(x, g_t, w):
    n, d = x.shape
    tm = min(ROW_TM, n)
    nc = tm // S5_CHUNK
    perm_t = jnp.asarray(_plane_perm(tm).T, MXU_DTYPE)
    return pl.pallas_call(
        _glu_planes_kernel,
        out_shape=jax.ShapeDtypeStruct((n, d), F32),
        grid=(n // tm,),
        in_specs=[pl.BlockSpec((tm, d), lambda i: (i, 0)),
                  pl.BlockSpec((S5_CHUNK, nc, d), lambda i: (0, i, 0)),
                  pl.BlockSpec((tm, tm), lambda i: (0, 0)),
                  pl.BlockSpec((d, 2 * d), lambda i: (0, 0))],
        out_specs=pl.BlockSpec((tm, d), lambda i: (i, 0)),
        compiler_params=_cparams(("parallel",)),
        name="glu",
    )(x, g_t, perm_t, w)


def _glu_kernel(x_ref, g_ref, w_ref, o_ref):
    d = x_ref.shape[1]
    vg = _dot(g_ref[...], w_ref[...])
    o_ref[...] = x_ref[...] + vg[:, :d] * jax.nn.sigmoid(vg[:, d:])


def _glu_residual(x, g, w):
    n, d = x.shape
    tm = min(ROW_TM, n)
    return pl.pallas_call(
        _glu_kernel,
        out_shape=jax.ShapeDtypeStruct((n, d), F32),
        grid=(n // tm,),
        in_specs=[pl.BlockSpec((tm, d), lambda i: (i, 0)),
                  pl.BlockSpec((tm, d), lambda i: (i, 0)),
                  pl.BlockSpec((d, 2 * d), lambda i: (0, 0))],
        out_specs=pl.BlockSpec((tm, d), lambda i: (i, 0)),
        compiler_params=_cparams(("parallel",)),
        name="glu",
    )(x, g, w)


def _head_perm(n_heads):
    n = np.arange(n_heads * HEAD_DIM)
    pb, r = n // LANES, n % LANES
    half, r2 = r // 64, r % 64
    hl, dp = r2 // 32, r2 % 32
    return (2 * pb + hl) * HEAD_DIM + 32 * half + dp


def _lane_head_mask(shape, hl):
    lane = lax.broadcasted_iota(I32, shape, len(shape) - 1)
    return ((lane % 64) // 32) == hl


def _proj_kernel(x_ref, g_ref, w_ref, wvt_ref, gq_ref, gk_ref, cos_ref, sin_ref, hm_ref,
                 q_ref, k_ref, vt_ref, qi_ref, ki_ref, wi_ref, *, d, dqi, att_scale, w_scale):
    h = _rms(x_ref[...], g_ref[...]).astype(MXU_DTYPE)
    cos, sin = cos_ref[...], sin_ref[...]
    hm = hm_ref[...]

    def rope(t):
        return t * cos + pltpu.roll(t, 64, 1) * sin

    def headnorm_rope(col0, gain_ref, out_ref, scale):
        t_all = _dot(h, w_ref[:, col0:col0 + d])
        for sb in range(d // MXU_DIM):
            c0 = sb * MXU_DIM
            t = t_all[:, c0:c0 + MXU_DIM]
            sq = t * t
            hi = sq.astype(MXU_DTYPE)
            lo = (sq - hi.astype(F32)).astype(MXU_DTYPE)
            ss = _dot(hi, hm) + _dot(lo, hm)
            tn = t * lax.rsqrt(ss * (1.0 / HEAD_DIM) + EPS) * gain_ref[:, c0:c0 + MXU_DIM]
            for b in range(MXU_DIM // LANES):
                r = rope(tn[:, b * LANES:(b + 1) * LANES])
                if scale != 1.0:
                    r = r * scale
                out_ref[:, c0 + b * LANES:c0 + (b + 1) * LANES] = r.astype(out_ref.dtype)

    headnorm_rope(0, gq_ref, q_ref, att_scale)
    headnorm_rope(d, gk_ref, k_ref, 1.0)
    vt = _dot_nt(wvt_ref[...], h)
    row = lax.broadcasted_iota(I32, vt.shape, 0)
    vt_ref[...] = jnp.where(row % VT_ROWS >= HEAD_DIM, 1.0, vt).astype(vt_ref.dtype)
    c0 = 2 * d
    t = _dot(h, w_ref[:, c0:c0 + dqi])
    for b in range(dqi // LANES):
        qi_ref[:, b * LANES:(b + 1) * LANES] = rope(t[:, b * LANES:(b + 1) * LANES]).astype(qi_ref.dtype)
    c0 += dqi
    t = _dot(h, w_ref[:, c0:c0 + LANES])
    ms = jnp.sum(t * t, axis=-1, keepdims=True) * (0.5 / IDX_DIM)
    ki_ref[...] = rope(t * lax.rsqrt(ms + EPS)).astype(ki_ref.dtype)
    c0 += LANES
    wi_ref[...] = _dot(h, w_ref[:, c0:c0 + LANES]) * w_scale


def _dsa_project(x, gain, w_in, q_gain, k_gain, cos_t, sin_t):
    n, d = x.shape
    dqi = IDX_HEADS * IDX_DIM
    pq = _head_perm(N_HEADS)
    pqi = _head_perm(IDX_HEADS)
    wq = w_in[:, 0:d][:, pq]
    wk = w_in[:, d:2 * d][:, pq]
    wvt = w_in[:, 2 * d:3 * d].T.reshape(N_HEADS, HEAD_DIM, d)
    wvt = jnp.pad(wvt, ((0, 0), (0, VT_PAD), (0, 0))).reshape(N_HEADS * VT_ROWS, d).astype(MXU_DTYPE)
    dvt = N_HEADS * VT_ROWS
    wqi = w_in[:, 3 * d:3 * d + dqi][:, pqi]
    lane = np.arange(LANES)
    wki = w_in[:, 3 * d + dqi:3 * d + dqi + IDX_DIM][:, 32 * (lane // 64) + lane % 32]
    wwi = jnp.pad(w_in[:, 3 * d + dqi + IDX_DIM:], ((0, 0), (0, LANES - IDX_HEADS)))
    w_all = jnp.concatenate([wq, wk, wqi, wki, wwi], axis=1).astype(MXU_DTYPE)
    dcol = (pq % HEAD_DIM)
    gq = q_gain.astype(F32)[dcol].reshape(1, d)
    gk = k_gain.astype(F32)[dcol].reshape(1, d)
    l2 = np.arange(MXU_DIM)
    hm = ((l2[:, None] // LANES == l2[None, :] // LANES)
          & ((l2[:, None] % 64) // 32 == (l2[None, :] % 64) // 32))
    hm = jnp.asarray(hm, MXU_DTYPE)
    tm = min(PROJ_TM, n)
    nw = w_all.shape[1]
    outs = pl.pallas_call(
        functools.partial(_proj_kernel, d=d, dqi=dqi, att_scale=HEAD_DIM ** -0.5 * LOG2E,
                          w_scale=(IDX_HEADS ** -0.5) * (IDX_DIM ** -0.5)),
        out_shape=[jax.ShapeDtypeStruct((n, d), MXU_DTYPE)] * 2
        + [jax.ShapeDtypeStruct((dvt, n), MXU_DTYPE),
           jax.ShapeDtypeStruct((n, dqi), MXU_DTYPE),
           jax.ShapeDtypeStruct((n, LANES), MXU_DTYPE),
           jax.ShapeDtypeStruct((n, LANES), F32)],
        grid=(n // tm,),
        in_specs=[
            pl.BlockSpec((tm, d), lambda i: (i, 0)),
            pl.BlockSpec((1, d), lambda i: (0, 0)),
            pl.BlockSpec((d, nw), lambda i: (0, 0)),
            pl.BlockSpec((dvt, d), lambda i: (0, 0)),
            pl.BlockSpec((1, d), lambda i: (0, 0)),
            pl.BlockSpec((1, d), lambda i: (0, 0)),
            pl.BlockSpec((tm, LANES), lambda i: (i, 0)),
            pl.BlockSpec((tm, LANES), lambda i: (i, 0)),
            pl.BlockSpec((MXU_DIM, MXU_DIM), lambda i: (0, 0)),
        ],
        out_specs=[pl.BlockSpec((tm, d), lambda i: (i, 0))] * 2
        + [pl.BlockSpec((dvt, tm), lambda i: (0, i)),
           pl.BlockSpec((tm, dqi), lambda i: (i, 0)),
           pl.BlockSpec((tm, LANES), lambda i: (i, 0)),
           pl.BlockSpec((tm, LANES), lambda i: (i, 0))],
        compiler_params=_cparams(("parallel",)),
        name="dsa_proj",
    )(x, gain.reshape(1, d), w_all, wvt, gq, gk, cos_t, sin_t, hm)
    return outs


KEY_NEG_INF = -2139095041
KEY_POS_INF = 2139095040
KEY16_NEG_INF = -32641
KEY16_POS_INF = 32640


def _key_to_f32(key):
    bits = key ^ ((key >> 31) & 0x7FFFFFFF)
    return lax.bitcast_convert_type(bits, F32)


def _select_kernel(qi_ref, wi_ref, ki_ref, mask_ref, s_ref, lo_ref, hi_ref, clo_ref, chi_ref,
                   *, tq, tk, topk, rb):
    qb = pl.program_id(0)
    n_all = mask_ref.shape[1]
    nkt = (qb * tq) // tk + 1
    ncol = nkt * (tk // LANES)

    qm = []
    for hh in range(IDX_HEADS):
        blk = qi_ref[:, (hh // 2) * LANES:(hh // 2 + 1) * LANES]
        qm.append(jnp.where(_lane_head_mask(blk.shape, hh % 2), blk, jnp.zeros_like(blk)))
    qm = jnp.concatenate(qm, axis=0)
    wv = wi_ref[...]
    row_pos = qb * tq + lax.broadcasted_iota(I32, (tq, tk), 0)
    col_in = lax.broadcasted_iota(I32, (tq, tk), 1)

    def score_tile(kt, _):
        k0 = pl.multiple_of(kt * tk, tk)
        s = _dot_nt(qm, ki_ref[pl.ds(k0, tk), :])
        acc = wv[:, 0:1] * jnp.maximum(s[0:tq], 0.0)
        for hh in range(1, IDX_HEADS):
            acc = acc + wv[:, hh:hh + 1] * jnp.maximum(s[hh * tq:(hh + 1) * tq], 0.0)
        acc = jnp.where(col_in + k0 <= row_pos, acc, -jnp.inf)
        s_ref[:, pl.ds(k0, tk)] = acc
        return 0

    lax.fori_loop(0, nkt, score_tile, 0)

    lo_ref[...] = jnp.full(lo_ref.shape, KEY_NEG_INF, I32)
    hi_ref[...] = jnp.full(hi_ref.shape, KEY_POS_INF, I32)
    clo_ref[...] = jnp.full(clo_ref.shape, 1, I32) * (ncol * LANES)
    chi_ref[...] = jnp.zeros(chi_ref.shape, I32)

    def count_rows(r0, pred):
        def body(j, acc):
            c0 = pl.multiple_of(j * LANES, LANES)
            return acc + pred(s_ref[r0:r0 + rb, pl.ds(c0, LANES)], j).astype(I32)
        acc = lax.fori_loop(0, ncol, body, jnp.zeros((rb, LANES), I32))
        return jnp.broadcast_to(jnp.sum(acc, axis=1, keepdims=True), (rb, LANES))

    def bisect_step(carry):
        it, _ = carry
        pending = jnp.zeros((), I32)
        for r0 in range(0, tq, rb):
            lo, hi = lo_ref[r0:r0 + rb, :], hi_ref[r0:r0 + rb, :]
            mid = (lo & hi) + ((lo ^ hi) >> 1)
            active = mid != lo
            cand = _key_to_f32(mid)
            cnt = count_rows(r0, lambda blk, j: blk >= cand)
            ge = cnt >= topk
            up = active & ge
            dn = active & jnp.logical_not(ge)
            hit = active & (cnt == topk)
            lo_ref[r0:r0 + rb, :] = jnp.where(up, mid, lo)
            hi_ref[r0:r0 + rb, :] = jnp.where(hit, mid + 1, jnp.where(dn, mid, hi))
            clo_ref[r0:r0 + rb, :] = jnp.where(up, cnt, clo_ref[r0:r0 + rb, :])
            chi_ref[r0:r0 + rb, :] = jnp.where(dn, cnt, chi_ref[r0:r0 + rb, :])
            pending = jnp.maximum(pending, jnp.max(active.astype(I32)))
        return it + 1, pending

    lax.while_loop(lambda c: (c[0] < 40) & (c[1] > 0), bisect_step,
                   (jnp.zeros((), I32), jnp.ones((), I32)))

    tie = (clo_ref[...] > topk) & (lo_ref[...] > KEY_NEG_INF)
    any_tie = jnp.max(tie.astype(I32))

    @pl.when(any_tie == 0)
    def _():
        hi_ref[...] = jnp.full(hi_ref.shape, n_all, I32)

    @pl.when(any_tie > 0)
    def _():
        need = topk - chi_ref[...]
        chi_ref[...] = need
        hi_ref[...] = jnp.full(hi_ref.shape, ncol * LANES - 1, I32)
        clo_ref[...] = jnp.full(clo_ref.shape, -1, I32)
        lane = lax.broadcasted_iota(I32, (rb, LANES), 1)

        def tie_step(_, c):
            for r0 in range(0, tq, rb):
                jl, jh = clo_ref[r0:r0 + rb, :], hi_ref[r0:r0 + rb, :]
                mid = jl + ((jh - jl) >> 1)
                active = (jh - jl) > 1
                thr = _key_to_f32(lo_ref[r0:r0 + rb, :])
                cnt = count_rows(r0, lambda blk, j: (blk == thr) & (lane + j * LANES <= mid))
                ok = cnt >= chi_ref[r0:r0 + rb, :]
                hi_ref[r0:r0 + rb, :] = jnp.where(active & ok, mid, jh)
                clo_ref[r0:r0 + rb, :] = jnp.where(active & jnp.logical_not(ok), mid, jl)
            return c

        lax.fori_loop(0, int(math.ceil(math.log2(n_all))) + 1, tie_step, 0)
        keep_all = jnp.logical_not(tie)
        hi_ref[...] = jnp.where(keep_all, n_all, hi_ref[...])

    lane_k = lax.broadcasted_iota(I32, (tq, tk), 1)
    thr_col = _key_to_f32(lo_ref[:, 0:1])
    last_col = hi_ref[:, 0:1]

    def mask_tile(kt, _):
        k0 = pl.multiple_of(kt * tk, tk)
        s = s_ref[:, pl.ds(k0, tk)]
        sel = ((s > thr_col) | ((s == thr_col) & (lane_k + k0 <= last_col))) & (s > -jnp.inf)
        mask_ref[:, pl.ds(k0, tk)] = jnp.where(sel, 1, 0).astype(mask_ref.dtype)
        return 0

    lax.fori_loop(0, nkt, mask_tile, 0)

    def zero_tile(kt, _):
        k0 = pl.multiple_of(kt * tk, tk)
        mask_ref[:, pl.ds(k0, tk)] = jnp.zeros((tq, tk), mask_ref.dtype)
        return 0

    lax.fori_loop(nkt, n_all // tk, zero_tile, 0)


def _dsa_select(qi, wi, ki, topk):
    n = qi.shape[0]
    tq = min(SEL_TQ, n)
    tk = min(SEL_TK, n)
    rb = min(SEL_RB, tq)
    return pl.pallas_call(
        functools.partial(_select_kernel, tq=tq, tk=tk, topk=topk, rb=rb),
        out_shape=jax.ShapeDtypeStruct((n, n), jnp.int8),
        grid=(n // tq,),
        in_specs=[pl.BlockSpec((tq, qi.shape[1]), lambda i: (i, 0)),
                  pl.BlockSpec((tq, LANES), lambda i: (i, 0)),
                  pl.BlockSpec((n, LANES), lambda i: (0, 0))],
        out_specs=pl.BlockSpec((tq, n), lambda i: (i, 0)),
        scratch_shapes=[pltpu.VMEM((tq, n), F32)] + [pltpu.VMEM((tq, LANES), I32)] * 4,
        compiler_params=_cparams(("parallel",)),
        name="dsa_select",
    )(qi, wi, ki)


def _attn_kernel(qb_tab, kt_tab, q_ref, k_ref, v_ref, mask_ref, o_ref, acc_ref, m_ref, l_ref,
                 *, tq, tk):
    step = pl.program_id(0)
    kt = kt_tab[step]
    d = q_ref.shape[1]
    slab = MXU_DIM
    heads_per_slab = slab // HEAD_DIM

    @pl.when(kt == 0)
    def _():
        acc_ref[...] = jnp.zeros_like(acc_ref)
        m_ref[...] = jnp.full(m_ref.shape, NEG_BIG, F32)
        l_ref[...] = jnp.zeros_like(l_ref)

    bias = jnp.where(mask_ref[...].astype(I32) != 0, 0.0, NEG_BIG)
    lane_s = lax.broadcasted_iota(I32, (tq, slab), 1) // HEAD_DIM

    for sb in range(d // slab):
        vs = v_ref[:, sb * slab:(sb + 1) * slab]
        lane_v = lax.broadcasted_iota(I32, vs.shape, 1) // HEAD_DIM
        alpha_l = jnp.zeros((tq, slab), F32)
        pv = jnp.zeros((tq, slab), F32)
        for i in range(heads_per_slab):
            hd = sb * heads_per_slab + i
            blk = hd // 2
            qblk = q_ref[:, blk * LANES:(blk + 1) * LANES]
            qm = jnp.where(_lane_head_mask(qblk.shape, hd % 2), qblk, jnp.zeros_like(qblk))
            s = _dot_nt(qm, k_ref[:, blk * LANES:(blk + 1) * LANES]) + bias
            m_old = m_ref[:, hd:hd + 1]
            m_new = jnp.maximum(m_old, jnp.max(s, axis=1, keepdims=True))
            alpha = jnp.exp(m_old - m_new)
            p = jnp.exp(s - m_new)
            l_ref[:, hd:hd + 1] = l_ref[:, hd:hd + 1] * alpha + jnp.sum(p, axis=1, keepdims=True)
            m_ref[:, hd:hd + 1] = m_new
            vm = jnp.where(lane_v == i, vs, jnp.zeros_like(vs))
            pv = pv + _dot(p.astype(MXU_DTYPE), vm)
            alpha_l = jnp.where(lane_s == i, alpha, alpha_l)
        acc_ref[:, sb * slab:(sb + 1) * slab] = acc_ref[:, sb * slab:(sb + 1) * slab] * alpha_l + pv

    last = kt == ((qb_tab[step] + 1) * tq - 1) // tk

    @pl.when(last)
    def _():
        lane_h = lax.broadcasted_iota(I32, (tq, d), 1) // HEAD_DIM
        l_l = jnp.zeros((tq, d), F32)
        for hd in range(N_HEADS):
            l_l = jnp.where(lane_h == hd, l_ref[:, hd:hd + 1], l_l)
        o_ref[...] = (acc_ref[...] / l_l).astype(o_ref.dtype)


def _dsa_attend(q, k, v, mask):
    n, d = q.shape
    tq = min(ATT_TQ, n)
    tk = min(ATT_TK, n)
    pairs = [(qb, kt) for qb in range(n // tq) for kt in range(((qb + 1) * tq - 1) // tk + 1)]
    qb_tab = jnp.asarray([p[0] for p in pairs], I32)
    kt_tab = jnp.asarray([p[1] for p in pairs], I32)
    grid_spec = pltpu.PrefetchScalarGridSpec(
        num_scalar_prefetch=2,
        grid=(len(pairs),),
        in_specs=[
            pl.BlockSpec((tq, d), lambda i, qt, kt: (qt[i], 0)),
            pl.BlockSpec((tk, d), lambda i, qt, kt: (kt[i], 0)),
            pl.BlockSpec((tk, d), lambda i, qt, kt: (kt[i], 0)),
            pl.BlockSpec((tq, tk), lambda i, qt, kt: (qt[i], kt[i])),
        ],
        out_specs=pl.BlockSpec((tq, d), lambda i, qt, kt: (qt[i], 0)),
        scratch_shapes=[pltpu.VMEM((tq, d), F32), pltpu.VMEM((tq, LANES), F32),
                        pltpu.VMEM((tq, LANES), F32)],
    )
    return pl.pallas_call(
        functools.partial(_attn_kernel, tq=tq, tk=tk),
        out_shape=jax.ShapeDtypeStruct((n, d), MXU_DTYPE),
        grid_spec=grid_spec,
        compiler_params=_cparams(("arbitrary",)),
        name="dsa_attend",
    )(qb_tab, kt_tab, q, k, v, mask)


def _out_kernel(x_ref, a_ref, w_ref, o_ref):
    o_ref[...] = x_ref[...] + _dot(a_ref[...], w_ref[...])


def _out_residual(x, a, w):
    n, d = x.shape
    tm = min(ROW_TM, n)
    return pl.pallas_call(
        _out_kernel,
        out_shape=jax.ShapeDtypeStruct((n, d), F32),
        grid=(n // tm,),
        in_specs=[pl.BlockSpec((tm, d), lambda i: (i, 0)),
                  pl.BlockSpec((tm, d), lambda i: (i, 0)),
                  pl.BlockSpec((d, d), lambda i: (0, 0))],
        out_specs=pl.BlockSpec((tm, d), lambda i: (i, 0)),
        compiler_params=_cparams(("parallel",)),
        name="attn_out",
    )(x, a, w)


def _select_t_kernel(qi_ref, wit_ref, ki_ref, mask_ref, s_ref, s16_ref, lo_ref, hi_ref, clo_ref,
                     chi_ref, *, tq, tk, topk, unroll, unroll16):
    qb = pl.program_id(0)
    n_all = mask_ref.shape[0]
    nkt = (qb * tq + tq - 1) // tk + 1
    nrow = nkt * tk
    rows_it = SUBLANES * unroll

    qm = []
    for hh in range(IDX_HEADS):
        blk = qi_ref[:, (hh // 2) * LANES:(hh // 2 + 1) * LANES]
        qm.append(jnp.where(_lane_head_mask(blk.shape, hh % 2), blk, jnp.zeros_like(blk)))
    qm = jnp.concatenate(qm, axis=0)
    wt = wit_ref[...]
    key_in = lax.broadcasted_iota(I32, (tk, tq), 0)
    q_pos = qb * tq + lax.broadcasted_iota(I32, (tk, tq), 1)

    def score_tile(kt, _):
        k0 = pl.multiple_of(kt * tk, tk)
        s = _dot_nt(ki_ref[pl.ds(k0, tk), :], qm)
        acc = wt[0:1, :] * jnp.maximum(s[:, 0:tq], 0.0)
        for hh in range(1, IDX_HEADS):
            acc = acc + wt[hh:hh + 1, :] * jnp.maximum(s[:, hh * tq:(hh + 1) * tq], 0.0)
        sc = jnp.where(key_in + k0 <= q_pos, acc, -jnp.inf)
        s_ref[pl.ds(k0, tk), :] = sc
        hi_bits = lax.bitcast_convert_type(sc, I32) & -65536
        s16_ref[pl.ds(k0, tk), :] = lax.bitcast_convert_type(hi_bits, F32).astype(BF16)
        return 0

    lax.fori_loop(0, nkt, score_tile, 0)

    lo_ref[...] = jnp.full(lo_ref.shape, KEY16_NEG_INF, I32)
    hi_ref[...] = jnp.full(hi_ref.shape, KEY16_POS_INF + 1, I32)
    clo_ref[...] = jnp.full(clo_ref.shape, 1, I32) * nrow
    chi_ref[...] = jnp.zeros(chi_ref.shape, I32)
    rows16 = 2 * SUBLANES * unroll16
    one16 = jnp.ones((), BF16)
    zero16 = jnp.zeros((), BF16)

    def count16(cand):
        def body(i, acc):
            r0 = pl.multiple_of(i * rows16, rows16)
            blk = s16_ref[pl.ds(r0, rows16), :].reshape(unroll16, 2 * SUBLANES, tq)
            ones = jnp.where(blk >= cand[None], one16, zero16)
            part = ones[0]
            for u in range(1, unroll16):
                part = part + ones[u]
            return acc + part.astype(F32)
        acc = lax.fori_loop(0, nrow // rows16, body, jnp.zeros((2 * SUBLANES, tq), F32))
        tot = jnp.sum(acc, axis=0, keepdims=True).astype(I32)
        return jnp.broadcast_to(tot, (SUBLANES, tq))

    def coarse_step(_, carry):
        lo, hi = lo_ref[...], hi_ref[...]
        mid = (lo + hi) >> 1
        active = mid != lo
        bits16 = (mid ^ ((mid >> 31) & 0x7FFF)) & 0xFFFF
        cand = lax.bitcast_convert_type(bits16 << 16, F32)
        cand = jnp.concatenate([cand, cand], axis=0).astype(BF16)
        cnt = count16(cand)
        ge = cnt >= topk
        up = active & ge
        dn = active & jnp.logical_not(ge)
        lo_ref[...] = jnp.where(up, mid, lo)
        hi_ref[...] = jnp.where(dn, mid, hi)
        clo_ref[...] = jnp.where(up, cnt, clo_ref[...])
        chi_ref[...] = jnp.where(dn, cnt, chi_ref[...])
        return carry

    lax.fori_loop(0, 16, coarse_step, 0)

    lo16 = lo_ref[...]
    none_finite = lo16 == KEY16_NEG_INF
    lo_ref[...] = jnp.where(none_finite, KEY_NEG_INF, lo16 << 16)
    hi_ref[...] = jnp.where(none_finite, KEY_NEG_INF + 1, (lo16 + 1) << 16)

    def count(pred):
        def body(i, acc):
            r0 = pl.multiple_of(i * rows_it, rows_it)
            blk = s_ref[pl.ds(r0, rows_it), :].reshape(unroll, SUBLANES, tq)
            return acc + jnp.sum(pred(blk, r0).astype(I32), axis=0)
        acc = lax.fori_loop(0, nrow // rows_it, body, jnp.zeros((SUBLANES, tq), I32))
        return jnp.broadcast_to(jnp.sum(acc, axis=0, keepdims=True), (SUBLANES, tq))

    def bisect_step(carry):
        it, _ = carry
        lo, hi = lo_ref[...], hi_ref[...]
        mid = (lo & hi) + ((lo ^ hi) >> 1)
        active = mid != lo
        cand = _key_to_f32(mid)
        cnt = count(lambda blk, r0: blk >= cand[None])
        ge = cnt >= topk
        up = active & ge
        dn = active & jnp.logical_not(ge)
        hit = active & (cnt == topk)
        lo_ref[...] = jnp.where(up, mid, lo)
        hi_ref[...] = jnp.where(hit, mid + 1, jnp.where(dn, mid, hi))
        clo_ref[...] = jnp.where(up, cnt, clo_ref[...])
        chi_ref[...] = jnp.where(dn, cnt, chi_ref[...])
        return it + 1, jnp.max(active.astype(I32))

    lax.while_loop(lambda c: (c[0] < 40) & (c[1] > 0), bisect_step,
                   (jnp.zeros((), I32), jnp.ones((), I32)))

    tie = (clo_ref[...] > topk) & (lo_ref[...] > KEY_NEG_INF)
    any_tie = jnp.max(tie.astype(I32))

    @pl.when(any_tie == 0)
    def _():
        hi_ref[...] = jnp.full(hi_ref.shape, n_all, I32)

    @pl.when(any_tie > 0)
    def _():
        chi_ref[...] = topk - chi_ref[...]
        hi_ref[...] = jnp.full(hi_ref.shape, 1, I32) * (nrow - 1)
        clo_ref[...] = jnp.full(clo_ref.shape, -1, I32)
        sub = (lax.broadcasted_iota(I32, (unroll, SUBLANES, tq), 0) * SUBLANES
               + lax.broadcasted_iota(I32, (unroll, SUBLANES, tq), 1))
        thr = _key_to_f32(lo_ref[...])

        def tie_step(_, c):
            jl, jh = clo_ref[...], hi_ref[...]
            mid = jl + ((jh - jl) >> 1)
            active = (jh - jl) > 1
            cnt = count(lambda blk, r0: (blk == thr[None]) & (sub + r0 <= mid[None]))
            ok = cnt >= chi_ref[...]
            hi_ref[...] = jnp.where(active & ok, mid, jh)
            clo_ref[...] = jnp.where(active & jnp.logical_not(ok), mid, jl)
            return c

        lax.fori_loop(0, int(math.ceil(math.log2(n_all))) + 1, tie_step, 0)
        hi_ref[...] = jnp.where(tie, hi_ref[...], n_all)

    thr_row = _key_to_f32(lo_ref[0:1, :])
    last_row = hi_ref[0:1, :]

    def mask_tile(kt, _):
        k0 = pl.multiple_of(kt * tk, tk)
        s = s_ref[pl.ds(k0, tk), :]
        sel = ((s > thr_row) | ((s == thr_row) & (key_in + k0 <= last_row))) & (s > -jnp.inf)
        mask_ref[pl.ds(k0, tk), :] = jnp.where(sel, 1, 0).astype(mask_ref.dtype)
        return 0

    lax.fori_loop(0, nkt, mask_tile, 0)

    def zero_tile(kt, _):
        k0 = pl.multiple_of(kt * tk, tk)
        mask_ref[pl.ds(k0, tk), :] = jnp.zeros((tk, tq), mask_ref.dtype)
        return 0

    lax.fori_loop(nkt, n_all // tk, zero_tile, 0)


def _dsa_select_t(qi, wit, ki, topk):
    n = qi.shape[0]
    tq = min(SEL_TQ, n)
    tk = min(SEL_TK, n)
    assert tk % (SUBLANES * SEL_UNROLL) == 0 and tk % (2 * SUBLANES * SEL_UNROLL16) == 0
    return pl.pallas_call(
        functools.partial(_select_t_kernel, tq=tq, tk=tk, topk=topk, unroll=SEL_UNROLL,
                          unroll16=SEL_UNROLL16),
        out_shape=jax.ShapeDtypeStruct((n, n), jnp.int8),
        grid=(n // tq,),
        in_specs=[pl.BlockSpec((tq, qi.shape[1]), lambda i: (i, 0)),
                  pl.BlockSpec((IDX_HEADS, tq), lambda i: (0, i)),
                  pl.BlockSpec((n, LANES), lambda i: (0, 0))],
        out_specs=pl.BlockSpec((n, tq), lambda i: (0, i)),
        scratch_shapes=[pltpu.VMEM((n, tq), F32), pltpu.VMEM((n, tq), BF16)]
        + [pltpu.VMEM((SUBLANES, tq), I32)] * 4,
        compiler_params=_cparams(("parallel",)),
        name="dsa_select",
    )(qi, wit, ki)


I16 = jnp.int16
L16_MIN = -32768
ROW_NEVER = 32767


def _select16_kernel(qi_ref, wit_ref, ki_ref, mask_ref, h16_ref, l16_ref, lo_ref, hi_ref, clo_ref,
                     chi_ref, base_ref, *, tq, tk, topk, unroll16):
    qb = pl.program_id(0)
    n_all = mask_ref.shape[0]
    nkt = (qb * tq + tq - 1) // tk + 1
    nrow = nkt * tk
    rows16 = 2 * SUBLANES * unroll16
    ntrip = nrow // rows16

    qm = []
    for hh in range(IDX_HEADS):
        blk = qi_ref[:, (hh // 2) * LANES:(hh // 2 + 1) * LANES]
        qm.append(jnp.where(_lane_head_mask(blk.shape, hh % 2), blk, jnp.zeros_like(blk)))
    qm = jnp.concatenate(qm, axis=0)
    wt = wit_ref[...]
    causal_slack = (qb * tq + lax.broadcasted_iota(I32, (tk, tq), 1)
                    - lax.broadcasted_iota(I32, (tk, tq), 0))

    def score_tile(kt, _):
        k0 = pl.multiple_of(kt * tk, tk)
        s = _dot_nt(ki_ref[pl.ds(k0, tk), :], qm)
        acc = wt[0:1, :] * jnp.maximum(s[:, 0:tq], 0.0)
        for hh in range(1, IDX_HEADS):
            acc = acc + wt[hh:hh + 1, :] * jnp.maximum(s[:, hh * tq:(hh + 1) * tq], 0.0)
        acc = jnp.where(acc == 0.0, 0.0, acc)
        sc = jnp.where(k0 <= causal_slack, acc, -jnp.inf)
        bits = lax.bitcast_convert_type(sc, I32)
        key = bits ^ ((bits >> 31) & 0x7FFFFFFF)
        h16_ref[pl.ds(k0, tk), :] = (key >> 16).astype(I16)
        l16_ref[pl.ds(k0, tk), :] = (key ^ 0x8000).astype(I16)
        return 0

    lax.fori_loop(0, nkt, score_tile, 0)

    one16 = jnp.ones((), I16)
    zero16 = jnp.zeros((), I16)
    sub16 = (lax.broadcasted_iota(I32, (unroll16, 2 * SUBLANES, tq), 0) * (2 * SUBLANES)
             + lax.broadcasted_iota(I32, (unroll16, 2 * SUBLANES, tq), 1)).astype(I16)

    def pack16(v):
        return jnp.concatenate([v, v], axis=0).astype(I16)

    def count(pred):
        def body(i, acc):
            r0 = pl.multiple_of(i * rows16, rows16)
            hb = h16_ref[pl.ds(r0, rows16), :].reshape(unroll16, 2 * SUBLANES, tq)
            lb = l16_ref[pl.ds(r0, rows16), :].reshape(unroll16, 2 * SUBLANES, tq)
            ones = jnp.where(pred(hb, lb, r0), one16, zero16)
            part = ones[0]
            for u in range(1, unroll16):
                part = part + ones[u]
            return acc + part.astype(I32)
        acc = lax.fori_loop(0, ntrip, body, jnp.zeros((2 * SUBLANES, tq), I32))
        return jnp.broadcast_to(jnp.sum(acc, axis=0, keepdims=True), (SUBLANES, tq))

    def bisect(count_ge, n_steps):
        def step(_, carry):
            lo, hi = lo_ref[...], hi_ref[...]
            mid = (lo + hi) >> 1
            active = mid != lo
            cnt = count_ge(pack16(mid)) + base_ref[...]
            ge = cnt >= topk
            up = active & ge
            dn = active & jnp.logical_not(ge)
            lo_ref[...] = jnp.where(up, mid, lo)
            hi_ref[...] = jnp.where(dn, mid, hi)
            clo_ref[...] = jnp.where(up, cnt, clo_ref[...])
            chi_ref[...] = jnp.where(dn, cnt, chi_ref[...])
            return carry
        lax.fori_loop(0, n_steps, step, 0)

    lo_ref[...] = jnp.full(lo_ref.shape, KEY16_NEG_INF, I32)
    hi_ref[...] = jnp.full(hi_ref.shape, KEY16_POS_INF + 1, I32)
    clo_ref[...] = jnp.full(clo_ref.shape, 1, I32) * nrow
    chi_ref[...] = jnp.zeros(chi_ref.shape, I32)
    base_ref[...] = jnp.zeros(base_ref.shape, I32)
    bisect(lambda c: count(lambda hb, lb, r0: hb >= c[None]), 16)

    t_hi = lo_ref[...]
    none_finite = t_hi == KEY16_NEG_INF
    t_hi16 = pack16(t_hi)

    def bucket_tile(i, _):
        r0 = pl.multiple_of(i * rows16, rows16)
        hb = h16_ref[pl.ds(r0, rows16), :].reshape(unroll16, 2 * SUBLANES, tq)
        lb = l16_ref[pl.ds(r0, rows16), :].reshape(unroll16, 2 * SUBLANES, tq)
        lb = jnp.where(hb == t_hi16[None], lb, jnp.full((), L16_MIN, I16))
        l16_ref[pl.ds(r0, rows16), :] = lb.reshape(rows16, tq)
        return 0

    lax.fori_loop(0, ntrip, bucket_tile, 0)
    base_ref[...] = chi_ref[...]
    lo_ref[...] = jnp.full(lo_ref.shape, L16_MIN, I32)
    hi_ref[...] = jnp.full(hi_ref.shape, -L16_MIN, I32)
    bisect(lambda c: count(lambda hb, lb, r0: lb >= c[None]), 16)
    t_lo = jnp.where(none_finite, -L16_MIN - 1, lo_ref[...])

    tie = (clo_ref[...] > topk) & jnp.logical_not(none_finite)
    any_tie = jnp.max(tie.astype(I32))
    t_lo16 = pack16(t_lo)

    def rank_tile(i, _):
        r0 = pl.multiple_of(i * rows16, rows16)
        hb = h16_ref[pl.ds(r0, rows16), :].reshape(unroll16, 2 * SUBLANES, tq)
        lb = l16_ref[pl.ds(r0, rows16), :].reshape(unroll16, 2 * SUBLANES, tq)
        in_bucket = hb == t_hi16[None]
        rows = sub16 + jnp.full((2 * SUBLANES, tq), r0, I32).astype(I16)[None]
        rank = jnp.where(in_bucket & (lb == t_lo16[None]), rows,
                         jnp.where(in_bucket & (lb > t_lo16[None]),
                                   jnp.full((), -1, I16), jnp.full((), ROW_NEVER, I16)))
        l16_ref[pl.ds(r0, rows16), :] = rank.reshape(rows16, tq)
        return 0

    lax.fori_loop(0, ntrip, rank_tile, 0)

    @pl.when(any_tie == 0)
    def _():
        hi_ref[...] = jnp.full(hi_ref.shape, n_all, I32)

    @pl.when(any_tie > 0)
    def _():
        chi_ref[...] = topk - base_ref[...]
        hi_ref[...] = jnp.full(hi_ref.shape, 1, I32) * (nrow - 1)
        clo_ref[...] = jnp.full(clo_ref.shape, -1, I32)

        def tie_step(_, c):
            jl, jh = clo_ref[...], hi_ref[...]
            mid = jl + ((jh - jl) >> 1)
            active = (jh - jl) > 1
            mid16 = pack16(mid)
            cnt = count(lambda hb, lb, r0: lb <= mid16[None])
            ok = cnt >= chi_ref[...]
            hi_ref[...] = jnp.where(active & ok, mid, jh)
            clo_ref[...] = jnp.where(active & jnp.logical_not(ok), mid, jl)
            return c

        n_halvings = 0
        for j in range(int(math.ceil(math.log2(n_all // tk))) + 1):
            n_halvings = n_halvings + (((nkt - 1) >> j) > 0).astype(I32)
        lax.fori_loop(0, n_halvings + int(math.log2(tk)) + 1, tie_step, 0)
        hi_ref[...] = jnp.where(tie, hi_ref[...], n_all)

    last16 = pack16(hi_ref[...])
    g16 = 2 * SUBLANES

    def mask_tile(kt, _):
        k0 = pl.multiple_of(kt * tk, tk)
        hb = h16_ref[pl.ds(k0, tk), :].reshape(tk // g16, g16, tq)
        rank = l16_ref[pl.ds(k0, tk), :].reshape(tk // g16, g16, tq)
        sel = ((hb > t_hi16[None]) | (rank <= last16[None])) & (hb > KEY16_NEG_INF)
        sel = jnp.where(sel, one16, zero16).reshape(tk, tq)
        mask_ref[pl.ds(k0, tk), :] = sel.astype(mask_ref.dtype)
        return 0

    lax.fori_loop(0, nkt, mask_tile, 0)

    def zero_tile(kt, _):
        k0 = pl.multiple_of(kt * tk, tk)
        mask_ref[pl.ds(k0, tk), :] = jnp.zeros((tk, tq), mask_ref.dtype)
        return 0

    lax.fori_loop(nkt, n_all // tk, zero_tile, 0)


def _dsa_select16(qi, wit, ki, topk):
    n = qi.shape[0]
    tq = min(SEL_TQ, n)
    tk = min(SEL_TK, n)
    assert tk % (2 * SUBLANES * SEL_UNROLL16) == 0 and n < -L16_MIN
    return pl.pallas_call(
        functools.partial(_select16_kernel, tq=tq, tk=tk, topk=topk, unroll16=SEL_UNROLL16),
        out_shape=jax.ShapeDtypeStruct((n, n), jnp.int8),
        grid=(n // tq,),
        in_specs=[pl.BlockSpec((tq, qi.shape[1]), lambda i: (i, 0)),
                  pl.BlockSpec((IDX_HEADS, tq), lambda i: (0, i)),
                  pl.BlockSpec((n, LANES), lambda i: (0, 0))],
        out_specs=pl.BlockSpec((n, tq), lambda i: (0, i)),
        scratch_shapes=[pltpu.VMEM((n, tq), I16), pltpu.VMEM((n, tq), I16)]
        + [pltpu.VMEM((SUBLANES, tq), I32)] * 5,
        compiler_params=_cparams(("parallel",)),
        name="dsa_select",
    )(qi, wit, ki)


def _attn_t_kernel(qb_tab, kt_tab, q_ref, k_ref, vt_ref, mask_ref, o_ref, acc_ref, m_ref,
                   *, tq, tk, qs):
    step = pl.program_id(0)
    kt = kt_tab[step]

    @pl.when(kt == 0)
    def _():
        acc_ref[...] = jnp.zeros_like(acc_ref)
        m_ref[...] = jnp.full(m_ref.shape, NEG_BIG, MXU_DTYPE).astype(F32)

    bias = jnp.where(mask_ref[...].astype(I32) != 0, 0.0, NEG_BIG).astype(MXU_DTYPE)

    def logits(hd):
        blk = hd // 2
        qblk = q_ref[:, blk * LANES:(blk + 1) * LANES]
        qm = jnp.where(_lane_head_mask(qblk.shape, hd % 2), qblk, jnp.zeros_like(qblk))
        return _dot_nt(k_ref[:, blk * LANES:(blk + 1) * LANES], qm).astype(MXU_DTYPE) + bias

    s_next = logits(0)
    for hd in range(N_HEADS):
        r0 = hd * VT_ROWS
        s = s_next
        if hd + 1 < N_HEADS:
            s_next = logits(hd + 1)
        m_old = m_ref[hd:hd + 1, :]
        m_new = jnp.maximum(m_old, jnp.max(s, axis=0, keepdims=True).astype(F32))
        alpha = jnp.exp2(m_old - m_new)
        p = jnp.exp2(s - m_new.astype(MXU_DTYPE))
        m_ref[hd:hd + 1, :] = m_new
        pv = _dot(vt_ref[r0:r0 + VT_ROWS, :], p)
        acc_ref[r0:r0 + VT_ROWS, :] = acc_ref[r0:r0 + VT_ROWS, :] * alpha + pv

    last = kt == ((qb_tab[step] + 1) * tq - 1) // tk

    @pl.when(last)
    def _():
        for hd in range(N_HEADS):
            r0 = hd * VT_ROWS
            o_ref[hd * HEAD_DIM:(hd + 1) * HEAD_DIM, :] = (
                acc_ref[r0:r0 + HEAD_DIM, :] / acc_ref[r0 + HEAD_DIM:r0 + HEAD_DIM + 1, :]
            ).astype(o_ref.dtype)


def _dsa_attend_t(q, k, vt, mask_t):
    n, d = q.shape
    tq = min(ATT_TQ, n)
    tk = min(ATT_TK, n)
    pairs = [(qb, kt) for qb in range(n // tq) for kt in range(((qb + 1) * tq - 1) // tk + 1)]
    qb_tab = jnp.asarray([p[0] for p in pairs], I32)
    kt_tab = jnp.asarray([p[1] for p in pairs], I32)
    grid_spec = pltpu.PrefetchScalarGridSpec(
        num_scalar_prefetch=2,
        grid=(len(pairs),),
        in_specs=[
            pl.BlockSpec((tq, d), lambda i, qt, kt: (qt[i], 0)),
            pl.BlockSpec((tk, d), lambda i, qt, kt: (kt[i], 0)),
            pl.BlockSpec((vt.shape[0], tk), lambda i, qt, kt: (0, kt[i])),
            pl.BlockSpec((tk, tq), lambda i, qt, kt: (kt[i], qt[i])),
        ],
        out_specs=pl.BlockSpec((d, tq), lambda i, qt, kt: (0, qt[i])),
        scratch_shapes=[pltpu.VMEM((vt.shape[0], tq), F32), pltpu.VMEM((N_HEADS, tq), F32)],
    )
    return pl.pallas_call(
        functools.partial(_attn_t_kernel, tq=tq, tk=tk, qs=min(ATT_QS, tq)),
        out_shape=jax.ShapeDtypeStruct((d, n), MXU_DTYPE),
        grid_spec=grid_spec,
        compiler_params=_cparams(("arbitrary",)),
        name="dsa_attend",
    )(qb_tab, kt_tab, q, k, vt, mask_t)


def _out_t_kernel(x_ref, at_ref, w_ref, o_ref):
    o_ref[...] = x_ref[...] + lax.dot_general(
        at_ref[...], w_ref[...], (((0,), (0,)), ((), ())), preferred_element_type=F32)


def _out_residual_t(x, at, w):
    n, d = x.shape
    tm = min(ROW_TM, n)
    return pl.pallas_call(
        _out_t_kernel,
        out_shape=jax.ShapeDtypeStruct((n, d), F32),
        grid=(n // tm,),
        in_specs=[pl.BlockSpec((tm, d), lambda i: (i, 0)),
                  pl.BlockSpec((d, tm), lambda i: (0, i)),
                  pl.BlockSpec((d, d), lambda i: (0, 0))],
        out_specs=pl.BlockSpec((tm, d), lambda i: (i, 0)),
        compiler_params=_cparams(("parallel",)),
        name="attn_out",
    )(x, at, w)


def _rope_lane_tables(length):
    inv_freq = ROPE_THETA ** (-jnp.arange(0, HEAD_DIM, 2, dtype=F32) / HEAD_DIM)
    ang = jnp.arange(length, dtype=F32)[:, None] * inv_freq[None, :]
    lane = np.arange(LANES)
    cos_t = jnp.cos(ang)[:, lane % 32]
    sin_t = jnp.sin(ang)[:, lane % 32] * jnp.asarray(np.where(lane < 64, -1.0, 1.0), F32)
    return cos_t, sin_t


def kernel(x, s5_lambda_re, s5_lambda_im, s5_log_dt, s5_b_re, s5_b_im, s5_c_re, s5_c_im, s5_d, s5_w_glu, dsa_w_in, dsa_q_norm, dsa_k_norm, dsa_w_o, ffn_w_gate_up, ffn_w_down, norm_mix, norm_ffn):
    bsz, length, d = x.shape
    depth = norm_mix.shape[0]
    topk = min(TOPK_MAX, length // 4)
    nchunk = length // S5_CHUNK
    cos_t, sin_t = _rope_lane_tables(length)
    outs = []
    wgu = ffn_w_gate_up.astype(MXU_DTYPE)
    wd = ffn_w_down.astype(MXU_DTYPE)
    tables = jax.vmap(_s5_tables)(s5_lambda_re, s5_lambda_im, s5_log_dt, s5_b_re, s5_b_im,
                                  s5_c_re, s5_c_im)
    for b in range(bsz):
        xs = x[b].astype(F32)
        for i in range(depth):
            j = i // 2
            if i % 2 == 0:
                h_t = _norm_planes(xs, norm_mix[i])
                g_t = _s5_scan(h_t, tables, s5_d[j], j)
                xs = _glu_planes(xs, g_t, s5_w_glu[j].astype(MXU_DTYPE))
                xs = _ffn(xs, norm_ffn[i], wgu, wd, i)
            else:
                q, k, vt, qi, ki, wi = _dsa_project(xs, norm_mix[i], dsa_w_in[j], dsa_q_norm[j],
                                                    dsa_k_norm[j], cos_t, sin_t)
                mask_t = _dsa_select16(qi, wi[:, :IDX_HEADS].T, ki, topk)
                att_t = _dsa_attend_t(q, k, vt, mask_t)
                xs = _out_residual_t(xs, att_t, dsa_w_o[j].astype(MXU_DTYPE))
                xs = _ffn(xs, norm_ffn[i], wgu, wd, i)
        outs.append(xs)
    return jnp.stack(outs, axis=0).astype(x.dtype)
```

```python
import functools
import math

import jax
import jax.numpy as jnp
import numpy as np
from jax import lax
from jax.experimental import pallas as pl
from jax.experimental.pallas import tpu as pltpu

F32 = jnp.float32
BF16 = jnp.bfloat16
I32 = jnp.int32
MXU_DTYPE = BF16

D_MODEL = 1024
S5_GROUP = 16
S5_STATE = 64
N_HEADS = 16
HEAD_DIM = 64
IDX_HEADS = 8
IDX_DIM = 64
TOPK_MAX = 256
ROPE_THETA = 10000.0
EPS = 1e-6

LANES = 128
SUBLANES = 8
MXU_DIM = 256
VMEM_LIMIT = 56 * 1024 * 1024

S5_CHUNK = 16
S5_SLAB_GROUPS = LANES // S5_GROUP
NEG_BIG = -1e30
LOG2E = math.log2(math.e)
VT_PAD = 16
VT_ROWS = HEAD_DIM + VT_PAD

ROW_TM = 512
NORM_TM = 1024
S5_TC = 512
PROJ_TM = 512
SEL_TQ = 256
SEL_TK = 512
SEL_RB = 64
SEL_UNROLL = 64
SEL_UNROLL16 = 32
ATT_TQ = 512
ATT_TK = 512
ATT_QS = 512


def _cparams(sem, flags=None):
    return pltpu.CompilerParams(dimension_semantics=sem, vmem_limit_bytes=VMEM_LIMIT, flags=flags)


def _rms(x, gain=None):
    y = x * lax.rsqrt(jnp.mean(x * x, axis=-1, keepdims=True) + EPS)
    return y if gain is None else y * gain


def _dot(a, b):
    return jnp.dot(a, b, preferred_element_type=F32)


def _dot_nt(a, b):
    return lax.dot_general(a, b, (((1,), (1,)), ((), ())), preferred_element_type=F32)


def _ffn_kernel(x_ref, g_ref, wgu_ref, wd_ref, o_ref, acc_ref, *, d_ff, fc):
    x = x_ref[...]
    h = _rms(x, g_ref[...]).astype(MXU_DTYPE)
    for c in range(d_ff // fc):
        g = _dot(h, wgu_ref[:, c * fc:(c + 1) * fc])
        u = _dot(h, wgu_ref[:, d_ff + c * fc:d_ff + (c + 1) * fc])
        a = (g * jax.nn.sigmoid(g) * u).astype(MXU_DTYPE)
        d = _dot(a, wd_ref[c * fc:(c + 1) * fc, :])
        if c == 0:
            acc_ref[...] = d
        else:
            acc_ref[...] += d
    o_ref[...] = x + acc_ref[...]


def _ffn(x, gain, wgu, wd, layer):
    n, d = x.shape
    d_ff = wd.shape[1]
    tm = min(ROW_TM, n)
    fc = MXU_DIM
    return pl.pallas_call(
        functools.partial(_ffn_kernel, d_ff=d_ff, fc=fc),
        out_shape=jax.ShapeDtypeStruct((n, d), F32),
        grid=(n // tm,),
        in_specs=[
            pl.BlockSpec((tm, d), lambda i: (i, 0)),
            pl.BlockSpec((1, d), lambda i: (0, 0)),
            pl.BlockSpec((None, d, 2 * d_ff), lambda i: (layer, 0, 0)),
            pl.BlockSpec((None, d_ff, d), lambda i: (layer, 0, 0)),
        ],
        out_specs=pl.BlockSpec((tm, d), lambda i: (i, 0)),
        scratch_shapes=[pltpu.VMEM((tm, d), F32)],
        compiler_params=_cparams(("parallel",)),
        name="ffn",
    )(x, gain.reshape(1, d), wgu, wd)


def _norm_kernel(x_ref, g_ref, o_ref):
    o_ref[...] = _rms(x_ref[...], g_ref[...]).astype(o_ref.dtype)


def _norm(x, gain):
    n, d = x.shape
    tm = min(NORM_TM, n)
    return pl.pallas_call(
        _norm_kernel,
        out_shape=jax.ShapeDtypeStruct((n, d), MXU_DTYPE),
        grid=(n // tm,),
        in_specs=[pl.BlockSpec((tm, d), lambda i: (i, 0)),
                  pl.BlockSpec((1, d), lambda i: (0, 0))],
        out_specs=pl.BlockSpec((tm, d), lambda i: (i, 0)),
        compiler_params=_cparams(("parallel",)),
        name="norm",
    )(x, gain.reshape(1, d))


def _s5_tables(lam_re, lam_im, log_dt, b_re, b_im, c_re, c_im):
    hp = lax.Precision.HIGHEST
    g, p = lam_re.shape
    h = S5_GROUP
    nsl = g // S5_SLAB_GROUPS
    sg = S5_SLAB_GROUPS
    t = S5_CHUNK
    lam_re, lam_im, log_dt = lam_re.astype(F32), lam_im.astype(F32), log_dt.astype(F32)
    b_re, b_im, c_re, c_im = (a.astype(F32) for a in (b_re, b_im, c_re, c_im))
    dt = jnp.exp(log_dt)[:, None]

    def apow(k):
        k = jnp.asarray(k, F32).reshape((-1, 1, 1))
        mag = jnp.exp(lam_re[None] * dt[None] * k)
        ang = lam_im[None] * dt[None] * k
        return mag * jnp.cos(ang), mag * jnp.sin(ang)

    ar, ai = apow([1.0])
    ar, ai = ar[0], ai[0]
    den = lam_re * lam_re + lam_im * lam_im
    nr, ni = ar - 1.0, ai
    qr = (nr * lam_re + ni * lam_im) / den
    qi = (ni * lam_re - nr * lam_im) / den
    bbr = qr[..., None] * b_re - qi[..., None] * b_im
    bbi = qr[..., None] * b_im + qi[..., None] * b_re

    pr, pi = apow(np.arange(t + 1))
    mr = c_re[None] * pr[:, :, None, :] - c_im[None] * pi[:, :, None, :]
    mi = c_re[None] * pi[:, :, None, :] + c_im[None] * pr[:, :, None, :]

    kk = (jnp.einsum('tghp,gpk->tghk', mr[:t], bbr, precision=hp)
          - jnp.einsum('tghp,gpk->tghk', mi[:t], bbi, precision=hp))
    kp = jnp.concatenate([jnp.zeros_like(kk[:1]), kk], axis=0)

    def block_diag(x, row_dims, col_dims):
        r0, r2 = row_dims
        c0, c1 = col_dims
        return x.reshape(nsl, t // 2, r0 * sg * r2, c0 * c1).astype(MXU_DTYPE)

    dl = np.arange(t // 2)[:, None, None]
    sl = np.arange(2)[None, :, None]
    jl = np.arange(2)[None, None, :]
    idx = 2 * dl + jl - sl + 1
    kg = kp[idx]
    kg = kg.reshape(t // 2, 2, 2, nsl, sg, h, h)
    tp = block_diag(kg.transpose(3, 0, 1, 4, 6, 2, 5), (2, h), (2, h))

    prs, pis = pr[t - 1::-1][:t], pi[t - 1::-1][:t]
    er = prs[..., None] * bbr[None] - pis[..., None] * bbi[None]
    ei = prs[..., None] * bbi[None] + pis[..., None] * bbr[None]
    bf = jnp.stack([er, ei], axis=2)
    bf = bf.reshape(t // 2, 2, nsl, sg, 2, p, h)
    bz = block_diag(bf.transpose(2, 0, 1, 3, 6, 4, 5), (2, h), (2, p))

    cf = jnp.stack([mr[1:], -mi[1:]], axis=2)
    cf = cf.reshape(t // 2, 2, nsl, sg, 2, h, p)
    cz = block_diag(cf.transpose(2, 0, 4, 3, 6, 1, 5), (2, p), (2, h))

    def slab_state(re, im):
        k = re.shape[0]
        x = jnp.stack([re, im], axis=1).reshape(k, 2, nsl, sg * p)
        return x.transpose(2, 0, 1, 3).reshape(nsl, k, 2 * sg * p)

    ad = slab_state(*apow([t * 1.0, t * 2.0, t * 4.0]))
    ap8 = slab_state(*apow(t * (np.arange(SUBLANES) + 1.0)))
    return tp, bz, cz, ad, ap8


def _s5_expander(row_dims, col_dims):
    sg = S5_SLAB_GROUPS
    r0, r2 = row_dims
    c0, c1 = col_dims
    ci = np.arange(c0 * sg * c1)
    src = (ci // (sg * c1)) * c1 + ci % c1
    spread = np.arange(c0 * c1)[:, None] == src[None, :]
    ri = np.arange(r0 * sg * r2)
    keep = ((ri // r2) % sg)[:, None] == ((ci // c1) % sg)[None, :]
    return jnp.asarray(spread, MXU_DTYPE), jnp.asarray(keep, MXU_DTYPE)


def _s5_kernel(h_ref, tpc_ref, bzc_ref, czc_ref, sph_ref, spp_ref, ktp_ref, kbz_ref, kcz_ref,
               ad_ref, ap8_ref, dsk_ref, o_ref,
               carry_ref, z_ref, xp_ref, tp_ref, bz_ref, cz_ref, *, tc):
    half = z_ref.shape[1] // 2
    npair = S5_CHUNK // 2

    @pl.when(pl.program_id(1) == 0)
    def _():
        carry_ref[...] = jnp.zeros_like(carry_ref)
        for z in range(npair):
            tp_ref[z] = (_dot(tpc_ref[z], sph_ref[...]) * ktp_ref[...]).astype(tp_ref.dtype)
            bz_ref[z] = (_dot(bzc_ref[z], spp_ref[...]) * kbz_ref[...]).astype(bz_ref.dtype)
            cz_ref[z] = (_dot(czc_ref[z], sph_ref[...]) * kcz_ref[...]).astype(cz_ref.dtype)

    u = [jnp.concatenate([h_ref[2 * s], h_ref[2 * s + 1]], axis=1) for s in range(npair)]

    z = _dot(u[0], bz_ref[0])
    for s in range(1, npair):
        z = z + _dot(u[s], bz_ref[s])
    z_ref[...] = z

    row = lax.broadcasted_iota(I32, (SUBLANES, half), 0)
    pr8, pi8 = ap8_ref[:, :half], ap8_ref[:, half:]

    def tile_step(t, carry):
        r0 = pl.multiple_of(t * SUBLANES, SUBLANES)
        zt = z_ref[pl.ds(r0, SUBLANES), :]
        xr, xi = zt[:, :half], zt[:, half:]
        for k, d in enumerate((1, 2, 4)):
            a = ad_ref[k:k + 1, :]
            a_r, a_i = a[:, :half], a[:, half:]
            sr = jnp.where(row >= d, pltpu.roll(xr, d, 0), 0.0)
            si = jnp.where(row >= d, pltpu.roll(xi, d, 0), 0.0)
            xr, xi = xr + a_r * sr - a_i * si, xi + a_r * si + a_i * sr
        cr, ci = carry[:, :half], carry[:, half:]
        xr, xi = xr + pr8 * cr - pi8 * ci, xi + pr8 * ci + pi8 * cr
        xpr = jnp.where(row >= 1, pltpu.roll(xr, 1, 0), cr)
        xpi = jnp.where(row >= 1, pltpu.roll(xi, 1, 0), ci)
        xp_ref[pl.ds(r0, SUBLANES), :] = jnp.concatenate([xpr, xpi], axis=1)
        return jnp.concatenate([xr[SUBLANES - 1:], xi[SUBLANES - 1:]], axis=1)

    carry_ref[...] = lax.fori_loop(0, tc // SUBLANES, tile_step, carry_ref[...])

    xp = xp_ref[...].astype(MXU_DTYPE)
    dsk = dsk_ref[...]
    for i in range(npair):
        y = _dot(xp, cz_ref[i])
        for s in range(i + 1):
            y = y + _dot(u[s], tp_ref[i - s])
        for jl in range(2):
            j = 2 * i + jl
            yj = y[:, jl * LANES:(jl + 1) * LANES] + dsk * h_ref[j].astype(F32)
            o_ref[j] = jax.nn.gelu(yj).astype(o_ref.dtype)


def _s5_scan(h_t, tables, d_skip, layer):
    tpc, bzc, czc, ad, ap8 = tables
    t, c, d = h_t.shape
    nsl = d // LANES
    tc = min(S5_TC, c)
    npair = S5_CHUNK // 2
    sg, hh, pp = S5_SLAB_GROUPS, S5_GROUP, S5_STATE
    sp_h, keep_tp = _s5_expander((2, hh), (2, hh))
    sp_p, keep_bz = _s5_expander((2, hh), (2, pp))
    _, keep_cz = _s5_expander((2, pp), (2, hh))
    nu, st = 2 * sg * hh, 2 * sg * pp
    dsk = d_skip.astype(F32).reshape(nsl, 1, LANES)
    const = lambda a: pl.BlockSpec(a.shape, lambda b, i: (0,) * a.ndim)
    per_slab = lambda a: pl.BlockSpec((None, None) + a.shape[2:],
                                      lambda b, i: (layer, b) + (0,) * (a.ndim - 2))
    return pl.pallas_call(
        functools.partial(_s5_kernel, tc=tc),
        out_shape=jax.ShapeDtypeStruct((t, c, d), MXU_DTYPE),
        grid=(nsl, c // tc),
        in_specs=[
            pl.BlockSpec((t, tc, LANES), lambda b, i: (0, i, b)),
            per_slab(tpc), per_slab(bzc), per_slab(czc),
            const(sp_h), const(sp_p), const(keep_tp), const(keep_bz), const(keep_cz),
            per_slab(ad), per_slab(ap8),
            pl.BlockSpec((None, 1, LANES), lambda b, i: (b, 0, 0)),
        ],
        out_specs=pl.BlockSpec((t, tc, LANES), lambda b, i: (0, i, b)),
        scratch_shapes=[pltpu.VMEM((1, st), F32), pltpu.VMEM((tc, st), F32),
                        pltpu.VMEM((tc, st), F32),
                        pltpu.VMEM((npair, nu, nu), MXU_DTYPE),
                        pltpu.VMEM((npair, nu, st), MXU_DTYPE),
                        pltpu.VMEM((npair, st, nu), MXU_DTYPE)],
        compiler_params=_cparams(("arbitrary", "arbitrary")),
        name="s5_scan",
    )(h_t, tpc, bzc, czc, sp_h, sp_p, keep_tp, keep_bz, keep_cz, ad, ap8, dsk)


def _plane_perm(rows):
    nc = rows // S5_CHUNK
    p = np.zeros((rows, rows), np.float32)
    c, s = np.meshgrid(np.arange(nc), np.arange(S5_CHUNK), indexing="ij")
    p[(s * nc + c).ravel(), (c * S5_CHUNK + s).ravel()] = 1.0
    return p


def _norm_planes_kernel(x_ref, g_ref, p_ref, o_ref):
    h = _rms(x_ref[...], g_ref[...]).astype(MXU_DTYPE)
    hp = _dot(p_ref[...], h).astype(o_ref.dtype)
    o_ref[...] = hp.reshape(o_ref.shape)


def _norm_planes(x, gain):
    n, d = x.shape
    tm = min(ROW_TM, n)
    nc = tm // S5_CHUNK
    perm = jnp.asarray(_plane_perm(tm), MXU_DTYPE)
    return pl.pallas_call(
        _norm_planes_kernel,
        out_shape=jax.ShapeDtypeStruct((S5_CHUNK, n // S5_CHUNK, d), MXU_DTYPE),
        grid=(n // tm,),
        in_specs=[pl.BlockSpec((tm, d), lambda i: (i, 0)),
                  pl.BlockSpec((1, d), lambda i: (0, 0)),
                  pl.BlockSpec((tm, tm), lambda i: (0, 0))],
        out_specs=pl.BlockSpec((S5_CHUNK, nc, d), lambda i: (0, i, 0)),
        compiler_params=_cparams(("parallel",)),
        name="norm_planes",
    )(x, gain.reshape(1, d), perm)


def _glu_planes_kernel(x_ref, g_ref, pt_ref, w_ref, o_ref):
    d = x_ref.shape[1]
    gp = g_ref[...].reshape(x_ref.shape)
    g = _dot(pt_ref[...], gp).astype(MXU_DTYPE)
    vg = _dot(g, w_ref[...])
    o_ref[...] = x_ref[...] + vg[:, :d] * jax.nn.sigmoid(vg[:, d:])


def _glu_planes(x, g_t, w):
    n, d = x.shape
    tm = min(ROW_TM, n)
    nc = tm // S5_CHUNK
    perm_t = jnp.asarray(_plane_perm(tm).T, MXU_DTYPE)
    return pl.pallas_call(
        _glu_planes_kernel,
        out_shape=jax.ShapeDtypeStruct((n, d), F32),
        grid=(n // tm,),
        in_specs=[pl.BlockSpec((tm, d), lambda i: (i, 0)),
                  pl.BlockSpec((S5_CHUNK, nc, d), lambda i: (0, i, 0)),
                  pl.BlockSpec((tm, tm), lambda i: (0, 0)),
                  pl.BlockSpec((d, 2 * d), lambda i: (0, 0))],
        out_specs=pl.BlockSpec((tm, d), lambda i: (i, 0)),
        compiler_params=_cparams(("parallel",)),
        name="glu",
    )(x, g_t, perm_t, w)


def _glu_kernel(x_ref, g_ref, w_ref, o_ref):
    d = x_ref.shape[1]
    vg = _dot(g_ref[...], w_ref[...])
    o_ref[...] = x_ref[...] + vg[:, :d] * jax.nn.sigmoid(vg[:, d:])


def _glu_residual(x, g, w):
    n, d = x.shape
    tm = min(ROW_TM, n)
    return pl.pallas_call(
        _glu_kernel,
        out_shape=jax.ShapeDtypeStruct((n, d), F32),
        grid=(n // tm,),
        in_specs=[pl.BlockSpec((tm, d), lambda i: (i, 0)),
                  pl.BlockSpec((tm, d), lambda i: (i, 0)),
                  pl.BlockSpec((d, 2 * d), lambda i: (0, 0))],
        out_specs=pl.BlockSpec((tm, d), lambda i: (i, 0)),
        compiler_params=_cparams(("parallel",)),
        name="glu",
    )(x, g, w)


def _head_perm(n_heads):
    n = np.arange(n_heads * HEAD_DIM)
    pb, r = n // LANES, n % LANES
    half, r2 = r // 64, r % 64
    hl, dp = r2 // 32, r2 % 32
    return (2 * pb + hl) * HEAD_DIM + 32 * half + dp


def _lane_head_mask(shape, hl):
    lane = lax.broadcasted_iota(I32, shape, len(shape) - 1)
    return ((lane % 64) // 32) == hl


def _proj_kernel(x_ref, g_ref, w_ref, wvt_ref, gq_ref, gk_ref, cos_ref, sin_ref, hm_ref,
                 q_ref, k_ref, vt_ref, qi_ref, ki_ref, wi_ref, *, d, dqi, att_scale, w_scale):
    h = _rms(x_ref[...], g_ref[...]).astype(MXU_DTYPE)
    cos, sin = cos_ref[...], sin_ref[...]
    hm = hm_ref[...]

    def rope(t):
        return t * cos + pltpu.roll(t, 64, 1) * sin

    def headnorm_rope(col0, gain_ref, out_ref, scale):
        t_all = _dot(h, w_ref[:, col0:col0 + d])
        for sb in range(d // MXU_DIM):
            c0 = sb * MXU_DIM
            t = t_all[:, c0:c0 + MXU_DIM]
            sq = t * t
            hi = sq.astype(MXU_DTYPE)
            lo = (sq - hi.astype(F32)).astype(MXU_DTYPE)
            ss = _dot(hi, hm) + _dot(lo, hm)
            tn = t * lax.rsqrt(ss * (1.0 / HEAD_DIM) + EPS) * gain_ref[:, c0:c0 + MXU_DIM]
            for b in range(MXU_DIM // LANES):
                r = rope(tn[:, b * LANES:(b + 1) * LANES])
                if scale != 1.0:
                    r = r * scale
                out_ref[:, c0 + b * LANES:c0 + (b + 1) * LANES] = r.astype(out_ref.dtype)

    headnorm_rope(0, gq_ref, q_ref, att_scale)
    headnorm_rope(d, gk_ref, k_ref, 1.0)
    vt = _dot_nt(wvt_ref[...], h)
    row = lax.broadcasted_iota(I32, vt.shape, 0)
    vt_ref[...] = jnp.where(row % VT_ROWS >= HEAD_DIM, 1.0, vt).astype(vt_ref.dtype)
    c0 = 2 * d
    t = _dot(h, w_ref[:, c0:c0 + dqi])
    for b in range(dqi // LANES):
        qi_ref[:, b * LANES:(b + 1) * LANES] = rope(t[:, b * LANES:(b + 1) * LANES]).astype(qi_ref.dtype)
    c0 += dqi
    t = _dot(h, w_ref[:, c0:c0 + LANES])
    ms = jnp.sum(t * t, axis=-1, keepdims=True) * (0.5 / IDX_DIM)
    ki_ref[...] = rope(t * lax.rsqrt(ms + EPS)).astype(ki_ref.dtype)
    c0 += LANES
    wi_ref[...] = _dot(h, w_ref[:, c0:c0 + LANES]) * w_scale


def _dsa_project(x, gain, w_in, q_gain, k_gain, cos_t, sin_t):
    n, d = x.shape
    dqi = IDX_HEADS * IDX_DIM
    pq = _head_perm(N_HEADS)
    pqi = _head_perm(IDX_HEADS)
    wq = w_in[:, 0:d][:, pq]
    wk = w_in[:, d:2 * d][:, pq]
    wvt = w_in[:, 2 * d:3 * d].T.reshape(N_HEADS, HEAD_DIM, d)
    wvt = jnp.pad(wvt, ((0, 0), (0, VT_PAD), (0, 0))).reshape(N_HEADS * VT_ROWS, d).astype(MXU_DTYPE)
    dvt = N_HEADS * VT_ROWS
    wqi = w_in[:, 3 * d:3 * d + dqi][:, pqi]
    lane = np.arange(LANES)
    wki = w_in[:, 3 * d + dqi:3 * d + dqi + IDX_DIM][:, 32 * (lane // 64) + lane % 32]
    wwi = jnp.pad(w_in[:, 3 * d + dqi + IDX_DIM:], ((0, 0), (0, LANES - IDX_HEADS)))
    w_all = jnp.concatenate([wq, wk, wqi, wki, wwi], axis=1).astype(MXU_DTYPE)
    dcol = (pq % HEAD_DIM)
    gq = q_gain.astype(F32)[dcol].reshape(1, d)
    gk = k_gain.astype(F32)[dcol].reshape(1, d)
    l2 = np.arange(MXU_DIM)
    hm = ((l2[:, None] // LANES == l2[None, :] // LANES)
          & ((l2[:, None] % 64) // 32 == (l2[None, :] % 64) // 32))
    hm = jnp.asarray(hm, MXU_DTYPE)
    tm = min(PROJ_TM, n)
    nw = w_all.shape[1]
    outs = pl.pallas_call(
        functools.partial(_proj_kernel, d=d, dqi=dqi, att_scale=HEAD_DIM ** -0.5 * LOG2E,
                          w_scale=(IDX_HEADS ** -0.5) * (IDX_DIM ** -0.5)),
        out_shape=[jax.ShapeDtypeStruct((n, d), MXU_DTYPE)] * 2
        + [jax.ShapeDtypeStruct((dvt, n), MXU_DTYPE),
           jax.ShapeDtypeStruct((n, dqi), MXU_DTYPE),
           jax.ShapeDtypeStruct((n, LANES), MXU_DTYPE),
           jax.ShapeDtypeStruct((n, LANES), F32)],
        grid=(n // tm,),
        in_specs=[
            pl.BlockSpec((tm, d), lambda i: (i, 0)),
            pl.BlockSpec((1, d), lambda i: (0, 0)),
            pl.BlockSpec((d, nw), lambda i: (0, 0)),
            pl.BlockSpec((dvt, d), lambda i: (0, 0)),
            pl.BlockSpec((1, d), lambda i: (0, 0)),
            pl.BlockSpec((1, d), lambda i: (0, 0)),
            pl.BlockSpec((tm, LANES), lambda i: (i, 0)),
            pl.BlockSpec((tm, LANES), lambda i: (i, 0)),
            pl.BlockSpec((MXU_DIM, MXU_DIM), lambda i: (0, 0)),
        ],
        out_specs=[pl.BlockSpec((tm, d), lambda i: (i, 0))] * 2
        + [pl.BlockSpec((dvt, tm), lambda i: (0, i)),
           pl.BlockSpec((tm, dqi), lambda i: (i, 0)),
           pl.BlockSpec((tm, LANES), lambda i: (i, 0)),
           pl.BlockSpec((tm, LANES), lambda i: (i, 0))],
        compiler_params=_cparams(("parallel",)),
        name="dsa_proj",
    )(x, gain.reshape(1, d), w_all, wvt, gq, gk, cos_t, sin_t, hm)
    return outs


KEY_NEG_INF = -2139095041
KEY_POS_INF = 2139095040
KEY16_NEG_INF = -32641
KEY16_POS_INF = 32640


def _key_to_f32(key):
    bits = key ^ ((key >> 31) & 0x7FFFFFFF)
    return lax.bitcast_convert_type(bits, F32)


def _select_kernel(qi_ref, wi_ref, ki_ref, mask_ref, s_ref, lo_ref, hi_ref, clo_ref, chi_ref,
                   *, tq, tk, topk, rb):
    qb = pl.program_id(0)
    n_all = mask_ref.shape[1]
    nkt = (qb * tq) // tk + 1
    ncol = nkt * (tk // LANES)

    qm = []
    for hh in range(IDX_HEADS):
        blk = qi_ref[:, (hh // 2) * LANES:(hh // 2 + 1) * LANES]
        qm.append(jnp.where(_lane_head_mask(blk.shape, hh % 2), blk, jnp.zeros_like(blk)))
    qm = jnp.concatenate(qm, axis=0)
    wv = wi_ref[...]
    row_pos = qb * tq + lax.broadcasted_iota(I32, (tq, tk), 0)
    col_in = lax.broadcasted_iota(I32, (tq, tk), 1)

    def score_tile(kt, _):
        k0 = pl.multiple_of(kt * tk, tk)
        s = _dot_nt(qm, ki_ref[pl.ds(k0, tk), :])
        acc = wv[:, 0:1] * jnp.maximum(s[0:tq], 0.0)
        for hh in range(1, IDX_HEADS):
            acc = acc + wv[:, hh:hh + 1] * jnp.maximum(s[hh * tq:(hh + 1) * tq], 0.0)
        acc = jnp.where(col_in + k0 <= row_pos, acc, -jnp.inf)
        s_ref[:, pl.ds(k0, tk)] = acc
        return 0

    lax.fori_loop(0, nkt, score_tile, 0)

    lo_ref[...] = jnp.full(lo_ref.shape, KEY_NEG_INF, I32)
    hi_ref[...] = jnp.full(hi_ref.shape, KEY_POS_INF, I32)
    clo_ref[...] = jnp.full(clo_ref.shape, 1, I32) * (ncol * LANES)
    chi_ref[...] = jnp.zeros(chi_ref.shape, I32)

    def count_rows(r0, pred):
        def body(j, acc):
            c0 = pl.multiple_of(j * LANES, LANES)
            return acc + pred(s_ref[r0:r0 + rb, pl.ds(c0, LANES)], j).astype(I32)
        acc = lax.fori_loop(0, ncol, body, jnp.zeros((rb, LANES), I32))
        return jnp.broadcast_to(jnp.sum(acc, axis=1, keepdims=True), (rb, LANES))

    def bisect_step(carry):
        it, _ = carry
        pending = jnp.zeros((), I32)
        for r0 in range(0, tq, rb):
            lo, hi = lo_ref[r0:r0 + rb, :], hi_ref[r0:r0 + rb, :]
            mid = (lo & hi) + ((lo ^ hi) >> 1)
            active = mid != lo
            cand = _key_to_f32(mid)
            cnt = count_rows(r0, lambda blk, j: blk >= cand)
            ge = cnt >= topk
            up = active & ge
            dn = active & jnp.logical_not(ge)
            hit = active & (cnt == topk)
            lo_ref[r0:r0 + rb, :] = jnp.where(up, mid, lo)
            hi_ref[r0:r0 + rb, :] = jnp.where(hit, mid + 1, jnp.where(dn, mid, hi))
            clo_ref[r0:r0 + rb, :] = jnp.where(up, cnt, clo_ref[r0:r0 + rb, :])
            chi_ref[r0:r0 + rb, :] = jnp.where(dn, cnt, chi_ref[r0:r0 + rb, :])
            pending = jnp.maximum(pending, jnp.max(active.astype(I32)))
        return it + 1, pending

    lax.while_loop(lambda c: (c[0] < 40) & (c[1] > 0), bisect_step,
                   (jnp.zeros((), I32), jnp.ones((), I32)))

    tie = (clo_ref[...] > topk) & (lo_ref[...] > KEY_NEG_INF)
    any_tie = jnp.max(tie.astype(I32))

    @pl.when(any_tie == 0)
    def _():
        hi_ref[...] = jnp.full(hi_ref.shape, n_all, I32)

    @pl.when(any_tie > 0)
    def _():
        need = topk - chi_ref[...]
        chi_ref[...] = need
        hi_ref[...] = jnp.full(hi_ref.shape, ncol * LANES - 1, I32)
        clo_ref[...] = jnp.full(clo_ref.shape, -1, I32)
        lane = lax.broadcasted_iota(I32, (rb, LANES), 1)

        def tie_step(_, c):
            for r0 in range(0, tq, rb):
                jl, jh = clo_ref[r0:r0 + rb, :], hi_ref[r0:r0 + rb, :]
                mid = jl + ((jh - jl) >> 1)
                active = (jh - jl) > 1
                thr = _key_to_f32(lo_ref[r0:r0 + rb, :])
                cnt = count_rows(r0, lambda blk, j: (blk == thr) & (lane + j * LANES <= mid))
                ok = cnt >= chi_ref[r0:r0 + rb, :]
                hi_ref[r0:r0 + rb, :] = jnp.where(active & ok, mid, jh)
                clo_ref[r0:r0 + rb, :] = jnp.where(active & jnp.logical_not(ok), mid, jl)
            return c

        lax.fori_loop(0, int(math.ceil(math.log2(n_all))) + 1, tie_step, 0)
        keep_all = jnp.logical_not(tie)
        hi_ref[...] = jnp.where(keep_all, n_all, hi_ref[...])

    lane_k = lax.broadcasted_iota(I32, (tq, tk), 1)
    thr_col = _key_to_f32(lo_ref[:, 0:1])
    last_col = hi_ref[:, 0:1]

    def mask_tile(kt, _):
        k0 = pl.multiple_of(kt * tk, tk)
        s = s_ref[:, pl.ds(k0, tk)]
        sel = ((s > thr_col) | ((s == thr_col) & (lane_k + k0 <= last_col))) & (s > -jnp.inf)
        mask_ref[:, pl.ds(k0, tk)] = jnp.where(sel, 1, 0).astype(mask_ref.dtype)
        return 0

    lax.fori_loop(0, nkt, mask_tile, 0)

    def zero_tile(kt, _):
        k0 = pl.multiple_of(kt * tk, tk)
        mask_ref[:, pl.ds(k0, tk)] = jnp.zeros((tq, tk), mask_ref.dtype)
        return 0

    lax.fori_loop(nkt, n_all // tk, zero_tile, 0)


def _dsa_select(qi, wi, ki, topk):
    n = qi.shape[0]
    tq = min(SEL_TQ, n)
    tk = min(SEL_TK, n)
    rb = min(SEL_RB, tq)
    return pl.pallas_call(
        functools.partial(_select_kernel, tq=tq, tk=tk, topk=topk, rb=rb),
        out_shape=jax.ShapeDtypeStruct((n, n), jnp.int8),
        grid=(n // tq,),
        in_specs=[pl.BlockSpec((tq, qi.shape[1]), lambda i: (i, 0)),
                  pl.BlockSpec((tq, LANES), lambda i: (i, 0)),
                  pl.BlockSpec((n, LANES), lambda i: (0, 0))],
        out_specs=pl.BlockSpec((tq, n), lambda i: (i, 0)),
        scratch_shapes=[pltpu.VMEM((tq, n), F32)] + [pltpu.VMEM((tq, LANES), I32)] * 4,
        compiler_params=_cparams(("parallel",)),
        name="dsa_select",
    )(qi, wi, ki)


def _attn_kernel(qb_tab, kt_tab, q_ref, k_ref, v_ref, mask_ref, o_ref, acc_ref, m_ref, l_ref,
                 *, tq, tk):
    step = pl.program_id(0)
    kt = kt_tab[step]
    d = q_ref.shape[1]
    slab = MXU_DIM
    heads_per_slab = slab // HEAD_DIM

    @pl.when(kt == 0)
    def _():
        acc_ref[...] = jnp.zeros_like(acc_ref)
        m_ref[...] = jnp.full(m_ref.shape, NEG_BIG, F32)
        l_ref[...] = jnp.zeros_like(l_ref)

    bias = jnp.where(mask_ref[...].astype(I32) != 0, 0.0, NEG_BIG)
    lane_s = lax.broadcasted_iota(I32, (tq, slab), 1) // HEAD_DIM

    for sb in range(d // slab):
        vs = v_ref[:, sb * slab:(sb + 1) * slab]
        lane_v = lax.broadcasted_iota(I32, vs.shape, 1) // HEAD_DIM
        alpha_l = jnp.zeros((tq, slab), F32)
        pv = jnp.zeros((tq, slab), F32)
        for i in range(heads_per_slab):
            hd = sb * heads_per_slab + i
            blk = hd // 2
            qblk = q_ref[:, blk * LANES:(blk + 1) * LANES]
            qm = jnp.where(_lane_head_mask(qblk.shape, hd % 2), qblk, jnp.zeros_like(qblk))
            s = _dot_nt(qm, k_ref[:, blk * LANES:(blk + 1) * LANES]) + bias
            m_old = m_ref[:, hd:hd + 1]
            m_new = jnp.maximum(m_old, jnp.max(s, axis=1, keepdims=True))
            alpha = jnp.exp(m_old - m_new)
            p = jnp.exp(s - m_new)
            l_ref[:, hd:hd + 1] = l_ref[:, hd:hd + 1] * alpha + jnp.sum(p, axis=1, keepdims=True)
            m_ref[:, hd:hd + 1] = m_new
            vm = jnp.where(lane_v == i, vs, jnp.zeros_like(vs))
            pv = pv + _dot(p.astype(MXU_DTYPE), vm)
            alpha_l = jnp.where(lane_s == i, alpha, alpha_l)
        acc_ref[:, sb * slab:(sb + 1) * slab] = acc_ref[:, sb * slab:(sb + 1) * slab] * alpha_l + pv

    last = kt == ((qb_tab[step] + 1) * tq - 1) // tk

    @pl.when(last)
    def _():
        lane_h = lax.broadcasted_iota(I32, (tq, d), 1) // HEAD_DIM
        l_l = jnp.zeros((tq, d), F32)
        for hd in range(N_HEADS):
            l_l = jnp.where(lane_h == hd, l_ref[:, hd:hd + 1], l_l)
        o_ref[...] = (acc_ref[...] / l_l).astype(o_ref.dtype)


def _dsa_attend(q, k, v, mask):
    n, d = q.shape
    tq = min(ATT_TQ, n)
    tk = min(ATT_TK, n)
    pairs = [(qb, kt) for qb in range(n // tq) for kt in range(((qb + 1) * tq - 1) // tk + 1)]
    qb_tab = jnp.asarray([p[0] for p in pairs], I32)
    kt_tab = jnp.asarray([p[1] for p in pairs], I32)
    grid_spec = pltpu.PrefetchScalarGridSpec(
        num_scalar_prefetch=2,
        grid=(len(pairs),),
        in_specs=[
            pl.BlockSpec((tq, d), lambda i, qt, kt: (qt[i], 0)),
            pl.BlockSpec((tk, d), lambda i, qt, kt: (kt[i], 0)),
            pl.BlockSpec((tk, d), lambda i, qt, kt: (kt[i], 0)),
            pl.BlockSpec((tq, tk), lambda i, qt, kt: (qt[i], kt[i])),
        ],
        out_specs=pl.BlockSpec((tq, d), lambda i, qt, kt: (qt[i], 0)),
        scratch_shapes=[pltpu.VMEM((tq, d), F32), pltpu.VMEM((tq, LANES), F32),
                        pltpu.VMEM((tq, LANES), F32)],
    )
    return pl.pallas_call(
        functools.partial(_attn_kernel, tq=tq, tk=tk),
        out_shape=jax.ShapeDtypeStruct((n, d), MXU_DTYPE),
        grid_spec=grid_spec,
        compiler_params=_cparams(("arbitrary",)),
        name="dsa_attend",
    )(qb_tab, kt_tab, q, k, v, mask)


def _out_kernel(x_ref, a_ref, w_ref, o_ref):
    o_ref[...] = x_ref[...] + _dot(a_ref[...], w_ref[...])


def _out_residual(x, a, w):
    n, d = x.shape
    tm = min(ROW_TM, n)
    return pl.pallas_call(
        _out_kernel,
        out_shape=jax.ShapeDtypeStruct((n, d), F32),
        grid=(n // tm,),
        in_specs=[pl.BlockSpec((tm, d), lambda i: (i, 0)),
                  pl.BlockSpec((tm, d), lambda i: (i, 0)),
                  pl.BlockSpec((d, d), lambda i: (0, 0))],
        out_specs=pl.BlockSpec((tm, d), lambda i: (i, 0)),
        compiler_params=_cparams(("parallel",)),
        name="attn_out",
    )(x, a, w)


def _select_t_kernel(qi_ref, wit_ref, ki_ref, mask_ref, s_ref, s16_ref, lo_ref, hi_ref, clo_ref,
                     chi_ref, *, tq, tk, topk, unroll, unroll16):
    qb = pl.program_id(0)
    n_all = mask_ref.shape[0]
    nkt = (qb * tq + tq - 1) // tk + 1
    nrow = nkt * tk
    rows_it = SUBLANES * unroll

    qm = []
    for hh in range(IDX_HEADS):
        blk = qi_ref[:, (hh // 2) * LANES:(hh // 2 + 1) * LANES]
        qm.append(jnp.where(_lane_head_mask(blk.shape, hh % 2), blk, jnp.zeros_like(blk)))
    qm = jnp.concatenate(qm, axis=0)
    wt = wit_ref[...]
    key_in = lax.broadcasted_iota(I32, (tk, tq), 0)
    q_pos = qb * tq + lax.broadcasted_iota(I32, (tk, tq), 1)

    def score_tile(kt, _):
        k0 = pl.multiple_of(kt * tk, tk)
        s = _dot_nt(ki_ref[pl.ds(k0, tk), :], qm)
        acc = wt[0:1, :] * jnp.maximum(s[:, 0:tq], 0.0)
        for hh in range(1, IDX_HEADS):
            acc = acc + wt[hh:hh + 1, :] * jnp.maximum(s[:, hh * tq:(hh + 1) * tq], 0.0)
        sc = jnp.where(key_in + k0 <= q_pos, acc, -jnp.inf)
        s_ref[pl.ds(k0, tk), :] = sc
        hi_bits = lax.bitcast_convert_type(sc, I32) & -65536
        s16_ref[pl.ds(k0, tk), :] = lax.bitcast_convert_type(hi_bits, F32).astype(BF16)
        return 0

    lax.fori_loop(0, nkt, score_tile, 0)

    lo_ref[...] = jnp.full(lo_ref.shape, KEY16_NEG_INF, I32)
    hi_ref[...] = jnp.full(hi_ref.shape, KEY16_POS_INF + 1, I32)
    clo_ref[...] = jnp.full(clo_ref.shape, 1, I32) * nrow
    chi_ref[...] = jnp.zeros(chi_ref.shape, I32)
    rows16 = 2 * SUBLANES * unroll16
    one16 = jnp.ones((), BF16)
    zero16 = jnp.zeros((), BF16)

    def count16(cand):
        def body(i, acc):
            r0 = pl.multiple_of(i * rows16, rows16)
            blk = s16_ref[pl.ds(r0, rows16), :].reshape(unroll16, 2 * SUBLANES, tq)
            ones = jnp.where(blk >= cand[None], one16, zero16)
            part = ones[0]
            for u in range(1, unroll16):
                part = part + ones[u]
            return acc + part.astype(F32)
        acc = lax.fori_loop(0, nrow // rows16, body, jnp.zeros((2 * SUBLANES, tq), F32))
        tot = jnp.sum(acc, axis=0, keepdims=True).astype(I32)
        return jnp.broadcast_to(tot, (SUBLANES, tq))

    def coarse_step(_, carry):
        lo, hi = lo_ref[...], hi_ref[...]
        mid = (lo + hi) >> 1
        active = mid != lo
        bits16 = (mid ^ ((mid >> 31) & 0x7FFF)) & 0xFFFF
        cand = lax.bitcast_convert_type(bits16 << 16, F32)
        cand = jnp.concatenate([cand, cand], axis=0).astype(BF16)
        cnt = count16(cand)
        ge = cnt >= topk
        up = active & ge
        dn = active & jnp.logical_not(ge)
        lo_ref[...] = jnp.where(up, mid, lo)
        hi_ref[...] = jnp.where(dn, mid, hi)
        clo_ref[...] = jnp.where(up, cnt, clo_ref[...])
        chi_ref[...] = jnp.where(dn, cnt, chi_ref[...])
        return carry

    lax.fori_loop(0, 16, coarse_step, 0)

    lo16 = lo_ref[...]
    none_finite = lo16 == KEY16_NEG_INF
    lo_ref[...] = jnp.where(none_finite, KEY_NEG_INF, lo16 << 16)
    hi_ref[...] = jnp.where(none_finite, KEY_NEG_INF + 1, (lo16 + 1) << 16)

    def count(pred):
        def body(i, acc):
            r0 = pl.multiple_of(i * rows_it, rows_it)
            blk = s_ref[pl.ds(r0, rows_it), :].reshape(unroll, SUBLANES, tq)
            return acc + jnp.sum(pred(blk, r0).astype(I32), axis=0)
        acc = lax.fori_loop(0, nrow // rows_it, body, jnp.zeros((SUBLANES, tq), I32))
        return jnp.broadcast_to(jnp.sum(acc, axis=0, keepdims=True), (SUBLANES, tq))

    def bisect_step(carry):
        it, _ = carry
        lo, hi = lo_ref[...], hi_ref[...]
        mid = (lo & hi) + ((lo ^ hi) >> 1)
        active = mid != lo
        cand = _key_to_f32(mid)
        cnt = count(lambda blk, r0: blk >= cand[None])
        ge = cnt >= topk
        up = active & ge
        dn = active & jnp.logical_not(ge)
        hit = active & (cnt == topk)
        lo_ref[...] = jnp.where(up, mid, lo)
        hi_ref[...] = jnp.where(hit, mid + 1, jnp.where(dn, mid, hi))
        clo_ref[...] = jnp.where(up, cnt, clo_ref[...])
        chi_ref[...] = jnp.where(dn, cnt, chi_ref[...])
        return it + 1, jnp.max(active.astype(I32))

    lax.while_loop(lambda c: (c[0] < 40) & (c[1] > 0), bisect_step,
                   (jnp.zeros((), I32), jnp.ones((), I32)))

    tie = (clo_ref[...] > topk) & (lo_ref[...] > KEY_NEG_INF)
    any_tie = jnp.max(tie.astype(I32))

    @pl.when(any_tie == 0)
    def _():
        hi_ref[...] = jnp.full(hi_ref.shape, n_all, I32)

    @pl.when(any_tie > 0)
    def _():
        chi_ref[...] = topk - chi_ref[...]
        hi_ref[...] = jnp.full(hi_ref.shape, 1, I32) * (nrow - 1)
        clo_ref[...] = jnp.full(clo_ref.shape, -1, I32)
        sub = (lax.broadcasted_iota(I32, (unroll, SUBLANES, tq), 0) * SUBLANES
               + lax.broadcasted_iota(I32, (unroll, SUBLANES, tq), 1))
        thr = _key_to_f32(lo_ref[...])

        def tie_step(_, c):
            jl, jh = clo_ref[...], hi_ref[...]
            mid = jl + ((jh - jl) >> 1)
            active = (jh - jl) > 1
            cnt = count(lambda blk, r0: (blk == thr[None]) & (sub + r0 <= mid[None]))
            ok = cnt >= chi_ref[...]
            hi_ref[...] = jnp.where(active & ok, mid, jh)
            clo_ref[...] = jnp.where(active & jnp.logical_not(ok), mid, jl)
            return c

        lax.fori_loop(0, int(math.ceil(math.log2(n_all))) + 1, tie_step, 0)
        hi_ref[...] = jnp.where(tie, hi_ref[...], n_all)

    thr_row = _key_to_f32(lo_ref[0:1, :])
    last_row = hi_ref[0:1, :]

    def mask_tile(kt, _):
        k0 = pl.multiple_of(kt * tk, tk)
        s = s_ref[pl.ds(k0, tk), :]
        sel = ((s > thr_row) | ((s == thr_row) & (key_in + k0 <= last_row))) & (s > -jnp.inf)
        mask_ref[pl.ds(k0, tk), :] = jnp.where(sel, 1, 0).astype(mask_ref.dtype)
        return 0

    lax.fori_loop(0, nkt, mask_tile, 0)

    def zero_tile(kt, _):
        k0 = pl.multiple_of(kt * tk, tk)
        mask_ref[pl.ds(k0, tk), :] = jnp.zeros((tk, tq), mask_ref.dtype)
        return 0

    lax.fori_loop(nkt, n_all // tk, zero_tile, 0)


def _dsa_select_t(qi, wit, ki, topk):
    n = qi.shape[0]
    tq = min(SEL_TQ, n)
    tk = min(SEL_TK, n)
    assert tk % (SUBLANES * SEL_UNROLL) == 0 and tk % (2 * SUBLANES * SEL_UNROLL16) == 0
    return pl.pallas_call(
        functools.partial(_select_t_kernel, tq=tq, tk=tk, topk=topk, unroll=SEL_UNROLL,
                          unroll16=SEL_UNROLL16),
        out_shape=jax.ShapeDtypeStruct((n, n), jnp.int8),
        grid=(n // tq,),
        in_specs=[pl.BlockSpec((tq, qi.shape[1]), lambda i: (i, 0)),
                  pl.BlockSpec((IDX_HEADS, tq), lambda i: (0, i)),
                  pl.BlockSpec((n, LANES), lambda i: (0, 0))],
        out_specs=pl.BlockSpec((n, tq), lambda i: (0, i)),
        scratch_shapes=[pltpu.VMEM((n, tq), F32), pltpu.VMEM((n, tq), BF16)]
        + [pltpu.VMEM((SUBLANES, tq), I32)] * 4,
        compiler_params=_cparams(("parallel",)),
        name="dsa_select",
    )(qi, wit, ki)


I16 = jnp.int16
L16_MIN = -32768
ROW_NEVER = 32767


def _select16_kernel(qi_ref, wit_ref, ki_ref, mask_ref, h16_ref, l16_ref, lo_ref, hi_ref, clo_ref,
                     chi_ref, base_ref, *, tq, tk, topk, unroll16):
    qb = pl.program_id(0)
    n_all = mask_ref.shape[0]
    nkt = (qb * tq + tq - 1) // tk + 1
    nrow = nkt * tk
    rows16 = 2 * SUBLANES * unroll16
    ntrip = nrow // rows16

    qm = []
    for hh in range(IDX_HEADS):
        blk = qi_ref[:, (hh // 2) * LANES:(hh // 2 + 1) * LANES]
        qm.append(jnp.where(_lane_head_mask(blk.shape, hh % 2), blk, jnp.zeros_like(blk)))
    qm = jnp.concatenate(qm, axis=0)
    wt = wit_ref[...]
    causal_slack = (qb * tq + lax.broadcasted_iota(I32, (tk, tq), 1)
                    - lax.broadcasted_iota(I32, (tk, tq), 0))

    def score_tile(kt, _):
        k0 = pl.multiple_of(kt * tk, tk)
        s = _dot_nt(ki_ref[pl.ds(k0, tk), :], qm)
        acc = wt[0:1, :] * jnp.maximum(s[:, 0:tq], 0.0)
        for hh in range(1, IDX_HEADS):
            acc = acc + wt[hh:hh + 1, :] * jnp.maximum(s[:, hh * tq:(hh + 1) * tq], 0.0)
        acc = jnp.where(acc == 0.0, 0.0, acc)
        sc = jnp.where(k0 <= causal_slack, acc, -jnp.inf)
        bits = lax.bitcast_convert_type(sc, I32)
        key = bits ^ ((bits >> 31) & 0x7FFFFFFF)
        h16_ref[pl.ds(k0, tk), :] = (key >> 16).astype(I16)
        l16_ref[pl.ds(k0, tk), :] = (key ^ 0x8000).astype(I16)
        return 0

    lax.fori_loop(0, nkt, score_tile, 0)

    one16 = jnp.ones((), I16)
    zero16 = jnp.zeros((), I16)
    sub16 = (lax.broadcasted_iota(I32, (unroll16, 2 * SUBLANES, tq), 0) * (2 * SUBLANES)
             + lax.broadcasted_iota(I32, (unroll16, 2 * SUBLANES, tq), 1)).astype(I16)

    def pack16(v):
        return jnp.concatenate([v, v], axis=0).astype(I16)

    def count(pred):
        def body(i, acc):
            r0 = pl.multiple_of(i * rows16, rows16)
            hb = h16_ref[pl.ds(r0, rows16), :].reshape(unroll16, 2 * SUBLANES, tq)
            lb = l16_ref[pl.ds(r0, rows16), :].reshape(unroll16, 2 * SUBLANES, tq)
            ones = jnp.where(pred(hb, lb, r0), one16, zero16)
            part = ones[0]
            for u in range(1, unroll16):
                part = part + ones[u]
            return acc + part.astype(I32)
        acc = lax.fori_loop(0, ntrip, body, jnp.zeros((2 * SUBLANES, tq), I32))
        return jnp.broadcast_to(jnp.sum(acc, axis=0, keepdims=True), (SUBLANES, tq))

    def bisect(count_ge, n_steps):
        def step(_, carry):
            lo, hi = lo_ref[...], hi_ref[...]
            mid = (lo + hi) >> 1
            active = mid != lo
            cnt = count_ge(pack16(mid)) + base_ref[...]
            ge = cnt >= topk
            up = active & ge
            dn = active & jnp.logical_not(ge)
            lo_ref[...] = jnp.where(up, mid, lo)
            hi_ref[...] = jnp.where(dn, mid, hi)
            clo_ref[...] = jnp.where(up, cnt, clo_ref[...])
            chi_ref[...] = jnp.where(dn, cnt, chi_ref[...])
            return carry
        lax.fori_loop(0, n_steps, step, 0)

    lo_ref[...] = jnp.full(lo_ref.shape, KEY16_NEG_INF, I32)
    hi_ref[...] = jnp.full(hi_ref.shape, KEY16_POS_INF + 1, I32)
    clo_ref[...] = jnp.full(clo_ref.shape, 1, I32) * nrow
    chi_ref[...] = jnp.zeros(chi_ref.shape, I32)
    base_ref[...] = jnp.zeros(base_ref.shape, I32)
    bisect(lambda c: count(lambda hb, lb, r0: hb >= c[None]), 16)

    t_hi = lo_ref[...]
    none_finite = t_hi == KEY16_NEG_INF
    t_hi16 = pack16(t_hi)

    def bucket_tile(i, _):
        r0 = pl.multiple_of(i * rows16, rows16)
        hb = h16_ref[pl.ds(r0, rows16), :].reshape(unroll16, 2 * SUBLANES, tq)
        lb = l16_ref[pl.ds(r0, rows16), :].reshape(unroll16, 2 * SUBLANES, tq)
        lb = jnp.where(hb == t_hi16[None], lb, jnp.full((), L16_MIN, I16))
        l16_ref[pl.ds(r0, rows16), :] = lb.reshape(rows16, tq)
        return 0

    lax.fori_loop(0, ntrip, bucket_tile, 0)
    base_ref[...] = chi_ref[...]
    lo_ref[...] = jnp.full(lo_ref.shape, L16_MIN, I32)
    hi_ref[...] = jnp.full(hi_ref.shape, -L16_MIN, I32)
    bisect(lambda c: count(lambda hb, lb, r0: lb >= c[None]), 16)
    t_lo = jnp.where(none_finite, -L16_MIN - 1, lo_ref[...])

    tie = (clo_ref[...] > topk) & jnp.logical_not(none_finite)
    any_tie = jnp.max(tie.astype(I32))
    t_lo16 = pack16(t_lo)

    def rank_tile(i, _):
        r0 = pl.multiple_of(i * rows16, rows16)
        hb = h16_ref[pl.ds(r0, rows16), :].reshape(unroll16, 2 * SUBLANES, tq)
        lb = l16_ref[pl.ds(r0, rows16), :].reshape(unroll16, 2 * SUBLANES, tq)
        in_bucket = hb == t_hi16[None]
        rows = sub16 + jnp.full((2 * SUBLANES, tq), r0, I32).astype(I16)[None]
        rank = jnp.where(in_bucket & (lb == t_lo16[None]), rows,
                         jnp.where(in_bucket & (lb > t_lo16[None]),
                                   jnp.full((), -1, I16), jnp.full((), ROW_NEVER, I16)))
        l16_ref[pl.ds(r0, rows16), :] = rank.reshape(rows16, tq)
        return 0

    lax.fori_loop(0, ntrip, rank_tile, 0)

    @pl.when(any_tie == 0)
    def _():
        hi_ref[...] = jnp.full(hi_ref.shape, n_all, I32)

    @pl.when(any_tie > 0)
    def _():
        chi_ref[...] = topk - base_ref[...]
        hi_ref[...] = jnp.full(hi_ref.shape, 1, I32) * (nrow - 1)
        clo_ref[...] = jnp.full(clo_ref.shape, -1, I32)

        def tie_step(_, c):
            jl, jh = clo_ref[...], hi_ref[...]
            mid = jl + ((jh - jl) >> 1)
            active = (jh - jl) > 1
            mid16 = pack16(mid)
            cnt = count(lambda hb, lb, r0: lb <= mid16[None])
            ok = cnt >= chi_ref[...]
            hi_ref[...] = jnp.where(active & ok, mid, jh)
            clo_ref[...] = jnp.where(active & jnp.logical_not(ok), mid, jl)
            return c

        n_halvings = 0
        for j in range(int(math.ceil(math.log2(n_all // tk))) + 1):
            n_halvings = n_halvings + (((nkt - 1) >> j) > 0).astype(I32)
        lax.fori_loop(0, n_halvings + int(math.log2(tk)) + 1, tie_step, 0)
        hi_ref[...] = jnp.where(tie, hi_ref[...], n_all)

    last16 = pack16(hi_ref[...])
    g16 = 2 * SUBLANES

    def mask_tile(kt, _):
        k0 = pl.multiple_of(kt * tk, tk)
        hb = h16_ref[pl.ds(k0, tk), :].reshape(tk // g16, g16, tq)
        rank = l16_ref[pl.ds(k0, tk), :].reshape(tk // g16, g16, tq)
        sel = ((hb > t_hi16[None]) | (rank <= last16[None])) & (hb > KEY16_NEG_INF)
        sel = jnp.where(sel, one16, zero16).reshape(tk, tq)
        mask_ref[pl.ds(k0, tk), :] = sel.astype(mask_ref.dtype)
        return 0

    lax.fori_loop(0, nkt, mask_tile, 0)

    def zero_tile(kt, _):
        k0 = pl.multiple_of(kt * tk, tk)
        mask_ref[pl.ds(k0, tk), :] = jnp.zeros((tk, tq), mask_ref.dtype)
        return 0

    lax.fori_loop(nkt, n_all // tk, zero_tile, 0)


def _dsa_select16(qi, wit, ki, topk):
    n = qi.shape[0]
    tq = min(SEL_TQ, n)
    tk = min(SEL_TK, n)
    assert tk % (2 * SUBLANES * SEL_UNROLL16) == 0 and n < -L16_MIN
    return pl.pallas_call(
        functools.partial(_select16_kernel, tq=tq, tk=tk, topk=topk, unroll16=SEL_UNROLL16),
        out_shape=jax.ShapeDtypeStruct((n, n), jnp.int8),
        grid=(n // tq,),
        in_specs=[pl.BlockSpec((tq, qi.shape[1]), lambda i: (i, 0)),
                  pl.BlockSpec((IDX_HEADS, tq), lambda i: (0, i)),
                  pl.BlockSpec((n, LANES), lambda i: (0, 0))],
        out_specs=pl.BlockSpec((n, tq), lambda i: (0, i)),
        scratch_shapes=[pltpu.VMEM((n, tq), I16), pltpu.VMEM((n, tq), I16)]
        + [pltpu.VMEM((SUBLANES, tq), I32)] * 5,
        compiler_params=_cparams(("parallel",)),
        name="dsa_select",
    )(qi, wit, ki)


def _attn_t_kernel(qb_tab, kt_tab, q_ref, k_ref, vt_ref, mask_ref, o_ref, acc_ref, m_ref,
                   *, tq, tk, qs):
    step = pl.program_id(0)
    kt = kt_tab[step]

    @pl.when(kt == 0)
    def _():
        acc_ref[...] = jnp.zeros_like(acc_ref)
        m_ref[...] = jnp.full(m_ref.shape, NEG_BIG, MXU_DTYPE).astype(F32)

    bias = jnp.where(mask_ref[...].astype(I32) != 0, 0.0, NEG_BIG).astype(MXU_DTYPE)

    def logits(hd):
        blk = hd // 2
        qblk = q_ref[:, blk * LANES:(blk + 1) * LANES]
        qm = jnp.where(_lane_head_mask(qblk.shape, hd % 2), qblk, jnp.zeros_like(qblk))
        return _dot_nt(k_ref[:, blk * LANES:(blk + 1) * LANES], qm).astype(MXU_DTYPE) + bias

    s_next = logits(0)
    for hd in range(N_HEADS):
        r0 = hd * VT_ROWS
        s = s_next
        if hd + 1 < N_HEADS:
            s_next = logits(hd + 1)
        m_old = m_ref[hd:hd + 1, :]
        m_new = jnp.maximum(m_old, jnp.max(s, axis=0, keepdims=True).astype(F32))
        alpha = jnp.exp2(m_old - m_new)
        p = jnp.exp2(s - m_new.astype(MXU_DTYPE))
        m_ref[hd:hd + 1, :] = m_new
        pv = _dot(vt_ref[r0:r0 + VT_ROWS, :], p)
        acc_ref[r0:r0 + VT_ROWS, :] = acc_ref[r0:r0 + VT_ROWS, :] * alpha + pv

    last = kt == ((qb_tab[step] + 1) * tq - 1) // tk

    @pl.when(last)
    def _():
        for hd in range(N_HEADS):
            r0 = hd * VT_ROWS
            o_ref[hd * HEAD_DIM:(hd + 1) * HEAD_DIM, :] = (
                acc_ref[r0:r0 + HEAD_DIM, :] / acc_ref[r0 + HEAD_DIM:r0 + HEAD_DIM + 1, :]
            ).astype(o_ref.dtype)


def _dsa_attend_t(q, k, vt, mask_t):
    n, d = q.shape
    tq = min(ATT_TQ, n)
    tk = min(ATT_TK, n)
    pairs = [(qb, kt) for qb in range(n // tq) for kt in range(((qb + 1) * tq - 1) // tk + 1)]
    qb_tab = jnp.asarray([p[0] for p in pairs], I32)
    kt_tab = jnp.asarray([p[1] for p in pairs], I32)
    grid_spec = pltpu.PrefetchScalarGridSpec(
        num_scalar_prefetch=2,
        grid=(len(pairs),),
        in_specs=[
            pl.BlockSpec((tq, d), lambda i, qt, kt: (qt[i], 0)),
            pl.BlockSpec((tk, d), lambda i, qt, kt: (kt[i], 0)),
            pl.BlockSpec((vt.shape[0], tk), lambda i, qt, kt: (0, kt[i])),
            pl.BlockSpec((tk, tq), lambda i, qt, kt: (kt[i], qt[i])),
        ],
        out_specs=pl.BlockSpec((d, tq), lambda i, qt, kt: (0, qt[i])),
        scratch_shapes=[pltpu.VMEM((vt.shape[0], tq), F32), pltpu.VMEM((N_HEADS, tq), F32)],
    )
    return pl.pallas_call(
        functools.partial(_attn_t_kernel, tq=tq, tk=tk, qs=min(ATT_QS, tq)),
        out_shape=jax.ShapeDtypeStruct((d, n), MXU_DTYPE),
        grid_spec=grid_spec,
        compiler_params=_cparams(("arbitrary",)),
        name="dsa_attend",
    )(qb_tab, kt_tab, q, k, vt, mask_t)


def _out_t_kernel(x_ref, at_ref, w_ref, o_ref):
    o_ref[...] = x_ref[...] + lax.dot_general(
        at_ref[...], w_ref[...], (((0,), (0,)), ((), ())), preferred_element_type=F32)


def _out_residual_t(x, at, w):
    n, d = x.shape
    tm = min(ROW_TM, n)
    return pl.pallas_call(
        _out_t_kernel,
        out_shape=jax.ShapeDtypeStruct((n, d), F32),
        grid=(n // tm,),
        in_specs=[pl.BlockSpec((tm, d), lambda i: (i, 0)),
                  pl.BlockSpec((d, tm), lambda i: (0, i)),
                  pl.BlockSpec((d, d), lambda i: (0, 0))],
        out_specs=pl.BlockSpec((tm, d), lambda i: (i, 0)),
        compiler_params=_cparams(("parallel",)),
        name="attn_out",
    )(x, at, w)


def _rope_lane_tables(length):
    inv_freq = ROPE_THETA ** (-jnp.arange(0, HEAD_DIM, 2, dtype=F32) / HEAD_DIM)
    ang = jnp.arange(length, dtype=F32)[:, None] * inv_freq[None, :]
    lane = np.arange(LANES)
    cos_t = jnp.cos(ang)[:, lane % 32]
    sin_t = jnp.sin(ang)[:, lane % 32] * jnp.asarray(np.where(lane < 64, -1.0, 1.0), F32)
    return cos_t, sin_t


def kernel(x, s5_lambda_re, s5_lambda_im, s5_log_dt, s5_b_re, s5_b_im, s5_c_re, s5_c_im, s5_d, s5_w_glu, dsa_w_in, dsa_q_norm, dsa_k_norm, dsa_w_o, ffn_w_gate_up, ffn_w_down, norm_mix, norm_ffn):
    bsz, length, d = x.shape
    depth = norm_mix.shape[0]
    topk = min(TOPK_MAX, length // 4)
    nchunk = length // S5_CHUNK
    cos_t, sin_t = _rope_lane_tables(length)
    outs = []
    wgu = ffn_w_gate_up.astype(MXU_DTYPE)
    wd = ffn_w_down.astype(MXU_DTYPE)
    tables = jax.vmap(_s5_tables)(s5_lambda_re, s5_lambda_im, s5_log_dt, s5_b_re, s5_b_im,
                                  s5_c_re, s5_c_im)
    for b in range(bsz):
        xs = x[b].astype(F32)
        for i in range(depth):
            j = i // 2
            if i % 2 == 0:
                h_t = _norm_planes(xs, norm_mix[i])
                g_t = _s5_scan(h_t, tables, s5_d[j], j)
                xs = _glu_planes(xs, g_t, s5_w_glu[j].astype(MXU_DTYPE))
                xs = _ffn(xs, norm_ffn[i], wgu, wd, i)
            else:
                q, k, vt, qi, ki, wi = _dsa_project(xs, norm_mix[i], dsa_w_in[j], dsa_q_norm[j],
                                                    dsa_k_norm[j], cos_t, sin_t)
                mask_t = _dsa_select16(qi, wi[:, :IDX_HEADS].T, ki, topk)
                att_t = _dsa_attend_t(q, k, vt, mask_t)
                xs = _out_residual_t(xs, att_t, dsa_w_o[j].astype(MXU_DTYPE))
                xs = _ffn(xs, norm_ffn[i], wgu, wd, i)
        outs.append(xs)
    return jnp.stack(outs, axis=0).astype(x.dtype)
```

```python
import functools
import math

import jax
import jax.numpy as jnp
import numpy as np
from jax import lax
from jax.experimental import pallas as pl
from jax.experimental.pallas import tpu as pltpu

F32 = jnp.float32
BF16 = jnp.bfloat16
I32 = jnp.int32
I16 = jnp.int16
MXU_DTYPE = BF16

S5_GROUP = 16
S5_STATE = 64
N_HEADS = 16
HEAD_DIM = 64
IDX_HEADS = 8
IDX_DIM = 64
TOPK_MAX = 256
ROPE_THETA = 10000.0
EPS = 1e-6

LANES = 128
SUBLANES = 8
MXU_DIM = 256
VMEM_LIMIT = 56 * 1024 * 1024

S5_CHUNK = 16
S5_SLAB_GROUPS = LANES // S5_GROUP
NEG_BIG = -1e30
LOG2E = math.log2(math.e)
VT_PAD = 16
VT_ROWS = HEAD_DIM + VT_PAD

ROW_TM = 512
S5_TC = 512
PROJ_TM = 512
SEL_TQ = 256
SEL_TK = 512
SEL_UNROLL16 = 32
ATT_TQ = 512
ATT_TK = 512


def _cparams(sem):
    return pltpu.CompilerParams(dimension_semantics=sem, vmem_limit_bytes=VMEM_LIMIT)


def _rms(x, gain=None):
    y = x * lax.rsqrt(jnp.mean(x * x, axis=-1, keepdims=True) + EPS)
    return y if gain is None else y * gain


def _dot(a, b):
    return jnp.dot(a, b, preferred_element_type=F32)


def _dot_nt(a, b):
    return lax.dot_general(a, b, (((1,), (1,)), ((), ())), preferred_element_type=F32)


def _ffn_kernel(x_ref, g_ref, wgu_ref, wd_ref, o_ref, acc_ref, *, d_ff, fc):
    x = x_ref[...]
    h = _rms(x, g_ref[...]).astype(MXU_DTYPE)
    for c in range(d_ff // fc):
        g = _dot(h, wgu_ref[:, c * fc:(c + 1) * fc])
        u = _dot(h, wgu_ref[:, d_ff + c * fc:d_ff + (c + 1) * fc])
        a = (g * jax.nn.sigmoid(g) * u).astype(MXU_DTYPE)
        d = _dot(a, wd_ref[c * fc:(c + 1) * fc, :])
        if c == 0:
            acc_ref[...] = d
        else:
            acc_ref[...] += d
    o_ref[...] = x + acc_ref[...]


def _ffn(x, gain, wgu, wd, layer):
    n, d = x.shape
    d_ff = wd.shape[1]
    tm = min(ROW_TM, n)
    fc = MXU_DIM
    return pl.pallas_call(
        functools.partial(_ffn_kernel, d_ff=d_ff, fc=fc),
        out_shape=jax.ShapeDtypeStruct((n, d), F32),
        grid=(n // tm,),
        in_specs=[
            pl.BlockSpec((tm, d), lambda i: (i, 0)),
            pl.BlockSpec((1, d), lambda i: (0, 0)),
            pl.BlockSpec((None, d, 2 * d_ff), lambda i: (layer, 0, 0)),
            pl.BlockSpec((None, d_ff, d), lambda i: (layer, 0, 0)),
        ],
        out_specs=pl.BlockSpec((tm, d), lambda i: (i, 0)),
        scratch_shapes=[pltpu.VMEM((tm, d), F32)],
        compiler_params=_cparams(("parallel",)),
        name="ffn",
    )(x, gain.reshape(1, d), wgu, wd)


def _s5_tables(lam_re, lam_im, log_dt, b_re, b_im, c_re, c_im):
    hp = lax.Precision.HIGHEST
    g, p = lam_re.shape
    h = S5_GROUP
    nsl = g // S5_SLAB_GROUPS
    sg = S5_SLAB_GROUPS
    t = S5_CHUNK
    lam_re, lam_im, log_dt = lam_re.astype(F32), lam_im.astype(F32), log_dt.astype(F32)
    b_re, b_im, c_re, c_im = (a.astype(F32) for a in (b_re, b_im, c_re, c_im))
    dt = jnp.exp(log_dt)[:, None]

    def apow(k):
        k = jnp.asarray(k, F32).reshape((-1, 1, 1))
        mag = jnp.exp(lam_re[None] * dt[None] * k)
        ang = lam_im[None] * dt[None] * k
        return mag * jnp.cos(ang), mag * jnp.sin(ang)

    ar, ai = apow([1.0])
    ar, ai = ar[0], ai[0]
    den = lam_re * lam_re + lam_im * lam_im
    nr, ni = ar - 1.0, ai
    qr = (nr * lam_re + ni * lam_im) / den
    qi = (ni * lam_re - nr * lam_im) / den
    bbr = qr[..., None] * b_re - qi[..., None] * b_im
    bbi = qr[..., None] * b_im + qi[..., None] * b_re

    pr, pi = apow(np.arange(t + 1))
    mr = c_re[None] * pr[:, :, None, :] - c_im[None] * pi[:, :, None, :]
    mi = c_re[None] * pi[:, :, None, :] + c_im[None] * pr[:, :, None, :]

    kk = (jnp.einsum('tghp,gpk->tghk', mr[:t], bbr, precision=hp)
          - jnp.einsum('tghp,gpk->tghk', mi[:t], bbi, precision=hp))
    kp = jnp.concatenate([jnp.zeros_like(kk[:1]), kk], axis=0)

    def compact(x, row_dims, col_dims):
        r0, r2 = row_dims
        c0, c1 = col_dims
        return x.reshape(nsl, t // 2, r0 * sg * r2, c0 * c1).astype(MXU_DTYPE)

    dl = np.arange(t // 2)[:, None, None]
    sl = np.arange(2)[None, :, None]
    jl = np.arange(2)[None, None, :]
    idx = 2 * dl + jl - sl + 1
    kg = kp[idx]
    kg = kg.reshape(t // 2, 2, 2, nsl, sg, h, h)
    tp = compact(kg.transpose(3, 0, 1, 4, 6, 2, 5), (2, h), (2, h))

    prs, pis = pr[t - 1::-1][:t], pi[t - 1::-1][:t]
    er = prs[..., None] * bbr[None] - pis[..., None] * bbi[None]
    ei = prs[..., None] * bbi[None] + pis[..., None] * bbr[None]
    bf = jnp.stack([er, ei], axis=2)
    bf = bf.reshape(t // 2, 2, nsl, sg, 2, p, h)
    bz = compact(bf.transpose(2, 0, 1, 3, 6, 4, 5), (2, h), (2, p))

    cf = jnp.stack([mr[1:], -mi[1:]], axis=2)
    cf = cf.reshape(t // 2, 2, nsl, sg, 2, h, p)
    cz = compact(cf.transpose(2, 0, 4, 3, 6, 1, 5), (2, p), (2, h))

    def slab_state(re, im):
        k = re.shape[0]
        x = jnp.stack([re, im], axis=1).reshape(k, 2, nsl, sg * p)
        return x.transpose(2, 0, 1, 3).reshape(nsl, k, 2 * sg * p)

    ad = slab_state(*apow([t * 1.0, t * 2.0, t * 4.0]))
    ap8 = slab_state(*apow(t * (np.arange(SUBLANES) + 1.0)))
    return tp, bz, cz, ad, ap8


def _s5_expander(row_dims, col_dims):
    sg = S5_SLAB_GROUPS
    r0, r2 = row_dims
    c0, c1 = col_dims
    ci = np.arange(c0 * sg * c1)
    src = (ci // (sg * c1)) * c1 + ci % c1
    spread = np.arange(c0 * c1)[:, None] == src[None, :]
    ri = np.arange(r0 * sg * r2)
    keep = ((ri // r2) % sg)[:, None] == ((ci // c1) % sg)[None, :]
    return jnp.asarray(spread, MXU_DTYPE), jnp.asarray(keep, MXU_DTYPE)


def _s5_kernel(h_ref, tpc_ref, bzc_ref, czc_ref, sph_ref, spp_ref, ktp_ref, kbz_ref, kcz_ref,
               ad_ref, ap8_ref, dsk_ref, o_ref,
               carry_ref, z_ref, xp_ref, tp_ref, bz_ref, cz_ref, *, tc):
    half = z_ref.shape[1] // 2
    npair = S5_CHUNK // 2

    @pl.when(pl.program_id(1) == 0)
    def _():
        carry_ref[...] = jnp.zeros_like(carry_ref)
        for z in range(npair):
            tp_ref[z] = (_dot(tpc_ref[z], sph_ref[...]) * ktp_ref[...]).astype(tp_ref.dtype)
            bz_ref[z] = (_dot(bzc_ref[z], spp_ref[...]) * kbz_ref[...]).astype(bz_ref.dtype)
            cz_ref[z] = (_dot(czc_ref[z], sph_ref[...]) * kcz_ref[...]).astype(cz_ref.dtype)

    u = [jnp.concatenate([h_ref[2 * s], h_ref[2 * s + 1]], axis=1) for s in range(npair)]

    z = _dot(u[0], bz_ref[0])
    for s in range(1, npair):
        z = z + _dot(u[s], bz_ref[s])
    z_ref[...] = z

    row = lax.broadcasted_iota(I32, (SUBLANES, half), 0)
    pr8, pi8 = ap8_ref[:, :half], ap8_ref[:, half:]

    def tile_step(t, carry):
        r0 = pl.multiple_of(t * SUBLANES, SUBLANES)
        zt = z_ref[pl.ds(r0, SUBLANES), :]
        xr, xi = zt[:, :half], zt[:, half:]
        for k, d in enumerate((1, 2, 4)):
            a = ad_ref[k:k + 1, :]
            a_r, a_i = a[:, :half], a[:, half:]
            sr = jnp.where(row >= d, pltpu.roll(xr, d, 0), 0.0)
            si = jnp.where(row >= d, pltpu.roll(xi, d, 0), 0.0)
            xr, xi = xr + a_r * sr - a_i * si, xi + a_r * si + a_i * sr
        cr, ci = carry[:, :half], carry[:, half:]
        xr, xi = xr + pr8 * cr - pi8 * ci, xi + pr8 * ci + pi8 * cr
        xpr = jnp.where(row >= 1, pltpu.roll(xr, 1, 0), cr)
        xpi = jnp.where(row >= 1, pltpu.roll(xi, 1, 0), ci)
        xp_ref[pl.ds(r0, SUBLANES), :] = jnp.concatenate([xpr, xpi], axis=1)
        return jnp.concatenate([xr[SUBLANES - 1:], xi[SUBLANES - 1:]], axis=1)

    carry_ref[...] = lax.fori_loop(0, tc // SUBLANES, tile_step, carry_ref[...])

    xp = xp_ref[...].astype(MXU_DTYPE)
    dsk = dsk_ref[...]
    for i in range(npair):
        y = _dot(xp, cz_ref[i])
        for s in range(i + 1):
            y = y + _dot(u[s], tp_ref[i - s])
        for jl in range(2):
            j = 2 * i + jl
            yj = y[:, jl * LANES:(jl + 1) * LANES] + dsk * h_ref[j].astype(F32)
            o_ref[j] = jax.nn.gelu(yj).astype(o_ref.dtype)


def _s5_scan(h_t, tables, d_skip, layer):
    tpc, bzc, czc, ad, ap8 = tables
    t, c, d = h_t.shape
    nsl = d // LANES
    tc = min(S5_TC, c)
    npair = S5_CHUNK // 2
    sg, hh, pp = S5_SLAB_GROUPS, S5_GROUP, S5_STATE
    sp_h, keep_tp = _s5_expander((2, hh), (2, hh))
    sp_p, keep_bz = _s5_expander((2, hh), (2, pp))
    _, keep_cz = _s5_expander((2, pp), (2, hh))
    nu, st = 2 * sg * hh, 2 * sg * pp
    dsk = d_skip.astype(F32).reshape(nsl, 1, LANES)
    const = lambda a: pl.BlockSpec(a.shape, lambda b, i: (0,) * a.ndim)
    per_slab = lambda a: pl.BlockSpec((None, None) + a.shape[2:],
                                      lambda b, i: (layer, b) + (0,) * (a.ndim - 2))
    return pl.pallas_call(
        functools.partial(_s5_kernel, tc=tc),
        out_shape=jax.ShapeDtypeStruct((t, c, d), MXU_DTYPE),
        grid=(nsl, c // tc),
        in_specs=[
            pl.BlockSpec((t, tc, LANES), lambda b, i: (0, i, b)),
            per_slab(tpc), per_slab(bzc), per_slab(czc),
            const(sp_h), const(sp_p), const(keep_tp), const(keep_bz), const(keep_cz),
            per_slab(ad), per_slab(ap8),
            pl.BlockSpec((None, 1, LANES), lambda b, i: (b, 0, 0)),
        ],
        out_specs=pl.BlockSpec((t, tc, LANES), lambda b, i: (0, i, b)),
        scratch_shapes=[pltpu.VMEM((1, st), F32), pltpu.VMEM((tc, st), F32),
                        pltpu.VMEM((tc, st), F32),
                        pltpu.VMEM((npair, nu, nu), MXU_DTYPE),
                        pltpu.VMEM((npair, nu, st), MXU_DTYPE),
                        pltpu.VMEM((npair, st, nu), MXU_DTYPE)],
        compiler_params=_cparams(("arbitrary", "arbitrary")),
        name="s5_scan",
    )(h_t, tpc, bzc, czc, sp_h, sp_p, keep_tp, keep_bz, keep_cz, ad, ap8, dsk)


def _plane_perm(rows):
    nc = rows // S5_CHUNK
    p = np.zeros((rows, rows), np.float32)
    c, s = np.meshgrid(np.arange(nc), np.arange(S5_CHUNK), indexing="ij")
    p[(s * nc + c).ravel(), (c * S5_CHUNK + s).ravel()] = 1.0
    return p


def _norm_planes_kernel(x_ref, g_ref, p_ref, o_ref):
    h = _rms(x_ref[...], g_ref[...]).astype(MXU_DTYPE)
    hp = _dot(p_ref[...], h).astype(o_ref.dtype)
    o_ref[...] = hp.reshape(o_ref.shape)


def _norm_planes(x, gain):
    n, d = x.shape
    tm = min(ROW_TM, n)
    nc = tm // S5_CHUNK
    perm = jnp.asarray(_plane_perm(tm), MXU_DTYPE)
    return pl.pallas_call(
        _norm_planes_kernel,
        out_shape=jax.ShapeDtypeStruct((S5_CHUNK, n // S5_CHUNK, d), MXU_DTYPE),
        grid=(n // tm,),
        in_specs=[pl.BlockSpec((tm, d), lambda i: (i, 0)),
                  pl.BlockSpec((1, d), lambda i: (0, 0)),
                  pl.BlockSpec((tm, tm), lambda i: (0, 0))],
        out_specs=pl.BlockSpec((S5_CHUNK, nc, d), lambda i: (0, i, 0)),
        compiler_params=_cparams(("parallel",)),
        name="norm_planes",
    )(x, gain.reshape(1, d), perm)


def _glu_planes_kernel(x_ref, g_ref, pt_ref, w_ref, o_ref):
    d = x_ref.shape[1]
    gp = g_ref[...].reshape(x_ref.shape)
    g = _dot(pt_ref[...], gp).astype(MXU_DTYPE)
    vg = _dot(g, w_ref[...])
    o_ref[...] = x_ref[...] + vg[:, :d] * jax.nn.sigmoid(vg[:, d:])


def _glu_planes(x, g_t, w):
    n, d = x.shape
    tm = min(ROW_TM, n)
    nc = tm // S5_CHUNK
    perm_t = jnp.asarray(_plane_perm(tm).T, MXU_DTYPE)
    return pl.pallas_call(
        _glu_planes_kernel,
        out_shape=jax.ShapeDtypeStruct((n, d), F32),
        grid=(n // tm,),
        in_specs=[pl.BlockSpec((tm, d), lambda i: (i, 0)),
                  pl.BlockSpec((S5_CHUNK, nc, d), lambda i: (0, i, 0)),
                  pl.BlockSpec((tm, tm), lambda i: (0, 0)),
                  pl.BlockSpec((d, 2 * d), lambda i: (0, 0))],
        out_specs=pl.BlockSpec((tm, d), lambda i: (i, 0)),
        compiler_params=_cparams(("parallel",)),
        name="glu",
    )(x, g_t, perm_t, w)


def _head_perm(n_heads):
    n = np.arange(n_heads * HEAD_DIM)
    pb, r = n // LANES, n % LANES
    half, r2 = r // 64, r % 64
    hl, dp = r2 // 32, r2 % 32
    return (2 * pb + hl) * HEAD_DIM + 32 * half + dp


def _lane_head_mask(shape, hl):
    lane = lax.broadcasted_iota(I32, shape, len(shape) - 1)
    return ((lane % 64) // 32) == hl


def _proj_kernel(x_ref, g_ref, w_ref, wvt_ref, gq_ref, gk_ref, cos_ref, sin_ref, hm_ref,
                 q_ref, k_ref, vt_ref, qi_ref, ki_ref, wi_ref, *, d, dqi, att_scale, w_scale):
    h = _rms(x_ref[...], g_ref[...]).astype(MXU_DTYPE)
    cos, sin = cos_ref[...], sin_ref[...]
    hm = hm_ref[...]

    def rope(t):
        return t * cos + pltpu.roll(t, 64, 1) * sin

    def headnorm_rope(col0, gain_ref, out_ref, scale):
        t_all = _dot(h, w_ref[:, col0:col0 + d])
        for sb in range(d // MXU_DIM):
            c0 = sb * MXU_DIM
            t = t_all[:, c0:c0 + MXU_DIM]
            sq = t * t
            hi = sq.astype(MXU_DTYPE)
            lo = (sq - hi.astype(F32)).astype(MXU_DTYPE)
            ss = _dot(hi, hm) + _dot(lo, hm)
            tn = t * lax.rsqrt(ss * (1.0 / HEAD_DIM) + EPS) * gain_ref[:, c0:c0 + MXU_DIM]
            for b in range(MXU_DIM // LANES):
                r = rope(tn[:, b * LANES:(b + 1) * LANES])
                if scale != 1.0:
                    r = r * scale
                out_ref[:, c0 + b * LANES:c0 + (b + 1) * LANES] = r.astype(out_ref.dtype)

    headnorm_rope(0, gq_ref, q_ref, att_scale)
    headnorm_rope(d, gk_ref, k_ref, 1.0)
    vt = _dot_nt(wvt_ref[...], h)
    row = lax.broadcasted_iota(I32, vt.shape, 0)
    vt_ref[...] = jnp.where(row % VT_ROWS >= HEAD_DIM, 1.0, vt).astype(vt_ref.dtype)
    c0 = 2 * d
    t = _dot(h, w_ref[:, c0:c0 + dqi])
    for b in range(dqi // LANES):
        qi_ref[:, b * LANES:(b + 1) * LANES] = rope(t[:, b * LANES:(b + 1) * LANES]).astype(qi_ref.dtype)
    c0 += dqi
    t = _dot(h, w_ref[:, c0:c0 + LANES])
    ms = jnp.sum(t * t, axis=-1, keepdims=True) * (0.5 / IDX_DIM)
    ki_ref[...] = rope(t * lax.rsqrt(ms + EPS)).astype(ki_ref.dtype)
    c0 += LANES
    wi_ref[...] = _dot(h, w_ref[:, c0:c0 + LANES]) * w_scale


def _dsa_project(x, gain, w_in, q_gain, k_gain, cos_t, sin_t):
    n, d = x.shape
    dqi = IDX_HEADS * IDX_DIM
    pq = _head_perm(N_HEADS)
    pqi = _head_perm(IDX_HEADS)
    wq = w_in[:, 0:d][:, pq]
    wk = w_in[:, d:2 * d][:, pq]
    wvt = w_in[:, 2 * d:3 * d].T.reshape(N_HEADS, HEAD_DIM, d)
    wvt = jnp.pad(wvt, ((0, 0), (0, VT_PAD), (0, 0))).reshape(N_HEADS * VT_ROWS, d).astype(MXU_DTYPE)
    dvt = N_HEADS * VT_ROWS
    wqi = w_in[:, 3 * d:3 * d + dqi][:, pqi]
    lane = np.arange(LANES)
    wki = w_in[:, 3 * d + dqi:3 * d + dqi + IDX_DIM][:, 32 * (lane // 64) + lane % 32]
    wwi = jnp.pad(w_in[:, 3 * d + dqi + IDX_DIM:], ((0, 0), (0, LANES - IDX_HEADS)))
    w_all = jnp.concatenate([wq, wk, wqi, wki, wwi], axis=1).astype(MXU_DTYPE)

    def lane_gain(g):
        g = g.astype(F32).reshape(1, 2, 1, HEAD_DIM // 2)
        return jnp.broadcast_to(g, (d // LANES, 2, 2, HEAD_DIM // 2)).reshape(1, d)

    gq = lane_gain(q_gain)
    gk = lane_gain(k_gain)
    l2 = np.arange(MXU_DIM)
    hm = ((l2[:, None] // LANES == l2[None, :] // LANES)
          & ((l2[:, None] % 64) // 32 == (l2[None, :] % 64) // 32))
    hm = jnp.asarray(hm, MXU_DTYPE)
    tm = min(PROJ_TM, n)
    nw = w_all.shape[1]
    outs = pl.pallas_call(
        functools.partial(_proj_kernel, d=d, dqi=dqi, att_scale=HEAD_DIM ** -0.5 * LOG2E,
                          w_scale=(IDX_HEADS ** -0.5) * (IDX_DIM ** -0.5)),
        out_shape=[jax.ShapeDtypeStruct((n, d), MXU_DTYPE)] * 2
        + [jax.ShapeDtypeStruct((dvt, n), MXU_DTYPE),
           jax.ShapeDtypeStruct((n, dqi), MXU_DTYPE),
           jax.ShapeDtypeStruct((n, LANES), MXU_DTYPE),
           jax.ShapeDtypeStruct((n, LANES), F32)],
        grid=(n // tm,),
        in_specs=[
            pl.BlockSpec((tm, d), lambda i: (i, 0)),
            pl.BlockSpec((1, d), lambda i: (0, 0)),
            pl.BlockSpec((d, nw), lambda i: (0, 0)),
            pl.BlockSpec((dvt, d), lambda i: (0, 0)),
            pl.BlockSpec((1, d), lambda i: (0, 0)),
            pl.BlockSpec((1, d), lambda i: (0, 0)),
            pl.BlockSpec((tm, LANES), lambda i: (i, 0)),
            pl.BlockSpec((tm, LANES), lambda i: (i, 0)),
            pl.BlockSpec((MXU_DIM, MXU_DIM), lambda i: (0, 0)),
        ],
        out_specs=[pl.BlockSpec((tm, d), lambda i: (i, 0))] * 2
        + [pl.BlockSpec((dvt, tm), lambda i: (0, i)),
           pl.BlockSpec((tm, dqi), lambda i: (i, 0)),
           pl.BlockSpec((tm, LANES), lambda i: (i, 0)),
           pl.BlockSpec((tm, LANES), lambda i: (i, 0))],
        compiler_params=_cparams(("parallel",)),
        name="dsa_proj",
    )(x, gain.reshape(1, d), w_all, wvt, gq, gk, cos_t, sin_t, hm)
    return outs


KEY16_NEG_INF = -32641
KEY16_POS_INF = 32640
L16_MIN = -32768
ROW_NEVER = 32767


def _select16_kernel(qi_ref, wit_ref, ki_ref, mask_ref, h16_ref, l16_ref, lo_ref, hi_ref, clo_ref,
                     chi_ref, base_ref, *, tq, tk, topk, unroll16):
    qb = pl.program_id(0)
    n_all = mask_ref.shape[0]
    nkt = (qb * tq + tq - 1) // tk + 1
    nrow = nkt * tk
    rows16 = 2 * SUBLANES * unroll16
    ntrip = nrow // rows16

    qm = []
    for hh in range(IDX_HEADS):
        blk = qi_ref[:, (hh // 2) * LANES:(hh // 2 + 1) * LANES]
        qm.append(jnp.where(_lane_head_mask(blk.shape, hh % 2), blk, jnp.zeros_like(blk)))
    qm = jnp.concatenate(qm, axis=0)
    wt = wit_ref[...]
    causal_slack = (qb * tq + lax.broadcasted_iota(I32, (tk, tq), 1)
                    - lax.broadcasted_iota(I32, (tk, tq), 0))

    def score_tile(kt, _):
        k0 = pl.multiple_of(kt * tk, tk)
        s = _dot_nt(ki_ref[pl.ds(k0, tk), :], qm)
        acc = wt[0:1, :] * jnp.maximum(s[:, 0:tq], 0.0)
        for hh in range(1, IDX_HEADS):
            acc = acc + wt[hh:hh + 1, :] * jnp.maximum(s[:, hh * tq:(hh + 1) * tq], 0.0)
        acc = jnp.where(acc == 0.0, 0.0, acc)
        sc = jnp.where(k0 <= causal_slack, acc, -jnp.inf)
        bits = lax.bitcast_convert_type(sc, I32)
        key = bits ^ ((bits >> 31) & 0x7FFFFFFF)
        h16_ref[pl.ds(k0, tk), :] = (key >> 16).astype(I16)
        l16_ref[pl.ds(k0, tk), :] = (key ^ 0x8000).astype(I16)
        return 0

    lax.fori_loop(0, nkt, score_tile, 0)

    one16 = jnp.ones((), I16)
    zero16 = jnp.zeros((), I16)
    sub16 = (lax.broadcasted_iota(I32, (unroll16, 2 * SUBLANES, tq), 0) * (2 * SUBLANES)
             + lax.broadcasted_iota(I32, (unroll16, 2 * SUBLANES, tq), 1)).astype(I16)

    def pack16(v):
        return jnp.concatenate([v, v], axis=0).astype(I16)

    def count(pred):
        def body(i, acc):
            r0 = pl.multiple_of(i * rows16, rows16)
            hb = h16_ref[pl.ds(r0, rows16), :].reshape(unroll16, 2 * SUBLANES, tq)
            lb = l16_ref[pl.ds(r0, rows16), :].reshape(unroll16, 2 * SUBLANES, tq)
            ones = jnp.where(pred(hb, lb), one16, zero16)
            part = ones[0]
            for u in range(1, unroll16):
                part = part + ones[u]
            return acc + part.astype(I32)
        acc = lax.fori_loop(0, ntrip, body, jnp.zeros((2 * SUBLANES, tq), I32))
        return jnp.broadcast_to(jnp.sum(acc, axis=0, keepdims=True), (SUBLANES, tq))

    def bisect(count_ge, n_steps):
        def step(_, carry):
            lo, hi = lo_ref[...], hi_ref[...]
            mid = (lo + hi) >> 1
            active = mid != lo
            cnt = count_ge(pack16(mid)) + base_ref[...]
            ge = cnt >= topk
            up = active & ge
            dn = active & jnp.logical_not(ge)
            lo_ref[...] = jnp.where(up, mid, lo)
            hi_ref[...] = jnp.where(dn, mid, hi)
            clo_ref[...] = jnp.where(up, cnt, clo_ref[...])
            chi_ref[...] = jnp.where(dn, cnt, chi_ref[...])
            return carry
        lax.fori_loop(0, n_steps, step, 0)

    lo_ref[...] = jnp.full(lo_ref.shape, KEY16_NEG_INF, I32)
    hi_ref[...] = jnp.full(hi_ref.shape, KEY16_POS_INF + 1, I32)
    clo_ref[...] = jnp.full(clo_ref.shape, 1, I32) * nrow
    chi_ref[...] = jnp.zeros(chi_ref.shape, I32)
    base_ref[...] = jnp.zeros(base_ref.shape, I32)
    bisect(lambda c: count(lambda hb, lb: hb >= c[None]), 16)

    t_hi = lo_ref[...]
    none_finite = t_hi == KEY16_NEG_INF
    t_hi16 = pack16(t_hi)

    def bucket_tile(i, _):
        r0 = pl.multiple_of(i * rows16, rows16)
        hb = h16_ref[pl.ds(r0, rows16), :].reshape(unroll16, 2 * SUBLANES, tq)
        lb = l16_ref[pl.ds(r0, rows16), :].reshape(unroll16, 2 * SUBLANES, tq)
        lb = jnp.where(hb == t_hi16[None], lb, jnp.full((), L16_MIN, I16))
        l16_ref[pl.ds(r0, rows16), :] = lb.reshape(rows16, tq)
        return 0

    lax.fori_loop(0, ntrip, bucket_tile, 0)
    base_ref[...] = chi_ref[...]
    lo_ref[...] = jnp.full(lo_ref.shape, L16_MIN, I32)
    hi_ref[...] = jnp.full(hi_ref.shape, -L16_MIN, I32)
    bisect(lambda c: count(lambda hb, lb: lb >= c[None]), 16)
    t_lo = jnp.where(none_finite, -L16_MIN - 1, lo_ref[...])

    tie = (clo_ref[...] > topk) & jnp.logical_not(none_finite)
    any_tie = jnp.max(tie.astype(I32))
    t_lo16 = pack16(t_lo)

    def rank_tile(i, _):
        r0 = pl.multiple_of(i * rows16, rows16)
        hb = h16_ref[pl.ds(r0, rows16), :].reshape(unroll16, 2 * SUBLANES, tq)
        lb = l16_ref[pl.ds(r0, rows16), :].reshape(unroll16, 2 * SUBLANES, tq)
        in_bucket = hb == t_hi16[None]
        rows = sub16 + jnp.full((2 * SUBLANES, tq), r0, I32).astype(I16)[None]
        rank = jnp.where(in_bucket & (lb == t_lo16[None]), rows,
                         jnp.where(in_bucket & (lb > t_lo16[None]),
                                   jnp.full((), -1, I16), jnp.full((), ROW_NEVER, I16)))
        l16_ref[pl.ds(r0, rows16), :] = rank.reshape(rows16, tq)
        return 0

    lax.fori_loop(0, ntrip, rank_tile, 0)

    @pl.when(any_tie == 0)
    def _():
        hi_ref[...] = jnp.full(hi_ref.shape, n_all, I32)

    @pl.when(any_tie > 0)
    def _():
        chi_ref[...] = topk - base_ref[...]
        hi_ref[...] = jnp.full(hi_ref.shape, 1, I32) * (nrow - 1)
        clo_ref[...] = jnp.full(clo_ref.shape, -1, I32)

        def tie_step(_, c):
            jl, jh = clo_ref[...], hi_ref[...]
            mid = jl + ((jh - jl) >> 1)
            active = (jh - jl) > 1
            mid16 = pack16(mid)
            cnt = count(lambda hb, lb: lb <= mid16[None])
            ok = cnt >= chi_ref[...]
            hi_ref[...] = jnp.where(active & ok, mid, jh)
            clo_ref[...] = jnp.where(active & jnp.logical_not(ok), mid, jl)
            return c

        n_halvings = 0
        for j in range(int(math.ceil(math.log2(n_all // tk))) + 1):
            n_halvings = n_halvings + (((nkt - 1) >> j) > 0).astype(I32)
        lax.fori_loop(0, n_halvings + int(math.log2(tk)) + 1, tie_step, 0)
        hi_ref[...] = jnp.where(tie, hi_ref[...], n_all)

    last16 = pack16(hi_ref[...])
    g16 = 2 * SUBLANES

    def mask_tile(kt, _):
        k0 = pl.multiple_of(kt * tk, tk)
        hb = h16_ref[pl.ds(k0, tk), :].reshape(tk // g16, g16, tq)
        rank = l16_ref[pl.ds(k0, tk), :].reshape(tk // g16, g16, tq)
        sel = ((hb > t_hi16[None]) | (rank <= last16[None])) & (hb > KEY16_NEG_INF)
        sel = jnp.where(sel, one16, zero16).reshape(tk, tq)
        mask_ref[pl.ds(k0, tk), :] = sel.astype(mask_ref.dtype)
        return 0

    lax.fori_loop(0, nkt, mask_tile, 0)

    def zero_tile(kt, _):
        k0 = pl.multiple_of(kt * tk, tk)
        mask_ref[pl.ds(k0, tk), :] = jnp.zeros((tk, tq), mask_ref.dtype)
        return 0

    lax.fori_loop(nkt, n_all // tk, zero_tile, 0)


def _dsa_select16(qi, wit, ki, topk):
    n = qi.shape[0]
    tq = min(SEL_TQ, n)
    tk = min(SEL_TK, n)
    assert tk % (2 * SUBLANES * SEL_UNROLL16) == 0 and n < -L16_MIN
    return pl.pallas_call(
        functools.partial(_select16_kernel, tq=tq, tk=tk, topk=topk, unroll16=SEL_UNROLL16),
        out_shape=jax.ShapeDtypeStruct((n, n), jnp.int8),
        grid=(n // tq,),
        in_specs=[pl.BlockSpec((tq, qi.shape[1]), lambda i: (i, 0)),
                  pl.BlockSpec((IDX_HEADS, tq), lambda i: (0, i)),
                  pl.BlockSpec((n, LANES), lambda i: (0, 0))],
        out_specs=pl.BlockSpec((n, tq), lambda i: (0, i)),
        scratch_shapes=[pltpu.VMEM((n, tq), I16), pltpu.VMEM((n, tq), I16)]
        + [pltpu.VMEM((SUBLANES, tq), I32)] * 5,
        compiler_params=_cparams(("parallel",)),
        name="dsa_select",
    )(qi, wit, ki)


def _attn_t_kernel(qb_tab, kt_tab, q_ref, k_ref, vt_ref, mask_ref, o_ref, acc_ref, m_ref,
                   *, tq, tk):
    step = pl.program_id(0)
    kt = kt_tab[step]

    @pl.when(kt == 0)
    def _():
        acc_ref[...] = jnp.zeros_like(acc_ref)
        m_ref[...] = jnp.full(m_ref.shape, NEG_BIG, MXU_DTYPE).astype(F32)

    bias = jnp.where(mask_ref[...].astype(I32) != 0, 0.0, NEG_BIG).astype(MXU_DTYPE)

    def logits(hd):
        blk = hd // 2
        qblk = q_ref[:, blk * LANES:(blk + 1) * LANES]
        qm = jnp.where(_lane_head_mask(qblk.shape, hd % 2), qblk, jnp.zeros_like(qblk))
        return _dot_nt(k_ref[:, blk * LANES:(blk + 1) * LANES], qm).astype(MXU_DTYPE) + bias

    s_next = logits(0)
    for hd in range(N_HEADS):
        r0 = hd * VT_ROWS
        s = s_next
        if hd + 1 < N_HEADS:
            s_next = logits(hd + 1)
        m_old = m_ref[hd:hd + 1, :]
        m_new = jnp.maximum(m_old, jnp.max(s, axis=0, keepdims=True).astype(F32))
        alpha = jnp.exp2(m_old - m_new)
        p = jnp.exp2(s - m_new.astype(MXU_DTYPE))
        m_ref[hd:hd + 1, :] = m_new
        pv = _dot(vt_ref[r0:r0 + VT_ROWS, :], p)
        acc_ref[r0:r0 + VT_ROWS, :] = acc_ref[r0:r0 + VT_ROWS, :] * alpha + pv

    last = kt == ((qb_tab[step] + 1) * tq - 1) // tk

    @pl.when(last)
    def _():
        for hd in range(N_HEADS):
            r0 = hd * VT_ROWS
            o_ref[hd * HEAD_DIM:(hd + 1) * HEAD_DIM, :] = (
                acc_ref[r0:r0 + HEAD_DIM, :] / acc_ref[r0 + HEAD_DIM:r0 + HEAD_DIM + 1, :]
            ).astype(o_ref.dtype)


def _dsa_attend_t(q, k, vt, mask_t):
    n, d = q.shape
    tq = min(ATT_TQ, n)
    tk = min(ATT_TK, n)
    pairs = [(qb, kt) for qb in range(n // tq) for kt in range(((qb + 1) * tq - 1) // tk + 1)]
    qb_tab = jnp.asarray([p[0] for p in pairs], I32)
    kt_tab = jnp.asarray([p[1] for p in pairs], I32)
    grid_spec = pltpu.PrefetchScalarGridSpec(
        num_scalar_prefetch=2,
        grid=(len(pairs),),
        in_specs=[
            pl.BlockSpec((tq, d), lambda i, qt, kt: (qt[i], 0)),
            pl.BlockSpec((tk, d), lambda i, qt, kt: (kt[i], 0)),
            pl.BlockSpec((vt.shape[0], tk), lambda i, qt, kt: (0, kt[i])),
            pl.BlockSpec((tk, tq), lambda i, qt, kt: (kt[i], qt[i])),
        ],
        out_specs=pl.BlockSpec((d, tq), lambda i, qt, kt: (0, qt[i])),
        scratch_shapes=[pltpu.VMEM((vt.shape[0], tq), F32), pltpu.VMEM((N_HEADS, tq), F32)],
    )
    return pl.pallas_call(
        functools.partial(_attn_t_kernel, tq=tq, tk=tk),
        out_shape=jax.ShapeDtypeStruct((d, n), MXU_DTYPE),
        grid_spec=grid_spec,
        compiler_params=_cparams(("arbitrary",)),
        name="dsa_attend",
    )(qb_tab, kt_tab, q, k, vt, mask_t)


def _out_t_kernel(x_ref, at_ref, w_ref, o_ref):
    o_ref[...] = x_ref[...] + lax.dot_general(
        at_ref[...], w_ref[...], (((0,), (0,)), ((), ())), preferred_element_type=F32)


def _out_residual_t(x, at, w):
    n, d = x.shape
    tm = min(ROW_TM, n)
    return pl.pallas_call(
        _out_t_kernel,
        out_shape=jax.ShapeDtypeStruct((n, d), F32),
        grid=(n // tm,),
        in_specs=[pl.BlockSpec((tm, d), lambda i: (i, 0)),
                  pl.BlockSpec((d, tm), lambda i: (0, i)),
                  pl.BlockSpec((d, d), lambda i: (0, 0))],
        out_specs=pl.BlockSpec((tm, d), lambda i: (i, 0)),
        compiler_params=_cparams(("parallel",)),
        name="attn_out",
    )(x, at, w)


def _rope_lane_tables(length):
    inv_freq = ROPE_THETA ** (-jnp.arange(0, HEAD_DIM, 2, dtype=F32) / HEAD_DIM)
    ang = jnp.arange(length, dtype=F32)[:, None] * inv_freq[None, :]
    lane = np.arange(LANES)
    cos_t = jnp.cos(ang)[:, lane % 32]
    sin_t = jnp.sin(ang)[:, lane % 32] * jnp.asarray(np.where(lane < 64, -1.0, 1.0), F32)
    return cos_t, sin_t


def kernel(x, s5_lambda_re, s5_lambda_im, s5_log_dt, s5_b_re, s5_b_im, s5_c_re, s5_c_im, s5_d, s5_w_glu, dsa_w_in, dsa_q_norm, dsa_k_norm, dsa_w_o, ffn_w_gate_up, ffn_w_down, norm_mix, norm_ffn):
    bsz, length, d = x.shape
    depth = norm_mix.shape[0]
    topk = min(TOPK_MAX, length // 4)
    cos_t, sin_t = _rope_lane_tables(length)
    wgu = ffn_w_gate_up.astype(MXU_DTYPE)
    wd = ffn_w_down.astype(MXU_DTYPE)
    tables = jax.vmap(_s5_tables)(s5_lambda_re, s5_lambda_im, s5_log_dt, s5_b_re, s5_b_im,
                                  s5_c_re, s5_c_im)
    outs = []
    for b in range(bsz):
        xs = x[b].astype(F32)
        for i in range(depth):
            j = i // 2
            if i % 2 == 0:
                h_t = _norm_planes(xs, norm_mix[i])
                g_t = _s5_scan(h_t, tables, s5_d[j], j)
                xs = _glu_planes(xs, g_t, s5_w_glu[j].astype(MXU_DTYPE))
                xs = _ffn(xs, norm_ffn[i], wgu, wd, i)
            else:
                q, k, vt, qi, ki, wi = _dsa_project(xs, norm_mix[i], dsa_w_in[j], dsa_q_norm[j],
                                                    dsa_k_norm[j], cos_t, sin_t)
                mask_t = _dsa_select16(qi, wi[:, :IDX_HEADS].T, ki, topk)
                att_t = _dsa_attend_t(q, k, vt, mask_t)
                xs = _out_residual_t(xs, att_t, dsa_w_o[j].astype(MXU_DTYPE))
                xs = _ffn(xs, norm_ffn[i], wgu, wd, i)
        outs.append(xs)
    return jnp.stack(outs, axis=0).astype(x.dtype)
```

```python
import functools
import math

import jax
import jax.numpy as jnp
import numpy as np
from jax import lax
from jax.experimental import pallas as pl
from jax.experimental.pallas import tpu as pltpu

F32 = jnp.float32
BF16 = jnp.bfloat16
I32 = jnp.int32
I16 = jnp.int16
MXU_DTYPE = BF16

S5_GROUP = 16
S5_STATE = 64
N_HEADS = 16
HEAD_DIM = 64
IDX_HEADS = 8
IDX_DIM = 64
TOPK_MAX = 256
ROPE_THETA = 10000.0
EPS = 1e-6

LANES = 128
SUBLANES = 8
MXU_DIM = 256
VMEM_LIMIT = 56 * 1024 * 1024

S5_CHUNK = 16
S5_SLAB_GROUPS = LANES // S5_GROUP
NEG_BIG = -1e30
LOG2E = math.log2(math.e)
VT_PAD = 16
VT_ROWS = HEAD_DIM + VT_PAD

ROW_TM = 512
S5_TC = 512
PROJ_TM = 512
SEL_TQ = 256
SEL_TK = 512
SEL_UNROLL16 = 32
SEL_UNROLL = 64
ATT_TQ = 512
ATT_TK = 512


def _cparams(sem):
    return pltpu.CompilerParams(dimension_semantics=sem, vmem_limit_bytes=VMEM_LIMIT)


def _rms(x, gain=None):
    y = x * lax.rsqrt(jnp.mean(x * x, axis=-1, keepdims=True) + EPS)
    return y if gain is None else y * gain


def _dot(a, b):
    return jnp.dot(a, b, preferred_element_type=F32)


def _dot_nt(a, b):
    return lax.dot_general(a, b, (((1,), (1,)), ((), ())), preferred_element_type=F32)


def _ffn_kernel(x_ref, g_ref, wgu_ref, wd_ref, o_ref, acc_ref, *, d_ff, fc):
    x = x_ref[...]
    h = _rms(x, g_ref[...]).astype(MXU_DTYPE)
    for c in range(d_ff // fc):
        g = _dot(h, wgu_ref[:, c * fc:(c + 1) * fc])
        u = _dot(h, wgu_ref[:, d_ff + c * fc:d_ff + (c + 1) * fc])
        a = (g * jax.nn.sigmoid(g) * u).astype(MXU_DTYPE)
        d = _dot(a, wd_ref[c * fc:(c + 1) * fc, :])
        if c == 0:
            acc_ref[...] = d
        else:
            acc_ref[...] += d
    o_ref[...] = x + acc_ref[...]


def _ffn(x, gain, wgu, wd, layer):
    n, d = x.shape
    d_ff = wd.shape[1]
    tm = min(ROW_TM, n)
    fc = MXU_DIM
    return pl.pallas_call(
        functools.partial(_ffn_kernel, d_ff=d_ff, fc=fc),
        out_shape=jax.ShapeDtypeStruct((n, d), F32),
        grid=(n // tm,),
        in_specs=[
            pl.BlockSpec((tm, d), lambda i: (i, 0)),
            pl.BlockSpec((1, d), lambda i: (0, 0)),
            pl.BlockSpec((None, d, 2 * d_ff), lambda i: (layer, 0, 0)),
            pl.BlockSpec((None, d_ff, d), lambda i: (layer, 0, 0)),
        ],
        out_specs=pl.BlockSpec((tm, d), lambda i: (i, 0)),
        scratch_shapes=[pltpu.VMEM((tm, d), F32)],
        compiler_params=_cparams(("parallel",)),
        name="ffn",
    )(x, gain.reshape(1, d), wgu, wd)


def _s5_tables(lam_re, lam_im, log_dt, b_re, b_im, c_re, c_im):
    g, p = lam_re.shape
    h = S5_GROUP
    nsl = g // S5_SLAB_GROUPS
    sg = S5_SLAB_GROUPS
    t = S5_CHUNK
    lam_re, lam_im, log_dt = lam_re.astype(F32), lam_im.astype(F32), log_dt.astype(F32)
    b_re, b_im, c_re, c_im = (a.astype(F32) for a in (b_re, b_im, c_re, c_im))
    dt = jnp.exp(log_dt)[:, None]

    def apow(k):
        k = jnp.asarray(k, F32).reshape((-1, 1, 1))
        mag = jnp.exp(lam_re[None] * dt[None] * k)
        ang = lam_im[None] * dt[None] * k
        return mag * jnp.cos(ang), mag * jnp.sin(ang)

    ar, ai = apow([1.0])
    ar, ai = ar[0], ai[0]
    den = lam_re * lam_re + lam_im * lam_im
    nr, ni = ar - 1.0, ai
    qr = (nr * lam_re + ni * lam_im) / den
    qi = (ni * lam_re - nr * lam_im) / den
    bbr = qr[..., None] * b_re - qi[..., None] * b_im
    bbi = qr[..., None] * b_im + qi[..., None] * b_re

    pr, pi = apow(np.arange(t + 1))
    mr = c_re[None] * pr[:, :, None, :] - c_im[None] * pi[:, :, None, :]
    mi = c_re[None] * pi[:, :, None, :] + c_im[None] * pr[:, :, None, :]

    bbr_t, bbi_t = bbr.transpose(0, 2, 1), bbi.transpose(0, 2, 1)
    kk = jnp.sum(mr[:t, :, :, None, :] * bbr_t[None, :, None, :, :]
                 - mi[:t, :, :, None, :] * bbi_t[None, :, None, :, :], axis=-1)
    kp = jnp.concatenate([jnp.zeros_like(kk[:1]), kk], axis=0)

    def compact(x, row_dims, col_dims):
        r0, r2 = row_dims
        c0, c1 = col_dims
        return x.reshape(nsl, t // 2, r0 * sg * r2, c0 * c1).astype(MXU_DTYPE)

    dl = np.arange(t // 2)[:, None, None]
    sl = np.arange(2)[None, :, None]
    jl = np.arange(2)[None, None, :]
    idx = 2 * dl + jl - sl + 1
    kg = kp[idx]
    kg = kg.reshape(t // 2, 2, 2, nsl, sg, h, h)
    tp = compact(kg.transpose(3, 0, 1, 4, 6, 2, 5), (2, h), (2, h))

    prs, pis = pr[t - 1::-1][:t], pi[t - 1::-1][:t]
    er = prs[..., None] * bbr[None] - pis[..., None] * bbi[None]
    ei = prs[..., None] * bbi[None] + pis[..., None] * bbr[None]
    bf = jnp.stack([er, ei], axis=2)
    bf = bf.reshape(t // 2, 2, nsl, sg, 2, p, h)
    bz = compact(bf.transpose(2, 0, 1, 3, 6, 4, 5), (2, h), (2, p))

    cf = jnp.stack([mr[1:], -mi[1:]], axis=2)
    cf = cf.reshape(t // 2, 2, nsl, sg, 2, h, p)
    cz = compact(cf.transpose(2, 0, 4, 3, 6, 1, 5), (2, p), (2, h))

    def slab_state(re, im):
        k = re.shape[0]
        x = jnp.stack([re, im], axis=1).reshape(k, 2, nsl, sg * p)
        return x.transpose(2, 0, 1, 3).reshape(nsl, k, 2 * sg * p)

    ad = slab_state(*apow([t * 1.0, t * 2.0, t * 4.0]))
    ap8 = slab_state(*apow(t * (np.arange(SUBLANES) + 1.0)))
    return tp, bz, cz, ad, ap8


def _s5_expander(row_dims, col_dims):
    sg = S5_SLAB_GROUPS
    r0, r2 = row_dims
    c0, c1 = col_dims
    ci = np.arange(c0 * sg * c1)
    src = (ci // (sg * c1)) * c1 + ci % c1
    spread = np.arange(c0 * c1)[:, None] == src[None, :]
    ri = np.arange(r0 * sg * r2)
    keep = ((ri // r2) % sg)[:, None] == ((ci // c1) % sg)[None, :]
    return jnp.asarray(spread, MXU_DTYPE), jnp.asarray(keep, MXU_DTYPE)


def _s5_kernel(h_ref, tpc_ref, bzc_ref, czc_ref, sph_ref, spp_ref, ktp_ref, kbz_ref, kcz_ref,
               ad_ref, ap8_ref, dsk_ref, o_ref,
               carry_ref, z_ref, xp_ref, tp_ref, bz_ref, cz_ref, *, tc):
    half = z_ref.shape[1] // 2
    npair = S5_CHUNK // 2

    @pl.when(pl.program_id(1) == 0)
    def _():
        carry_ref[...] = jnp.zeros_like(carry_ref)
        for z in range(npair):
            tp_ref[z] = (_dot(tpc_ref[z], sph_ref[...]) * ktp_ref[...]).astype(tp_ref.dtype)
            bz_ref[z] = (_dot(bzc_ref[z], spp_ref[...]) * kbz_ref[...]).astype(bz_ref.dtype)
            cz_ref[z] = (_dot(czc_ref[z], sph_ref[...]) * kcz_ref[...]).astype(cz_ref.dtype)

    u = [jnp.concatenate([h_ref[2 * s], h_ref[2 * s + 1]], axis=1) for s in range(npair)]

    z = _dot(u[0], bz_ref[0])
    for s in range(1, npair):
        z = z + _dot(u[s], bz_ref[s])
    z_ref[...] = z

    row = lax.broadcasted_iota(I32, (SUBLANES, half), 0)
    pr8, pi8 = ap8_ref[:, :half], ap8_ref[:, half:]

    def tile_step(t, carry):
        r0 = pl.multiple_of(t * SUBLANES, SUBLANES)
        zt = z_ref[pl.ds(r0, SUBLANES), :]
        xr, xi = zt[:, :half], zt[:, half:]
        for k, d in enumerate((1, 2, 4)):
            a = ad_ref[k:k + 1, :]
            a_r, a_i = a[:, :half], a[:, half:]
            sr = jnp.where(row >= d, pltpu.roll(xr, d, 0), 0.0)
            si = jnp.where(row >= d, pltpu.roll(xi, d, 0), 0.0)
            xr, xi = xr + a_r * sr - a_i * si, xi + a_r * si + a_i * sr
        cr, ci = carry[:, :half], carry[:, half:]
        xr, xi = xr + pr8 * cr - pi8 * ci, xi + pr8 * ci + pi8 * cr
        xpr = jnp.where(row >= 1, pltpu.roll(xr, 1, 0), cr)
        xpi = jnp.where(row >= 1, pltpu.roll(xi, 1, 0), ci)
        xp_ref[pl.ds(r0, SUBLANES), :] = jnp.concatenate([xpr, xpi], axis=1)
        return jnp.concatenate([xr[SUBLANES - 1:], xi[SUBLANES - 1:]], axis=1)

    carry_ref[...] = lax.fori_loop(0, tc // SUBLANES, tile_step, carry_ref[...])

    xp = xp_ref[...].astype(MXU_DTYPE)
    dsk = dsk_ref[...]
    for i in range(npair):
        y = _dot(xp, cz_ref[i])
        for s in range(i + 1):
            y = y + _dot(u[s], tp_ref[i - s])
        for jl in range(2):
            j = 2 * i + jl
            yj = y[:, jl * LANES:(jl + 1) * LANES] + dsk * h_ref[j].astype(F32)
            o_ref[j] = jax.nn.gelu(yj).astype(o_ref.dtype)


def _s5_scan(h_t, tables, d_skip, layer):
    tpc, bzc, czc, ad, ap8 = tables
    t, c, d = h_t.shape
    nsl = d // LANES
    tc = min(S5_TC, c)
    npair = S5_CHUNK // 2
    sg, hh, pp = S5_SLAB_GROUPS, S5_GROUP, S5_STATE
    sp_h, keep_tp = _s5_expander((2, hh), (2, hh))
    sp_p, keep_bz = _s5_expander((2, hh), (2, pp))
    _, keep_cz = _s5_expander((2, pp), (2, hh))
    nu, st = 2 * sg * hh, 2 * sg * pp
    dsk = d_skip.astype(F32).reshape(nsl, 1, LANES)
    const = lambda a: pl.BlockSpec(a.shape, lambda b, i: (0,) * a.ndim)
    per_slab = lambda a: pl.BlockSpec((None, None) + a.shape[2:],
                                      lambda b, i: (layer, b) + (0,) * (a.ndim - 2))
    return pl.pallas_call(
        functools.partial(_s5_kernel, tc=tc),
        out_shape=jax.ShapeDtypeStruct((t, c, d), MXU_DTYPE),
        grid=(nsl, c // tc),
        in_specs=[
            pl.BlockSpec((t, tc, LANES), lambda b, i: (0, i, b)),
            per_slab(tpc), per_slab(bzc), per_slab(czc),
            const(sp_h), const(sp_p), const(keep_tp), const(keep_bz), const(keep_cz),
            per_slab(ad), per_slab(ap8),
            pl.BlockSpec((None, 1, LANES), lambda b, i: (b, 0, 0)),
        ],
        out_specs=pl.BlockSpec((t, tc, LANES), lambda b, i: (0, i, b)),
        scratch_shapes=[pltpu.VMEM((1, st), F32), pltpu.VMEM((tc, st), F32),
                        pltpu.VMEM((tc, st), F32),
                        pltpu.VMEM((npair, nu, nu), MXU_DTYPE),
                        pltpu.VMEM((npair, nu, st), MXU_DTYPE),
                        pltpu.VMEM((npair, st, nu), MXU_DTYPE)],
        compiler_params=_cparams(("arbitrary", "arbitrary")),
        name="s5_scan",
    )(h_t, tpc, bzc, czc, sp_h, sp_p, keep_tp, keep_bz, keep_cz, ad, ap8, dsk)


def _plane_perm(rows):
    nc = rows // S5_CHUNK
    p = np.zeros((rows, rows), np.float32)
    c, s = np.meshgrid(np.arange(nc), np.arange(S5_CHUNK), indexing="ij")
    p[(s * nc + c).ravel(), (c * S5_CHUNK + s).ravel()] = 1.0
    return p


def _norm_planes_kernel(x_ref, g_ref, p_ref, o_ref):
    h = _rms(x_ref[...], g_ref[...]).astype(MXU_DTYPE)
    hp = _dot(p_ref[...], h).astype(o_ref.dtype)
    o_ref[...] = hp.reshape(o_ref.shape)


def _norm_planes(x, gain):
    n, d = x.shape
    tm = min(ROW_TM, n)
    nc = tm // S5_CHUNK
    perm = jnp.asarray(_plane_perm(tm), MXU_DTYPE)
    return pl.pallas_call(
        _norm_planes_kernel,
        out_shape=jax.ShapeDtypeStruct((S5_CHUNK, n // S5_CHUNK, d), MXU_DTYPE),
        grid=(n // tm,),
        in_specs=[pl.BlockSpec((tm, d), lambda i: (i, 0)),
                  pl.BlockSpec((1, d), lambda i: (0, 0)),
                  pl.BlockSpec((tm, tm), lambda i: (0, 0))],
        out_specs=pl.BlockSpec((S5_CHUNK, nc, d), lambda i: (0, i, 0)),
        compiler_params=_cparams(("parallel",)),
        name="norm_planes",
    )(x, gain.reshape(1, d), perm)


def _glu_planes_kernel(x_ref, g_ref, pt_ref, w_ref, o_ref):
    d = x_ref.shape[1]
    gp = g_ref[...].reshape(x_ref.shape)
    g = _dot(pt_ref[...], gp).astype(MXU_DTYPE)
    vg = _dot(g, w_ref[...])
    o_ref[...] = x_ref[...] + vg[:, :d] * jax.nn.sigmoid(vg[:, d:])


def _glu_planes(x, g_t, w):
    n, d = x.shape
    tm = min(ROW_TM, n)
    nc = tm // S5_CHUNK
    perm_t = jnp.asarray(_plane_perm(tm).T, MXU_DTYPE)
    return pl.pallas_call(
        _glu_planes_kernel,
        out_shape=jax.ShapeDtypeStruct((n, d), F32),
        grid=(n // tm,),
        in_specs=[pl.BlockSpec((tm, d), lambda i: (i, 0)),
                  pl.BlockSpec((S5_CHUNK, nc, d), lambda i: (0, i, 0)),
                  pl.BlockSpec((tm, tm), lambda i: (0, 0)),
                  pl.BlockSpec((d, 2 * d), lambda i: (0, 0))],
        out_specs=pl.BlockSpec((tm, d), lambda i: (i, 0)),
        compiler_params=_cparams(("parallel",)),
        name="glu",
    )(x, g_t, perm_t, w)


def _head_perm(n_heads):
    n = np.arange(n_heads * HEAD_DIM)
    pb, r = n // LANES, n % LANES
    half, r2 = r // 64, r % 64
    hl, dp = r2 // 32, r2 % 32
    return (2 * pb + hl) * HEAD_DIM + 32 * half + dp


def _lane_head_mask(shape, hl):
    lane = lax.broadcasted_iota(I32, shape, len(shape) - 1)
    return ((lane % 64) // 32) == hl


def _proj_kernel(x_ref, g_ref, w_ref, wvt_ref, gq_ref, gk_ref, cos_ref, sin_ref, hm_ref,
                 q_ref, k_ref, vt_ref, qi_ref, ki_ref, wi_ref, *, d, dqi, att_scale, w_scale):
    h = _rms(x_ref[...], g_ref[...]).astype(MXU_DTYPE)
    cos, sin = cos_ref[...], sin_ref[...]
    hm = hm_ref[...]

    def rope(t):
        return t * cos + pltpu.roll(t, 64, 1) * sin

    def headnorm_rope(col0, gain_ref, out_ref, scale):
        t_all = _dot(h, w_ref[:, col0:col0 + d])
        for sb in range(d // MXU_DIM):
            c0 = sb * MXU_DIM
            t = t_all[:, c0:c0 + MXU_DIM]
            sq = t * t
            hi = sq.astype(MXU_DTYPE)
            lo = (sq - hi.astype(F32)).astype(MXU_DTYPE)
            ss = _dot(hi, hm) + _dot(lo, hm)
            tn = t * lax.rsqrt(ss * (1.0 / HEAD_DIM) + EPS) * gain_ref[:, c0:c0 + MXU_DIM]
            for b in range(MXU_DIM // LANES):
                r = rope(tn[:, b * LANES:(b + 1) * LANES])
                if scale != 1.0:
                    r = r * scale
                out_ref[:, c0 + b * LANES:c0 + (b + 1) * LANES] = r.astype(out_ref.dtype)

    headnorm_rope(0, gq_ref, q_ref, att_scale)
    headnorm_rope(d, gk_ref, k_ref, 1.0)
    vt = _dot_nt(wvt_ref[...], h)
    row = lax.broadcasted_iota(I32, vt.shape, 0)
    vt_ref[...] = jnp.where(row % VT_ROWS >= HEAD_DIM, 1.0, vt).astype(vt_ref.dtype)
    c0 = 2 * d
    t = _dot(h, w_ref[:, c0:c0 + dqi])
    for b in range(dqi // LANES):
        qi_ref[:, b * LANES:(b + 1) * LANES] = rope(t[:, b * LANES:(b + 1) * LANES]).astype(qi_ref.dtype)
    c0 += dqi
    t = _dot(h, w_ref[:, c0:c0 + LANES])
    ms = jnp.sum(t * t, axis=-1, keepdims=True) * (0.5 / IDX_DIM)
    ki_ref[...] = rope(t * lax.rsqrt(ms + EPS)).astype(ki_ref.dtype)
    c0 += LANES
    wi_ref[...] = _dot(h, w_ref[:, c0:c0 + LANES]) * w_scale


def _dsa_project(x, gain, w_in, q_gain, k_gain, cos_t, sin_t):
    n, d = x.shape
    dqi = IDX_HEADS * IDX_DIM
    pq = _head_perm(N_HEADS)
    pqi = _head_perm(IDX_HEADS)
    wq = w_in[:, 0:d][:, pq]
    wk = w_in[:, d:2 * d][:, pq]
    wvt = w_in[:, 2 * d:3 * d].T.reshape(N_HEADS, HEAD_DIM, d)
    wvt = jnp.pad(wvt, ((0, 0), (0, VT_PAD), (0, 0))).reshape(N_HEADS * VT_ROWS, d).astype(MXU_DTYPE)
    dvt = N_HEADS * VT_ROWS
    wqi = w_in[:, 3 * d:3 * d + dqi][:, pqi]
    lane = np.arange(LANES)
    wki = w_in[:, 3 * d + dqi:3 * d + dqi + IDX_DIM][:, 32 * (lane // 64) + lane % 32]
    wwi = jnp.pad(w_in[:, 3 * d + dqi + IDX_DIM:], ((0, 0), (0, LANES - IDX_HEADS)))
    w_all = jnp.concatenate([wq, wk, wqi, wki, wwi], axis=1).astype(MXU_DTYPE)

    def lane_gain(g):
        g = g.astype(F32).reshape(1, 2, 1, HEAD_DIM // 2)
        return jnp.broadcast_to(g, (d // LANES, 2, 2, HEAD_DIM // 2)).reshape(1, d)

    gq = lane_gain(q_gain)
    gk = lane_gain(k_gain)
    l2 = np.arange(MXU_DIM)
    hm = ((l2[:, None] // LANES == l2[None, :] // LANES)
          & ((l2[:, None] % 64) // 32 == (l2[None, :] % 64) // 32))
    hm = jnp.asarray(hm, MXU_DTYPE)
    tm = min(PROJ_TM, n)
    nw = w_all.shape[1]
    outs = pl.pallas_call(
        functools.partial(_proj_kernel, d=d, dqi=dqi, att_scale=HEAD_DIM ** -0.5 * LOG2E,
                          w_scale=(IDX_HEADS ** -0.5) * (IDX_DIM ** -0.5)),
        out_shape=[jax.ShapeDtypeStruct((n, d), MXU_DTYPE)] * 2
        + [jax.ShapeDtypeStruct((dvt, n), MXU_DTYPE),
           jax.ShapeDtypeStruct((n, dqi), MXU_DTYPE),
           jax.ShapeDtypeStruct((n, LANES), MXU_DTYPE),
           jax.ShapeDtypeStruct((n, LANES), F32)],
        grid=(n // tm,),
        in_specs=[
            pl.BlockSpec((tm, d), lambda i: (i, 0)),
            pl.BlockSpec((1, d), lambda i: (0, 0)),
            pl.BlockSpec((d, nw), lambda i: (0, 0)),
            pl.BlockSpec((dvt, d), lambda i: (0, 0)),
            pl.BlockSpec((1, d), lambda i: (0, 0)),
            pl.BlockSpec((1, d), lambda i: (0, 0)),
            pl.BlockSpec((tm, LANES), lambda i: (i, 0)),
            pl.BlockSpec((tm, LANES), lambda i: (i, 0)),
            pl.BlockSpec((MXU_DIM, MXU_DIM), lambda i: (0, 0)),
        ],
        out_specs=[pl.BlockSpec((tm, d), lambda i: (i, 0))] * 2
        + [pl.BlockSpec((dvt, tm), lambda i: (0, i)),
           pl.BlockSpec((tm, dqi), lambda i: (i, 0)),
           pl.BlockSpec((tm, LANES), lambda i: (i, 0)),
           pl.BlockSpec((tm, LANES), lambda i: (i, 0))],
        compiler_params=_cparams(("parallel",)),
        name="dsa_proj",
    )(x, gain.reshape(1, d), w_all, wvt, gq, gk, cos_t, sin_t, hm)
    return outs


KEY16_NEG_INF = -32641
KEY16_POS_INF = 32640
L16_MIN = -32768
ROW_NEVER = 32767


def _select16_kernel(qi_ref, wit_ref, ki_ref, mask_ref, h16_ref, l16_ref, lo_ref, hi_ref, clo_ref,
                     chi_ref, base_ref, *, tq, tk, topk, unroll16):
    qb = pl.program_id(0)
    n_all = mask_ref.shape[0]
    nkt = (qb * tq + tq - 1) // tk + 1
    nrow = nkt * tk
    rows16 = 2 * SUBLANES * unroll16
    ntrip = nrow // rows16

    qm = []
    for hh in range(IDX_HEADS):
        blk = qi_ref[:, (hh // 2) * LANES:(hh // 2 + 1) * LANES]
        qm.append(jnp.where(_lane_head_mask(blk.shape, hh % 2), blk, jnp.zeros_like(blk)))
    qm = jnp.concatenate(qm, axis=0)
    wt = wit_ref[...]
    causal_slack = (qb * tq + lax.broadcasted_iota(I32, (tk, tq), 1)
                    - lax.broadcasted_iota(I32, (tk, tq), 0))

    def score_tile(kt, _):
        k0 = pl.multiple_of(kt * tk, tk)
        s = _dot_nt(ki_ref[pl.ds(k0, tk), :], qm)
        acc = wt[0:1, :] * jnp.maximum(s[:, 0:tq], 0.0)
        for hh in range(1, IDX_HEADS):
            acc = acc + wt[hh:hh + 1, :] * jnp.maximum(s[:, hh * tq:(hh + 1) * tq], 0.0)
        acc = jnp.where(acc == 0.0, 0.0, acc)
        sc = jnp.where(k0 <= causal_slack, acc, -jnp.inf)
        bits = lax.bitcast_convert_type(sc, I32)
        key = bits ^ ((bits >> 31) & 0x7FFFFFFF)
        h16_ref[pl.ds(k0, tk), :] = (key >> 16).astype(I16)
        l16_ref[pl.ds(k0, tk), :] = (key ^ 0x8000).astype(I16)
        return 0

    lax.fori_loop(0, nkt, score_tile, 0)

    one16 = jnp.ones((), I16)
    zero16 = jnp.zeros((), I16)
    sub16 = (lax.broadcasted_iota(I32, (unroll16, 2 * SUBLANES, tq), 0) * (2 * SUBLANES)
             + lax.broadcasted_iota(I32, (unroll16, 2 * SUBLANES, tq), 1)).astype(I16)

    def pack16(v):
        return jnp.concatenate([v, v], axis=0).astype(I16)

    def count(pred):
        def body(i, acc):
            r0 = pl.multiple_of(i * rows16, rows16)
            hb = h16_ref[pl.ds(r0, rows16), :].reshape(unroll16, 2 * SUBLANES, tq)
            lb = l16_ref[pl.ds(r0, rows16), :].reshape(unroll16, 2 * SUBLANES, tq)
            ones = jnp.where(pred(hb, lb), one16, zero16)
            part = ones[0]
            for u in range(1, unroll16):
                part = part + ones[u]
            return acc + part.astype(I32)
        acc = lax.fori_loop(0, ntrip, body, jnp.zeros((2 * SUBLANES, tq), I32))
        return jnp.broadcast_to(jnp.sum(acc, axis=0, keepdims=True), (SUBLANES, tq))

    def bisect(count_ge, n_steps):
        def step(_, carry):
            lo, hi = lo_ref[...], hi_ref[...]
            mid = (lo + hi) >> 1
            active = mid != lo
            cnt = count_ge(pack16(mid)) + base_ref[...]
            ge = cnt >= topk
            up = active & ge
            dn = active & jnp.logical_not(ge)
            lo_ref[...] = jnp.where(up, mid, lo)
            hi_ref[...] = jnp.where(dn, mid, hi)
            clo_ref[...] = jnp.where(up, cnt, clo_ref[...])
            chi_ref[...] = jnp.where(dn, cnt, chi_ref[...])
            return carry
        lax.fori_loop(0, n_steps, step, 0)

    lo_ref[...] = jnp.full(lo_ref.shape, KEY16_NEG_INF, I32)
    hi_ref[...] = jnp.full(hi_ref.shape, KEY16_POS_INF + 1, I32)
    clo_ref[...] = jnp.full(clo_ref.shape, 1, I32) * nrow
    chi_ref[...] = jnp.zeros(chi_ref.shape, I32)
    base_ref[...] = jnp.zeros(base_ref.shape, I32)
    bisect(lambda c: count(lambda hb, lb: hb >= c[None]), 16)

    t_hi = lo_ref[...]
    none_finite = t_hi == KEY16_NEG_INF
    t_hi16 = pack16(t_hi)

    def bucket_tile(i, _):
        r0 = pl.multiple_of(i * rows16, rows16)
        hb = h16_ref[pl.ds(r0, rows16), :].reshape(unroll16, 2 * SUBLANES, tq)
        lb = l16_ref[pl.ds(r0, rows16), :].reshape(unroll16, 2 * SUBLANES, tq)
        lb = jnp.where(hb == t_hi16[None], lb, jnp.full((), L16_MIN, I16))
        l16_ref[pl.ds(r0, rows16), :] = lb.reshape(rows16, tq)
        return 0

    lax.fori_loop(0, ntrip, bucket_tile, 0)
    base_ref[...] = chi_ref[...]
    lo_ref[...] = jnp.full(lo_ref.shape, L16_MIN, I32)
    hi_ref[...] = jnp.full(hi_ref.shape, -L16_MIN, I32)
    bisect(lambda c: count(lambda hb, lb: lb >= c[None]), 16)
    t_lo = jnp.where(none_finite, -L16_MIN - 1, lo_ref[...])

    tie = (clo_ref[...] > topk) & jnp.logical_not(none_finite)
    any_tie = jnp.max(tie.astype(I32))
    t_lo16 = pack16(t_lo)

    def rank_tile(i, _):
        r0 = pl.multiple_of(i * rows16, rows16)
        hb = h16_ref[pl.ds(r0, rows16), :].reshape(unroll16, 2 * SUBLANES, tq)
        lb = l16_ref[pl.ds(r0, rows16), :].reshape(unroll16, 2 * SUBLANES, tq)
        in_bucket = hb == t_hi16[None]
        rows = sub16 + jnp.full((2 * SUBLANES, tq), r0, I32).astype(I16)[None]
        rank = jnp.where(in_bucket & (lb == t_lo16[None]), rows,
                         jnp.where(in_bucket & (lb > t_lo16[None]),
                                   jnp.full((), -1, I16), jnp.full((), ROW_NEVER, I16)))
        l16_ref[pl.ds(r0, rows16), :] = rank.reshape(rows16, tq)
        return 0

    lax.fori_loop(0, ntrip, rank_tile, 0)

    @pl.when(any_tie == 0)
    def _():
        hi_ref[...] = jnp.full(hi_ref.shape, n_all, I32)

    @pl.when(any_tie > 0)
    def _():
        chi_ref[...] = topk - base_ref[...]
        hi_ref[...] = jnp.full(hi_ref.shape, 1, I32) * (nrow - 1)
        clo_ref[...] = jnp.full(clo_ref.shape, -1, I32)

        def tie_step(_, c):
            jl, jh = clo_ref[...], hi_ref[...]
            mid = jl + ((jh - jl) >> 1)
            active = (jh - jl) > 1
            mid16 = pack16(mid)
            cnt = count(lambda hb, lb: lb <= mid16[None])
            ok = cnt >= chi_ref[...]
            hi_ref[...] = jnp.where(active & ok, mid, jh)
            clo_ref[...] = jnp.where(active & jnp.logical_not(ok), mid, jl)
            return c

        n_halvings = 0
        for j in range(int(math.ceil(math.log2(n_all // tk))) + 1):
            n_halvings = n_halvings + (((nkt - 1) >> j) > 0).astype(I32)
        lax.fori_loop(0, n_halvings + int(math.log2(tk)) + 1, tie_step, 0)
        hi_ref[...] = jnp.where(tie, hi_ref[...], n_all)

    last16 = pack16(hi_ref[...])
    g16 = 2 * SUBLANES

    def mask_tile(kt, _):
        k0 = pl.multiple_of(kt * tk, tk)
        hb = h16_ref[pl.ds(k0, tk), :].reshape(tk // g16, g16, tq)
        rank = l16_ref[pl.ds(k0, tk), :].reshape(tk // g16, g16, tq)
        sel = ((hb > t_hi16[None]) | (rank <= last16[None])) & (hb > KEY16_NEG_INF)
        sel = jnp.where(sel, one16, zero16).reshape(tk, tq)
        mask_ref[pl.ds(k0, tk), :] = sel.astype(mask_ref.dtype)
        return 0

    lax.fori_loop(0, nkt, mask_tile, 0)

    def zero_tile(kt, _):
        k0 = pl.multiple_of(kt * tk, tk)
        mask_ref[pl.ds(k0, tk), :] = jnp.zeros((tk, tq), mask_ref.dtype)
        return 0

    lax.fori_loop(nkt, n_all // tk, zero_tile, 0)


def _dsa_select16(qi, wit, ki, topk):
    n = qi.shape[0]
    tq = min(SEL_TQ, n)
    tk = min(SEL_TK, n)
    assert tk % (2 * SUBLANES * SEL_UNROLL16) == 0 and n < -L16_MIN
    return pl.pallas_call(
        functools.partial(_select16_kernel, tq=tq, tk=tk, topk=topk, unroll16=SEL_UNROLL16),
        out_shape=jax.ShapeDtypeStruct((n, n), jnp.int8),
        grid=(n // tq,),
        in_specs=[pl.BlockSpec((tq, qi.shape[1]), lambda i: (i, 0)),
                  pl.BlockSpec((IDX_HEADS, tq), lambda i: (0, i)),
                  pl.BlockSpec((n, LANES), lambda i: (0, 0))],
        out_specs=pl.BlockSpec((n, tq), lambda i: (0, i)),
        scratch_shapes=[pltpu.VMEM((n, tq), I16), pltpu.VMEM((n, tq), I16)]
        + [pltpu.VMEM((SUBLANES, tq), I32)] * 5,
        compiler_params=_cparams(("parallel",)),
        name="dsa_select",
    )(qi, wit, ki)


KEY_NEG_INF = -2139095041
KEY_POS_INF = 2139095040


def _key_to_f32(key):
    bits = key ^ ((key >> 31) & 0x7FFFFFFF)
    return lax.bitcast_convert_type(bits, F32)


def _select_f_kernel(qi_ref, wit_ref, ki_ref, mask_ref, s_ref, s16_ref, lo_ref, hi_ref, clo_ref,
                     chi_ref, *, tq, tk, topk, unroll, unroll16):
    qb = pl.program_id(0)
    n_all = mask_ref.shape[0]
    nkt = (qb * tq + tq - 1) // tk + 1
    nrow = nkt * tk
    rows_it = SUBLANES * unroll
    rows16 = 2 * SUBLANES * unroll16

    qm = []
    for hh in range(IDX_HEADS):
        blk = qi_ref[:, (hh // 2) * LANES:(hh // 2 + 1) * LANES]
        qm.append(jnp.where(_lane_head_mask(blk.shape, hh % 2), blk, jnp.zeros_like(blk)))
    qm = jnp.concatenate(qm, axis=0)
    wt = wit_ref[...]
    causal_slack = (qb * tq + lax.broadcasted_iota(I32, (tk, tq), 1)
                    - lax.broadcasted_iota(I32, (tk, tq), 0))
    key_in = lax.broadcasted_iota(I32, (tk, tq), 0)

    def score_tile(kt, _):
        k0 = pl.multiple_of(kt * tk, tk)
        s = _dot_nt(ki_ref[pl.ds(k0, tk), :], qm)
        acc = wt[0:1, :] * jnp.maximum(s[:, 0:tq], 0.0)
        for hh in range(1, IDX_HEADS):
            acc = acc + wt[hh:hh + 1, :] * jnp.maximum(s[:, hh * tq:(hh + 1) * tq], 0.0)
        acc = jnp.where(acc == 0.0, 0.0, acc)
        sc = jnp.where(k0 <= causal_slack, acc, -jnp.inf)
        s_ref[pl.ds(k0, tk), :] = sc
        s16_ref[pl.ds(k0, tk), :] = sc.astype(BF16)
        return 0

    lax.fori_loop(0, nkt, score_tile, 0)

    one16 = jnp.ones((), BF16)
    zero16 = jnp.zeros((), BF16)

    def count16(cand):
        def body(i, acc):
            r0 = pl.multiple_of(i * rows16, rows16)
            blk = s16_ref[pl.ds(r0, rows16), :].reshape(unroll16, 2 * SUBLANES, tq)
            ones = jnp.where(blk >= cand[None], one16, zero16)
            part = ones[0]
            for u in range(1, unroll16):
                part = part + ones[u]
            return acc + part.astype(F32)
        acc = lax.fori_loop(0, nrow // rows16, body, jnp.zeros((2 * SUBLANES, tq), F32))
        tot = jnp.sum(acc, axis=0, keepdims=True).astype(I32)
        return jnp.broadcast_to(tot, (SUBLANES, tq))

    def count(pred):
        def body(i, acc):
            r0 = pl.multiple_of(i * rows_it, rows_it)
            blk = s_ref[pl.ds(r0, rows_it), :].reshape(unroll, SUBLANES, tq)
            return acc + jnp.sum(pred(blk, r0).astype(I32), axis=0)
        acc = lax.fori_loop(0, nrow // rows_it, body, jnp.zeros((SUBLANES, tq), I32))
        return jnp.broadcast_to(jnp.sum(acc, axis=0, keepdims=True), (SUBLANES, tq))

    def bisect(count_ge, n_steps):
        def step(_, carry):
            lo, hi = lo_ref[...], hi_ref[...]
            mid = (lo & hi) + ((lo ^ hi) >> 1)
            active = mid != lo
            cnt = count_ge(mid)
            ge = cnt >= topk
            up = active & ge
            dn = active & jnp.logical_not(ge)
            lo_ref[...] = jnp.where(up, mid, lo)
            hi_ref[...] = jnp.where(dn, mid, hi)
            clo_ref[...] = jnp.where(up, cnt, clo_ref[...])
            chi_ref[...] = jnp.where(dn, cnt, chi_ref[...])
            return carry
        lax.fori_loop(0, n_steps, step, 0)

    lo_ref[...] = jnp.full(lo_ref.shape, KEY16_NEG_INF, I32)
    hi_ref[...] = jnp.full(hi_ref.shape, KEY16_POS_INF + 1, I32)
    clo_ref[...] = jnp.full(clo_ref.shape, 1, I32) * nrow
    chi_ref[...] = jnp.zeros(chi_ref.shape, I32)

    def cand16(mid):
        bits16 = (mid ^ ((mid >> 31) & 0x7FFF)) & 0xFFFF
        c = lax.bitcast_convert_type(bits16 << 16, F32)
        return jnp.concatenate([c, c], axis=0).astype(BF16)

    bisect(lambda mid: count16(cand16(mid)), 16)

    lo16 = lo_ref[...]
    none_finite = lo16 == KEY16_NEG_INF
    lo32 = jnp.maximum((lo16 - 1) << 16, KEY_NEG_INF)
    hi32 = jnp.minimum(((lo16 + 1) << 16) | 0xFFFF, KEY_POS_INF)
    lo32 = jnp.where(none_finite, KEY_NEG_INF, lo32)
    hi32 = jnp.where(none_finite, KEY_NEG_INF + 1, hi32)
    lo_ref[...] = lo32
    hi_ref[...] = hi32
    clo_ref[...] = count(lambda blk, r0: blk >= _key_to_f32(lo32)[None])
    chi_ref[...] = count(lambda blk, r0: blk >= _key_to_f32(hi32)[None])
    bisect(lambda mid: count(lambda blk, r0: blk >= _key_to_f32(mid)[None]), 18)

    tie = (clo_ref[...] > topk) & (lo_ref[...] > KEY_NEG_INF)
    any_tie = jnp.max(tie.astype(I32))

    @pl.when(any_tie == 0)
    def _():
        hi_ref[...] = jnp.full(hi_ref.shape, n_all, I32)

    @pl.when(any_tie > 0)
    def _():
        chi_ref[...] = topk - chi_ref[...]
        hi_ref[...] = jnp.full(hi_ref.shape, 1, I32) * (nrow - 1)
        clo_ref[...] = jnp.full(clo_ref.shape, -1, I32)
        sub = (lax.broadcasted_iota(I32, (unroll, SUBLANES, tq), 0) * SUBLANES
               + lax.broadcasted_iota(I32, (unroll, SUBLANES, tq), 1))
        thr = _key_to_f32(lo_ref[...])

        def tie_step(_, c):
            jl, jh = clo_ref[...], hi_ref[...]
            mid = jl + ((jh - jl) >> 1)
            active = (jh - jl) > 1
            cnt = count(lambda blk, r0: (blk == thr[None]) & (sub + r0 <= mid[None]))
            ok = cnt >= chi_ref[...]
            hi_ref[...] = jnp.where(active & ok, mid, jh)
            clo_ref[...] = jnp.where(active & jnp.logical_not(ok), mid, jl)
            return c

        lax.fori_loop(0, int(math.ceil(math.log2(n_all))) + 1, tie_step, 0)
        hi_ref[...] = jnp.where(tie, hi_ref[...], n_all)

    thr_row = _key_to_f32(lo_ref[0:1, :])
    last_row = hi_ref[0:1, :]

    def mask_tile(kt, _):
        k0 = pl.multiple_of(kt * tk, tk)
        s = s_ref[pl.ds(k0, tk), :]
        sel = ((s > thr_row) | ((s == thr_row) & (key_in + k0 <= last_row))) & (s > -jnp.inf)
        mask_ref[pl.ds(k0, tk), :] = jnp.where(sel, 1, 0).astype(mask_ref.dtype)
        return 0

    lax.fori_loop(0, nkt, mask_tile, 0)

    def zero_tile(kt, _):
        k0 = pl.multiple_of(kt * tk, tk)
        mask_ref[pl.ds(k0, tk), :] = jnp.zeros((tk, tq), mask_ref.dtype)
        return 0

    lax.fori_loop(nkt, n_all // tk, zero_tile, 0)


def _dsa_select_f(qi, wit, ki, topk):
    n = qi.shape[0]
    tq = min(SEL_TQ, n)
    tk = min(SEL_TK, n)
    assert tk % (SUBLANES * SEL_UNROLL) == 0 and tk % (2 * SUBLANES * SEL_UNROLL16) == 0
    return pl.pallas_call(
        functools.partial(_select_f_kernel, tq=tq, tk=tk, topk=topk, unroll=SEL_UNROLL,
                          unroll16=SEL_UNROLL16),
        out_shape=jax.ShapeDtypeStruct((n, n), jnp.int8),
        grid=(n // tq,),
        in_specs=[pl.BlockSpec((tq, qi.shape[1]), lambda i: (i, 0)),
                  pl.BlockSpec((IDX_HEADS, tq), lambda i: (0, i)),
                  pl.BlockSpec((n, LANES), lambda i: (0, 0))],
        out_specs=pl.BlockSpec((n, tq), lambda i: (0, i)),
        scratch_shapes=[pltpu.VMEM((n, tq), F32), pltpu.VMEM((n, tq), BF16)]
        + [pltpu.VMEM((SUBLANES, tq), I32)] * 4,
        compiler_params=_cparams(("parallel",)),
        name="dsa_select",
    )(qi, wit, ki)


def _attn_t_kernel(qb_tab, kt_tab, q_ref, k_ref, vt_ref, mask_ref, o_ref, acc_ref, m_ref,
                   *, tq, tk):
    step = pl.program_id(0)
    kt = kt_tab[step]

    @pl.when(kt == 0)
    def _():
        acc_ref[...] = jnp.zeros_like(acc_ref)
        m_ref[...] = jnp.full(m_ref.shape, NEG_BIG, MXU_DTYPE).astype(F32)

    bias = jnp.where(mask_ref[...].astype(I32) != 0, 0.0, NEG_BIG).astype(MXU_DTYPE)

    def logits(hd):
        blk = hd // 2
        qblk = q_ref[:, blk * LANES:(blk + 1) * LANES]
        qm = jnp.where(_lane_head_mask(qblk.shape, hd % 2), qblk, jnp.zeros_like(qblk))
        return _dot_nt(k_ref[:, blk * LANES:(blk + 1) * LANES], qm).astype(MXU_DTYPE) + bias

    s_next = logits(0)
    for hd in range(N_HEADS):
        r0 = hd * VT_ROWS
        s = s_next
        if hd + 1 < N_HEADS:
            s_next = logits(hd + 1)
        m_old = m_ref[hd:hd + 1, :]
        m_new = jnp.maximum(m_old, jnp.max(s, axis=0, keepdims=True).astype(F32))
        alpha = jnp.exp2(m_old - m_new)
        p = jnp.exp2(s - m_new.astype(MXU_DTYPE))
        m_ref[hd:hd + 1, :] = m_new
        pv = _dot(vt_ref[r0:r0 + VT_ROWS, :], p)
        acc_ref[r0:r0 + VT_ROWS, :] = acc_ref[r0:r0 + VT_ROWS, :] * alpha + pv

    last = kt == ((qb_tab[step] + 1) * tq - 1) // tk

    @pl.when(last)
    def _():
        for hd in range(N_HEADS):
            r0 = hd * VT_ROWS
            o_ref[hd * HEAD_DIM:(hd + 1) * HEAD_DIM, :] = (
                acc_ref[r0:r0 + HEAD_DIM, :] / acc_ref[r0 + HEAD_DIM:r0 + HEAD_DIM + 1, :]
            ).astype(o_ref.dtype)


def _dsa_attend_t(q, k, vt, mask_t):
    n, d = q.shape
    tq = min(ATT_TQ, n)
    tk = min(ATT_TK, n)
    pairs = [(qb, kt) for qb in range(n // tq) for kt in range(((qb + 1) * tq - 1) // tk + 1)]
    qb_tab = jnp.asarray([p[0] for p in pairs], I32)
    kt_tab = jnp.asarray([p[1] for p in pairs], I32)
    grid_spec = pltpu.PrefetchScalarGridSpec(
        num_scalar_prefetch=2,
        grid=(len(pairs),),
        in_specs=[
            pl.BlockSpec((tq, d), lambda i, qt, kt: (qt[i], 0)),
            pl.BlockSpec((tk, d), lambda i, qt, kt: (kt[i], 0)),
            pl.BlockSpec((vt.shape[0], tk), lambda i, qt, kt: (0, kt[i])),
            pl.BlockSpec((tk, tq), lambda i, qt, kt: (kt[i], qt[i])),
        ],
        out_specs=pl.BlockSpec((d, tq), lambda i, qt, kt: (0, qt[i])),
        scratch_shapes=[pltpu.VMEM((vt.shape[0], tq), F32), pltpu.VMEM((N_HEADS, tq), F32)],
    )
    return pl.pallas_call(
        functools.partial(_attn_t_kernel, tq=tq, tk=tk),
        out_shape=jax.ShapeDtypeStruct((d, n), MXU_DTYPE),
        grid_spec=grid_spec,
        compiler_params=_cparams(("arbitrary",)),
        name="dsa_attend",
    )(qb_tab, kt_tab, q, k, vt, mask_t)


def _out_t_kernel(x_ref, at_ref, w_ref, o_ref):
    o_ref[...] = x_ref[...] + lax.dot_general(
        at_ref[...], w_ref[...], (((0,), (0,)), ((), ())), preferred_element_type=F32)


def _out_residual_t(x, at, w):
    n, d = x.shape
    tm = min(ROW_TM, n)
    return pl.pallas_call(
        _out_t_kernel,
        out_shape=jax.ShapeDtypeStruct((n, d), F32),
        grid=(n // tm,),
        in_specs=[pl.BlockSpec((tm, d), lambda i: (i, 0)),
                  pl.BlockSpec((d, tm), lambda i: (0, i)),
                  pl.BlockSpec((d, d), lambda i: (0, 0))],
        out_specs=pl.BlockSpec((tm, d), lambda i: (i, 0)),
        compiler_params=_cparams(("parallel",)),
        name="attn_out",
    )(x, at, w)


def _rope_lane_tables(length):
    inv_freq = ROPE_THETA ** (-jnp.arange(0, HEAD_DIM, 2, dtype=F32) / HEAD_DIM)
    ang = jnp.arange(length, dtype=F32)[:, None] * inv_freq[None, :]
    lane = np.arange(LANES)
    cos_t = jnp.cos(ang)[:, lane % 32]
    sin_t = jnp.sin(ang)[:, lane % 32] * jnp.asarray(np.where(lane < 64, -1.0, 1.0), F32)
    return cos_t, sin_t


def kernel(x, s5_lambda_re, s5_lambda_im, s5_log_dt, s5_b_re, s5_b_im, s5_c_re, s5_c_im, s5_d, s5_w_glu, dsa_w_in, dsa_q_norm, dsa_k_norm, dsa_w_o, ffn_w_gate_up, ffn_w_down, norm_mix, norm_ffn):
    bsz, length, d = x.shape
    depth = norm_mix.shape[0]
    topk = min(TOPK_MAX, length // 4)
    cos_t, sin_t = _rope_lane_tables(length)
    wgu = ffn_w_gate_up.astype(MXU_DTYPE)
    wd = ffn_w_down.astype(MXU_DTYPE)
    tables = jax.vmap(_s5_tables)(s5_lambda_re, s5_lambda_im, s5_log_dt, s5_b_re, s5_b_im,
                                  s5_c_re, s5_c_im)
    outs = []
    for b in range(bsz):
        xs = x[b].astype(F32)
        for i in range(depth):
            j = i // 2
            if i % 2 == 0:
                h_t = _norm_planes(xs, norm_mix[i])
                g_t = _s5_scan(h_t, tables, s5_d[j], j)
                xs = _glu_planes(xs, g_t, s5_w_glu[j].astype(MXU_DTYPE))
                xs = _ffn(xs, norm_ffn[i], wgu, wd, i)
            else:
                q, k, vt, qi, ki, wi = _dsa_project(xs, norm_mix[i], dsa_w_in[j], dsa_q_norm[j],
                                                    dsa_k_norm[j], cos_t, sin_t)
                mask_t = _dsa_select_f(qi, wi[:, :IDX_HEADS].T, ki, topk)
                att_t = _dsa_attend_t(q, k, vt, mask_t)
                xs = _out_residual_t(xs, att_t, dsa_w_o[j].astype(MXU_DTYPE))
                xs = _ffn(xs, norm_ffn[i], wgu, wd, i)
        outs.append(xs)
    return jnp.stack(outs, axis=0).astype(x.dtype)
```

```python
import functools
import math

import jax
import jax.numpy as jnp
import numpy as np
from jax import lax
from jax.experimental import pallas as pl
from jax.experimental.pallas import tpu as pltpu

F32 = jnp.float32
BF16 = jnp.bfloat16
I32 = jnp.int32
I16 = jnp.int16
MXU_DTYPE = BF16

S5_GROUP = 16
S5_STATE = 64
N_HEADS = 16
HEAD_DIM = 64
IDX_HEADS = 8
IDX_DIM = 64
TOPK_MAX = 256
ROPE_THETA = 10000.0
EPS = 1e-6

LANES = 128
SUBLANES = 8
MXU_DIM = 256
VMEM_LIMIT = 56 * 1024 * 1024

S5_CHUNK = 16
S5_SLAB_GROUPS = LANES // S5_GROUP
NEG_BIG = -1e30
LOG2E = math.log2(math.e)
VT_PAD = 16
VT_ROWS = HEAD_DIM + VT_PAD

ROW_TM = 512
S5_TC = 512
PROJ_TM = 512
SEL_TQ = 256
SEL_TK = 512
SEL_UNROLL16 = 32
SEL_UNROLL = 64
ATT_TQ = 512
ATT_TK = 512


def _cparams(sem):
    return pltpu.CompilerParams(dimension_semantics=sem, vmem_limit_bytes=VMEM_LIMIT)


def _rms(x, gain=None):
    y = x * lax.rsqrt(jnp.mean(x * x, axis=-1, keepdims=True) + EPS)
    return y if gain is None else y * gain


def _dot(a, b):
    return jnp.dot(a, b, preferred_element_type=F32)


def _dot_nt(a, b):
    return lax.dot_general(a, b, (((1,), (1,)), ((), ())), preferred_element_type=F32)


def _ffn_kernel(x_ref, g_ref, wgu_ref, wd_ref, o_ref, acc_ref, *, d_ff, fc):
    x = x_ref[...]
    h = _rms(x, g_ref[...]).astype(MXU_DTYPE)
    for c in range(d_ff // fc):
        g = _dot(h, wgu_ref[:, c * fc:(c + 1) * fc])
        u = _dot(h, wgu_ref[:, d_ff + c * fc:d_ff + (c + 1) * fc])
        a = (g * jax.nn.sigmoid(g) * u).astype(MXU_DTYPE)
        d = _dot(a, wd_ref[c * fc:(c + 1) * fc, :])
        if c == 0:
            acc_ref[...] = d
        else:
            acc_ref[...] += d
    o_ref[...] = x + acc_ref[...]


def _ffn(x, gain, wgu, wd, layer):
    n, d = x.shape
    d_ff = wd.shape[1]
    tm = min(ROW_TM, n)
    fc = MXU_DIM
    return pl.pallas_call(
        functools.partial(_ffn_kernel, d_ff=d_ff, fc=fc),
        out_shape=jax.ShapeDtypeStruct((n, d), F32),
        grid=(n // tm,),
        in_specs=[
            pl.BlockSpec((tm, d), lambda i: (i, 0)),
            pl.BlockSpec((1, d), lambda i: (0, 0)),
            pl.BlockSpec((None, d, 2 * d_ff), lambda i: (layer, 0, 0)),
            pl.BlockSpec((None, d_ff, d), lambda i: (layer, 0, 0)),
        ],
        out_specs=pl.BlockSpec((tm, d), lambda i: (i, 0)),
        scratch_shapes=[pltpu.VMEM((tm, d), F32)],
        compiler_params=_cparams(("parallel",)),
        name="ffn",
    )(x, gain.reshape(1, d), wgu, wd)


def _s5_tables(lam_re, lam_im, log_dt, b_re, b_im, c_re, c_im):
    g, p = lam_re.shape
    h = S5_GROUP
    nsl = g // S5_SLAB_GROUPS
    sg = S5_SLAB_GROUPS
    t = S5_CHUNK
    lam_re, lam_im, log_dt = lam_re.astype(F32), lam_im.astype(F32), log_dt.astype(F32)
    b_re, b_im, c_re, c_im = (a.astype(F32) for a in (b_re, b_im, c_re, c_im))
    dt = jnp.exp(log_dt)[:, None]

    def apow(k):
        k = jnp.asarray(k, F32).reshape((-1, 1, 1))
        mag = jnp.exp(lam_re[None] * dt[None] * k)
        ang = lam_im[None] * dt[None] * k
        return mag * jnp.cos(ang), mag * jnp.sin(ang)

    ar, ai = apow([1.0])
    ar, ai = ar[0], ai[0]
    den = lam_re * lam_re + lam_im * lam_im
    nr, ni = ar - 1.0, ai
    qr = (nr * lam_re + ni * lam_im) / den
    qi = (ni * lam_re - nr * lam_im) / den
    bbr = qr[..., None] * b_re - qi[..., None] * b_im
    bbi = qr[..., None] * b_im + qi[..., None] * b_re

    pr, pi = apow(np.arange(t + 1))
    mr = c_re[None] * pr[:, :, None, :] - c_im[None] * pi[:, :, None, :]
    mi = c_re[None] * pi[:, :, None, :] + c_im[None] * pr[:, :, None, :]

    bbr_t, bbi_t = bbr.transpose(0, 2, 1), bbi.transpose(0, 2, 1)
    kk = jnp.sum(mr[:t, :, :, None, :] * bbr_t[None, :, None, :, :]
                 - mi[:t, :, :, None, :] * bbi_t[None, :, None, :, :], axis=-1)
    kp = jnp.concatenate([jnp.zeros_like(kk[:1]), kk], axis=0)

    def compact(x, row_dims, col_dims):
        r0, r2 = row_dims
        c0, c1 = col_dims
        return x.reshape(nsl, t // 2, r0 * sg * r2, c0 * c1).astype(MXU_DTYPE)

    dl = np.arange(t // 2)[:, None, None]
    sl = np.arange(2)[None, :, None]
    jl = np.arange(2)[None, None, :]
    idx = 2 * dl + jl - sl + 1
    kg = kp[idx]
    kg = kg.reshape(t // 2, 2, 2, nsl, sg, h, h)
    tp = compact(kg.transpose(3, 0, 1, 4, 6, 2, 5), (2, h), (2, h))

    prs, pis = pr[t - 1::-1][:t], pi[t - 1::-1][:t]
    er = prs[..., None] * bbr[None] - pis[..., None] * bbi[None]
    ei = prs[..., None] * bbi[None] + pis[..., None] * bbr[None]
    bf = jnp.stack([er, ei], axis=2)
    bf = bf.reshape(t // 2, 2, nsl, sg, 2, p, h)
    bz = compact(bf.transpose(2, 0, 1, 3, 6, 4, 5), (2, h), (2, p))

    cf = jnp.stack([mr[1:], -mi[1:]], axis=2)
    cf = cf.reshape(t // 2, 2, nsl, sg, 2, h, p)
    cz = compact(cf.transpose(2, 0, 4, 3, 6, 1, 5), (2, p), (2, h))

    def slab_state(re, im):
        k = re.shape[0]
        x = jnp.stack([re, im], axis=1).reshape(k, 2, nsl, sg * p)
        return x.transpose(2, 0, 1, 3).reshape(nsl, k, 2 * sg * p)

    ad = slab_state(*apow([t * 1.0, t * 2.0, t * 4.0]))
    ap8 = slab_state(*apow(t * (np.arange(SUBLANES) + 1.0)))
    return tp, bz, cz, ad, ap8


def _s5_expander(row_dims, col_dims):
    sg = S5_SLAB_GROUPS
    r0, r2 = row_dims
    c0, c1 = col_dims
    ci = np.arange(c0 * sg * c1)
    src = (ci // (sg * c1)) * c1 + ci % c1
    spread = np.arange(c0 * c1)[:, None] == src[None, :]
    ri = np.arange(r0 * sg * r2)
    keep = ((ri // r2) % sg)[:, None] == ((ci // c1) % sg)[None, :]
    return jnp.asarray(spread, MXU_DTYPE), jnp.asarray(keep, MXU_DTYPE)


def _s5_kernel(h_ref, tpc_ref, bzc_ref, czc_ref, sph_ref, spp_ref, ktp_ref, kbz_ref, kcz_ref,
               ad_ref, ap8_ref, dsk_ref, o_ref,
               carry_ref, z_ref, xp_ref, tp_ref, bz_ref, cz_ref, *, tc):
    half = z_ref.shape[1] // 2
    npair = S5_CHUNK // 2

    @pl.when(pl.program_id(1) == 0)
    def _():
        carry_ref[...] = jnp.zeros_like(carry_ref)
        for z in range(npair):
            tp_ref[z] = (_dot(tpc_ref[z], sph_ref[...]) * ktp_ref[...]).astype(tp_ref.dtype)
            bz_ref[z] = (_dot(bzc_ref[z], spp_ref[...]) * kbz_ref[...]).astype(bz_ref.dtype)
            cz_ref[z] = (_dot(czc_ref[z], sph_ref[...]) * kcz_ref[...]).astype(cz_ref.dtype)

    u = [jnp.concatenate([h_ref[2 * s], h_ref[2 * s + 1]], axis=1) for s in range(npair)]

    z = _dot(u[0], bz_ref[0])
    for s in range(1, npair):
        z = z + _dot(u[s], bz_ref[s])
    z_ref[...] = z

    row = lax.broadcasted_iota(I32, (SUBLANES, half), 0)
    pr8, pi8 = ap8_ref[:, :half], ap8_ref[:, half:]

    def tile_step(t, carry):
        r0 = pl.multiple_of(t * SUBLANES, SUBLANES)
        zt = z_ref[pl.ds(r0, SUBLANES), :]
        xr, xi = zt[:, :half], zt[:, half:]
        for k, d in enumerate((1, 2, 4)):
            a = ad_ref[k:k + 1, :]
            a_r, a_i = a[:, :half], a[:, half:]
            sr = jnp.where(row >= d, pltpu.roll(xr, d, 0), 0.0)
            si = jnp.where(row >= d, pltpu.roll(xi, d, 0), 0.0)
            xr, xi = xr + a_r * sr - a_i * si, xi + a_r * si + a_i * sr
        cr, ci = carry[:, :half], carry[:, half:]
        xr, xi = xr + pr8 * cr - pi8 * ci, xi + pr8 * ci + pi8 * cr
        xpr = jnp.where(row >= 1, pltpu.roll(xr, 1, 0), cr)
        xpi = jnp.where(row >= 1, pltpu.roll(xi, 1, 0), ci)
        xp_ref[pl.ds(r0, SUBLANES), :] = jnp.concatenate([xpr, xpi], axis=1)
        return jnp.concatenate([xr[SUBLANES - 1:], xi[SUBLANES - 1:]], axis=1)

    carry_ref[...] = lax.fori_loop(0, tc // SUBLANES, tile_step, carry_ref[...])

    xp = xp_ref[...].astype(MXU_DTYPE)
    dsk = dsk_ref[...]
    for i in range(npair):
        y = _dot(xp, cz_ref[i])
        for s in range(i + 1):
            y = y + _dot(u[s], tp_ref[i - s])
        for jl in range(2):
            j = 2 * i + jl
            yj = y[:, jl * LANES:(jl + 1) * LANES] + dsk * h_ref[j].astype(F32)
            o_ref[j] = jax.nn.gelu(yj).astype(o_ref.dtype)


def _s5_scan(h_t, tables, d_skip, layer):
    tpc, bzc, czc, ad, ap8 = tables
    t, c, d = h_t.shape
    nsl = d // LANES
    tc = min(S5_TC, c)
    npair = S5_CHUNK // 2
    sg, hh, pp = S5_SLAB_GROUPS, S5_GROUP, S5_STATE
    sp_h, keep_tp = _s5_expander((2, hh), (2, hh))
    sp_p, keep_bz = _s5_expander((2, hh), (2, pp))
    _, keep_cz = _s5_expander((2, pp), (2, hh))
    nu, st = 2 * sg * hh, 2 * sg * pp
    dsk = d_skip.astype(F32).reshape(nsl, 1, LANES)
    const = lambda a: pl.BlockSpec(a.shape, lambda b, i: (0,) * a.ndim)
    per_slab = lambda a: pl.BlockSpec((None, None) + a.shape[2:],
                                      lambda b, i: (layer, b) + (0,) * (a.ndim - 2))
    return pl.pallas_call(
        functools.partial(_s5_kernel, tc=tc),
        out_shape=jax.ShapeDtypeStruct((t, c, d), MXU_DTYPE),
        grid=(nsl, c // tc),
        in_specs=[
            pl.BlockSpec((t, tc, LANES), lambda b, i: (0, i, b)),
            per_slab(tpc), per_slab(bzc), per_slab(czc),
            const(sp_h), const(sp_p), const(keep_tp), const(keep_bz), const(keep_cz),
            per_slab(ad), per_slab(ap8),
            pl.BlockSpec((None, 1, LANES), lambda b, i: (b, 0, 0)),
        ],
        out_specs=pl.BlockSpec((t, tc, LANES), lambda b, i: (0, i, b)),
        scratch_shapes=[pltpu.VMEM((1, st), F32), pltpu.VMEM((tc, st), F32),
                        pltpu.VMEM((tc, st), F32),
                        pltpu.VMEM((npair, nu, nu), MXU_DTYPE),
                        pltpu.VMEM((npair, nu, st), MXU_DTYPE),
                        pltpu.VMEM((npair, st, nu), MXU_DTYPE)],
        compiler_params=_cparams(("arbitrary", "arbitrary")),
        name="s5_scan",
    )(h_t, tpc, bzc, czc, sp_h, sp_p, keep_tp, keep_bz, keep_cz, ad, ap8, dsk)


def _plane_perm(rows):
    nc = rows // S5_CHUNK
    p = np.zeros((rows, rows), np.float32)
    c, s = np.meshgrid(np.arange(nc), np.arange(S5_CHUNK), indexing="ij")
    p[(s * nc + c).ravel(), (c * S5_CHUNK + s).ravel()] = 1.0
    return p


def _norm_planes_kernel(x_ref, g_ref, p_ref, o_ref):
    h = _rms(x_ref[...], g_ref[...]).astype(MXU_DTYPE)
    hp = _dot(p_ref[...], h).astype(o_ref.dtype)
    o_ref[...] = hp.reshape(o_ref.shape)


def _norm_planes(x, gain):
    n, d = x.shape
    tm = min(ROW_TM, n)
    nc = tm // S5_CHUNK
    perm = jnp.asarray(_plane_perm(tm), MXU_DTYPE)
    return pl.pallas_call(
        _norm_planes_kernel,
        out_shape=jax.ShapeDtypeStruct((S5_CHUNK, n // S5_CHUNK, d), MXU_DTYPE),
        grid=(n // tm,),
        in_specs=[pl.BlockSpec((tm, d), lambda i: (i, 0)),
                  pl.BlockSpec((1, d), lambda i: (0, 0)),
                  pl.BlockSpec((tm, tm), lambda i: (0, 0))],
        out_specs=pl.BlockSpec((S5_CHUNK, nc, d), lambda i: (0, i, 0)),
        compiler_params=_cparams(("parallel",)),
        name="norm_planes",
    )(x, gain.reshape(1, d), perm)


def _glu_planes_kernel(x_ref, g_ref, pt_ref, w_ref, o_ref):
    d = x_ref.shape[1]
    gp = g_ref[...].reshape(x_ref.shape)
    g = _dot(pt_ref[...], gp).astype(MXU_DTYPE)
    vg = _dot(g, w_ref[...])
    o_ref[...] = x_ref[...] + vg[:, :d] * jax.nn.sigmoid(vg[:, d:])


def _glu_planes(x, g_t, w):
    n, d = x.shape
    tm = min(ROW_TM, n)
    nc = tm // S5_CHUNK
    perm_t = jnp.asarray(_plane_perm(tm).T, MXU_DTYPE)
    return pl.pallas_call(
        _glu_planes_kernel,
        out_shape=jax.ShapeDtypeStruct((n, d), F32),
        grid=(n // tm,),
        in_specs=[pl.BlockSpec((tm, d), lambda i: (i, 0)),
                  pl.BlockSpec((S5_CHUNK, nc, d), lambda i: (0, i, 0)),
                  pl.BlockSpec((tm, tm), lambda i: (0, 0)),
                  pl.BlockSpec((d, 2 * d), lambda i: (0, 0))],
        out_specs=pl.BlockSpec((tm, d), lambda i: (i, 0)),
        compiler_params=_cparams(("parallel",)),
        name="glu",
    )(x, g_t, perm_t, w)


def _head_perm(n_heads):
    n = np.arange(n_heads * HEAD_DIM)
    pb, r = n // LANES, n % LANES
    half, r2 = r // 64, r % 64
    hl, dp = r2 // 32, r2 % 32
    return (2 * pb + hl) * HEAD_DIM + 32 * half + dp


def _lane_head_mask(shape, hl):
    lane = lax.broadcasted_iota(I32, shape, len(shape) - 1)
    return ((lane % 64) // 32) == hl


def _proj_kernel(x_ref, g_ref, w_ref, wvt_ref, gq_ref, gk_ref, cos_ref, sin_ref, hm_ref,
                 q_ref, k_ref, vt_ref, qi_ref, ki_ref, wi_ref, *, d, dqi, att_scale, w_scale):
    h = _rms(x_ref[...], g_ref[...]).astype(MXU_DTYPE)
    cos, sin = cos_ref[...], sin_ref[...]
    hm = hm_ref[...]

    def rope(t):
        return t * cos + pltpu.roll(t, 64, 1) * sin

    def headnorm_rope(col0, gain_ref, out_ref, scale):
        t_all = _dot(h, w_ref[:, col0:col0 + d])
        for sb in range(d // MXU_DIM):
            c0 = sb * MXU_DIM
            t = t_all[:, c0:c0 + MXU_DIM]
            sq = t * t
            hi = sq.astype(MXU_DTYPE)
            lo = (sq - hi.astype(F32)).astype(MXU_DTYPE)
            ss = _dot(hi, hm) + _dot(lo, hm)
            tn = t * lax.rsqrt(ss * (1.0 / HEAD_DIM) + EPS) * gain_ref[:, c0:c0 + MXU_DIM]
            for b in range(MXU_DIM // LANES):
                r = rope(tn[:, b * LANES:(b + 1) * LANES])
                if scale != 1.0:
                    r = r * scale
                out_ref[:, c0 + b * LANES:c0 + (b + 1) * LANES] = r.astype(out_ref.dtype)

    headnorm_rope(0, gq_ref, q_ref, att_scale)
    headnorm_rope(d, gk_ref, k_ref, 1.0)
    vt = _dot_nt(wvt_ref[...], h)
    row = lax.broadcasted_iota(I32, vt.shape, 0)
    vt_ref[...] = jnp.where(row % VT_ROWS >= HEAD_DIM, 1.0, vt).astype(vt_ref.dtype)
    c0 = 2 * d
    t = _dot(h, w_ref[:, c0:c0 + dqi])
    for b in range(dqi // LANES):
        qi_ref[:, b * LANES:(b + 1) * LANES] = rope(t[:, b * LANES:(b + 1) * LANES]).astype(qi_ref.dtype)
    c0 += dqi
    t = _dot(h, w_ref[:, c0:c0 + LANES])
    ms = jnp.sum(t * t, axis=-1, keepdims=True) * (0.5 / IDX_DIM)
    ki_ref[...] = rope(t * lax.rsqrt(ms + EPS)).astype(ki_ref.dtype)
    c0 += LANES
    wi_ref[...] = _dot(h, w_ref[:, c0:c0 + LANES]) * w_scale


def _dsa_project(x, gain, w_in, q_gain, k_gain, cos_t, sin_t):
    n, d = x.shape
    dqi = IDX_HEADS * IDX_DIM
    pq = _head_perm(N_HEADS)
    pqi = _head_perm(IDX_HEADS)
    wq = w_in[:, 0:d][:, pq]
    wk = w_in[:, d:2 * d][:, pq]
    wvt = w_in[:, 2 * d:3 * d].T.reshape(N_HEADS, HEAD_DIM, d)
    wvt = jnp.pad(wvt, ((0, 0), (0, VT_PAD), (0, 0))).reshape(N_HEADS * VT_ROWS, d).astype(MXU_DTYPE)
    dvt = N_HEADS * VT_ROWS
    wqi = w_in[:, 3 * d:3 * d + dqi][:, pqi]
    lane = np.arange(LANES)
    wki = w_in[:, 3 * d + dqi:3 * d + dqi + IDX_DIM][:, 32 * (lane // 64) + lane % 32]
    wwi = jnp.pad(w_in[:, 3 * d + dqi + IDX_DIM:], ((0, 0), (0, LANES - IDX_HEADS)))
    w_all = jnp.concatenate([wq, wk, wqi, wki, wwi], axis=1).astype(MXU_DTYPE)

    def lane_gain(g):
        g = g.astype(F32).reshape(1, 2, 1, HEAD_DIM // 2)
        return jnp.broadcast_to(g, (d // LANES, 2, 2, HEAD_DIM // 2)).reshape(1, d)

    gq = lane_gain(q_gain)
    gk = lane_gain(k_gain)
    l2 = np.arange(MXU_DIM)
    hm = ((l2[:, None] // LANES == l2[None, :] // LANES)
          & ((l2[:, None] % 64) // 32 == (l2[None, :] % 64) // 32))
    hm = jnp.asarray(hm, MXU_DTYPE)
    tm = min(PROJ_TM, n)
    nw = w_all.shape[1]
    outs = pl.pallas_call(
        functools.partial(_proj_kernel, d=d, dqi=dqi, att_scale=HEAD_DIM ** -0.5 * LOG2E,
                          w_scale=(IDX_HEADS ** -0.5) * (IDX_DIM ** -0.5)),
        out_shape=[jax.ShapeDtypeStruct((n, d), MXU_DTYPE)] * 2
        + [jax.ShapeDtypeStruct((dvt, n), MXU_DTYPE),
           jax.ShapeDtypeStruct((n, dqi), MXU_DTYPE),
           jax.ShapeDtypeStruct((n, LANES), MXU_DTYPE),
           jax.ShapeDtypeStruct((n, LANES), F32)],
        grid=(n // tm,),
        in_specs=[
            pl.BlockSpec((tm, d), lambda i: (i, 0)),
            pl.BlockSpec((1, d), lambda i: (0, 0)),
            pl.BlockSpec((d, nw), lambda i: (0, 0)),
            pl.BlockSpec((dvt, d), lambda i: (0, 0)),
            pl.BlockSpec((1, d), lambda i: (0, 0)),
            pl.BlockSpec((1, d), lambda i: (0, 0)),
            pl.BlockSpec((tm, LANES), lambda i: (i, 0)),
            pl.BlockSpec((tm, LANES), lambda i: (i, 0)),
            pl.BlockSpec((MXU_DIM, MXU_DIM), lambda i: (0, 0)),
        ],
        out_specs=[pl.BlockSpec((tm, d), lambda i: (i, 0))] * 2
        + [pl.BlockSpec((dvt, tm), lambda i: (0, i)),
           pl.BlockSpec((tm, dqi), lambda i: (i, 0)),
           pl.BlockSpec((tm, LANES), lambda i: (i, 0)),
           pl.BlockSpec((tm, LANES), lambda i: (i, 0))],
        compiler_params=_cparams(("parallel",)),
        name="dsa_proj",
    )(x, gain.reshape(1, d), w_all, wvt, gq, gk, cos_t, sin_t, hm)
    return outs


KEY16_NEG_INF = -32641
KEY16_POS_INF = 32640
L16_MIN = -32768
ROW_NEVER = 32767


def _select16_kernel(qi_ref, wit_ref, ki_ref, mask_ref, h16_ref, l16_ref, lo_ref, hi_ref, clo_ref,
                     chi_ref, base_ref, *, tq, tk, topk, unroll16):
    qb = pl.program_id(0)
    n_all = mask_ref.shape[0]
    nkt = (qb * tq + tq - 1) // tk + 1
    nrow = nkt * tk
    rows16 = 2 * SUBLANES * unroll16
    ntrip = nrow // rows16

    qm = []
    for hh in range(IDX_HEADS):
        blk = qi_ref[:, (hh // 2) * LANES:(hh // 2 + 1) * LANES]
        qm.append(jnp.where(_lane_head_mask(blk.shape, hh % 2), blk, jnp.zeros_like(blk)))
    qm = jnp.concatenate(qm, axis=0)
    wt = wit_ref[...]
    causal_slack = (qb * tq + lax.broadcasted_iota(I32, (tk, tq), 1)
                    - lax.broadcasted_iota(I32, (tk, tq), 0))

    def score_tile(kt, _):
        k0 = pl.multiple_of(kt * tk, tk)
        s = _dot_nt(ki_ref[pl.ds(k0, tk), :], qm)
        acc = wt[0:1, :] * jnp.maximum(s[:, 0:tq], 0.0)
        for hh in range(1, IDX_HEADS):
            acc = acc + wt[hh:hh + 1, :] * jnp.maximum(s[:, hh * tq:(hh + 1) * tq], 0.0)
        acc = jnp.where(acc == 0.0, 0.0, acc)
        sc = jnp.where(k0 <= causal_slack, acc, -jnp.inf)
        bits = lax.bitcast_convert_type(sc, I32)
        key = bits ^ ((bits >> 31) & 0x7FFFFFFF)
        h16_ref[pl.ds(k0, tk), :] = (key >> 16).astype(I16)
        l16_ref[pl.ds(k0, tk), :] = (key ^ 0x8000).astype(I16)
        return 0

    lax.fori_loop(0, nkt, score_tile, 0)

    one16 = jnp.ones((), I16)
    zero16 = jnp.zeros((), I16)
    sub16 = (lax.broadcasted_iota(I32, (unroll16, 2 * SUBLANES, tq), 0) * (2 * SUBLANES)
             + lax.broadcasted_iota(I32, (unroll16, 2 * SUBLANES, tq), 1)).astype(I16)

    def pack16(v):
        return jnp.concatenate([v, v], axis=0).astype(I16)

    def count(pred):
        def body(i, acc):
            r0 = pl.multiple_of(i * rows16, rows16)
            hb = h16_ref[pl.ds(r0, rows16), :].reshape(unroll16, 2 * SUBLANES, tq)
            lb = l16_ref[pl.ds(r0, rows16), :].reshape(unroll16, 2 * SUBLANES, tq)
            ones = jnp.where(pred(hb, lb), one16, zero16)
            part = ones[0]
            for u in range(1, unroll16):
                part = part + ones[u]
            return acc + part.astype(I32)
        acc = lax.fori_loop(0, ntrip, body, jnp.zeros((2 * SUBLANES, tq), I32))
        return jnp.broadcast_to(jnp.sum(acc, axis=0, keepdims=True), (SUBLANES, tq))

    def bisect(count_ge, n_steps):
        def step(_, carry):
            lo, hi = lo_ref[...], hi_ref[...]
            mid = (lo + hi) >> 1
            active = mid != lo
            cnt = count_ge(pack16(mid)) + base_ref[...]
            ge = cnt >= topk
            up = active & ge
            dn = active & jnp.logical_not(ge)
            lo_ref[...] = jnp.where(up, mid, lo)
            hi_ref[...] = jnp.where(dn, mid, hi)
            clo_ref[...] = jnp.where(up, cnt, clo_ref[...])
            chi_ref[...] = jnp.where(dn, cnt, chi_ref[...])
            return carry
        lax.fori_loop(0, n_steps, step, 0)

    lo_ref[...] = jnp.full(lo_ref.shape, KEY16_NEG_INF, I32)
    hi_ref[...] = jnp.full(hi_ref.shape, KEY16_POS_INF + 1, I32)
    clo_ref[...] = jnp.full(clo_ref.shape, 1, I32) * nrow
    chi_ref[...] = jnp.zeros(chi_ref.shape, I32)
    base_ref[...] = jnp.zeros(base_ref.shape, I32)
    bisect(lambda c: count(lambda hb, lb: hb >= c[None]), 16)

    t_hi = lo_ref[...]
    none_finite = t_hi == KEY16_NEG_INF
    t_hi16 = pack16(t_hi)

    def bucket_tile(i, _):
        r0 = pl.multiple_of(i * rows16, rows16)
        hb = h16_ref[pl.ds(r0, rows16), :].reshape(unroll16, 2 * SUBLANES, tq)
        lb = l16_ref[pl.ds(r0, rows16), :].reshape(unroll16, 2 * SUBLANES, tq)
        lb = jnp.where(hb == t_hi16[None], lb, jnp.full((), L16_MIN, I16))
        l16_ref[pl.ds(r0, rows16), :] = lb.reshape(rows16, tq)
        return 0

    lax.fori_loop(0, ntrip, bucket_tile, 0)
    base_ref[...] = chi_ref[...]
    lo_ref[...] = jnp.full(lo_ref.shape, L16_MIN, I32)
    hi_ref[...] = jnp.full(hi_ref.shape, -L16_MIN, I32)
    bisect(lambda c: count(lambda hb, lb: lb >= c[None]), 16)
    t_lo = jnp.where(none_finite, -L16_MIN - 1, lo_ref[...])

    tie = (clo_ref[...] > topk) & jnp.logical_not(none_finite)
    any_tie = jnp.max(tie.astype(I32))
    t_lo16 = pack16(t_lo)

    def rank_tile(i, _):
        r0 = pl.multiple_of(i * rows16, rows16)
        hb = h16_ref[pl.ds(r0, rows16), :].reshape(unroll16, 2 * SUBLANES, tq)
        lb = l16_ref[pl.ds(r0, rows16), :].reshape(unroll16, 2 * SUBLANES, tq)
        in_bucket = hb == t_hi16[None]
        rows = sub16 + jnp.full((2 * SUBLANES, tq), r0, I32).astype(I16)[None]
        rank = jnp.where(in_bucket & (lb == t_lo16[None]), rows,
                         jnp.where(in_bucket & (lb > t_lo16[None]),
                                   jnp.full((), -1, I16), jnp.full((), ROW_NEVER, I16)))
        l16_ref[pl.ds(r0, rows16), :] = rank.reshape(rows16, tq)
        return 0

    lax.fori_loop(0, ntrip, rank_tile, 0)

    @pl.when(any_tie == 0)
    def _():
        hi_ref[...] = jnp.full(hi_ref.shape, n_all, I32)

    @pl.when(any_tie > 0)
    def _():
        chi_ref[...] = topk - base_ref[...]
        hi_ref[...] = jnp.full(hi_ref.shape, 1, I32) * (nrow - 1)
        clo_ref[...] = jnp.full(clo_ref.shape, -1, I32)

        def tie_step(_, c):
            jl, jh = clo_ref[...], hi_ref[...]
            mid = jl + ((jh - jl) >> 1)
            active = (jh - jl) > 1
            mid16 = pack16(mid)
            cnt = count(lambda hb, lb: lb <= mid16[None])
            ok = cnt >= chi_ref[...]
            hi_ref[...] = jnp.where(active & ok, mid, jh)
            clo_ref[...] = jnp.where(active & jnp.logical_not(ok), mid, jl)
            return c

        n_halvings = 0
        for j in range(int(math.ceil(math.log2(n_all // tk))) + 1):
            n_halvings = n_halvings + (((nkt - 1) >> j) > 0).astype(I32)
        lax.fori_loop(0, n_halvings + int(math.log2(tk)) + 1, tie_step, 0)
        hi_ref[...] = jnp.where(tie, hi_ref[...], n_all)

    last16 = pack16(hi_ref[...])
    g16 = 2 * SUBLANES

    def mask_tile(kt, _):
        k0 = pl.multiple_of(kt * tk, tk)
        hb = h16_ref[pl.ds(k0, tk), :].reshape(tk // g16, g16, tq)
        rank = l16_ref[pl.ds(k0, tk), :].reshape(tk // g16, g16, tq)
        sel = ((hb > t_hi16[None]) | (rank <= last16[None])) & (hb > KEY16_NEG_INF)
        sel = jnp.where(sel, one16, zero16).reshape(tk, tq)
        mask_ref[pl.ds(k0, tk), :] = sel.astype(mask_ref.dtype)
        return 0

    lax.fori_loop(0, nkt, mask_tile, 0)

    def zero_tile(kt, _):
        k0 = pl.multiple_of(kt * tk, tk)
        mask_ref[pl.ds(k0, tk), :] = jnp.zeros((tk, tq), mask_ref.dtype)
        return 0

    lax.fori_loop(nkt, n_all // tk, zero_tile, 0)


def _dsa_select16(qi, wit, ki, topk):
    n = qi.shape[0]
    tq = min(SEL_TQ, n)
    tk = min(SEL_TK, n)
    assert tk % (2 * SUBLANES * SEL_UNROLL16) == 0 and n < -L16_MIN
    return pl.pallas_call(
        functools.partial(_select16_kernel, tq=tq, tk=tk, topk=topk, unroll16=SEL_UNROLL16),
        out_shape=jax.ShapeDtypeStruct((n, n), jnp.int8),
        grid=(n // tq,),
        in_specs=[pl.BlockSpec((tq, qi.shape[1]), lambda i: (i, 0)),
                  pl.BlockSpec((IDX_HEADS, tq), lambda i: (0, i)),
                  pl.BlockSpec((n, LANES), lambda i: (0, 0))],
        out_specs=pl.BlockSpec((n, tq), lambda i: (0, i)),
        scratch_shapes=[pltpu.VMEM((n, tq), I16), pltpu.VMEM((n, tq), I16)]
        + [pltpu.VMEM((SUBLANES, tq), I32)] * 5,
        compiler_params=_cparams(("parallel",)),
        name="dsa_select",
    )(qi, wit, ki)


KEY_NEG_INF = -2139095041
KEY_POS_INF = 2139095040


def _key_to_f32(key):
    bits = key ^ ((key >> 31) & 0x7FFFFFFF)
    return lax.bitcast_convert_type(bits, F32)


def _select_f_kernel(qi_ref, wit_ref, ki_ref, mask_ref, s_ref, s16_ref, lo_ref, hi_ref, clo_ref,
                     chi_ref, *, tq, tk, topk, unroll, unroll16):
    qb = pl.program_id(0)
    n_all = mask_ref.shape[0]
    nkt = (qb * tq + tq - 1) // tk + 1
    nrow = nkt * tk
    rows_it = SUBLANES * unroll
    rows16 = 2 * SUBLANES * unroll16

    qm = []
    for hh in range(IDX_HEADS):
        blk = qi_ref[:, (hh // 2) * LANES:(hh // 2 + 1) * LANES]
        qm.append(jnp.where(_lane_head_mask(blk.shape, hh % 2), blk, jnp.zeros_like(blk)))
    qm = jnp.concatenate(qm, axis=0)
    wt = wit_ref[...]
    causal_slack = (qb * tq + lax.broadcasted_iota(I32, (tk, tq), 1)
                    - lax.broadcasted_iota(I32, (tk, tq), 0))
    key_in = lax.broadcasted_iota(I32, (tk, tq), 0)

    def score_tile(kt, _):
        k0 = pl.multiple_of(kt * tk, tk)
        s = _dot_nt(ki_ref[pl.ds(k0, tk), :], qm)
        acc = wt[0:1, :] * jnp.maximum(s[:, 0:tq], 0.0)
        for hh in range(1, IDX_HEADS):
            acc = acc + wt[hh:hh + 1, :] * jnp.maximum(s[:, hh * tq:(hh + 1) * tq], 0.0)
        acc = jnp.where(acc == 0.0, 0.0, acc)
        sc = jnp.where(k0 <= causal_slack, acc, -jnp.inf)
        s_ref[pl.ds(k0, tk), :] = sc
        s16_ref[pl.ds(k0, tk), :] = sc.astype(BF16)
        return 0

    lax.fori_loop(0, nkt, score_tile, 0)

    one16 = jnp.ones((), BF16)
    zero16 = jnp.zeros((), BF16)

    def count16(cand):
        def body(i, acc):
            r0 = pl.multiple_of(i * rows16, rows16)
            blk = s16_ref[pl.ds(r0, rows16), :].reshape(unroll16, 2 * SUBLANES, tq)
            ones = jnp.where(blk >= cand[None], one16, zero16)
            part = ones[0]
            for u in range(1, unroll16):
                part = part + ones[u]
            return acc + part.astype(F32)
        acc = lax.fori_loop(0, nrow // rows16, body, jnp.zeros((2 * SUBLANES, tq), F32))
        tot = jnp.sum(acc, axis=0, keepdims=True).astype(I32)
        return jnp.broadcast_to(tot, (SUBLANES, tq))

    def count(pred):
        def body(i, acc):
            r0 = pl.multiple_of(i * rows_it, rows_it)
            blk = s_ref[pl.ds(r0, rows_it), :].reshape(unroll, SUBLANES, tq)
            return acc + jnp.sum(pred(blk, r0).astype(I32), axis=0)
        acc = lax.fori_loop(0, nrow // rows_it, body, jnp.zeros((SUBLANES, tq), I32))
        return jnp.broadcast_to(jnp.sum(acc, axis=0, keepdims=True), (SUBLANES, tq))

    def bisect(count_ge, n_steps):
        def step(_, carry):
            lo, hi = lo_ref[...], hi_ref[...]
            mid = (lo & hi) + ((lo ^ hi) >> 1)
            active = mid != lo
            cnt = count_ge(mid)
            ge = cnt >= topk
            up = active & ge
            dn = active & jnp.logical_not(ge)
            lo_ref[...] = jnp.where(up, mid, lo)
            hi_ref[...] = jnp.where(dn, mid, hi)
            clo_ref[...] = jnp.where(up, cnt, clo_ref[...])
            chi_ref[...] = jnp.where(dn, cnt, chi_ref[...])
            return carry
        lax.fori_loop(0, n_steps, step, 0)

    lo_ref[...] = jnp.full(lo_ref.shape, KEY16_NEG_INF, I32)
    hi_ref[...] = jnp.full(hi_ref.shape, KEY16_POS_INF + 1, I32)
    clo_ref[...] = jnp.full(clo_ref.shape, 1, I32) * nrow
    chi_ref[...] = jnp.zeros(chi_ref.shape, I32)

    def cand16(mid):
        bits16 = (mid ^ ((mid >> 31) & 0x7FFF)) & 0xFFFF
        c = lax.bitcast_convert_type(bits16 << 16, F32)
        return jnp.concatenate([c, c], axis=0).astype(BF16)

    bisect(lambda mid: count16(cand16(mid)), 16)

    lo16 = lo_ref[...]
    none_finite = lo16 == KEY16_NEG_INF
    lo32 = jnp.maximum((lo16 - 1) << 16, KEY_NEG_INF)
    hi32 = jnp.minimum(((lo16 + 1) << 16) | 0xFFFF, KEY_POS_INF)
    lo32 = jnp.where(none_finite, KEY_NEG_INF, lo32)
    hi32 = jnp.where(none_finite, KEY_NEG_INF + 1, hi32)
    lo_ref[...] = lo32
    hi_ref[...] = hi32
    clo_ref[...] = count(lambda blk, r0: blk >= _key_to_f32(lo32)[None])
    chi_ref[...] = count(lambda blk, r0: blk >= _key_to_f32(hi32)[None])
    bisect(lambda mid: count(lambda blk, r0: blk >= _key_to_f32(mid)[None]), 18)

    tie = (clo_ref[...] > topk) & (lo_ref[...] > KEY_NEG_INF)
    any_tie = jnp.max(tie.astype(I32))

    @pl.when(any_tie == 0)
    def _():
        hi_ref[...] = jnp.full(hi_ref.shape, n_all, I32)

    @pl.when(any_tie > 0)
    def _():
        chi_ref[...] = topk - chi_ref[...]
        hi_ref[...] = jnp.full(hi_ref.shape, 1, I32) * (nrow - 1)
        clo_ref[...] = jnp.full(clo_ref.shape, -1, I32)
        sub = (lax.broadcasted_iota(I32, (unroll, SUBLANES, tq), 0) * SUBLANES
               + lax.broadcasted_iota(I32, (unroll, SUBLANES, tq), 1))
        thr = _key_to_f32(lo_ref[...])

        def tie_step(_, c):
            jl, jh = clo_ref[...], hi_ref[...]
            mid = jl + ((jh - jl) >> 1)
            active = (jh - jl) > 1
            cnt = count(lambda blk, r0: (blk == thr[None]) & (sub + r0 <= mid[None]))
            ok = cnt >= chi_ref[...]
            hi_ref[...] = jnp.where(active & ok, mid, jh)
            clo_ref[...] = jnp.where(active & jnp.logical_not(ok), mid, jl)
            return c

        n_halvings = 0
        for j in range(int(math.ceil(math.log2(n_all // tk))) + 1):
            n_halvings = n_halvings + (((nkt - 1) >> j) > 0).astype(I32)
        lax.fori_loop(0, n_halvings + int(math.log2(tk)) + 1, tie_step, 0)
        hi_ref[...] = jnp.where(tie, hi_ref[...], n_all)

    thr_row = _key_to_f32(lo_ref[0:1, :])
    last_row = hi_ref[0:1, :]

    def mask_tile(kt, _):
        k0 = pl.multiple_of(kt * tk, tk)
        s = s_ref[pl.ds(k0, tk), :]
        sel = ((s > thr_row) | ((s == thr_row) & (key_in + k0 <= last_row))) & (s > -jnp.inf)
        mask_ref[pl.ds(k0, tk), :] = jnp.where(sel, 1, 0).astype(mask_ref.dtype)
        return 0

    lax.fori_loop(0, nkt, mask_tile, 0)

    def zero_tile(kt, _):
        k0 = pl.multiple_of(kt * tk, tk)
        mask_ref[pl.ds(k0, tk), :] = jnp.zeros((tk, tq), mask_ref.dtype)
        return 0

    lax.fori_loop(nkt, n_all // tk, zero_tile, 0)


def _dsa_select_f(qi, wit, ki, topk):
    n = qi.shape[0]
    tq = min(SEL_TQ, n)
    tk = min(SEL_TK, n)
    assert tk % (SUBLANES * SEL_UNROLL) == 0 and tk % (2 * SUBLANES * SEL_UNROLL16) == 0
    return pl.pallas_call(
        functools.partial(_select_f_kernel, tq=tq, tk=tk, topk=topk, unroll=SEL_UNROLL,
                          unroll16=SEL_UNROLL16),
        out_shape=jax.ShapeDtypeStruct((n, n), jnp.int8),
        grid=(n // tq,),
        in_specs=[pl.BlockSpec((tq, qi.shape[1]), lambda i: (i, 0)),
                  pl.BlockSpec((IDX_HEADS, tq), lambda i: (0, i)),
                  pl.BlockSpec((n, LANES), lambda i: (0, 0))],
        out_specs=pl.BlockSpec((n, tq), lambda i: (0, i)),
        scratch_shapes=[pltpu.VMEM((n, tq), F32), pltpu.VMEM((n, tq), BF16)]
        + [pltpu.VMEM((SUBLANES, tq), I32)] * 4,
        compiler_params=_cparams(("parallel",)),
        name="dsa_select",
    )(qi, wit, ki)


def _attn_t_kernel(qb_tab, kt_tab, q_ref, k_ref, vt_ref, mask_ref, o_ref, acc_ref, m_ref,
                   *, tq, tk):
    step = pl.program_id(0)
    kt = kt_tab[step]

    @pl.when(kt == 0)
    def _():
        acc_ref[...] = jnp.zeros_like(acc_ref)
        m_ref[...] = jnp.full(m_ref.shape, NEG_BIG, MXU_DTYPE).astype(F32)

    bias = jnp.where(mask_ref[...].astype(I32) != 0, 0.0, NEG_BIG).astype(MXU_DTYPE)

    def logits(hd):
        blk = hd // 2
        qblk = q_ref[:, blk * LANES:(blk + 1) * LANES]
        qm = jnp.where(_lane_head_mask(qblk.shape, hd % 2), qblk, jnp.zeros_like(qblk))
        return _dot_nt(k_ref[:, blk * LANES:(blk + 1) * LANES], qm).astype(MXU_DTYPE) + bias

    s_next = logits(0)
    for hd in range(N_HEADS):
        r0 = hd * VT_ROWS
        s = s_next
        if hd + 1 < N_HEADS:
            s_next = logits(hd + 1)
        m_old = m_ref[hd:hd + 1, :]
        m_new = jnp.maximum(m_old, jnp.max(s, axis=0, keepdims=True).astype(F32))
        alpha = jnp.exp2(m_old - m_new)
        p = jnp.exp2(s - m_new.astype(MXU_DTYPE))
        m_ref[hd:hd + 1, :] = m_new
        pv = _dot(vt_ref[r0:r0 + VT_ROWS, :], p)
        acc_ref[r0:r0 + VT_ROWS, :] = acc_ref[r0:r0 + VT_ROWS, :] * alpha + pv

    last = kt == ((qb_tab[step] + 1) * tq - 1) // tk

    @pl.when(last)
    def _():
        for hd in range(N_HEADS):
            r0 = hd * VT_ROWS
            o_ref[hd * HEAD_DIM:(hd + 1) * HEAD_DIM, :] = (
                acc_ref[r0:r0 + HEAD_DIM, :] / acc_ref[r0 + HEAD_DIM:r0 + HEAD_DIM + 1, :]
            ).astype(o_ref.dtype)


def _dsa_attend_t(q, k, vt, mask_t):
    n, d = q.shape
    tq = min(ATT_TQ, n)
    tk = min(ATT_TK, n)
    pairs = [(qb, kt) for qb in range(n // tq) for kt in range(((qb + 1) * tq - 1) // tk + 1)]
    qb_tab = jnp.asarray([p[0] for p in pairs], I32)
    kt_tab = jnp.asarray([p[1] for p in pairs], I32)
    grid_spec = pltpu.PrefetchScalarGridSpec(
        num_scalar_prefetch=2,
        grid=(len(pairs),),
        in_specs=[
            pl.BlockSpec((tq, d), lambda i, qt, kt: (qt[i], 0)),
            pl.BlockSpec((tk, d), lambda i, qt, kt: (kt[i], 0)),
            pl.BlockSpec((vt.shape[0], tk), lambda i, qt, kt: (0, kt[i])),
            pl.BlockSpec((tk, tq), lambda i, qt, kt: (kt[i], qt[i])),
        ],
        out_specs=pl.BlockSpec((d, tq), lambda i, qt, kt: (0, qt[i])),
        scratch_shapes=[pltpu.VMEM((vt.shape[0], tq), F32), pltpu.VMEM((N_HEADS, tq), F32)],
    )
    return pl.pallas_call(
        functools.partial(_attn_t_kernel, tq=tq, tk=tk),
        out_shape=jax.ShapeDtypeStruct((d, n), MXU_DTYPE),
        grid_spec=grid_spec,
        compiler_params=_cparams(("arbitrary",)),
        name="dsa_attend",
    )(qb_tab, kt_tab, q, k, vt, mask_t)


def _out_t_kernel(x_ref, at_ref, w_ref, o_ref):
    o_ref[...] = x_ref[...] + lax.dot_general(
        at_ref[...], w_ref[...], (((0,), (0,)), ((), ())), preferred_element_type=F32)


def _out_residual_t(x, at, w):
    n, d = x.shape
    tm = min(ROW_TM, n)
    return pl.pallas_call(
        _out_t_kernel,
        out_shape=jax.ShapeDtypeStruct((n, d), F32),
        grid=(n // tm,),
        in_specs=[pl.BlockSpec((tm, d), lambda i: (i, 0)),
                  pl.BlockSpec((d, tm), lambda i: (0, i)),
                  pl.BlockSpec((d, d), lambda i: (0, 0))],
        out_specs=pl.BlockSpec((tm, d), lambda i: (i, 0)),
        compiler_params=_cparams(("parallel",)),
        name="attn_out",
    )(x, at, w)


def _rope_lane_tables(length):
    inv_freq = ROPE_THETA ** (-jnp.arange(0, HEAD_DIM, 2, dtype=F32) / HEAD_DIM)
    ang = jnp.arange(length, dtype=F32)[:, None] * inv_freq[None, :]
    lane = np.arange(LANES)
    cos_t = jnp.cos(ang)[:, lane % 32]
    sin_t = jnp.sin(ang)[:, lane % 32] * jnp.asarray(np.where(lane < 64, -1.0, 1.0), F32)
    return cos_t, sin_t


def kernel(x, s5_lambda_re, s5_lambda_im, s5_log_dt, s5_b_re, s5_b_im, s5_c_re, s5_c_im, s5_d, s5_w_glu, dsa_w_in, dsa_q_norm, dsa_k_norm, dsa_w_o, ffn_w_gate_up, ffn_w_down, norm_mix, norm_ffn):
    bsz, length, d = x.shape
    depth = norm_mix.shape[0]
    topk = min(TOPK_MAX, length // 4)
    cos_t, sin_t = _rope_lane_tables(length)
    wgu = ffn_w_gate_up.astype(MXU_DTYPE)
    wd = ffn_w_down.astype(MXU_DTYPE)
    tables = jax.vmap(_s5_tables)(s5_lambda_re, s5_lambda_im, s5_log_dt, s5_b_re, s5_b_im,
                                  s5_c_re, s5_c_im)
    outs = []
    for b in range(bsz):
        xs = x[b].astype(F32)
        for i in range(depth):
            j = i // 2
            if i % 2 == 0:
                h_t = _norm_planes(xs, norm_mix[i])
                g_t = _s5_scan(h_t, tables, s5_d[j], j)
                xs = _glu_planes(xs, g_t, s5_w_glu[j].astype(MXU_DTYPE))
                xs = _ffn(xs, norm_ffn[i], wgu, wd, i)
            else:
                q, k, vt, qi, ki, wi = _dsa_project(xs, norm_mix[i], dsa_w_in[j], dsa_q_norm[j],
                                                    dsa_k_norm[j], cos_t, sin_t)
                mask_t = _dsa_select_f(qi, wi[:, :IDX_HEADS].T, ki, topk)
                att_t = _dsa_attend_t(q, k, vt, mask_t)
                xs = _out_residual_t(xs, att_t, dsa_w_o[j].astype(MXU_DTYPE))
                xs = _ffn(xs, norm_ffn[i], wgu, wd, i)
        outs.append(xs)
    return jnp.stack(outs, axis=0).astype(x.dtype)
```
